```python
import math
import jax, jax.numpy as jnp
from jax import lax
import numpy as np

D_MODEL = 2048
BATCH = 1
SEQ = 8192
DEPTH = 1

N_ATTN_HEADS = 16
HEAD_DIM = 64
ATTN_WIDTH = N_ATTN_HEADS * HEAD_DIM
CONV_CHANNELS = D_MODEL - ATTN_WIDTH
CONV_WIDTH = 31
MIX_WIDTH = CONV_CHANNELS + ATTN_WIDTH
IN_WIDTH = 2 * CONV_CHANNELS + 3 * ATTN_WIDTH
DILATED_PATTERNS = ((128, 1), (512, 4), (2048, 16))
BLOCK = 128
NUM_BUCKETS = 32
MAX_DISTANCE = 2048
N_GROUPS = 4
EXPERTS_PER_GROUP = 8
EXPERT_TOP_K = 2
D_FF_EXPERT = D_MODEL // 2
NORM_EPS = 1e-6
NEG_INF = -1e30

kernel_name = "hymba_conformer_dilated_hmoe_layer"


def _rmsnorm(x, g):
    xf = x.astype(jnp.float32)
    y = xf * lax.rsqrt(jnp.mean(xf * xf, axis=-1, keepdims=True) + NORM_EPS)
    return (y * g.astype(jnp.float32)).astype(x.dtype)


def _layernorm(x, g, b):
    xf = x.astype(jnp.float32)
    mu = jnp.mean(xf, axis=-1, keepdims=True)
    var = jnp.mean(jnp.square(xf - mu), axis=-1, keepdims=True)
    y = (xf - mu) * lax.rsqrt(var + NORM_EPS)
    return (y * g.astype(jnp.float32) + b.astype(jnp.float32)).astype(x.dtype)


def _t5_bucket(dist):
    max_exact = NUM_BUCKETS // 2
    nf = jnp.maximum(dist, 1).astype(jnp.float32)
    large = max_exact + (jnp.log(nf / max_exact) / math.log(MAX_DISTANCE / max_exact)
                         * (NUM_BUCKETS - max_exact)).astype(jnp.int32)
    large = jnp.minimum(large, NUM_BUCKETS - 1)
    return jnp.where(dist < max_exact, dist, large)


def _conv_mixer(a_val, a_gate, conv_w, conv_b, ln_g, ln_b):
    u = a_val * jax.nn.sigmoid(a_gate)
    kern = conv_w.astype(u.dtype)[:, None, :]
    u = lax.conv_general_dilated(u, kern, window_strides=(1,), padding=[(CONV_WIDTH - 1, 0)],
                                 dimension_numbers=("NWC", "WIO", "NWC"),
                                 feature_group_count=CONV_CHANNELS) + conv_b.astype(u.dtype)
    return jax.nn.silu(_layernorm(u, ln_g, ln_b))


def _dilated_window_attention(q, k, v, rel_bias, window, dilation):
    B, H, S, hd = q.shape
    span = window // dilation
    assert span <= BLOCK
    chunk = dilation * BLOCK
    s_pad = -(-S // chunk) * chunk
    L = s_pad // dilation
    nb = L // BLOCK
    padw = ((0, 0), (0, 0), (0, s_pad - S), (0, 0))

    def to_blocks(t):
        t = jnp.pad(t, padw).reshape(B, H, L, dilation, hd).transpose(0, 1, 3, 2, 4)
        return t.reshape(B, H, dilation, nb, BLOCK, hd)

    def with_prev(t):
        prev = jnp.pad(t[:, :, :, :-1], ((0, 0), (0, 0), (0, 0), (1, 0), (0, 0), (0, 0)))
        return jnp.concatenate([prev, t], axis=4)

    qb = to_blocks(q)
    kc = with_prev(to_blocks(k))
    vc = with_prev(to_blocks(v))

    qi = jnp.arange(BLOCK)[:, None]
    kj = jnp.arange(2 * BLOCK)[None, :]
    rel = BLOCK + qi - kj
    in_band = (rel >= 0) & (rel <= span)
    key_idx = jnp.arange(nb)[:, None, None] * BLOCK + kj[None] - BLOCK
    mask = in_band[None] & (key_idx >= 0)
    bucket = _t5_bucket(jnp.clip(rel, 0, span) * dilation)
    bias = jnp.transpose(rel_bias.astype(jnp.float32)[bucket], (2, 0, 1))

    scale = 1.0 / math.sqrt(hd)
    scores = jnp.einsum("bhrnqd,bhrnkd->bhrnqk", qb, kc) * scale + bias[None, :, None, None]
    scores = jnp.where(mask[None, None, None], scores, NEG_INF)
    m = jnp.max(scores, axis=-1, keepdims=True)
    p = jnp.exp(scores - m)
    denom = jnp.sum(p, axis=-1)
    o = jnp.einsum("bhrnqk,bhrnkd->bhrnqd", p, vc) / denom[..., None]
    lse = m[..., 0] + jnp.log(denom)

    o = o.reshape(B, H, dilation, L, hd).transpose(0, 1, 3, 2, 4).reshape(B, H, s_pad, hd)[:, :, :S]
    lse = lse.reshape(B, H, dilation, L).transpose(0, 1, 3, 2).reshape(B, H, s_pad)[:, :, :S]
    return o, lse


def _dilated_attention_mixer(q, k, v, q_g, k_g, rel_bias):
    B, S, _ = q.shape
    qh = _rmsnorm(q.reshape(B, S, N_ATTN_HEADS, HEAD_DIM), q_g).astype(jnp.float32).transpose(0, 2, 1, 3)
    kh = _rmsnorm(k.reshape(B, S, N_ATTN_HEADS, HEAD_DIM), k_g).astype(jnp.float32).transpose(0, 2, 1, 3)
    vh = v.reshape(B, S, N_ATTN_HEADS, HEAD_DIM).astype(jnp.float32).transpose(0, 2, 1, 3)
    outs, lses = [], []
    for window, dilation in DILATED_PATTERNS:
        o, lse = _dilated_window_attention(qh, kh, vh, rel_bias, window, dilation)
        outs.append(o)
        lses.append(lse)
    o = jnp.stack(outs)
    wts = jax.nn.softmax(jnp.stack(lses), axis=0)
    o = jnp.sum(wts[..., None] * o, axis=0)
    return o.transpose(0, 2, 1, 3).reshape(B, S, ATTN_WIDTH).astype(q.dtype)


def _hier_moe(xn, w_rg, b_rg, w_re, b_re, w_gate, w_up, w_down):
    B, S, D = xn.shape
    t = xn.reshape(B * S, D)
    glog = (t @ w_rg + b_rg).astype(jnp.float32)
    gprob = jax.nn.softmax(glog, axis=-1)
    _, gidx = lax.top_k(glog, 1)
    gw = jnp.take_along_axis(gprob, gidx, axis=1)[:, 0]
    elog = (jnp.einsum("td,gde->tge", t, w_re) + b_re).astype(jnp.float32)
    elog_sel = jnp.take_along_axis(elog, gidx[:, :, None], axis=1)[:, 0]
    top_v, top_i = lax.top_k(elog_sel, EXPERT_TOP_K)
    top_w = jax.nn.softmax(top_v, axis=-1)
    ew = jnp.sum(jax.nn.one_hot(top_i, EXPERTS_PER_GROUP, dtype=jnp.float32) * top_w[..., None], axis=1)
    gate = (jax.nn.one_hot(gidx[:, 0], N_GROUPS, dtype=jnp.float32)[:, :, None]
            * (gw[:, None] * ew)[:, None, :]).astype(t.dtype)
    y = jnp.zeros_like(t)
    for g in range(N_GROUPS):
        h = (jax.nn.silu(jnp.einsum("td,edf->tef", t, w_gate[g]))
             * jnp.einsum("td,edf->tef", t, w_up[g]) * gate[:, g, :, None])
        y = y + jnp.einsum("tef,efd->td", h, w_down[g])
    return y.reshape(B, S, D)


def setup_inputs(seed: int = 0) -> dict:
    key = jax.random.key(seed)
    ks = jax.random.split(key, 20)
    f32 = jnp.float32
    nrm = lambda k, shape, s: jax.random.normal(k, shape, f32) * s
    return {
        "x": nrm(ks[0], (BATCH, SEQ, D_MODEL), 1.0),
        "norm1_g": 1.0 + nrm(ks[1], (DEPTH, D_MODEL), 0.02),
        "w_in": nrm(ks[2], (DEPTH, D_MODEL, IN_WIDTH), D_MODEL ** -0.5),
        "q_norm_g": 1.0 + nrm(ks[3], (DEPTH, HEAD_DIM), 0.02),
        "k_norm_g": 1.0 + nrm(ks[4], (DEPTH, HEAD_DIM), 0.02),
        "conv_w": nrm(ks[5], (DEPTH, CONV_WIDTH, CONV_CHANNELS), CONV_WIDTH ** -0.5),
        "conv_b": nrm(ks[6], (DEPTH, CONV_CHANNELS), 0.02),
        "conv_ln_g": 1.0 + nrm(ks[7], (DEPTH, CONV_CHANNELS), 0.02),
        "conv_ln_b": nrm(ks[8], (DEPTH, CONV_CHANNELS), 0.02),
        "rel_bias": nrm(ks[9], (NUM_BUCKETS, N_ATTN_HEADS), 0.1),
        "w_out": nrm(ks[10], (DEPTH, MIX_WIDTH, D_MODEL), MIX_WIDTH ** -0.5),
        "norm2_g": 1.0 + nrm(ks[11], (DEPTH, D_MODEL), 0.02),
        "w_router_group": nrm(ks[12], (DEPTH, D_MODEL, N_GROUPS), D_MODEL ** -0.5),
        "b_router_group": nrm(ks[13], (DEPTH, N_GROUPS), 0.01),
        "w_router_expert": nrm(ks[14], (DEPTH, N_GROUPS, D_MODEL, EXPERTS_PER_GROUP), D_MODEL ** -0.5),
        "b_router_expert": nrm(ks[15], (DEPTH, N_GROUPS, EXPERTS_PER_GROUP), 0.01),
        "w_gate": nrm(ks[16], (DEPTH, N_GROUPS, EXPERTS_PER_GROUP, D_MODEL, D_FF_EXPERT), D_MODEL ** -0.5),
        "w_up": nrm(ks[17], (DEPTH, N_GROUPS, EXPERTS_PER_GROUP, D_MODEL, D_FF_EXPERT), D_MODEL ** -0.5),
        "w_down": nrm(ks[18], (DEPTH, N_GROUPS, EXPERTS_PER_GROUP, D_FF_EXPERT, D_MODEL), D_FF_EXPERT ** -0.5),
    }


def reference(x, norm1_g, w_in, q_norm_g, k_norm_g, conv_w, conv_b, conv_ln_g, conv_ln_b,
              rel_bias, w_out, norm2_g, w_router_group, b_router_group, w_router_expert,
              b_router_expert, w_gate, w_up, w_down):
    C, A = CONV_CHANNELS, ATTN_WIDTH
    for layer in range(DEPTH):
        xn = _rmsnorm(x, norm1_g[layer])
        proj = xn @ w_in[layer]
        a_val, a_gate, q, k, v = jnp.split(proj, [C, 2 * C, 2 * C + A, 2 * C + 2 * A], axis=-1)
        conv_out = _conv_mixer(a_val, a_gate, conv_w[layer], conv_b[layer],
                               conv_ln_g[layer], conv_ln_b[layer])
        attn_out = _dilated_attention_mixer(q, k, v, q_norm_g[layer], k_norm_g[layer], rel_bias)
        x = x + jnp.concatenate([conv_out, attn_out], axis=-1) @ w_out[layer]
        hn = _rmsnorm(x, norm2_g[layer])
        x = x + _hier_moe(hn, w_router_group[layer], b_router_group[layer], w_router_expert[layer],
                          b_router_expert[layer], w_gate[layer], w_up[layer], w_down[layer])
    return x
```

```python
import functools
import math

import numpy as np
import jax
import jax.numpy as jnp
from jax import lax
from jax.experimental import pallas as pl
from jax.experimental.pallas import tpu as pltpu

F32 = jnp.float32
BF16 = jnp.bfloat16

D_MODEL = 2048
SEQ = 8192
N_HEADS = 16
HEAD_DIM = 64
ATTN_W = N_HEADS * HEAD_DIM
CONV_C = D_MODEL - ATTN_W
CONV_K = 31
IN_W = 2 * CONV_C + 3 * ATTN_W
PATTERNS = ((128, 1), (512, 4), (2048, 16))
QBLK = 128
NUM_BUCKETS = 32
MAX_DISTANCE = 2048
N_GROUPS = 4
E_PER_G = 8
N_EXPERTS = N_GROUPS * E_PER_G
D_FF = D_MODEL // 2
EPS = 1e-6
NEG_INF = -1e30

LANES = 128
VMEM_LIMIT = 56 * 1024 * 1024

TM_IN = 1024
TN_IN = 1024
TT_CONV = 512
HALO = 32
R_CONV = 64
R_LN = 32
TM_OUT = 512
TM_R = 512
R_ROWS = 8 + N_EXPERTS
TM_E = 768
SUB_E = 256
TF_E = 256
NF_E = D_FF // TF_E
N_ASSIGN = 2 * SEQ
MAX_ITEMS = -(-N_ASSIGN // TM_E) + N_EXPERTS
SORTED_LEN = N_ASSIGN + TM_E
TT_COMB = 512


def _cparams(sem, vmem=VMEM_LIMIT):
    return pltpu.CompilerParams(dimension_semantics=sem, vmem_limit_bytes=vmem)


def _inproj_kernel(x_ref, g_ref, w_ref, o_ref, xn_ref):
    @pl.when(pl.program_id(1) == 0)
    def _():
        x = x_ref[...]
        ms = jnp.mean(x * x, axis=-1, keepdims=True)
        xn_ref[...] = (x * lax.rsqrt(ms + EPS) * g_ref[...]).astype(BF16)

    o_ref[...] = jnp.dot(xn_ref[...], w_ref[...], preferred_element_type=F32)


def _inproj(x, g, w_bf16):
    return pl.pallas_call(
        _inproj_kernel,
        grid=(SEQ // TM_IN, IN_W // TN_IN),
        in_specs=[
            pl.BlockSpec((TM_IN, D_MODEL), lambda i, j: (i, 0)),
            pl.BlockSpec((1, D_MODEL), lambda i, j: (0, 0)),
            pl.BlockSpec((D_MODEL, TN_IN), lambda i, j: (0, j)),
        ],
        out_specs=pl.BlockSpec((TM_IN, TN_IN), lambda i, j: (i, j)),
        out_shape=jax.ShapeDtypeStruct((SEQ, IN_W), F32),
        scratch_shapes=[pltpu.VMEM((TM_IN, D_MODEL), BF16)],
        compiler_params=_cparams(("arbitrary", "arbitrary")),
        name="inproj",
    )(x, g, w_bf16)


def _conv_kernel(val_ref, gate_ref, hval_ref, hgate_ref, cw_ref, cb_ref, lg_ref, lb_ref,
                 o_ref, ubuf, ybuf):
    i = pl.program_id(0)
    u = val_ref[...] * jax.nn.sigmoid(gate_ref[...])
    hu = hval_ref[...] * jax.nn.sigmoid(hgate_ref[...])
    hu = jnp.where(i > 0, hu, 0.0)
    for c in range(CONV_C // LANES):
        cols = slice(c * LANES, (c + 1) * LANES)
        ubuf[c, pl.ds(0, HALO, stride=2), :] = hu[:, cols]
        ubuf[c, pl.ds(2 * HALO, TT_CONV, stride=2), :] = u[:, cols]

    for c in range(CONV_C // LANES):
        cols = slice(c * LANES, (c + 1) * LANES)

        def taps(r, carry, c=c, cols=cols):
            base = r * R_CONV
            acc = jnp.zeros((R_CONV, LANES), F32)
            for j in range(CONV_K):
                first = base + (HALO - CONV_K + 1) + j
                acc = acc + cw_ref[j:j + 1, cols] * ubuf[c, pl.ds(2 * first, R_CONV, stride=2), :]
            ybuf[pl.ds(pl.multiple_of(base, R_CONV), R_CONV), cols] = acc
            return carry

        lax.fori_loop(0, TT_CONV // R_CONV, taps, 0)

    def norm(r, carry):
        rows = pl.ds(pl.multiple_of(r * R_LN, R_LN), R_LN)
        acc = ybuf[rows, :] + cb_ref[...]
        mu = jnp.mean(acc, axis=-1, keepdims=True)
        xc = acc - mu
        var = jnp.mean(xc * xc, axis=-1, keepdims=True)
        y = xc * lax.rsqrt(var + EPS) * lg_ref[...] + lb_ref[...]
        o_ref[rows, :] = (y * jax.nn.sigmoid(y)).astype(BF16)
        return carry

    lax.fori_loop(0, TT_CONV // R_LN, norm, 0)


def _conv_mixer(proj, cw, cb, lg, lb):
    hb = TT_CONV // HALO
    return pl.pallas_call(
        _conv_kernel,
        grid=(SEQ // TT_CONV,),
        in_specs=[
            pl.BlockSpec((TT_CONV, CONV_C), lambda i: (i, 0)),
            pl.BlockSpec((TT_CONV, CONV_C), lambda i: (i, 1)),
            pl.BlockSpec((HALO, CONV_C), lambda i: (jnp.maximum(i * hb - 1, 0), 0)),
            pl.BlockSpec((HALO, CONV_C), lambda i: (jnp.maximum(i * hb - 1, 0), 1)),
            pl.BlockSpec((CONV_K, CONV_C), lambda i: (0, 0)),
            pl.BlockSpec((1, CONV_C), lambda i: (0, 0)),
            pl.BlockSpec((1, CONV_C), lambda i: (0, 0)),
            pl.BlockSpec((1, CONV_C), lambda i: (0, 0)),
        ],
        out_specs=pl.BlockSpec((TT_CONV, CONV_C), lambda i: (i, 0)),
        out_shape=jax.ShapeDtypeStruct((SEQ, CONV_C), BF16),
        scratch_shapes=[pltpu.VMEM((CONV_C // LANES, 2 * (HALO + TT_CONV), LANES), F32),
                        pltpu.VMEM((TT_CONV, CONV_C), F32)],
        compiler_params=_cparams(("arbitrary",)),
        name="conv_mixer",
    )(proj, proj, proj, proj, cw, cb, lg, lb)


def _t5_bucket_np(dist):
    max_exact = NUM_BUCKETS // 2
    nf = np.maximum(dist, 1).astype(np.float32)
    large = max_exact + (np.log(nf / np.float32(max_exact)) / np.float32(math.log(MAX_DISTANCE / max_exact))
                         * np.float32(NUM_BUCKETS - max_exact)).astype(np.int32)
    large = np.minimum(large, NUM_BUCKETS - 1)
    return np.where(dist < max_exact, dist, large)


def _attn_bias_tables(rel_bias):
    qi = np.arange(QBLK)[:, None]
    kj = np.arange(2 * QBLK)[None, :]
    rel = QBLK + qi - kj
    tables = []
    for window, dil in PATTERNS:
        span = window // dil
        assert span <= QBLK
        in_band = (rel >= 0) & (rel <= span)
        bucket = _t5_bucket_np(np.clip(rel, 0, span) * dil)
        bias = jnp.transpose(rel_bias.astype(F32)[bucket], (2, 0, 1))
        full = jnp.where(in_band[None], bias, NEG_INF)
        first = jnp.where((in_band & (kj >= QBLK))[None], bias, NEG_INF)
        both = jnp.stack([full, first])
        tables.append(both.reshape(2, N_HEADS // 2, 2 * QBLK, 2 * QBLK))
    return jnp.stack(tables)


def _attn_kernel(q_ref, k_ref, v_ref, qg_ref, kg_ref, bias_ref, o_ref,
                 qn_ref, kn_ref, acc_ref, m_ref, l_ref):
    lane = lax.broadcasted_iota(jnp.int32, (QBLK, LANES), 1)
    head_a = lane < HEAD_DIM
    ri = lax.broadcasted_iota(jnp.int32, (LANES, LANES), 0) // HEAD_DIM
    ci = lax.broadcasted_iota(jnp.int32, (LANES, LANES), 1) // HEAD_DIM
    seg = (ri == ci).astype(BF16)

    def head_rms(x, g):
        sq = x * x
        hi = sq.astype(BF16)
        lo = (sq - hi.astype(F32)).astype(BF16)
        ss = (jnp.dot(hi, seg, preferred_element_type=F32)
              + jnp.dot(lo, seg, preferred_element_type=F32))
        return x * lax.rsqrt(ss * (1.0 / HEAD_DIM) + EPS) * g

    NCH = 512

    def norm_body(c, carry):
        rows = pl.ds(pl.multiple_of(c * NCH, NCH), NCH)
        qn_ref[rows, :] = head_rms(q_ref[rows, :], qg_ref[...]) * (1.0 / math.sqrt(HEAD_DIM))
        kn_ref[rows, :] = head_rms(k_ref[rows, :], kg_ref[...])
        return carry

    lax.fori_loop(0, SEQ // NCH, norm_body, 0)

    for p, (window, dil) in enumerate(PATTERNS):
        nb = SEQ // (dil * QBLK)

        def unit(u, carry, p=p, dil=dil, nb=nb):
            r = u // nb
            n = u - r * nb
            cur = n * (QBLK * dil) + r
            prev = jnp.maximum(n - 1, 0) * (QBLK * dil) + r

            def rows(start):
                if dil == 1:
                    return pl.ds(start, QBLK)
                return pl.ds(start, QBLK, stride=dil)

            q = qn_ref[rows(cur), :]
            q2 = jnp.concatenate([jnp.where(head_a, q, 0.0), jnp.where(head_a, 0.0, q)],
                                 axis=0).astype(BF16)
            k2 = jnp.concatenate([kn_ref[rows(prev), :], kn_ref[rows(cur), :]], axis=0).astype(BF16)
            v2 = jnp.concatenate([v_ref[rows(prev), :], v_ref[rows(cur), :]], axis=0).astype(BF16)
            s = lax.dot_general(q2, k2, (((1,), (1,)), ((), ())), preferred_element_type=F32)
            first = jnp.where(n == 0, 1, 0)
            s = s + bias_ref[p, first]
            m = jnp.max(s, axis=-1, keepdims=True)
            e = jnp.exp(s - m)
            l = jnp.sum(e, axis=-1, keepdims=True)
            pv = jnp.dot(e.astype(BF16), v2, preferred_element_type=F32)
            o_new = jnp.where(head_a, pv[:QBLK], pv[QBLK:])
            m_new = jnp.where(head_a, m[:QBLK], m[QBLK:])
            l_new = jnp.where(head_a, l[:QBLK], l[QBLK:])
            if p == 0:
                acc_ref[rows(cur), :] = o_new
                m_ref[rows(cur), :] = m_new
                l_ref[rows(cur), :] = l_new
            else:
                m_old = m_ref[rows(cur), :]
                m_tot = jnp.maximum(m_old, m_new)
                a = jnp.exp(m_old - m_tot)
                b = jnp.exp(m_new - m_tot)
                acc_ref[rows(cur), :] = acc_ref[rows(cur), :] * a + o_new * b
                l_ref[rows(cur), :] = l_ref[rows(cur), :] * a + l_new * b
                m_ref[rows(cur), :] = m_tot
            return carry

        lax.fori_loop(0, SEQ // QBLK, unit, 0)

    def out_body(c, carry):
        rows = pl.ds(pl.multiple_of(c * NCH, NCH), NCH)
        o_ref[rows, :] = (acc_ref[rows, :] / l_ref[rows, :]).astype(BF16)
        return carry

    lax.fori_loop(0, SEQ // NCH, out_body, 0)


def _attention(proj, qg2, kg2, bias_tab):
    qoff = 2 * CONV_C // LANES
    koff = qoff + ATTN_W // LANES
    voff = koff + ATTN_W // LANES
    return pl.pallas_call(
        _attn_kernel,
        grid=(N_HEADS // 2,),
        in_specs=[
            pl.BlockSpec((SEQ, LANES), lambda h: (0, qoff + h)),
            pl.BlockSpec((SEQ, LANES), lambda h: (0, koff + h)),
            pl.BlockSpec((SEQ, LANES), lambda h: (0, voff + h)),
            pl.BlockSpec((1, LANES), lambda h: (0, 0)),
            pl.BlockSpec((1, LANES), lambda h: (0, 0)),
            pl.BlockSpec((len(PATTERNS), 2, None, 2 * QBLK, 2 * QBLK), lambda h: (0, 0, h, 0, 0)),
        ],
        out_specs=pl.BlockSpec((SEQ, LANES), lambda h: (0, h)),
        out_shape=jax.ShapeDtypeStruct((SEQ, ATTN_W), BF16),
        scratch_shapes=[pltpu.VMEM((SEQ, LANES), F32) for _ in range(5)],
        compiler_params=_cparams(("arbitrary",)),
        name="dilated_attn",
    )(proj, proj, proj, qg2, kg2, bias_tab)


def _outproj_kernel(x_ref, c_ref, a_ref, wc_ref, wa_ref, o_ref):
    o_ref[...] = (x_ref[...]
                  + jnp.dot(c_ref[...], wc_ref[...], preferred_element_type=F32)
                  + jnp.dot(a_ref[...], wa_ref[...], preferred_element_type=F32))


def _outproj(x, conv_out, attn_out, w_out_bf16):
    return pl.pallas_call(
        _outproj_kernel,
        grid=(SEQ // TM_OUT,),
        in_specs=[
            pl.BlockSpec((TM_OUT, D_MODEL), lambda i: (i, 0)),
            pl.BlockSpec((TM_OUT, CONV_C), lambda i: (i, 0)),
            pl.BlockSpec((TM_OUT, ATTN_W), lambda i: (i, 0)),
            pl.BlockSpec((CONV_C, D_MODEL), lambda i: (0, 0)),
            pl.BlockSpec((ATTN_W, D_MODEL), lambda i: (1, 0)),
        ],
        out_specs=pl.BlockSpec((TM_OUT, D_MODEL), lambda i: (i, 0)),
        out_shape=jax.ShapeDtypeStruct((SEQ, D_MODEL), F32),
        compiler_params=_cparams(("arbitrary",)),
        name="outproj",
    )(x, conv_out, attn_out, w_out_bf16, w_out_bf16)


def _split3(a):
    a1 = a.astype(BF16)
    r1 = a - a1.astype(F32)
    a2 = r1.astype(BF16)
    a3 = (r1 - a2.astype(F32)).astype(BF16)
    return a1, a2, a3


def _router_kernel(x_ref, g_ref, wr_ref, br_ref, eid_ref, wts_ref, rank_ref, cnt_ref):
    i = pl.program_id(0)

    @pl.when(i == 0)
    def _():
        cnt_ref[...] = jnp.zeros_like(cnt_ref)

    x = x_ref[...]
    ms = jnp.mean(x * x, axis=-1, keepdims=True)
    hn = x * lax.rsqrt(ms + EPS) * g_ref[...]
    h1, h2, h3 = _split3(hn)
    w1, w2, w3 = _split3(wr_ref[...])
    dn = (((1,), (1,)), ((), ()))
    lt = None
    for wa, ha in ((w1, h1), (w1, h2), (w2, h1), (w2, h2), (w1, h3), (w3, h1)):
        t = lax.dot_general(wa, ha, dn, preferred_element_type=F32)
        lt = t if lt is None else lt + t
    lt = lt + br_ref[:, 0:1]

    row8 = lax.broadcasted_iota(jnp.int32, (8, TM_R), 0)
    gl = jnp.where(row8 < N_GROUPS, lt[0:8], -jnp.inf)
    gmax = jnp.max(gl, axis=0, keepdims=True)
    gidx = jnp.min(jnp.where(gl == gmax, row8, 8), axis=0, keepdims=True)
    gw = 1.0 / jnp.sum(jnp.exp(gl - gmax), axis=0, keepdims=True)

    esel = lt[8:16]
    for g in range(1, N_GROUPS):
        esel = jnp.where(gidx == g, lt[8 + 8 * g:16 + 8 * g], esel)
    v1 = jnp.max(esel, axis=0, keepdims=True)
    i1 = jnp.min(jnp.where(esel == v1, row8, 8), axis=0, keepdims=True)
    rest = jnp.where(row8 == i1, -jnp.inf, esel)
    v2 = jnp.max(rest, axis=0, keepdims=True)
    i2 = jnp.min(jnp.where(rest == v2, row8, 8), axis=0, keepdims=True)
    e21 = jnp.exp(v2 - v1)
    den = 1.0 + e21
    e1 = gidx * E_PER_G + i1
    e2 = gidx * E_PER_G + i2
    eid_ref[0:1, :] = e1
    eid_ref[1:2, :] = e2
    wts_ref[0:1, :] = gw * (1.0 / den)
    wts_ref[1:2, :] = gw * (e21 / den)

    erow = lax.broadcasted_iota(jnp.int32, (N_EXPERTS, TM_R), 0)
    oh1 = erow == e1
    oh2 = erow == e2
    member = jnp.where(oh1 | oh2, 1.0, 0.0)
    ti = lax.broadcasted_iota(jnp.int32, (TM_R, TM_R), 0)
    tj = lax.broadcasted_iota(jnp.int32, (TM_R, TM_R), 1)
    upper = jnp.where(ti < tj, 1.0, 0.0).astype(BF16)
    before = jnp.dot(member.astype(BF16), upper, preferred_element_type=F32)
    pos = before + cnt_ref[:, 0:1]
    rank_ref[0:1, :] = jnp.sum(jnp.where(oh1, pos, 0.0), axis=0, keepdims=True).astype(jnp.int32)
    rank_ref[1:2, :] = jnp.sum(jnp.where(oh2, pos, 0.0), axis=0, keepdims=True).astype(jnp.int32)
    cnt_ref[...] = cnt_ref[...] + jnp.sum(member, axis=1, keepdims=True)


def _router(x1, g2, wr_t, br):
    return pl.pallas_call(
        _router_kernel,
        grid=(SEQ // TM_R,),
        in_specs=[
            pl.BlockSpec((TM_R, D_MODEL), lambda i: (i, 0)),
            pl.BlockSpec((1, D_MODEL), lambda i: (0, 0)),
            pl.BlockSpec((R_ROWS, D_MODEL), lambda i: (0, 0)),
            pl.BlockSpec((R_ROWS, LANES), lambda i: (0, 0)),
        ],
        out_specs=[
            pl.BlockSpec((2, TM_R), lambda i: (0, i)),
            pl.BlockSpec((2, TM_R), lambda i: (0, i)),
            pl.BlockSpec((2, TM_R), lambda i: (0, i)),
            pl.BlockSpec((N_EXPERTS, LANES), lambda i: (0, 0)),
        ],
        out_shape=[
            jax.ShapeDtypeStruct((2, SEQ), jnp.int32),
            jax.ShapeDtypeStruct((2, SEQ), F32),
            jax.ShapeDtypeStruct((2, SEQ), jnp.int32),
            jax.ShapeDtypeStruct((N_EXPERTS, LANES), F32),
        ],
        compiler_params=_cparams(("arbitrary",)),
        name="router",
    )(x1, g2, wr_t, br)


def _moe_kernel(ie_ref, row0_ref, rows_ref, nitems_ref, sorted_ref,
                x_hbm, g_ref, wg_ref, wu_ref, wd_ref, dest_hbm,
                xg_ref, xb_ref, y_ref, gsem, ssem):
    it = pl.program_id(0)
    f = pl.program_id(1)
    nrows = rows_ref[it]
    row0 = row0_ref[it]
    nsub = (nrows + SUB_E - 1) // SUB_E

    def gather_copy(j):
        tok = sorted_ref[row0 + j] & (SEQ - 1)
        return pltpu.make_async_copy(x_hbm.at[pl.ds(tok, 1), :], xg_ref.at[pl.ds(j, 1), :], gsem)

    def scatter_copy(j):
        dst = sorted_ref[row0 + j]
        return pltpu.make_async_copy(y_ref.at[pl.ds(j, 1), :], dest_hbm.at[pl.ds(dst, 1), :], ssem)

    @pl.when((f == 0) & (nrows > 0))
    def _():
        def start(j, c):
            gather_copy(j).start()
            return c
        lax.fori_loop(0, nsub * SUB_E, start, 0)

        def wait(j, c):
            gather_copy(j).wait()
            return c
        lax.fori_loop(0, nsub * SUB_E, wait, 0)

        def norm(s, c):
            rows = pl.ds(pl.multiple_of(s * SUB_E, SUB_E), SUB_E)
            x = xg_ref[rows, :]
            ms = jnp.mean(x * x, axis=-1, keepdims=True)
            xb_ref[rows, :] = (x * lax.rsqrt(ms + EPS) * g_ref[...]).astype(BF16)
            y_ref[rows, :] = jnp.zeros((SUB_E, D_MODEL), F32)
            return c
        lax.fori_loop(0, nsub, norm, 0)

    @pl.when(nrows > 0)
    def _():
        wg = wg_ref[...].astype(BF16)
        wu = wu_ref[...].astype(BF16)
        wd = wd_ref[...].astype(BF16)

        def sub(s, c):
            rows = pl.ds(pl.multiple_of(s * SUB_E, SUB_E), SUB_E)
            xb = xb_ref[rows, :]
            hg = jnp.dot(xb, wg, preferred_element_type=F32)
            hu = jnp.dot(xb, wu, preferred_element_type=F32)
            h = (hg * jax.nn.sigmoid(hg) * hu).astype(BF16)
            y_ref[rows, :] = y_ref[rows, :] + jnp.dot(h, wd, preferred_element_type=F32)
            return c
        lax.fori_loop(0, nsub, sub, 0)

    @pl.when((f == NF_E - 1) & (nrows > 0))
    def _():
        def start(j, c):
            scatter_copy(j).start()
            return c
        lax.fori_loop(0, nrows, start, 0)

        def wait(j, c):
            scatter_copy(j).wait()
            return c
        lax.fori_loop(0, nrows, wait, 0)


def _moe(ie, row0, rows, nitems, sorted_i, x1, g2, wg, wu, wd):
    def wmap_cols(i, f, ie_ref, row0_ref, rows_ref, n_ref, s_ref):
        return (ie_ref[i], 0, jnp.where(i < n_ref[0], f, NF_E - 1))

    def wmap_rows(i, f, ie_ref, row0_ref, rows_ref, n_ref, s_ref):
        return (ie_ref[i], jnp.where(i < n_ref[0], f, NF_E - 1), 0)

    grid_spec = pltpu.PrefetchScalarGridSpec(
        num_scalar_prefetch=5,
        grid=(MAX_ITEMS, NF_E),
        in_specs=[
            pl.BlockSpec(memory_space=pl.ANY),
            pl.BlockSpec((1, D_MODEL), lambda i, f, *_: (0, 0)),
            pl.BlockSpec((None, D_MODEL, TF_E), wmap_cols),
            pl.BlockSpec((None, D_MODEL, TF_E), wmap_cols),
            pl.BlockSpec((None, TF_E, D_MODEL), wmap_rows),
        ],
        out_specs=pl.BlockSpec(memory_space=pl.ANY),
        scratch_shapes=[
            pltpu.VMEM((TM_E, D_MODEL), F32),
            pltpu.VMEM((TM_E, D_MODEL), BF16),
            pltpu.VMEM((TM_E, D_MODEL), F32),
            pltpu.SemaphoreType.DMA,
            pltpu.SemaphoreType.DMA,
        ],
    )
    return pl.pallas_call(
        _moe_kernel,
        grid_spec=grid_spec,
        out_shape=jax.ShapeDtypeStruct((N_ASSIGN, D_MODEL), F32),
        compiler_params=_cparams(("arbitrary", "arbitrary")),
        name="moe_experts",
    )(ie, row0, rows, nitems, sorted_i, x1, g2, wg, wu, wd)


def _combine_kernel(x_ref, w_ref, d0_ref, d1_ref, o_ref):
    w = w_ref[...]
    o_ref[...] = x_ref[...] + w[:, 0:1] * d0_ref[...] + w[:, 1:2] * d1_ref[...]


def _combine(x1, wts_t, dest):
    dest3 = dest.reshape(2, SEQ, D_MODEL)
    return pl.pallas_call(
        _combine_kernel,
        grid=(SEQ // TT_COMB,),
        in_specs=[
            pl.BlockSpec((TT_COMB, D_MODEL), lambda i: (i, 0)),
            pl.BlockSpec((TT_COMB, 2), lambda i: (i, 0)),
            pl.BlockSpec((None, TT_COMB, D_MODEL), lambda i: (0, i, 0)),
            pl.BlockSpec((None, TT_COMB, D_MODEL), lambda i: (1, i, 0)),
        ],
        out_specs=pl.BlockSpec((TT_COMB, D_MODEL), lambda i: (i, 0)),
        out_shape=jax.ShapeDtypeStruct((SEQ, D_MODEL), F32),
        compiler_params=_cparams(("arbitrary",)),
        name="moe_combine",
    )(x1, wts_t, dest3, dest3)


def _work_items(eid, rank, counts_f):
    counts = counts_f[:, 0].astype(jnp.int32)
    cum = jnp.cumsum(counts)
    base = cum - counts
    tiles = (counts + TM_E - 1) // TM_E
    tcum = jnp.cumsum(tiles)
    tstart = tcum - tiles
    nitems = tcum[-1]
    ids = jnp.arange(MAX_ITEMS, dtype=jnp.int32)
    ie = jnp.clip(jnp.searchsorted(tcum, ids, side="right"), 0, N_EXPERTS - 1).astype(jnp.int32)
    live = ids < nitems
    ie = jnp.where(live, ie, ie[jnp.maximum(nitems - 1, 0)])
    jt = ids - tstart[ie]
    row0 = jnp.where(live, base[ie] + jt * TM_E, 0)
    rows = jnp.where(live, jnp.clip(counts[ie] - jt * TM_E, 0, TM_E), 0)
    slot = base[eid] + rank
    sorted_i = jnp.zeros((SORTED_LEN,), jnp.int32).at[slot.reshape(-1)].set(
        jnp.arange(N_ASSIGN, dtype=jnp.int32))
    return ie, row0.astype(jnp.int32), rows.astype(jnp.int32), nitems.reshape(1).astype(jnp.int32), sorted_i


def kernel(x, norm1_g, w_in, q_norm_g, k_norm_g, conv_w, conv_b, conv_ln_g, conv_ln_b, rel_bias,
           w_out, norm2_g, w_router_group, b_router_group, w_router_expert, b_router_expert,
           w_gate, w_up, w_down):
    assert x.shape == (1, SEQ, D_MODEL) and w_in.shape[0] == 1
    xs = x[0]
    bias_tab = _attn_bias_tables(rel_bias)
    qg2 = jnp.tile(q_norm_g[0], 2)[None]
    kg2 = jnp.tile(k_norm_g[0], 2)[None]

    proj = _inproj(xs, norm1_g[0][None], w_in[0].astype(BF16))
    conv_out = _conv_mixer(proj, conv_w[0], conv_b[0][None], conv_ln_g[0][None], conv_ln_b[0][None])
    attn_out = _attention(proj, qg2, kg2, bias_tab)
    x1 = _outproj(xs, conv_out, attn_out, w_out[0].astype(BF16))

    wr_t = jnp.concatenate([
        w_router_group[0].T, jnp.zeros((8 - N_GROUPS, D_MODEL), F32),
        jnp.transpose(w_router_expert[0], (0, 2, 1)).reshape(N_EXPERTS, D_MODEL)], axis=0)
    br = jnp.concatenate([b_router_group[0], jnp.zeros((8 - N_GROUPS,), F32),
                          b_router_expert[0].reshape(-1)])
    br = jnp.broadcast_to(br[:, None], (R_ROWS, LANES))
    eid, wts, rank, counts_f = _router(x1, norm2_g[0][None], wr_t, br)

    ie, row0, rows, nitems, sorted_i = _work_items(eid, rank, counts_f)
    dest = _moe(ie, row0, rows, nitems, sorted_i, x1, norm2_g[0][None],
                w_gate[0].reshape(N_EXPERTS, D_MODEL, D_FF),
                w_up[0].reshape(N_EXPERTS, D_MODEL, D_FF),
                w_down[0].reshape(N_EXPERTS, D_FF, D_MODEL))
    out = _combine(x1, wts.T, dest)
    return out[None]
```

```python
import functools
import math

import numpy as np
import jax
import jax.numpy as jnp
from jax import lax
from jax.experimental import pallas as pl
from jax.experimental.pallas import tpu as pltpu

F32 = jnp.float32
BF16 = jnp.bfloat16

D_MODEL = 2048
SEQ = 8192
N_HEADS = 16
HEAD_DIM = 64
ATTN_W = N_HEADS * HEAD_DIM
CONV_C = D_MODEL - ATTN_W
CONV_K = 31
IN_W = 2 * CONV_C + 3 * ATTN_W
PATTERNS = ((128, 1), (512, 4), (2048, 16))
QBLK = 128
NUM_BUCKETS = 32
MAX_DISTANCE = 2048
N_GROUPS = 4
E_PER_G = 8
N_EXPERTS = N_GROUPS * E_PER_G
D_FF = D_MODEL // 2
EPS = 1e-6
NEG_INF = -1e30
LOG2E = math.log2(math.e)

LANES = 128
VMEM_LIMIT = 56 * 1024 * 1024

TM_IN = 1024
TN_IN = 1024
TT_CONV = 512
HALO = 32
R_CONV = 64
R_LN = 32
ATTN_UNROLL = 4
TM_OUT = 512
TM_R = 512
R_ROWS = 8 + N_EXPERTS
TM_E = 768
SUB_E = 256
TF_E = 256
NF_E = D_FF // TF_E
N_ASSIGN = 2 * SEQ
MAX_ITEMS = -(-N_ASSIGN // TM_E) + N_EXPERTS
SORTED_LEN = N_ASSIGN + TM_E
TT_COMB = 512


def _cparams(sem, vmem=VMEM_LIMIT):
    return pltpu.CompilerParams(dimension_semantics=sem, vmem_limit_bytes=vmem)


def _inproj_kernel(x_ref, g_ref, w_ref, o_ref, xn_ref):
    @pl.when(pl.program_id(1) == 0)
    def _():
        x = x_ref[...]
        ms = jnp.mean(x * x, axis=-1, keepdims=True)
        xn_ref[...] = (x * lax.rsqrt(ms + EPS) * g_ref[...]).astype(BF16)

    o_ref[...] = jnp.dot(xn_ref[...], w_ref[...], preferred_element_type=F32)


def _inproj(x, g, w_bf16):
    return pl.pallas_call(
        _inproj_kernel,
        grid=(SEQ // TM_IN, IN_W // TN_IN),
        in_specs=[
            pl.BlockSpec((TM_IN, D_MODEL), lambda i, j: (i, 0)),
            pl.BlockSpec((1, D_MODEL), lambda i, j: (0, 0)),
            pl.BlockSpec((D_MODEL, TN_IN), lambda i, j: (0, j)),
        ],
        out_specs=pl.BlockSpec((TM_IN, TN_IN), lambda i, j: (i, j)),
        out_shape=jax.ShapeDtypeStruct((SEQ, IN_W), F32),
        scratch_shapes=[pltpu.VMEM((TM_IN, D_MODEL), BF16)],
        compiler_params=_cparams(("arbitrary", "arbitrary")),
        name="inproj",
    )(x, g, w_bf16)


def _conv_kernel(val_ref, gate_ref, hval_ref, hgate_ref, cw_ref, cb_ref, lg_ref, lb_ref,
                 o_ref, ubuf, ybuf):
    i = pl.program_id(0)
    u = val_ref[...] * jax.nn.sigmoid(gate_ref[...])
    hu = hval_ref[...] * jax.nn.sigmoid(hgate_ref[...])
    hu = jnp.where(i > 0, hu, 0.0)
    for c in range(CONV_C // LANES):
        cols = slice(c * LANES, (c + 1) * LANES)
        ubuf[c, pl.ds(0, HALO, stride=2), :] = hu[:, cols]
        ubuf[c, pl.ds(2 * HALO, TT_CONV, stride=2), :] = u[:, cols]

    for c in range(CONV_C // LANES):
        cols = slice(c * LANES, (c + 1) * LANES)

        def taps(r, carry, c=c, cols=cols):
            base = r * R_CONV
            acc = jnp.zeros((R_CONV, LANES), F32)
            for j in range(CONV_K):
                first = base + (HALO - CONV_K + 1) + j
                acc = acc + cw_ref[j:j + 1, cols] * ubuf[c, pl.ds(2 * first, R_CONV, stride=2), :]
            ybuf[pl.ds(pl.multiple_of(base, R_CONV), R_CONV), cols] = acc
            return carry

        lax.fori_loop(0, TT_CONV // R_CONV, taps, 0)

    def norm(r, carry):
        rows = pl.ds(pl.multiple_of(r * R_LN, R_LN), R_LN)
        acc = ybuf[rows, :] + cb_ref[...]
        mu = jnp.mean(acc, axis=-1, keepdims=True)
        xc = acc - mu
        var = jnp.mean(xc * xc, axis=-1, keepdims=True)
        y = xc * lax.rsqrt(var + EPS) * lg_ref[...] + lb_ref[...]
        o_ref[rows, :] = (y * jax.nn.sigmoid(y)).astype(BF16)
        return carry

    lax.fori_loop(0, TT_CONV // R_LN, norm, 0)


def _conv_mixer(proj, cw, cb, lg, lb):
    hb = TT_CONV // HALO
    return pl.pallas_call(
        _conv_kernel,
        grid=(SEQ // TT_CONV,),
        in_specs=[
            pl.BlockSpec((TT_CONV, CONV_C), lambda i: (i, 0)),
            pl.BlockSpec((TT_CONV, CONV_C), lambda i: (i, 1)),
            pl.BlockSpec((HALO, CONV_C), lambda i: (jnp.maximum(i * hb - 1, 0), 0)),
            pl.BlockSpec((HALO, CONV_C), lambda i: (jnp.maximum(i * hb - 1, 0), 1)),
            pl.BlockSpec((CONV_K, CONV_C), lambda i: (0, 0)),
            pl.BlockSpec((1, CONV_C), lambda i: (0, 0)),
            pl.BlockSpec((1, CONV_C), lambda i: (0, 0)),
            pl.BlockSpec((1, CONV_C), lambda i: (0, 0)),
        ],
        out_specs=pl.BlockSpec((TT_CONV, CONV_C), lambda i: (i, 0)),
        out_shape=jax.ShapeDtypeStruct((SEQ, CONV_C), BF16),
        scratch_shapes=[pltpu.VMEM((CONV_C // LANES, 2 * (HALO + TT_CONV), LANES), F32),
                        pltpu.VMEM((TT_CONV, CONV_C), F32)],
        compiler_params=_cparams(("arbitrary",)),
        name="conv_mixer",
    )(proj, proj, proj, proj, cw, cb, lg, lb)


def _t5_bucket_np(dist):
    max_exact = NUM_BUCKETS // 2
    nf = np.maximum(dist, 1).astype(np.float32)
    large = max_exact + (np.log(nf / np.float32(max_exact)) / np.float32(math.log(MAX_DISTANCE / max_exact))
                         * np.float32(NUM_BUCKETS - max_exact)).astype(np.int32)
    large = np.minimum(large, NUM_BUCKETS - 1)
    return np.where(dist < max_exact, dist, large)


def _attn_bias_tables(rel_bias):
    kj = np.arange(2 * QBLK)[None, None, :]
    period = 3 * QBLK
    tables = []
    for window, dil in PATTERNS:
        span = window // dil
        assert span <= QBLK
        bucket = _t5_bucket_np(np.arange(span + 1) * dil)
        onehot = np.eye(NUM_BUCKETS, dtype=np.float32)[bucket]
        vec = jnp.einsum("rb,bh->hr", onehot, rel_bias.astype(F32),
                         precision=lax.Precision.HIGHEST)
        diag = jnp.full((N_HEADS, period), NEG_INF, F32)
        diag = diag.at[:, 2 * QBLK - 1 - span:2 * QBLK].set(vec[:, ::-1])
        skew = jnp.tile(diag, (1, QBLK))[:, :QBLK * (period - 1)].reshape(N_HEADS, QBLK, period - 1)
        full = skew[:, :, QBLK - 1:3 * QBLK - 1]
        first = jnp.where(kj >= QBLK, full, NEG_INF)
        both = jnp.stack([full, first])
        tables.append(both.reshape(2, N_HEADS // 2, 2 * QBLK, 2 * QBLK))
    return jnp.stack(tables) * LOG2E


def _attn_kernel(q_ref, k_ref, v_ref, qg_ref, kg_ref, bias_ref, o_ref,
                 qn_ref, kn_ref, acc_ref, m_ref, l_ref):
    lane = lax.broadcasted_iota(jnp.int32, (QBLK, LANES), 1)
    head_a = lane < HEAD_DIM
    ri = lax.broadcasted_iota(jnp.int32, (LANES, LANES), 0) // HEAD_DIM
    ci = lax.broadcasted_iota(jnp.int32, (LANES, LANES), 1) // HEAD_DIM
    seg = (ri == ci).astype(BF16)

    def head_rms(x, g):
        sq = x * x
        hi = sq.astype(BF16)
        lo = (sq - hi.astype(F32)).astype(BF16)
        ss = (jnp.dot(hi, seg, preferred_element_type=F32)
              + jnp.dot(lo, seg, preferred_element_type=F32))
        return x * lax.rsqrt(ss * (1.0 / HEAD_DIM) + EPS) * g

    NCH = 512

    def norm_body(c, carry):
        rows = pl.ds(pl.multiple_of(c * NCH, NCH), NCH)
        qn_ref[rows, :] = head_rms(q_ref[rows, :], qg_ref[...]) * (LOG2E / math.sqrt(HEAD_DIM))
        kn_ref[rows, :] = head_rms(k_ref[rows, :], kg_ref[...])
        return carry

    lax.fori_loop(0, SEQ // NCH, norm_body, 0)

    for p, (window, dil) in enumerate(PATTERNS):
        nb = SEQ // (dil * QBLK)

        def unit(u, carry, p=p, dil=dil, nb=nb):
            r = u // nb
            n = u - r * nb
            cur = n * (QBLK * dil) + r
            prev = jnp.maximum(n - 1, 0) * (QBLK * dil) + r

            def rows(start):
                if dil == 1:
                    return pl.ds(start, QBLK)
                return pl.ds(start, QBLK, stride=dil)

            q = qn_ref[rows(cur), :]
            q2 = jnp.concatenate([jnp.where(head_a, q, 0.0), jnp.where(head_a, 0.0, q)],
                                 axis=0).astype(BF16)
            k2 = jnp.concatenate([kn_ref[rows(prev), :], kn_ref[rows(cur), :]], axis=0).astype(BF16)
            v2 = jnp.concatenate([v_ref[rows(prev), :], v_ref[rows(cur), :]], axis=0).astype(BF16)
            s = lax.dot_general(q2, k2, (((1,), (1,)), ((), ())), preferred_element_type=F32)
            first = jnp.where(n == 0, 1, 0)
            s = s + bias_ref[p, first]
            m = jnp.max(s, axis=-1, keepdims=True)
            e = jnp.exp2(s - m)
            l = jnp.sum(e, axis=-1, keepdims=True)
            pv = jnp.dot(e.astype(BF16), v2, preferred_element_type=F32)
            o_new = jnp.where(head_a, pv[:QBLK], pv[QBLK:])
            m_new = jnp.where(head_a, m[:QBLK], m[QBLK:])
            l_new = jnp.where(head_a, l[:QBLK], l[QBLK:])
            if p == 0:
                acc_ref[rows(cur), :] = o_new
                m_ref[rows(cur), :] = m_new
                l_ref[rows(cur), :] = l_new
            else:
                m_old = m_ref[rows(cur), :]
                m_tot = jnp.maximum(m_old, m_new)
                a = jnp.exp2(m_old - m_tot)
                b = jnp.exp2(m_new - m_tot)
                acc_ref[rows(cur), :] = acc_ref[rows(cur), :] * a + o_new * b
                l_ref[rows(cur), :] = l_ref[rows(cur), :] * a + l_new * b
                m_ref[rows(cur), :] = m_tot
            return carry

        lax.fori_loop(0, SEQ // QBLK, unit, 0, unroll=ATTN_UNROLL)

    def out_body(c, carry):
        rows = pl.ds(pl.multiple_of(c * NCH, NCH), NCH)
        o_ref[rows, :] = (acc_ref[rows, :] / l_ref[rows, :]).astype(BF16)
        return carry

    lax.fori_loop(0, SEQ // NCH, out_body, 0)


def _attention(proj, qg2, kg2, bias_tab):
    qoff = 2 * CONV_C // LANES
    koff = qoff + ATTN_W // LANES
    voff = koff + ATTN_W // LANES
    return pl.pallas_call(
        _attn_kernel,
        grid=(N_HEADS // 2,),
        in_specs=[
            pl.BlockSpec((SEQ, LANES), lambda h: (0, qoff + h)),
            pl.BlockSpec((SEQ, LANES), lambda h: (0, koff + h)),
            pl.BlockSpec((SEQ, LANES), lambda h: (0, voff + h)),
            pl.BlockSpec((1, LANES), lambda h: (0, 0)),
            pl.BlockSpec((1, LANES), lambda h: (0, 0)),
            pl.BlockSpec((len(PATTERNS), 2, None, 2 * QBLK, 2 * QBLK), lambda h: (0, 0, h, 0, 0)),
        ],
        out_specs=pl.BlockSpec((SEQ, LANES), lambda h: (0, h)),
        out_shape=jax.ShapeDtypeStruct((SEQ, ATTN_W), BF16),
        scratch_shapes=[pltpu.VMEM((SEQ, LANES), F32) for _ in range(5)],
        compiler_params=_cparams(("arbitrary",)),
        name="dilated_attn",
    )(proj, proj, proj, qg2, kg2, bias_tab)


def _outproj_kernel(x_ref, c_ref, a_ref, wc_ref, wa_ref, o_ref):
    o_ref[...] = (x_ref[...]
                  + jnp.dot(c_ref[...], wc_ref[...], preferred_element_type=F32)
                  + jnp.dot(a_ref[...], wa_ref[...], preferred_element_type=F32))


def _outproj(x, conv_out, attn_out, w_out_bf16):
    return pl.pallas_call(
        _outproj_kernel,
        grid=(SEQ // TM_OUT,),
        in_specs=[
            pl.BlockSpec((TM_OUT, D_MODEL), lambda i: (i, 0)),
            pl.BlockSpec((TM_OUT, CONV_C), lambda i: (i, 0)),
            pl.BlockSpec((TM_OUT, ATTN_W), lambda i: (i, 0)),
            pl.BlockSpec((CONV_C, D_MODEL), lambda i: (0, 0)),
            pl.BlockSpec((ATTN_W, D_MODEL), lambda i: (1, 0)),
        ],
        out_specs=pl.BlockSpec((TM_OUT, D_MODEL), lambda i: (i, 0)),
        out_shape=jax.ShapeDtypeStruct((SEQ, D_MODEL), F32),
        compiler_params=_cparams(("arbitrary",)),
        name="outproj",
    )(x, conv_out, attn_out, w_out_bf16, w_out_bf16)


def _split3(a):
    a1 = a.astype(BF16)
    r1 = a - a1.astype(F32)
    a2 = r1.astype(BF16)
    a3 = (r1 - a2.astype(F32)).astype(BF16)
    return a1, a2, a3


def _router_kernel(x_ref, g_ref, wr_ref, br_ref, eid_ref, wts_ref, rank_ref, cnt_ref):
    i = pl.program_id(0)

    @pl.when(i == 0)
    def _():
        cnt_ref[...] = jnp.zeros_like(cnt_ref)

    x = x_ref[...]
    ms = jnp.mean(x * x, axis=-1, keepdims=True)
    hn = x * lax.rsqrt(ms + EPS) * g_ref[...]
    h1, h2, h3 = _split3(hn)
    w1, w2, w3 = _split3(wr_ref[...])
    dn = (((1,), (1,)), ((), ()))
    lt = None
    for wa, ha in ((w1, h1), (w1, h2), (w2, h1), (w2, h2), (w1, h3), (w3, h1)):
        t = lax.dot_general(wa, ha, dn, preferred_element_type=F32)
        lt = t if lt is None else lt + t
    lt = lt + br_ref[:, 0:1]

    row8 = lax.broadcasted_iota(jnp.int32, (8, TM_R), 0)
    gl = jnp.where(row8 < N_GROUPS, lt[0:8], -jnp.inf)
    gmax = jnp.max(gl, axis=0, keepdims=True)
    gidx = jnp.min(jnp.where(gl == gmax, row8, 8), axis=0, keepdims=True)
    gw = 1.0 / jnp.sum(jnp.exp(gl - gmax), axis=0, keepdims=True)

    esel = lt[8:16]
    for g in range(1, N_GROUPS):
        esel = jnp.where(gidx == g, lt[8 + 8 * g:16 + 8 * g], esel)
    v1 = jnp.max(esel, axis=0, keepdims=True)
    i1 = jnp.min(jnp.where(esel == v1, row8, 8), axis=0, keepdims=True)
    rest = jnp.where(row8 == i1, -jnp.inf, esel)
    v2 = jnp.max(rest, axis=0, keepdims=True)
    i2 = jnp.min(jnp.where(rest == v2, row8, 8), axis=0, keepdims=True)
    e21 = jnp.exp(v2 - v1)
    den = 1.0 + e21
    e1 = gidx * E_PER_G + i1
    e2 = gidx * E_PER_G + i2
    eid_ref[0:1, :] = e1
    eid_ref[1:2, :] = e2
    wts_ref[0:1, :] = gw * (1.0 / den)
    wts_ref[1:2, :] = gw * (e21 / den)

    erow = lax.broadcasted_iota(jnp.int32, (N_EXPERTS, TM_R), 0)
    oh1 = erow == e1
    oh2 = erow == e2
    member = jnp.where(oh1 | oh2, 1.0, 0.0)
    ti = lax.broadcasted_iota(jnp.int32, (TM_R, TM_R), 0)
    tj = lax.broadcasted_iota(jnp.int32, (TM_R, TM_R), 1)
    upper = jnp.where(ti < tj, 1.0, 0.0).astype(BF16)
    before = jnp.dot(member.astype(BF16), upper, preferred_element_type=F32)
    pos = before + cnt_ref[:, 0:1]
    rank_ref[0:1, :] = jnp.sum(jnp.where(oh1, pos, 0.0), axis=0, keepdims=True).astype(jnp.int32)
    rank_ref[1:2, :] = jnp.sum(jnp.where(oh2, pos, 0.0), axis=0, keepdims=True).astype(jnp.int32)
    cnt_ref[...] = cnt_ref[...] + jnp.sum(member, axis=1, keepdims=True)


def _router(x1, g2, wr_t, br):
    return pl.pallas_call(
        _router_kernel,
        grid=(SEQ // TM_R,),
        in_specs=[
            pl.BlockSpec((TM_R, D_MODEL), lambda i: (i, 0)),
            pl.BlockSpec((1, D_MODEL), lambda i: (0, 0)),
            pl.BlockSpec((R_ROWS, D_MODEL), lambda i: (0, 0)),
            pl.BlockSpec((R_ROWS, LANES), lambda i: (0, 0)),
        ],
        out_specs=[
            pl.BlockSpec((2, TM_R), lambda i: (0, i)),
            pl.BlockSpec((2, TM_R), lambda i: (0, i)),
            pl.BlockSpec((2, TM_R), lambda i: (0, i)),
            pl.BlockSpec((N_EXPERTS, LANES), lambda i: (0, 0)),
        ],
        out_shape=[
            jax.ShapeDtypeStruct((2, SEQ), jnp.int32),
            jax.ShapeDtypeStruct((2, SEQ), F32),
            jax.ShapeDtypeStruct((2, SEQ), jnp.int32),
            jax.ShapeDtypeStruct((N_EXPERTS, LANES), F32),
        ],
        compiler_params=_cparams(("arbitrary",)),
        name="router",
    )(x1, g2, wr_t, br)


def _moe_kernel(ie_ref, row0_ref, rows_ref, nitems_ref, sorted_ref,
                x_hbm, g_ref, wg_ref, wu_ref, wd_ref, dest_hbm,
                xg_ref, xb_ref, y_ref, gsem, ssem):
    it = pl.program_id(0)
    f = pl.program_id(1)
    nrows = rows_ref[it]
    row0 = row0_ref[it]
    nsub = (nrows + SUB_E - 1) // SUB_E

    def gather_copy(j):
        tok = sorted_ref[row0 + j] & (SEQ - 1)
        return pltpu.make_async_copy(x_hbm.at[pl.ds(tok, 1), :], xg_ref.at[pl.ds(j, 1), :], gsem)

    def scatter_copy(j):
        dst = sorted_ref[row0 + j]
        return pltpu.make_async_copy(y_ref.at[pl.ds(j, 1), :], dest_hbm.at[pl.ds(dst, 1), :], ssem)

    @pl.when((f == 0) & (nrows > 0))
    def _():
        def start(j8, c):
            for k in range(8):
                gather_copy(j8 * 8 + k).start()
            return c
        lax.fori_loop(0, nsub * (SUB_E // 8), start, 0)
        got = pl.ds(0, pl.multiple_of(nsub * SUB_E, SUB_E))
        pltpu.make_async_copy(x_hbm.at[got, :], xg_ref.at[got, :], gsem).wait()

        def norm(s, c):
            rows = pl.ds(pl.multiple_of(s * SUB_E, SUB_E), SUB_E)
            x = xg_ref[rows, :]
            ms = jnp.mean(x * x, axis=-1, keepdims=True)
            xb_ref[rows, :] = (x * lax.rsqrt(ms + EPS) * g_ref[...]).astype(BF16)
            y_ref[rows, :] = jnp.zeros((SUB_E, D_MODEL), F32)
            return c
        lax.fori_loop(0, nsub, norm, 0)

    @pl.when(nrows > 0)
    def _():
        wg = wg_ref[...].astype(BF16)
        wu = wu_ref[...].astype(BF16)
        wd = wd_ref[...].astype(BF16)

        def sub(s, c):
            rows = pl.ds(pl.multiple_of(s * SUB_E, SUB_E), SUB_E)
            xb = xb_ref[rows, :]
            hg = jnp.dot(xb, wg, preferred_element_type=F32)
            hu = jnp.dot(xb, wu, preferred_element_type=F32)
            h = (hg * jax.nn.sigmoid(hg) * hu).astype(BF16)
            y_ref[rows, :] = y_ref[rows, :] + jnp.dot(h, wd, preferred_element_type=F32)
            return c
        lax.fori_loop(0, nsub, sub, 0)

    @pl.when((f == NF_E - 1) & (nrows > 0))
    def _():
        whole = pl.multiple_of((nrows // 8) * 8, 8)

        def start8(j8, c):
            for k in range(8):
                scatter_copy(j8 * 8 + k).start()
            return c
        lax.fori_loop(0, nrows // 8, start8, 0)

        def start(j, c):
            scatter_copy(j).start()
            return c
        lax.fori_loop(whole, nrows, start, 0)

        @pl.when(whole > 0)
        def _():
            sent = pl.ds(0, whole)
            pltpu.make_async_copy(y_ref.at[sent, :], dest_hbm.at[sent, :], ssem).wait()

        def wait(j, c):
            scatter_copy(j).wait()
            return c
        lax.fori_loop(whole, nrows, wait, 0)


def _moe(ie, row0, rows, nitems, sorted_i, x1, g2, wg, wu, wd):
    def wmap_cols(i, f, ie_ref, row0_ref, rows_ref, n_ref, s_ref):
        return (ie_ref[i], 0, jnp.where(i < n_ref[0], f, NF_E - 1))

    def wmap_rows(i, f, ie_ref, row0_ref, rows_ref, n_ref, s_ref):
        return (ie_ref[i], jnp.where(i < n_ref[0], f, NF_E - 1), 0)

    grid_spec = pltpu.PrefetchScalarGridSpec(
        num_scalar_prefetch=5,
        grid=(MAX_ITEMS, NF_E),
        in_specs=[
            pl.BlockSpec(memory_space=pl.ANY),
            pl.BlockSpec((1, D_MODEL), lambda i, f, *_: (0, 0)),
            pl.BlockSpec((None, D_MODEL, TF_E), wmap_cols),
            pl.BlockSpec((None, D_MODEL, TF_E), wmap_cols),
            pl.BlockSpec((None, TF_E, D_MODEL), wmap_rows),
        ],
        out_specs=pl.BlockSpec(memory_space=pl.ANY),
        scratch_shapes=[
            pltpu.VMEM((TM_E, D_MODEL), F32),
            pltpu.VMEM((TM_E, D_MODEL), BF16),
            pltpu.VMEM((TM_E, D_MODEL), F32),
            pltpu.SemaphoreType.DMA,
            pltpu.SemaphoreType.DMA,
        ],
    )
    return pl.pallas_call(
        _moe_kernel,
        grid_spec=grid_spec,
        out_shape=jax.ShapeDtypeStruct((N_ASSIGN, D_MODEL), F32),
        compiler_params=_cparams(("arbitrary", "arbitrary")),
        name="moe_experts",
    )(ie, row0, rows, nitems, sorted_i, x1, g2, wg, wu, wd)


def _combine_kernel(x_ref, w_ref, d0_ref, d1_ref, o_ref):
    w = w_ref[...]
    o_ref[...] = x_ref[...] + w[:, 0:1] * d0_ref[...] + w[:, 1:2] * d1_ref[...]


def _combine(x1, wts_t, dest):
    dest3 = dest.reshape(2, SEQ, D_MODEL)
    return pl.pallas_call(
        _combine_kernel,
        grid=(SEQ // TT_COMB,),
        in_specs=[
            pl.BlockSpec((TT_COMB, D_MODEL), lambda i: (i, 0)),
            pl.BlockSpec((TT_COMB, 2), lambda i: (i, 0)),
            pl.BlockSpec((None, TT_COMB, D_MODEL), lambda i: (0, i, 0)),
            pl.BlockSpec((None, TT_COMB, D_MODEL), lambda i: (1, i, 0)),
        ],
        out_specs=pl.BlockSpec((TT_COMB, D_MODEL), lambda i: (i, 0)),
        out_shape=jax.ShapeDtypeStruct((SEQ, D_MODEL), F32),
        compiler_params=_cparams(("arbitrary",)),
        name="moe_combine",
    )(x1, wts_t, dest3, dest3)


def _work_items(eid, rank, counts_f):
    counts = counts_f[:, 0].astype(jnp.int32)
    cum = jnp.cumsum(counts)
    base = cum - counts
    tiles = (counts + TM_E - 1) // TM_E
    tcum = jnp.cumsum(tiles)
    tstart = tcum - tiles
    nitems = tcum[-1]
    ids = jnp.arange(MAX_ITEMS, dtype=jnp.int32)
    ie = jnp.clip(jnp.searchsorted(tcum, ids, side="right"), 0, N_EXPERTS - 1).astype(jnp.int32)
    live = ids < nitems
    ie = jnp.where(live, ie, ie[jnp.maximum(nitems - 1, 0)])
    jt = ids - tstart[ie]
    row0 = jnp.where(live, base[ie] + jt * TM_E, 0)
    rows = jnp.where(live, jnp.clip(counts[ie] - jt * TM_E, 0, TM_E), 0)
    eoh = eid[:, :, None] == jnp.arange(N_EXPERTS, dtype=jnp.int32)
    slot = jnp.sum(jnp.where(eoh, base, 0), axis=-1) + rank
    sorted_i = jnp.zeros((SORTED_LEN,), jnp.int32).at[slot.reshape(-1)].set(
        jnp.arange(N_ASSIGN, dtype=jnp.int32))
    return ie, row0.astype(jnp.int32), rows.astype(jnp.int32), nitems.reshape(1).astype(jnp.int32), sorted_i


def kernel(x, norm1_g, w_in, q_norm_g, k_norm_g, conv_w, conv_b, conv_ln_g, conv_ln_b, rel_bias,
           w_out, norm2_g, w_router_group, b_router_group, w_router_expert, b_router_expert,
           w_gate, w_up, w_down):
    assert x.shape == (1, SEQ, D_MODEL) and w_in.shape[0] == 1
    xs = x[0]
    bias_tab = _attn_bias_tables(rel_bias)
    qg2 = jnp.tile(q_norm_g[0], 2)[None]
    kg2 = jnp.tile(k_norm_g[0], 2)[None]

    proj = _inproj(xs, norm1_g[0][None], w_in[0].astype(BF16))
    conv_out = _conv_mixer(proj, conv_w[0], conv_b[0][None], conv_ln_g[0][None], conv_ln_b[0][None])
    attn_out = _attention(proj, qg2, kg2, bias_tab)
    x1 = _outproj(xs, conv_out, attn_out, w_out[0].astype(BF16))

    wr_t = jnp.concatenate([
        w_router_group[0].T, jnp.zeros((8 - N_GROUPS, D_MODEL), F32),
        jnp.transpose(w_router_expert[0], (0, 2, 1)).reshape(N_EXPERTS, D_MODEL)], axis=0)
    br = jnp.concatenate([b_router_group[0], jnp.zeros((8 - N_GROUPS,), F32),
                          b_router_expert[0].reshape(-1)])
    br = jnp.broadcast_to(br[:, None], (R_ROWS, LANES))
    eid, wts, rank, counts_f = _router(x1, norm2_g[0][None], wr_t, br)

    ie, row0, rows, nitems, sorted_i = _work_items(eid, rank, counts_f)
    dest = _moe(ie, row0, rows, nitems, sorted_i, x1, norm2_g[0][None],
                w_gate[0].reshape(N_EXPERTS, D_MODEL, D_FF),
                w_up[0].reshape(N_EXPERTS, D_MODEL, D_FF),
                w_down[0].reshape(N_EXPERTS, D_FF, D_MODEL))
    out = _combine(x1, wts.T, dest)
    return out[None]
```

```python
import functools
import math

import numpy as np
import jax
import jax.numpy as jnp
from jax import lax
from jax.experimental import pallas as pl
from jax.experimental.pallas import tpu as pltpu

F32 = jnp.float32
BF16 = jnp.bfloat16

D_MODEL = 2048
SEQ = 8192
N_HEADS = 16
HEAD_DIM = 64
ATTN_W = N_HEADS * HEAD_DIM
CONV_C = D_MODEL - ATTN_W
CONV_K = 31
IN_W = 2 * CONV_C + 3 * ATTN_W
PATTERNS = ((128, 1), (512, 4), (2048, 16))
QBLK = 128
NUM_BUCKETS = 32
MAX_DISTANCE = 2048
N_GROUPS = 4
E_PER_G = 8
N_EXPERTS = N_GROUPS * E_PER_G
D_FF = D_MODEL // 2
EPS = 1e-6
NEG_INF = -1e30
LOG2E = math.log2(math.e)

LANES = 128
VMEM_LIMIT = 56 * 1024 * 1024

TM_IN = 1024
TN_IN = 1024
TT_CONV = 512
HALO = 32
R_CONV = 64
R_LN = 32
ATTN_UNROLL = 4
TM_OUT = 512
TM_R = 512
R_ROWS = 8 + N_EXPERTS
TM_E = 768
SUB_E = 256
TF_E = 256
NF_E = D_FF // TF_E
N_ASSIGN = 2 * SEQ
MAX_ITEMS = -(-N_ASSIGN // TM_E) + N_EXPERTS
SORTED_LEN = N_ASSIGN + TM_E
TT_COMB = 512


def _cparams(sem, vmem=VMEM_LIMIT):
    return pltpu.CompilerParams(dimension_semantics=sem, vmem_limit_bytes=vmem)


def _inproj_kernel(x_ref, g_ref, w_ref, o_ref, xn_ref):
    @pl.when(pl.program_id(1) == 0)
    def _():
        x = x_ref[...]
        ms = jnp.mean(x * x, axis=-1, keepdims=True)
        xn_ref[...] = (x * lax.rsqrt(ms + EPS) * g_ref[...]).astype(BF16)

    o_ref[...] = jnp.dot(xn_ref[...], w_ref[...], preferred_element_type=F32)


def _inproj(x, g, w_bf16):
    return pl.pallas_call(
        _inproj_kernel,
        grid=(SEQ // TM_IN, IN_W // TN_IN),
        in_specs=[
            pl.BlockSpec((TM_IN, D_MODEL), lambda i, j: (i, 0)),
            pl.BlockSpec((1, D_MODEL), lambda i, j: (0, 0)),
            pl.BlockSpec((D_MODEL, TN_IN), lambda i, j: (0, j)),
        ],
        out_specs=pl.BlockSpec((TM_IN, TN_IN), lambda i, j: (i, j)),
        out_shape=jax.ShapeDtypeStruct((SEQ, IN_W), F32),
        scratch_shapes=[pltpu.VMEM((TM_IN, D_MODEL), BF16)],
        compiler_params=_cparams(("arbitrary", "arbitrary")),
        name="inproj",
    )(x, g, w_bf16)


def _conv_kernel(val_ref, gate_ref, hval_ref, hgate_ref, cw_ref, cb_ref, lg_ref, lb_ref,
                 o_ref, ubuf, ybuf):
    i = pl.program_id(0)
    u = val_ref[...] * jax.nn.sigmoid(gate_ref[...])
    hu = hval_ref[...] * jax.nn.sigmoid(hgate_ref[...])
    hu = jnp.where(i > 0, hu, 0.0)
    for c in range(CONV_C // LANES):
        cols = slice(c * LANES, (c + 1) * LANES)
        ubuf[c, pl.ds(0, HALO, stride=2), :] = hu[:, cols]
        ubuf[c, pl.ds(2 * HALO, TT_CONV, stride=2), :] = u[:, cols]

    for c in range(CONV_C // LANES):
        cols = slice(c * LANES, (c + 1) * LANES)

        def taps(r, carry, c=c, cols=cols):
            base = r * R_CONV
            acc = jnp.zeros((R_CONV, LANES), F32)
            for j in range(CONV_K):
                first = base + (HALO - CONV_K + 1) + j
                acc = acc + cw_ref[j:j + 1, cols] * ubuf[c, pl.ds(2 * first, R_CONV, stride=2), :]
            ybuf[pl.ds(pl.multiple_of(base, R_CONV), R_CONV), cols] = acc
            return carry

        lax.fori_loop(0, TT_CONV // R_CONV, taps, 0)

    def norm(r, carry):
        rows = pl.ds(pl.multiple_of(r * R_LN, R_LN), R_LN)
        acc = ybuf[rows, :] + cb_ref[...]
        mu = jnp.mean(acc, axis=-1, keepdims=True)
        xc = acc - mu
        var = jnp.mean(xc * xc, axis=-1, keepdims=True)
        y = xc * lax.rsqrt(var + EPS) * lg_ref[...] + lb_ref[...]
        o_ref[rows, :] = (y * jax.nn.sigmoid(y)).astype(BF16)
        return carry

    lax.fori_loop(0, TT_CONV // R_LN, norm, 0)


def _conv_mixer(proj, cw, cb, lg, lb):
    hb = TT_CONV // HALO
    return pl.pallas_call(
        _conv_kernel,
        grid=(SEQ // TT_CONV,),
        in_specs=[
            pl.BlockSpec((TT_CONV, CONV_C), lambda i: (i, 0)),
            pl.BlockSpec((TT_CONV, CONV_C), lambda i: (i, 1)),
            pl.BlockSpec((HALO, CONV_C), lambda i: (jnp.maximum(i * hb - 1, 0), 0)),
            pl.BlockSpec((HALO, CONV_C), lambda i: (jnp.maximum(i * hb - 1, 0), 1)),
            pl.BlockSpec((CONV_K, CONV_C), lambda i: (0, 0)),
            pl.BlockSpec((1, CONV_C), lambda i: (0, 0)),
            pl.BlockSpec((1, CONV_C), lambda i: (0, 0)),
            pl.BlockSpec((1, CONV_C), lambda i: (0, 0)),
        ],
        out_specs=pl.BlockSpec((TT_CONV, CONV_C), lambda i: (i, 0)),
        out_shape=jax.ShapeDtypeStruct((SEQ, CONV_C), BF16),
        scratch_shapes=[pltpu.VMEM((CONV_C // LANES, 2 * (HALO + TT_CONV), LANES), F32),
                        pltpu.VMEM((TT_CONV, CONV_C), F32)],
        compiler_params=_cparams(("arbitrary",)),
        name="conv_mixer",
    )(proj, proj, proj, proj, cw, cb, lg, lb)


def _t5_bucket_np(dist):
    max_exact = NUM_BUCKETS // 2
    nf = np.maximum(dist, 1).astype(np.float32)
    large = max_exact + (np.log(nf / np.float32(max_exact)) / np.float32(math.log(MAX_DISTANCE / max_exact))
                         * np.float32(NUM_BUCKETS - max_exact)).astype(np.int32)
    large = np.minimum(large, NUM_BUCKETS - 1)
    return np.where(dist < max_exact, dist, large)


def _attn_bias_tables(rel_bias):
    kj = np.arange(2 * QBLK)[None, None, :]
    period = 3 * QBLK
    tables = []
    for window, dil in PATTERNS:
        span = window // dil
        assert span <= QBLK
        bucket = _t5_bucket_np(np.arange(span + 1) * dil)
        onehot = np.eye(NUM_BUCKETS, dtype=np.float32)[bucket]
        vec = jnp.einsum("rb,bh->hr", onehot, rel_bias.astype(F32),
                         precision=lax.Precision.HIGHEST)
        diag = jnp.full((N_HEADS, period), NEG_INF, F32)
        diag = diag.at[:, 2 * QBLK - 1 - span:2 * QBLK].set(vec[:, ::-1])
        skew = jnp.tile(diag, (1, QBLK))[:, :QBLK * (period - 1)].reshape(N_HEADS, QBLK, period - 1)
        full = skew[:, :, QBLK - 1:3 * QBLK - 1]
        first = jnp.where(kj >= QBLK, full, NEG_INF)
        both = jnp.stack([full, first])
        tables.append(both.reshape(2, N_HEADS // 2, 2 * QBLK, 2 * QBLK))
    return jnp.stack(tables) * LOG2E


def _attn_kernel(q_ref, k_ref, v_ref, qg_ref, kg_ref, bias_ref, o_ref,
                 qn_ref, kn_ref, acc_ref, m_ref, l_ref):
    lane = lax.broadcasted_iota(jnp.int32, (QBLK, LANES), 1)
    head_a = lane < HEAD_DIM
    ri = lax.broadcasted_iota(jnp.int32, (LANES, LANES), 0) // HEAD_DIM
    ci = lax.broadcasted_iota(jnp.int32, (LANES, LANES), 1) // HEAD_DIM
    seg = (ri == ci).astype(BF16)

    def head_rms(x, g):
        sq = x * x
        hi = sq.astype(BF16)
        lo = (sq - hi.astype(F32)).astype(BF16)
        ss = (jnp.dot(hi, seg, preferred_element_type=F32)
              + jnp.dot(lo, seg, preferred_element_type=F32))
        return x * lax.rsqrt(ss * (1.0 / HEAD_DIM) + EPS) * g

    NCH = 512

    def norm_body(c, carry):
        rows = pl.ds(pl.multiple_of(c * NCH, NCH), NCH)
        qn_ref[rows, :] = head_rms(q_ref[rows, :], qg_ref[...]) * (LOG2E / math.sqrt(HEAD_DIM))
        kn_ref[rows, :] = head_rms(k_ref[rows, :], kg_ref[...])
        return carry

    lax.fori_loop(0, SEQ // NCH, norm_body, 0)

    for p, (window, dil) in enumerate(PATTERNS):
        nb = SEQ // (dil * QBLK)

        def unit(u, carry, p=p, dil=dil, nb=nb):
            r = u // nb
            n = u - r * nb
            cur = n * (QBLK * dil) + r
            prev = jnp.maximum(n - 1, 0) * (QBLK * dil) + r

            def rows(start):
                if dil == 1:
                    return pl.ds(start, QBLK)
                return pl.ds(start, QBLK, stride=dil)

            q = qn_ref[rows(cur), :]
            q2 = jnp.concatenate([jnp.where(head_a, q, 0.0), jnp.where(head_a, 0.0, q)],
                                 axis=0).astype(BF16)
            k2 = jnp.concatenate([kn_ref[rows(prev), :], kn_ref[rows(cur), :]], axis=0).astype(BF16)
            v2 = jnp.concatenate([v_ref[rows(prev), :], v_ref[rows(cur), :]], axis=0).astype(BF16)
            s = lax.dot_general(q2, k2, (((1,), (1,)), ((), ())), preferred_element_type=F32)
            first = jnp.where(n == 0, 1, 0)
            s = s + bias_ref[p, first]
            m = jnp.max(s, axis=-1, keepdims=True)
            e = jnp.exp2(s - m)
            l = jnp.sum(e, axis=-1, keepdims=True)
            pv = jnp.dot(e.astype(BF16), v2, preferred_element_type=F32)
            o_new = jnp.where(head_a, pv[:QBLK], pv[QBLK:])
            m_new = jnp.where(head_a, m[:QBLK], m[QBLK:])
            l_new = jnp.where(head_a, l[:QBLK], l[QBLK:])
            if p == 0:
                acc_ref[rows(cur), :] = o_new
                m_ref[rows(cur), :] = m_new
                l_ref[rows(cur), :] = l_new
            else:
                m_old = m_ref[rows(cur), :]
                m_tot = jnp.maximum(m_old, m_new)
                a = jnp.exp2(m_old - m_tot)
                b = jnp.exp2(m_new - m_tot)
                acc_ref[rows(cur), :] = acc_ref[rows(cur), :] * a + o_new * b
                l_ref[rows(cur), :] = l_ref[rows(cur), :] * a + l_new * b
                m_ref[rows(cur), :] = m_tot
            return carry

        lax.fori_loop(0, SEQ // QBLK, unit, 0, unroll=ATTN_UNROLL)

    def out_body(c, carry):
        rows = pl.ds(pl.multiple_of(c * NCH, NCH), NCH)
        o_ref[rows, :] = (acc_ref[rows, :] / l_ref[rows, :]).astype(BF16)
        return carry

    lax.fori_loop(0, SEQ // NCH, out_body, 0)


def _attention(proj, qg2, kg2, bias_tab):
    qoff = 2 * CONV_C // LANES
    koff = qoff + ATTN_W // LANES
    voff = koff + ATTN_W // LANES
    return pl.pallas_call(
        _attn_kernel,
        grid=(N_HEADS // 2,),
        in_specs=[
            pl.BlockSpec((SEQ, LANES), lambda h: (0, qoff + h)),
            pl.BlockSpec((SEQ, LANES), lambda h: (0, koff + h)),
            pl.BlockSpec((SEQ, LANES), lambda h: (0, voff + h)),
            pl.BlockSpec((1, LANES), lambda h: (0, 0)),
            pl.BlockSpec((1, LANES), lambda h: (0, 0)),
            pl.BlockSpec((len(PATTERNS), 2, None, 2 * QBLK, 2 * QBLK), lambda h: (0, 0, h, 0, 0)),
        ],
        out_specs=pl.BlockSpec((SEQ, LANES), lambda h: (0, h)),
        out_shape=jax.ShapeDtypeStruct((SEQ, ATTN_W), BF16),
        scratch_shapes=[pltpu.VMEM((SEQ, LANES), F32) for _ in range(5)],
        compiler_params=_cparams(("arbitrary",)),
        name="dilated_attn",
    )(proj, proj, proj, qg2, kg2, bias_tab)


def _outproj_kernel(x_ref, c_ref, a_ref, wc_ref, wa_ref, o_ref):
    o_ref[...] = (x_ref[...]
                  + jnp.dot(c_ref[...], wc_ref[...], preferred_element_type=F32)
                  + jnp.dot(a_ref[...], wa_ref[...], preferred_element_type=F32))


def _outproj(x, conv_out, attn_out, w_out_bf16):
    return pl.pallas_call(
        _outproj_kernel,
        grid=(SEQ // TM_OUT,),
        in_specs=[
            pl.BlockSpec((TM_OUT, D_MODEL), lambda i: (i, 0)),
            pl.BlockSpec((TM_OUT, CONV_C), lambda i: (i, 0)),
            pl.BlockSpec((TM_OUT, ATTN_W), lambda i: (i, 0)),
            pl.BlockSpec((CONV_C, D_MODEL), lambda i: (0, 0)),
            pl.BlockSpec((ATTN_W, D_MODEL), lambda i: (1, 0)),
        ],
        out_specs=pl.BlockSpec((TM_OUT, D_MODEL), lambda i: (i, 0)),
        out_shape=jax.ShapeDtypeStruct((SEQ, D_MODEL), F32),
        compiler_params=_cparams(("arbitrary",)),
        name="outproj",
    )(x, conv_out, attn_out, w_out_bf16, w_out_bf16)


def _split3(a):
    a1 = a.astype(BF16)
    r1 = a - a1.astype(F32)
    a2 = r1.astype(BF16)
    a3 = (r1 - a2.astype(F32)).astype(BF16)
    return a1, a2, a3


def _router_kernel(x_ref, g_ref, wr_ref, br_ref, eid_ref, wts_ref, rank_ref, cnt_ref):
    i = pl.program_id(0)

    @pl.when(i == 0)
    def _():
        cnt_ref[...] = jnp.zeros_like(cnt_ref)

    x = x_ref[...]
    ms = jnp.mean(x * x, axis=-1, keepdims=True)
    hn = x * lax.rsqrt(ms + EPS) * g_ref[...]
    h1, h2, h3 = _split3(hn)
    w1, w2, w3 = _split3(wr_ref[...])
    dn = (((1,), (1,)), ((), ()))
    lt = None
    for wa, ha in ((w1, h1), (w1, h2), (w2, h1), (w2, h2), (w1, h3), (w3, h1)):
        t = lax.dot_general(wa, ha, dn, preferred_element_type=F32)
        lt = t if lt is None else lt + t
    lt = lt + br_ref[:, 0:1]

    row8 = lax.broadcasted_iota(jnp.int32, (8, TM_R), 0)
    gl = jnp.where(row8 < N_GROUPS, lt[0:8], -jnp.inf)
    gmax = jnp.max(gl, axis=0, keepdims=True)
    gidx = jnp.min(jnp.where(gl == gmax, row8, 8), axis=0, keepdims=True)
    gw = 1.0 / jnp.sum(jnp.exp(gl - gmax), axis=0, keepdims=True)

    esel = lt[8:16]
    for g in range(1, N_GROUPS):
        esel = jnp.where(gidx == g, lt[8 + 8 * g:16 + 8 * g], esel)
    v1 = jnp.max(esel, axis=0, keepdims=True)
    i1 = jnp.min(jnp.where(esel == v1, row8, 8), axis=0, keepdims=True)
    rest = jnp.where(row8 == i1, -jnp.inf, esel)
    v2 = jnp.max(rest, axis=0, keepdims=True)
    i2 = jnp.min(jnp.where(rest == v2, row8, 8), axis=0, keepdims=True)
    e21 = jnp.exp(v2 - v1)
    den = 1.0 + e21
    e1 = gidx * E_PER_G + i1
    e2 = gidx * E_PER_G + i2
    eid_ref[0:1, :] = e1
    eid_ref[1:2, :] = e2
    wts_ref[0:1, :] = gw * (1.0 / den)
    wts_ref[1:2, :] = gw * (e21 / den)

    erow = lax.broadcasted_iota(jnp.int32, (N_EXPERTS, TM_R), 0)
    oh1 = erow == e1
    oh2 = erow == e2
    member = jnp.where(oh1 | oh2, 1.0, 0.0)
    ti = lax.broadcasted_iota(jnp.int32, (TM_R, TM_R), 0)
    tj = lax.broadcasted_iota(jnp.int32, (TM_R, TM_R), 1)
    upper = jnp.where(ti < tj, 1.0, 0.0).astype(BF16)
    before = jnp.dot(member.astype(BF16), upper, preferred_element_type=F32)
    pos = before + cnt_ref[:, 0:1]
    rank_ref[0:1, :] = jnp.sum(jnp.where(oh1, pos, 0.0), axis=0, keepdims=True).astype(jnp.int32)
    rank_ref[1:2, :] = jnp.sum(jnp.where(oh2, pos, 0.0), axis=0, keepdims=True).astype(jnp.int32)
    cnt_ref[...] = cnt_ref[...] + jnp.sum(member, axis=1, keepdims=True)


def _router(x1, g2, wr_t, br):
    return pl.pallas_call(
        _router_kernel,
        grid=(SEQ // TM_R,),
        in_specs=[
            pl.BlockSpec((TM_R, D_MODEL), lambda i: (i, 0)),
            pl.BlockSpec((1, D_MODEL), lambda i: (0, 0)),
            pl.BlockSpec((R_ROWS, D_MODEL), lambda i: (0, 0)),
            pl.BlockSpec((R_ROWS, LANES), lambda i: (0, 0)),
        ],
        out_specs=[
            pl.BlockSpec((2, TM_R), lambda i: (0, i)),
            pl.BlockSpec((2, TM_R), lambda i: (0, i)),
            pl.BlockSpec((2, TM_R), lambda i: (0, i)),
            pl.BlockSpec((N_EXPERTS, LANES), lambda i: (0, 0)),
        ],
        out_shape=[
            jax.ShapeDtypeStruct((2, SEQ), jnp.int32),
            jax.ShapeDtypeStruct((2, SEQ), F32),
            jax.ShapeDtypeStruct((2, SEQ), jnp.int32),
            jax.ShapeDtypeStruct((N_EXPERTS, LANES), F32),
        ],
        compiler_params=_cparams(("arbitrary",)),
        name="router",
    )(x1, g2, wr_t, br)


def _moe_kernel(ie_ref, row0_ref, rows_ref, nitems_ref, sorted_ref,
                x_hbm, g_ref, wg_ref, wu_ref, wd_ref, dest_hbm,
                xg_ref, xb_ref, y_ref, gsem, ssem):
    it = pl.program_id(0)
    f = pl.program_id(1)
    nrows = rows_ref[it]
    nitems = nitems_ref[0]
    slot = it % 2
    nsub = (nrows + SUB_E - 1) // SUB_E

    def padded(item):
        return pl.multiple_of(((rows_ref[item] + SUB_E - 1) // SUB_E) * SUB_E, SUB_E)

    def start_gather(item):
        base = row0_ref[item]
        buf = item % 2

        def start(j8, c):
            for k in range(8):
                j = j8 * 8 + k
                tok = sorted_ref[base + j] & (SEQ - 1)
                pltpu.make_async_copy(x_hbm.at[pl.ds(tok, 1), :], xg_ref.at[buf, pl.ds(j, 1), :],
                                      gsem.at[buf]).start()
            return c
        lax.fori_loop(0, padded(item) // 8, start, 0)

    def wait_gather(item):
        got = pl.ds(0, padded(item))
        buf = item % 2
        pltpu.make_async_copy(x_hbm.at[got, :], xg_ref.at[buf, got, :], gsem.at[buf]).wait()

    def scatter_copy(item, j):
        dst = sorted_ref[row0_ref[item] + j]
        buf = item % 2
        return pltpu.make_async_copy(y_ref.at[buf, pl.ds(j, 1), :], dest_hbm.at[pl.ds(dst, 1), :],
                                     ssem.at[buf])

    def start_scatter(item):
        n = rows_ref[item]

        def start8(j8, c):
            for k in range(8):
                scatter_copy(item, j8 * 8 + k).start()
            return c
        lax.fori_loop(0, n // 8, start8, 0)

        def start(j, c):
            scatter_copy(item, j).start()
            return c
        lax.fori_loop((n // 8) * 8, n, start, 0)

    def wait_scatter(item):
        n = rows_ref[item]
        whole = pl.multiple_of((n // 8) * 8, 8)
        buf = item % 2

        @pl.when(whole > 0)
        def _():
            sent = pl.ds(0, whole)
            pltpu.make_async_copy(y_ref.at[buf, sent, :], dest_hbm.at[sent, :], ssem.at[buf]).wait()

        def wait(j, c):
            scatter_copy(item, j).wait()
            return c
        lax.fori_loop(whole, n, wait, 0)

    @pl.when((f == 0) & (nrows > 0))
    def _():
        @pl.when(it == 0)
        def _():
            start_gather(0)

        wait_gather(it)

        def norm(s, c):
            rows = pl.ds(pl.multiple_of(s * SUB_E, SUB_E), SUB_E)
            x = xg_ref[slot, rows, :]
            ms = jnp.mean(x * x, axis=-1, keepdims=True)
            xb_ref[rows, :] = (x * lax.rsqrt(ms + EPS) * g_ref[...]).astype(BF16)
            y_ref[slot, rows, :] = jnp.zeros((SUB_E, D_MODEL), F32)
            return c
        lax.fori_loop(0, nsub, norm, 0)

        @pl.when(it + 1 < nitems)
        def _():
            start_gather(it + 1)

    @pl.when(nrows > 0)
    def _():
        wg = wg_ref[...].astype(BF16)
        wu = wu_ref[...].astype(BF16)
        wd = wd_ref[...].astype(BF16)

        def sub(s, c):
            rows = pl.ds(pl.multiple_of(s * SUB_E, SUB_E), SUB_E)
            xb = xb_ref[rows, :]
            hg = jnp.dot(xb, wg, preferred_element_type=F32)
            hu = jnp.dot(xb, wu, preferred_element_type=F32)
            h = (hg * jax.nn.sigmoid(hg) * hu).astype(BF16)
            y_ref[slot, rows, :] = y_ref[slot, rows, :] + jnp.dot(h, wd, preferred_element_type=F32)
            return c
        lax.fori_loop(0, nsub, sub, 0)

    @pl.when((f == NF_E - 1) & (nrows > 0))
    def _():
        @pl.when(it > 0)
        def _():
            wait_scatter(it - 1)

        start_scatter(it)

        @pl.when(it + 1 >= nitems)
        def _():
            wait_scatter(it)


def _moe(ie, row0, rows, nitems, sorted_i, x1, g2, wg, wu, wd):
    def wmap_cols(i, f, ie_ref, row0_ref, rows_ref, n_ref, s_ref):
        return (ie_ref[i], 0, jnp.where(i < n_ref[0], f, NF_E - 1))

    def wmap_rows(i, f, ie_ref, row0_ref, rows_ref, n_ref, s_ref):
        return (ie_ref[i], jnp.where(i < n_ref[0], f, NF_E - 1), 0)

    grid_spec = pltpu.PrefetchScalarGridSpec(
        num_scalar_prefetch=5,
        grid=(MAX_ITEMS, NF_E),
        in_specs=[
            pl.BlockSpec(memory_space=pl.ANY),
            pl.BlockSpec((1, D_MODEL), lambda i, f, *_: (0, 0)),
            pl.BlockSpec((None, D_MODEL, TF_E), wmap_cols),
            pl.BlockSpec((None, D_MODEL, TF_E), wmap_cols),
            pl.BlockSpec((None, TF_E, D_MODEL), wmap_rows),
        ],
        out_specs=pl.BlockSpec(memory_space=pl.ANY),
        scratch_shapes=[
            pltpu.VMEM((2, TM_E, D_MODEL), F32),
            pltpu.VMEM((TM_E, D_MODEL), BF16),
            pltpu.VMEM((2, TM_E, D_MODEL), F32),
            pltpu.SemaphoreType.DMA((2,)),
            pltpu.SemaphoreType.DMA((2,)),
        ],
    )
    return pl.pallas_call(
        _moe_kernel,
        grid_spec=grid_spec,
        out_shape=jax.ShapeDtypeStruct((N_ASSIGN, D_MODEL), F32),
        compiler_params=_cparams(("arbitrary", "arbitrary")),
        name="moe_experts",
    )(ie, row0, rows, nitems, sorted_i, x1, g2, wg, wu, wd)


def _combine_kernel(x_ref, w_ref, d0_ref, d1_ref, o_ref):
    w = w_ref[...]
    o_ref[...] = x_ref[...] + w[:, 0:1] * d0_ref[...] + w[:, 1:2] * d1_ref[...]


def _combine(x1, wts_t, dest):
    dest3 = dest.reshape(2, SEQ, D_MODEL)
    return pl.pallas_call(
        _combine_kernel,
        grid=(SEQ // TT_COMB,),
        in_specs=[
            pl.BlockSpec((TT_COMB, D_MODEL), lambda i: (i, 0)),
            pl.BlockSpec((TT_COMB, 2), lambda i: (i, 0)),
            pl.BlockSpec((None, TT_COMB, D_MODEL), lambda i: (0, i, 0)),
            pl.BlockSpec((None, TT_COMB, D_MODEL), lambda i: (1, i, 0)),
        ],
        out_specs=pl.BlockSpec((TT_COMB, D_MODEL), lambda i: (i, 0)),
        out_shape=jax.ShapeDtypeStruct((SEQ, D_MODEL), F32),
        compiler_params=_cparams(("arbitrary",)),
        name="moe_combine",
    )(x1, wts_t, dest3, dest3)


def _work_items(eid, rank, counts_f):
    counts = counts_f[:, 0].astype(jnp.int32)
    cum = jnp.cumsum(counts)
    base = cum - counts
    tiles = (counts + TM_E - 1) // TM_E
    tcum = jnp.cumsum(tiles)
    tstart = tcum - tiles
    nitems = tcum[-1]
    ids = jnp.arange(MAX_ITEMS, dtype=jnp.int32)
    ie = jnp.clip(jnp.searchsorted(tcum, ids, side="right"), 0, N_EXPERTS - 1).astype(jnp.int32)
    live = ids < nitems
    ie = jnp.where(live, ie, ie[jnp.maximum(nitems - 1, 0)])
    jt = ids - tstart[ie]
    row0 = jnp.where(live, base[ie] + jt * TM_E, 0)
    rows = jnp.where(live, jnp.clip(counts[ie] - jt * TM_E, 0, TM_E), 0)
    eoh = eid[:, :, None] == jnp.arange(N_EXPERTS, dtype=jnp.int32)
    slot = jnp.sum(jnp.where(eoh, base, 0), axis=-1) + rank
    sorted_i = jnp.zeros((SORTED_LEN,), jnp.int32).at[slot.reshape(-1)].set(
        jnp.arange(N_ASSIGN, dtype=jnp.int32))
    return ie, row0.astype(jnp.int32), rows.astype(jnp.int32), nitems.reshape(1).astype(jnp.int32), sorted_i


def kernel(x, norm1_g, w_in, q_norm_g, k_norm_g, conv_w, conv_b, conv_ln_g, conv_ln_b, rel_bias,
           w_out, norm2_g, w_router_group, b_router_group, w_router_expert, b_router_expert,
           w_gate, w_up, w_down):
    assert x.shape == (1, SEQ, D_MODEL) and w_in.shape[0] == 1
    xs = x[0]
    bias_tab = _attn_bias_tables(rel_bias)
    qg2 = jnp.tile(q_norm_g[0], 2)[None]
    kg2 = jnp.tile(k_norm_g[0], 2)[None]

    proj = _inproj(xs, norm1_g[0][None], w_in[0].astype(BF16))
    conv_out = _conv_mixer(proj, conv_w[0], conv_b[0][None], conv_ln_g[0][None], conv_ln_b[0][None])
    attn_out = _attention(proj, qg2, kg2, bias_tab)
    x1 = _outproj(xs, conv_out, attn_out, w_out[0].astype(BF16))

    wr_t = jnp.concatenate([
        w_router_group[0].T, jnp.zeros((8 - N_GROUPS, D_MODEL), F32),
        jnp.transpose(w_router_expert[0], (0, 2, 1)).reshape(N_EXPERTS, D_MODEL)], axis=0)
    br = jnp.concatenate([b_router_group[0], jnp.zeros((8 - N_GROUPS,), F32),
                          b_router_expert[0].reshape(-1)])
    br = jnp.broadcast_to(br[:, None], (R_ROWS, LANES))
    eid, wts, rank, counts_f = _router(x1, norm2_g[0][None], wr_t, br)

    ie, row0, rows, nitems, sorted_i = _work_items(eid, rank, counts_f)
    dest = _moe(ie, row0, rows, nitems, sorted_i, x1, norm2_g[0][None],
                w_gate[0].reshape(N_EXPERTS, D_MODEL, D_FF),
                w_up[0].reshape(N_EXPERTS, D_MODEL, D_FF),
                w_down[0].reshape(N_EXPERTS, D_FF, D_MODEL))
    out = _combine(x1, wts.T, dest)
    return out[None]
```

```python
import functools
import math

import numpy as np
import jax
import jax.numpy as jnp
from jax import lax
from jax.experimental import pallas as pl
from jax.experimental.pallas import tpu as pltpu

F32 = jnp.float32
BF16 = jnp.bfloat16

D_MODEL = 2048
SEQ = 8192
N_HEADS = 16
HEAD_DIM = 64
ATTN_W = N_HEADS * HEAD_DIM
CONV_C = D_MODEL - ATTN_W
CONV_K = 31
IN_W = 2 * CONV_C + 3 * ATTN_W
PATTERNS = ((128, 1), (512, 4), (2048, 16))
QBLK = 128
NUM_BUCKETS = 32
MAX_DISTANCE = 2048
N_GROUPS = 4
E_PER_G = 8
N_EXPERTS = N_GROUPS * E_PER_G
D_FF = D_MODEL // 2
EPS = 1e-6
NEG_INF = -1e30
LOG2E = math.log2(math.e)

LANES = 128
VMEM_LIMIT = 56 * 1024 * 1024

TM_IN = 1024
TN_IN = 1024
TT_CONV = 512
HALO = 32
R_CONV = 64
R_LN = 16
ATTN_UNROLL = 8
TM_OUT = 512
TM_R = 512
R_ROWS = 8 + N_EXPERTS
TM_E = 768
SUB_E = 256
TF_E = 256
NF_E = D_FF // TF_E
N_ASSIGN = 2 * SEQ
MAX_ITEMS = -(-N_ASSIGN // TM_E) + N_EXPERTS
SORTED_LEN = N_ASSIGN + TM_E
TT_COMB = 512


def _cparams(sem, vmem=VMEM_LIMIT):
    return pltpu.CompilerParams(dimension_semantics=sem, vmem_limit_bytes=vmem)


def _inproj_kernel(x_ref, g_ref, w_ref, o_ref, xn_ref):
    @pl.when(pl.program_id(1) == 0)
    def _():
        x = x_ref[...]
        ms = jnp.mean(x * x, axis=-1, keepdims=True)
        xn_ref[...] = (x * lax.rsqrt(ms + EPS) * g_ref[...]).astype(BF16)

    o_ref[...] = jnp.dot(xn_ref[...], w_ref[...], preferred_element_type=F32)


def _inproj(x, g, w_bf16):
    return pl.pallas_call(
        _inproj_kernel,
        grid=(SEQ // TM_IN, IN_W // TN_IN),
        in_specs=[
            pl.BlockSpec((TM_IN, D_MODEL), lambda i, j: (i, 0)),
            pl.BlockSpec((1, D_MODEL), lambda i, j: (0, 0)),
            pl.BlockSpec((D_MODEL, TN_IN), lambda i, j: (0, j)),
        ],
        out_specs=pl.BlockSpec((TM_IN, TN_IN), lambda i, j: (i, j)),
        out_shape=jax.ShapeDtypeStruct((SEQ, IN_W), F32),
        scratch_shapes=[pltpu.VMEM((TM_IN, D_MODEL), BF16)],
        compiler_params=_cparams(("arbitrary", "arbitrary")),
        name="inproj",
    )(x, g, w_bf16)


def _conv_kernel(val_ref, gate_ref, hval_ref, hgate_ref, cw_ref, cb_ref, lg_ref, lb_ref,
                 o_ref, ubuf, zbuf, ybuf):
    i = pl.program_id(0)
    u = val_ref[...] * jax.nn.sigmoid(gate_ref[...])
    hu = hval_ref[...] * jax.nn.sigmoid(hgate_ref[...])
    hu = jnp.where(i > 0, hu, 0.0)
    for c in range(CONV_C // LANES):
        cols = slice(c * LANES, (c + 1) * LANES)
        ubuf[c, 0:HALO, :] = hu[:, cols]
        ubuf[c, HALO:HALO + TT_CONV, :] = u[:, cols]

    n_a = -(-CONV_K // 8)
    assert HALO == 8 * n_a
    for c in range(CONV_C // LANES):
        cols = slice(c * LANES, (c + 1) * LANES)

        def taps(r, carry, c=c, cols=cols):
            base = pl.multiple_of(r * R_CONV, R_CONV)
            win = ubuf[c, pl.ds(base, R_CONV + HALO), :]
            for b in range(8):
                z = None
                for a in range(n_a):
                    s = 8 * a + b
                    if s >= CONV_K:
                        continue
                    lo = HALO - 8 - 8 * a
                    t = cw_ref[CONV_K - 1 - s:CONV_K - s, cols] * win[lo:lo + R_CONV + 8, :]
                    z = t if z is None else z + t
                zbuf[b, pl.ds(0, R_CONV + 8, stride=2), :] = z
            acc = None
            for b in range(8):
                t = zbuf[b, pl.ds(2 * (8 - b), R_CONV, stride=2), :]
                acc = t if acc is None else acc + t
            ybuf[pl.ds(base, R_CONV), cols] = acc
            return carry

        lax.fori_loop(0, TT_CONV // R_CONV, taps, 0)

    def norm(r, carry):
        rows = pl.ds(pl.multiple_of(r * R_LN, R_LN), R_LN)
        acc = ybuf[rows, :] + cb_ref[...]
        mu = jnp.mean(acc, axis=-1, keepdims=True)
        xc = acc - mu
        var = jnp.mean(xc * xc, axis=-1, keepdims=True)
        y = xc * lax.rsqrt(var + EPS) * lg_ref[...] + lb_ref[...]
        o_ref[rows, :] = (y * jax.nn.sigmoid(y)).astype(BF16)
        return carry

    lax.fori_loop(0, TT_CONV // R_LN, norm, 0, unroll=4)


def _conv_mixer(proj, cw, cb, lg, lb):
    hb = TT_CONV // HALO
    return pl.pallas_call(
        _conv_kernel,
        grid=(SEQ // TT_CONV,),
        in_specs=[
            pl.BlockSpec((TT_CONV, CONV_C), lambda i: (i, 0)),
            pl.BlockSpec((TT_CONV, CONV_C), lambda i: (i, 1)),
            pl.BlockSpec((HALO, CONV_C), lambda i: (jnp.maximum(i * hb - 1, 0), 0)),
            pl.BlockSpec((HALO, CONV_C), lambda i: (jnp.maximum(i * hb - 1, 0), 1)),
            pl.BlockSpec((CONV_K, CONV_C), lambda i: (0, 0)),
            pl.BlockSpec((1, CONV_C), lambda i: (0, 0)),
            pl.BlockSpec((1, CONV_C), lambda i: (0, 0)),
            pl.BlockSpec((1, CONV_C), lambda i: (0, 0)),
        ],
        out_specs=pl.BlockSpec((TT_CONV, CONV_C), lambda i: (i, 0)),
        out_shape=jax.ShapeDtypeStruct((SEQ, CONV_C), BF16),
        scratch_shapes=[pltpu.VMEM((CONV_C // LANES, HALO + TT_CONV, LANES), F32),
                        pltpu.VMEM((8, 2 * (R_CONV + 8), LANES), F32),
                        pltpu.VMEM((TT_CONV, CONV_C), F32)],
        compiler_params=_cparams(("arbitrary",)),
        name="conv_mixer",
    )(proj, proj, proj, proj, cw, cb, lg, lb)


def _t5_bucket_np(dist):
    max_exact = NUM_BUCKETS // 2
    nf = np.maximum(dist, 1).astype(np.float32)
    large = max_exact + (np.log(nf / np.float32(max_exact)) / np.float32(math.log(MAX_DISTANCE / max_exact))
                         * np.float32(NUM_BUCKETS - max_exact)).astype(np.int32)
    large = np.minimum(large, NUM_BUCKETS - 1)
    return np.where(dist < max_exact, dist, large)


def _attn_bias_tables(rel_bias):
    kj = np.arange(2 * QBLK)[None, None, :]
    period = 3 * QBLK
    tables = []
    for window, dil in PATTERNS:
        span = window // dil
        assert span <= QBLK
        bucket = _t5_bucket_np(np.arange(span + 1) * dil)
        onehot = np.eye(NUM_BUCKETS, dtype=np.float32)[bucket]
        vec = jnp.einsum("rb,bh->hr", onehot, rel_bias.astype(F32),
                         precision=lax.Precision.HIGHEST)
        diag = jnp.full((N_HEADS, period), NEG_INF, F32)
        diag = diag.at[:, 2 * QBLK - 1 - span:2 * QBLK].set(vec[:, ::-1])
        skew = jnp.tile(diag, (1, QBLK))[:, :QBLK * (period - 1)].reshape(N_HEADS, QBLK, period - 1)
        full = skew[:, :, QBLK - 1:3 * QBLK - 1]
        first = jnp.where(kj >= QBLK, full, NEG_INF)
        both = jnp.stack([full, first])
        tables.append(both.reshape(2, N_HEADS // 2, 2 * QBLK, 2 * QBLK))
    return jnp.stack(tables) * LOG2E


def _attn_kernel(q_ref, k_ref, v_ref, qg_ref, kg_ref, bias_ref, o_ref,
                 qn_ref, kn_ref, acc_ref, m_ref, l_ref):
    lane = lax.broadcasted_iota(jnp.int32, (QBLK, LANES), 1)
    head_a = lane < HEAD_DIM
    ri = lax.broadcasted_iota(jnp.int32, (LANES, LANES), 0) // HEAD_DIM
    ci = lax.broadcasted_iota(jnp.int32, (LANES, LANES), 1) // HEAD_DIM
    seg = (ri == ci).astype(BF16)

    def head_rms(x, g):
        sq = x * x
        hi = sq.astype(BF16)
        lo = (sq - hi.astype(F32)).astype(BF16)
        ss = (jnp.dot(hi, seg, preferred_element_type=F32)
              + jnp.dot(lo, seg, preferred_element_type=F32))
        return x * lax.rsqrt(ss * (1.0 / HEAD_DIM) + EPS) * g

    NCH = 512

    def norm_body(c, carry):
        rows = pl.ds(pl.multiple_of(c * NCH, NCH), NCH)
        qn_ref[rows, :] = head_rms(q_ref[rows, :], qg_ref[...]) * (LOG2E / math.sqrt(HEAD_DIM))
        kn_ref[rows, :] = head_rms(k_ref[rows, :], kg_ref[...])
        return carry

    lax.fori_loop(0, SEQ // NCH, norm_body, 0)

    order = sorted(range(len(PATTERNS)), key=lambda i: -PATTERNS[i][1])
    assert PATTERNS[order[-1]][1] == 1
    for step, p in enumerate(order):
        dil = PATTERNS[p][1]
        nb = SEQ // (dil * QBLK)
        is_first = step == 0
        is_last = step == len(order) - 1

        def unit(u, carry, p=p, dil=dil, nb=nb, is_first=is_first, is_last=is_last):
            r = u // nb
            n = u - r * nb
            cur = n * (QBLK * dil) + r
            prev = jnp.maximum(n - 1, 0) * (QBLK * dil) + r

            def rows(start):
                if dil == 1:
                    return pl.ds(start, QBLK)
                return pl.ds(start, QBLK, stride=dil)

            q = qn_ref[rows(cur), :]
            q2 = jnp.concatenate([jnp.where(head_a, q, 0.0), jnp.where(head_a, 0.0, q)],
                                 axis=0).astype(BF16)
            k2 = jnp.concatenate([kn_ref[rows(prev), :], kn_ref[rows(cur), :]], axis=0).astype(BF16)
            v2 = jnp.concatenate([v_ref[rows(prev), :], v_ref[rows(cur), :]], axis=0).astype(BF16)
            s = lax.dot_general(q2, k2, (((1,), (1,)), ((), ())), preferred_element_type=F32)
            first = jnp.where(n == 0, 1, 0)
            s = s + bias_ref[p, first]
            m = jnp.max(s, axis=-1, keepdims=True)
            e = jnp.exp2(s - m)
            l = jnp.sum(e, axis=-1, keepdims=True)
            pv = jnp.dot(e.astype(BF16), v2, preferred_element_type=F32)
            o_new = jnp.where(head_a, pv[:QBLK], pv[QBLK:])
            m_new = jnp.where(head_a, m[:QBLK], m[QBLK:])
            l_new = jnp.where(head_a, l[:QBLK], l[QBLK:])
            if is_first:
                acc_ref[rows(cur), :] = o_new
                m_ref[rows(cur), :] = m_new
                l_ref[rows(cur), :] = l_new
            else:
                m_old = m_ref[rows(cur), :]
                m_tot = jnp.maximum(m_old, m_new)
                a = jnp.exp2(m_old - m_tot)
                b = jnp.exp2(m_new - m_tot)
                acc = acc_ref[rows(cur), :] * a + o_new * b
                den = l_ref[rows(cur), :] * a + l_new * b
                if is_last:
                    o_ref[pl.ds(pl.multiple_of(cur, QBLK), QBLK), :] = (acc / den).astype(BF16)
                else:
                    acc_ref[rows(cur), :] = acc
                    l_ref[rows(cur), :] = den
                    m_ref[rows(cur), :] = m_tot
            return carry

        lax.fori_loop(0, SEQ // QBLK, unit, 0, unroll=ATTN_UNROLL)


def _attention(proj, qg2, kg2, bias_tab):
    qoff = 2 * CONV_C // LANES
    koff = qoff + ATTN_W // LANES
    voff = koff + ATTN_W // LANES
    return pl.pallas_call(
        _attn_kernel,
        grid=(N_HEADS // 2,),
        in_specs=[
            pl.BlockSpec((SEQ, LANES), lambda h: (0, qoff + h)),
            pl.BlockSpec((SEQ, LANES), lambda h: (0, koff + h)),
            pl.BlockSpec((SEQ, LANES), lambda h: (0, voff + h)),
            pl.BlockSpec((1, LANES), lambda h: (0, 0)),
            pl.BlockSpec((1, LANES), lambda h: (0, 0)),
            pl.BlockSpec((len(PATTERNS), 2, None, 2 * QBLK, 2 * QBLK), lambda h: (0, 0, h, 0, 0)),
        ],
        out_specs=pl.BlockSpec((SEQ, LANES), lambda h: (0, h)),
        out_shape=jax.ShapeDtypeStruct((SEQ, ATTN_W), BF16),
        scratch_shapes=[pltpu.VMEM((SEQ, LANES), F32) for _ in range(5)],
        compiler_params=_cparams(("arbitrary",)),
        name="dilated_attn",
    )(proj, proj, proj, qg2, kg2, bias_tab)


def _outproj_kernel(x_ref, c_ref, a_ref, wc_ref, wa_ref, o_ref):
    o_ref[...] = (x_ref[...]
                  + jnp.dot(c_ref[...], wc_ref[...], preferred_element_type=F32)
                  + jnp.dot(a_ref[...], wa_ref[...], preferred_element_type=F32))


def _outproj(x, conv_out, attn_out, w_out_bf16):
    return pl.pallas_call(
        _outproj_kernel,
        grid=(SEQ // TM_OUT,),
        in_specs=[
            pl.BlockSpec((TM_OUT, D_MODEL), lambda i: (i, 0)),
            pl.BlockSpec((TM_OUT, CONV_C), lambda i: (i, 0)),
            pl.BlockSpec((TM_OUT, ATTN_W), lambda i: (i, 0)),
            pl.BlockSpec((CONV_C, D_MODEL), lambda i: (0, 0)),
            pl.BlockSpec((ATTN_W, D_MODEL), lambda i: (1, 0)),
        ],
        out_specs=pl.BlockSpec((TM_OUT, D_MODEL), lambda i: (i, 0)),
        out_shape=jax.ShapeDtypeStruct((SEQ, D_MODEL), F32),
        compiler_params=_cparams(("arbitrary",)),
        name="outproj",
    )(x, conv_out, attn_out, w_out_bf16, w_out_bf16)


def _split3(a):
    a1 = a.astype(BF16)
    r1 = a - a1.astype(F32)
    a2 = r1.astype(BF16)
    a3 = (r1 - a2.astype(F32)).astype(BF16)
    return a1, a2, a3


def _router_kernel(x_ref, g_ref, wr_ref, br_ref, eid_ref, wts_ref, rank_ref, cnt_ref):
    i = pl.program_id(0)

    @pl.when(i == 0)
    def _():
        cnt_ref[...] = jnp.zeros_like(cnt_ref)

    x = x_ref[...]
    ms = jnp.mean(x * x, axis=-1, keepdims=True)
    hn = x * lax.rsqrt(ms + EPS) * g_ref[...]
    h1, h2, h3 = _split3(hn)
    w1, w2, w3 = _split3(wr_ref[...])
    dn = (((1,), (1,)), ((), ()))
    lt = None
    for wa, ha in ((w1, h1), (w1, h2), (w2, h1), (w2, h2), (w1, h3), (w3, h1)):
        t = lax.dot_general(wa, ha, dn, preferred_element_type=F32)
        lt = t if lt is None else lt + t
    lt = lt + br_ref[:, 0:1]

    row8 = lax.broadcasted_iota(jnp.int32, (8, TM_R), 0)
    gl = jnp.where(row8 < N_GROUPS, lt[0:8], -jnp.inf)
    gmax = jnp.max(gl, axis=0, keepdims=True)
    gidx = jnp.min(jnp.where(gl == gmax, row8, 8), axis=0, keepdims=True)
    gw = 1.0 / jnp.sum(jnp.exp(gl - gmax), axis=0, keepdims=True)

    esel = lt[8:16]
    for g in range(1, N_GROUPS):
        esel = jnp.where(gidx == g, lt[8 + 8 * g:16 + 8 * g], esel)
    v1 = jnp.max(esel, axis=0, keepdims=True)
    i1 = jnp.min(jnp.where(esel == v1, row8, 8), axis=0, keepdims=True)
    rest = jnp.where(row8 == i1, -jnp.inf, esel)
    v2 = jnp.max(rest, axis=0, keepdims=True)
    i2 = jnp.min(jnp.where(rest == v2, row8, 8), axis=0, keepdims=True)
    e21 = jnp.exp(v2 - v1)
    den = 1.0 + e21
    e1 = gidx * E_PER_G + i1
    e2 = gidx * E_PER_G + i2
    eid_ref[0:1, :] = e1
    eid_ref[1:2, :] = e2
    wts_ref[0:1, :] = gw * (1.0 / den)
    wts_ref[1:2, :] = gw * (e21 / den)

    erow = lax.broadcasted_iota(jnp.int32, (N_EXPERTS, TM_R), 0)
    oh1 = erow == e1
    oh2 = erow == e2
    member = jnp.where(oh1 | oh2, 1.0, 0.0)
    ti = lax.broadcasted_iota(jnp.int32, (TM_R, TM_R), 0)
    tj = lax.broadcasted_iota(jnp.int32, (TM_R, TM_R), 1)
    upper = jnp.where(ti < tj, 1.0, 0.0).astype(BF16)
    before = jnp.dot(member.astype(BF16), upper, preferred_element_type=F32)
    pos = before + cnt_ref[:, 0:1]
    rank_ref[0:1, :] = jnp.sum(jnp.where(oh1, pos, 0.0), axis=0, keepdims=True).astype(jnp.int32)
    rank_ref[1:2, :] = jnp.sum(jnp.where(oh2, pos, 0.0), axis=0, keepdims=True).astype(jnp.int32)
    cnt_ref[...] = cnt_ref[...] + jnp.sum(member, axis=1, keepdims=True)


def _router(x1, g2, wr_t, br):
    return pl.pallas_call(
        _router_kernel,
        grid=(SEQ // TM_R,),
        in_specs=[
            pl.BlockSpec((TM_R, D_MODEL), lambda i: (i, 0)),
            pl.BlockSpec((1, D_MODEL), lambda i: (0, 0)),
            pl.BlockSpec((R_ROWS, D_MODEL), lambda i: (0, 0)),
            pl.BlockSpec((R_ROWS, LANES), lambda i: (0, 0)),
        ],
        out_specs=[
            pl.BlockSpec((2, TM_R), lambda i: (0, i)),
            pl.BlockSpec((2, TM_R), lambda i: (0, i)),
            pl.BlockSpec((2, TM_R), lambda i: (0, i)),
            pl.BlockSpec((N_EXPERTS, LANES), lambda i: (0, 0)),
        ],
        out_shape=[
            jax.ShapeDtypeStruct((2, SEQ), jnp.int32),
            jax.ShapeDtypeStruct((2, SEQ), F32),
            jax.ShapeDtypeStruct((2, SEQ), jnp.int32),
            jax.ShapeDtypeStruct((N_EXPERTS, LANES), F32),
        ],
        compiler_params=_cparams(("arbitrary",)),
        name="router",
    )(x1, g2, wr_t, br)


def _moe_kernel(ie_ref, row0_ref, rows_ref, nitems_ref, sorted_ref,
                x_hbm, g_ref, wg_ref, wu_ref, wd_ref, dest_hbm,
                xg_ref, xb_ref, y_ref, gsem, ssem):
    it = pl.program_id(0)
    f = pl.program_id(1)
    nrows = rows_ref[it]
    nitems = nitems_ref[0]
    slot = it % 2
    nsub = (nrows + SUB_E - 1) // SUB_E

    def padded(item):
        return pl.multiple_of(((rows_ref[item] + SUB_E - 1) // SUB_E) * SUB_E, SUB_E)

    def start_gather(item):
        base = row0_ref[item]
        buf = item % 2

        def start(j8, c):
            for k in range(8):
                j = j8 * 8 + k
                tok = sorted_ref[base + j] & (SEQ - 1)
                pltpu.make_async_copy(x_hbm.at[pl.ds(tok, 1), :], xg_ref.at[buf, pl.ds(j, 1), :],
                                      gsem.at[buf]).start()
            return c
        lax.fori_loop(0, padded(item) // 8, start, 0)

    def wait_gather(item):
        got = pl.ds(0, padded(item))
        buf = item % 2
        pltpu.make_async_copy(x_hbm.at[got, :], xg_ref.at[buf, got, :], gsem.at[buf]).wait()

    def scatter_copy(item, j):
        dst = sorted_ref[row0_ref[item] + j]
        buf = item % 2
        return pltpu.make_async_copy(y_ref.at[buf, pl.ds(j, 1), :], dest_hbm.at[pl.ds(dst, 1), :],
                                     ssem.at[buf])

    def start_scatter(item):
        n = rows_ref[item]

        def start8(j8, c):
            for k in range(8):
                scatter_copy(item, j8 * 8 + k).start()
            return c
        lax.fori_loop(0, n // 8, start8, 0)

        def start(j, c):
            scatter_copy(item, j).start()
            return c
        lax.fori_loop((n // 8) * 8, n, start, 0)

    def wait_scatter(item):
        n = rows_ref[item]
        whole = pl.multiple_of((n // 8) * 8, 8)
        buf = item % 2

        @pl.when(whole > 0)
        def _():
            sent = pl.ds(0, whole)
            pltpu.make_async_copy(y_ref.at[buf, sent, :], dest_hbm.at[sent, :], ssem.at[buf]).wait()

        def wait(j, c):
            scatter_copy(item, j).wait()
            return c
        lax.fori_loop(whole, n, wait, 0)

    @pl.when((f == 0) & (nrows > 0))
    def _():
        @pl.when(it == 0)
        def _():
            start_gather(0)

        wait_gather(it)

        def norm(s, c):
            rows = pl.ds(pl.multiple_of(s * SUB_E, SUB_E), SUB_E)
            x = xg_ref[slot, rows, :]
            ms = jnp.mean(x * x, axis=-1, keepdims=True)
            xb_ref[rows, :] = (x * lax.rsqrt(ms + EPS) * g_ref[...]).astype(BF16)
            y_ref[slot, rows, :] = jnp.zeros((SUB_E, D_MODEL), F32)
            return c
        lax.fori_loop(0, nsub, norm, 0)

        @pl.when(it + 1 < nitems)
        def _():
            start_gather(it + 1)

    @pl.when(nrows > 0)
    def _():
        wg = wg_ref[...].astype(BF16)
        wu = wu_ref[...].astype(BF16)
        wd = wd_ref[...].astype(BF16)

        def sub(s, c):
            rows = pl.ds(pl.multiple_of(s * SUB_E, SUB_E), SUB_E)
            xb = xb_ref[rows, :]
            hg = jnp.dot(xb, wg, preferred_element_type=F32)
            hu = jnp.dot(xb, wu, preferred_element_type=F32)
            h = (hg * jax.nn.sigmoid(hg) * hu).astype(BF16)
            y_ref[slot, rows, :] = y_ref[slot, rows, :] + jnp.dot(h, wd, preferred_element_type=F32)
            return c
        lax.fori_loop(0, nsub, sub, 0)

    @pl.when((f == NF_E - 1) & (nrows > 0))
    def _():
        @pl.when(it > 0)
        def _():
            wait_scatter(it - 1)

        start_scatter(it)

        @pl.when(it + 1 >= nitems)
        def _():
            wait_scatter(it)


def _moe(ie, row0, rows, nitems, sorted_i, x1, g2, wg, wu, wd):
    def wmap_cols(i, f, ie_ref, row0_ref, rows_ref, n_ref, s_ref):
        return (ie_ref[i], 0, jnp.where(i < n_ref[0], f, NF_E - 1))

    def wmap_rows(i, f, ie_ref, row0_ref, rows_ref, n_ref, s_ref):
        return (ie_ref[i], jnp.where(i < n_ref[0], f, NF_E - 1), 0)

    grid_spec = pltpu.PrefetchScalarGridSpec(
        num_scalar_prefetch=5,
        grid=(MAX_ITEMS, NF_E),
        in_specs=[
            pl.BlockSpec(memory_space=pl.ANY),
            pl.BlockSpec((1, D_MODEL), lambda i, f, *_: (0, 0)),
            pl.BlockSpec((None, D_MODEL, TF_E), wmap_cols),
            pl.BlockSpec((None, D_MODEL, TF_E), wmap_cols),
            pl.BlockSpec((None, TF_E, D_MODEL), wmap_rows),
        ],
        out_specs=pl.BlockSpec(memory_space=pl.ANY),
        scratch_shapes=[
            pltpu.VMEM((2, TM_E, D_MODEL), F32),
            pltpu.VMEM((TM_E, D_MODEL), BF16),
            pltpu.VMEM((2, TM_E, D_MODEL), F32),
            pltpu.SemaphoreType.DMA((2,)),
            pltpu.SemaphoreType.DMA((2,)),
        ],
    )
    return pl.pallas_call(
        _moe_kernel,
        grid_spec=grid_spec,
        out_shape=jax.ShapeDtypeStruct((N_ASSIGN, D_MODEL), F32),
        compiler_params=_cparams(("arbitrary", "arbitrary")),
        name="moe_experts",
    )(ie, row0, rows, nitems, sorted_i, x1, g2, wg, wu, wd)


def _combine_kernel(x_ref, w_ref, d0_ref, d1_ref, o_ref):
    w = w_ref[...]
    o_ref[...] = x_ref[...] + w[:, 0:1] * d0_ref[...] + w[:, 1:2] * d1_ref[...]


def _combine(x1, wts_t, dest):
    dest3 = dest.reshape(2, SEQ, D_MODEL)
    return pl.pallas_call(
        _combine_kernel,
        grid=(SEQ // TT_COMB,),
        in_specs=[
            pl.BlockSpec((TT_COMB, D_MODEL), lambda i: (i, 0)),
            pl.BlockSpec((TT_COMB, 2), lambda i: (i, 0)),
            pl.BlockSpec((None, TT_COMB, D_MODEL), lambda i: (0, i, 0)),
            pl.BlockSpec((None, TT_COMB, D_MODEL), lambda i: (1, i, 0)),
        ],
        out_specs=pl.BlockSpec((TT_COMB, D_MODEL), lambda i: (i, 0)),
        out_shape=jax.ShapeDtypeStruct((SEQ, D_MODEL), F32),
        compiler_params=_cparams(("arbitrary",)),
        name="moe_combine",
    )(x1, wts_t, dest3, dest3)


def _work_items(eid, rank, counts_f):
    counts = counts_f[:, 0].astype(jnp.int32)
    cum = jnp.cumsum(counts)
    base = cum - counts
    tiles = (counts + TM_E - 1) // TM_E
    tcum = jnp.cumsum(tiles)
    tstart = tcum - tiles
    nitems = tcum[-1]
    ids = jnp.arange(MAX_ITEMS, dtype=jnp.int32)
    ie = jnp.clip(jnp.searchsorted(tcum, ids, side="right"), 0, N_EXPERTS - 1).astype(jnp.int32)
    live = ids < nitems
    ie = jnp.where(live, ie, ie[jnp.maximum(nitems - 1, 0)])
    jt = ids - tstart[ie]
    row0 = jnp.where(live, base[ie] + jt * TM_E, 0)
    rows = jnp.where(live, jnp.clip(counts[ie] - jt * TM_E, 0, TM_E), 0)
    eoh = eid[:, :, None] == jnp.arange(N_EXPERTS, dtype=jnp.int32)
    slot = jnp.sum(jnp.where(eoh, base, 0), axis=-1) + rank
    sorted_i = jnp.zeros((SORTED_LEN,), jnp.int32).at[slot.reshape(-1)].set(
        jnp.arange(N_ASSIGN, dtype=jnp.int32))
    return ie, row0.astype(jnp.int32), rows.astype(jnp.int32), nitems.reshape(1).astype(jnp.int32), sorted_i


def kernel(x, norm1_g, w_in, q_norm_g, k_norm_g, conv_w, conv_b, conv_ln_g, conv_ln_b, rel_bias,
           w_out, norm2_g, w_router_group, b_router_group, w_router_expert, b_router_expert,
           w_gate, w_up, w_down):
    assert x.shape == (1, SEQ, D_MODEL) and w_in.shape[0] == 1
    xs = x[0]
    bias_tab = _attn_bias_tables(rel_bias)
    qg2 = jnp.tile(q_norm_g[0], 2)[None]
    kg2 = jnp.tile(k_norm_g[0], 2)[None]

    proj = _inproj(xs, norm1_g[0][None], w_in[0].astype(BF16))
    conv_out = _conv_mixer(proj, conv_w[0], conv_b[0][None], conv_ln_g[0][None], conv_ln_b[0][None])
    attn_out = _attention(proj, qg2, kg2, bias_tab)
    x1 = _outproj(xs, conv_out, attn_out, w_out[0].astype(BF16))

    wr_t = jnp.concatenate([
        w_router_group[0].T, jnp.zeros((8 - N_GROUPS, D_MODEL), F32),
        jnp.transpose(w_router_expert[0], (0, 2, 1)).reshape(N_EXPERTS, D_MODEL)], axis=0)
    br = jnp.concatenate([b_router_group[0], jnp.zeros((8 - N_GROUPS,), F32),
                          b_router_expert[0].reshape(-1)])
    br = jnp.broadcast_to(br[:, None], (R_ROWS, LANES))
    eid, wts, rank, counts_f = _router(x1, norm2_g[0][None], wr_t, br)

    ie, row0, rows, nitems, sorted_i = _work_items(eid, rank, counts_f)
    dest = _moe(ie, row0, rows, nitems, sorted_i, x1, norm2_g[0][None],
                w_gate[0].reshape(N_EXPERTS, D_MODEL, D_FF),
                w_up[0].reshape(N_EXPERTS, D_MODEL, D_FF),
                w_down[0].reshape(N_EXPERTS, D_FF, D_MODEL))
    out = _combine(x1, wts.T, dest)
    return out[None]
```

```python
import functools
import math

import numpy as np
import jax
import jax.numpy as jnp
from jax import lax
from jax.experimental import pallas as pl
from jax.experimental.pallas import tpu as pltpu

F32 = jnp.float32
BF16 = jnp.bfloat16

D_MODEL = 2048
SEQ = 8192
N_HEADS = 16
HEAD_DIM = 64
ATTN_W = N_HEADS * HEAD_DIM
CONV_C = D_MODEL - ATTN_W
CONV_K = 31
IN_W = 2 * CONV_C + 3 * ATTN_W
PATTERNS = ((128, 1), (512, 4), (2048, 16))
QBLK = 128
NUM_BUCKETS = 32
MAX_DISTANCE = 2048
N_GROUPS = 4
E_PER_G = 8
N_EXPERTS = N_GROUPS * E_PER_G
D_FF = D_MODEL // 2
EPS = 1e-6
NEG_INF = -1e30
LOG2E = math.log2(math.e)

LANES = 128
VMEM_LIMIT = 56 * 1024 * 1024

TM_IN = 1024
TN_IN = 1024
TT_CONV = 512
HALO = 32
R_CONV = 64
R_LN = 16
ATTN_UNROLL = 8
TM_OUT = 512
TM_R = 512
R_ROWS = 8 + N_EXPERTS
TM_E = 768
SUB_E = 256
TF_E = 256
NF_E = D_FF // TF_E
N_ASSIGN = 2 * SEQ
MAX_ITEMS = -(-N_ASSIGN // TM_E) + N_EXPERTS
SORTED_LEN = N_ASSIGN + TM_E
TT_COMB = 512


def _cparams(sem, vmem=VMEM_LIMIT):
    return pltpu.CompilerParams(dimension_semantics=sem, vmem_limit_bytes=vmem)


def _inproj_kernel(x_ref, g_ref, w_ref, o_ref, xn_ref):
    @pl.when(pl.program_id(1) == 0)
    def _():
        x = x_ref[...]
        ms = jnp.mean(x * x, axis=-1, keepdims=True)
        xn_ref[...] = (x * lax.rsqrt(ms + EPS) * g_ref[...]).astype(BF16)

    o_ref[...] = jnp.dot(xn_ref[...], w_ref[...], preferred_element_type=F32)


def _inproj(x, g, w_bf16):
    return pl.pallas_call(
        _inproj_kernel,
        grid=(SEQ // TM_IN, IN_W // TN_IN),
        in_specs=[
            pl.BlockSpec((TM_IN, D_MODEL), lambda i, j: (i, 0)),
            pl.BlockSpec((1, D_MODEL), lambda i, j: (0, 0)),
            pl.BlockSpec((D_MODEL, TN_IN), lambda i, j: (0, j)),
        ],
        out_specs=pl.BlockSpec((TM_IN, TN_IN), lambda i, j: (i, j)),
        out_shape=jax.ShapeDtypeStruct((SEQ, IN_W), F32),
        scratch_shapes=[pltpu.VMEM((TM_IN, D_MODEL), BF16)],
        compiler_params=_cparams(("arbitrary", "arbitrary")),
        name="inproj",
    )(x, g, w_bf16)


def _conv_kernel(val_ref, gate_ref, hval_ref, hgate_ref, cw_ref, cb_ref, lg_ref, lb_ref,
                 o_ref, ubuf, zbuf, ybuf):
    i = pl.program_id(0)
    u = val_ref[...] * jax.nn.sigmoid(gate_ref[...])
    hu = hval_ref[...] * jax.nn.sigmoid(hgate_ref[...])
    hu = jnp.where(i > 0, hu, 0.0)
    for c in range(CONV_C // LANES):
        cols = slice(c * LANES, (c + 1) * LANES)
        ubuf[c, 0:HALO, :] = hu[:, cols]
        ubuf[c, HALO:HALO + TT_CONV, :] = u[:, cols]

    n_a = -(-CONV_K // 8)
    assert HALO == 8 * n_a
    for c in range(CONV_C // LANES):
        cols = slice(c * LANES, (c + 1) * LANES)

        def taps(r, carry, c=c, cols=cols):
            base = pl.multiple_of(r * R_CONV, R_CONV)
            win = ubuf[c, pl.ds(base, R_CONV + HALO), :]
            for b in range(8):
                z = None
                for a in range(n_a):
                    s = 8 * a + b
                    if s >= CONV_K:
                        continue
                    lo = HALO - 8 - 8 * a
                    t = cw_ref[CONV_K - 1 - s:CONV_K - s, cols] * win[lo:lo + R_CONV + 8, :]
                    z = t if z is None else z + t
                zbuf[b, pl.ds(0, R_CONV + 8, stride=2), :] = z
            acc = None
            for b in range(8):
                t = zbuf[b, pl.ds(2 * (8 - b), R_CONV, stride=2), :]
                acc = t if acc is None else acc + t
            ybuf[pl.ds(base, R_CONV), cols] = acc
            return carry

        lax.fori_loop(0, TT_CONV // R_CONV, taps, 0)

    def norm(r, carry):
        rows = pl.ds(pl.multiple_of(r * R_LN, R_LN), R_LN)
        acc = ybuf[rows, :] + cb_ref[...]
        mu = jnp.mean(acc, axis=-1, keepdims=True)
        xc = acc - mu
        var = jnp.mean(xc * xc, axis=-1, keepdims=True)
        y = xc * lax.rsqrt(var + EPS) * lg_ref[...] + lb_ref[...]
        o_ref[rows, :] = (y * jax.nn.sigmoid(y)).astype(BF16)
        return carry

    lax.fori_loop(0, TT_CONV // R_LN, norm, 0, unroll=4)


def _conv_mixer(proj, cw, cb, lg, lb):
    hb = TT_CONV // HALO
    return pl.pallas_call(
        _conv_kernel,
        grid=(SEQ // TT_CONV,),
        in_specs=[
            pl.BlockSpec((TT_CONV, CONV_C), lambda i: (i, 0)),
            pl.BlockSpec((TT_CONV, CONV_C), lambda i: (i, 1)),
            pl.BlockSpec((HALO, CONV_C), lambda i: (jnp.maximum(i * hb - 1, 0), 0)),
            pl.BlockSpec((HALO, CONV_C), lambda i: (jnp.maximum(i * hb - 1, 0), 1)),
            pl.BlockSpec((CONV_K, CONV_C), lambda i: (0, 0)),
            pl.BlockSpec((1, CONV_C), lambda i: (0, 0)),
            pl.BlockSpec((1, CONV_C), lambda i: (0, 0)),
            pl.BlockSpec((1, CONV_C), lambda i: (0, 0)),
        ],
        out_specs=pl.BlockSpec((TT_CONV, CONV_C), lambda i: (i, 0)),
        out_shape=jax.ShapeDtypeStruct((SEQ, CONV_C), BF16),
        scratch_shapes=[pltpu.VMEM((CONV_C // LANES, HALO + TT_CONV, LANES), F32),
                        pltpu.VMEM((8, 2 * (R_CONV + 8), LANES), F32),
                        pltpu.VMEM((TT_CONV, CONV_C), F32)],
        compiler_params=_cparams(("arbitrary",)),
        name="conv_mixer",
    )(proj, proj, proj, proj, cw, cb, lg, lb)


def _t5_bucket_np(dist):
    max_exact = NUM_BUCKETS // 2
    nf = np.maximum(dist, 1).astype(np.float32)
    large = max_exact + (np.log(nf / np.float32(max_exact)) / np.float32(math.log(MAX_DISTANCE / max_exact))
                         * np.float32(NUM_BUCKETS - max_exact)).astype(np.int32)
    large = np.minimum(large, NUM_BUCKETS - 1)
    return np.where(dist < max_exact, dist, large)


def _attn_bias_tables(rel_bias):
    kj = np.arange(2 * QBLK)[None, None, :]
    period = 3 * QBLK
    tables = []
    for window, dil in PATTERNS:
        span = window // dil
        assert span <= QBLK
        bucket = _t5_bucket_np(np.arange(span + 1) * dil)
        onehot = np.eye(NUM_BUCKETS, dtype=np.float32)[bucket]
        vec = jnp.einsum("rb,bh->hr", onehot, rel_bias.astype(F32),
                         precision=lax.Precision.HIGHEST)
        diag = jnp.full((N_HEADS, period), NEG_INF, F32)
        diag = diag.at[:, 2 * QBLK - 1 - span:2 * QBLK].set(vec[:, ::-1])
        skew = jnp.tile(diag, (1, QBLK))[:, :QBLK * (period - 1)].reshape(N_HEADS, QBLK, period - 1)
        full = skew[:, :, QBLK - 1:3 * QBLK - 1]
        first = jnp.where(kj >= QBLK, full, NEG_INF)
        both = jnp.stack([full, first])
        tables.append(both.reshape(2, N_HEADS // 2, 2 * QBLK, 2 * QBLK))
    return jnp.stack(tables) * LOG2E


def _attn_kernel(q_ref, k_ref, v_ref, qg_ref, kg_ref, bias_ref, o_ref,
                 qn_ref, kn_ref, acc_ref, m_ref, l_ref):
    lane = lax.broadcasted_iota(jnp.int32, (QBLK, LANES), 1)
    head_a = lane < HEAD_DIM
    ri = lax.broadcasted_iota(jnp.int32, (LANES, LANES), 0) // HEAD_DIM
    ci = lax.broadcasted_iota(jnp.int32, (LANES, LANES), 1) // HEAD_DIM
    seg = (ri == ci).astype(BF16)

    def head_rms(x, g):
        sq = x * x
        hi = sq.astype(BF16)
        lo = (sq - hi.astype(F32)).astype(BF16)
        ss = (jnp.dot(hi, seg, preferred_element_type=F32)
              + jnp.dot(lo, seg, preferred_element_type=F32))
        return x * lax.rsqrt(ss * (1.0 / HEAD_DIM) + EPS) * g

    NCH = 512

    def norm_body(c, carry):
        rows = pl.ds(pl.multiple_of(c * NCH, NCH), NCH)
        qn_ref[rows, :] = head_rms(q_ref[rows, :], qg_ref[...]) * (LOG2E / math.sqrt(HEAD_DIM))
        kn_ref[rows, :] = head_rms(k_ref[rows, :], kg_ref[...])
        return carry

    lax.fori_loop(0, SEQ // NCH, norm_body, 0)

    order = sorted(range(len(PATTERNS)), key=lambda i: -PATTERNS[i][1])
    assert PATTERNS[order[-1]][1] == 1
    for step, p in enumerate(order):
        dil = PATTERNS[p][1]
        nb = SEQ // (dil * QBLK)
        is_first = step == 0
        is_last = step == len(order) - 1

        def unit(u, carry, p=p, dil=dil, nb=nb, is_first=is_first, is_last=is_last):
            r = u // nb
            n = u - r * nb
            cur = n * (QBLK * dil) + r
            prev = jnp.maximum(n - 1, 0) * (QBLK * dil) + r

            def rows(start):
                if dil == 1:
                    return pl.ds(start, QBLK)
                return pl.ds(start, QBLK, stride=dil)

            q = qn_ref[rows(cur), :]
            q2 = jnp.concatenate([jnp.where(head_a, q, 0.0), jnp.where(head_a, 0.0, q)],
                                 axis=0).astype(BF16)
            k2 = jnp.concatenate([kn_ref[rows(prev), :], kn_ref[rows(cur), :]], axis=0).astype(BF16)
            v2 = jnp.concatenate([v_ref[rows(prev), :], v_ref[rows(cur), :]], axis=0).astype(BF16)
            s = lax.dot_general(q2, k2, (((1,), (1,)), ((), ())), preferred_element_type=F32)
            first = jnp.where(n == 0, 1, 0)
            s = s + bias_ref[p, first]
            m = jnp.max(s, axis=-1, keepdims=True)
            e = jnp.exp2(s - m)
            l = jnp.sum(e, axis=-1, keepdims=True)
            pv = jnp.dot(e.astype(BF16), v2, preferred_element_type=F32)
            o_new = jnp.where(head_a, pv[:QBLK], pv[QBLK:])
            m_new = jnp.where(head_a, m[:QBLK], m[QBLK:])
            l_new = jnp.where(head_a, l[:QBLK], l[QBLK:])
            if is_first:
                acc_ref[rows(cur), :] = o_new
                m_ref[rows(cur), :] = m_new
                l_ref[rows(cur), :] = l_new
            else:
                m_old = m_ref[rows(cur), :]
                m_tot = jnp.maximum(m_old, m_new)
                a = jnp.exp2(m_old - m_tot)
                b = jnp.exp2(m_new - m_tot)
                acc = acc_ref[rows(cur), :] * a + o_new * b
                den = l_ref[rows(cur), :] * a + l_new * b
                if is_last:
                    o_ref[pl.ds(pl.multiple_of(cur, QBLK), QBLK), :] = (acc / den).astype(BF16)
                else:
                    acc_ref[rows(cur), :] = acc
                    l_ref[rows(cur), :] = den
                    m_ref[rows(cur), :] = m_tot
            return carry

        lax.fori_loop(0, SEQ // QBLK, unit, 0, unroll=ATTN_UNROLL)


def _attention(proj, qg2, kg2, bias_tab):
    qoff = 2 * CONV_C // LANES
    koff = qoff + ATTN_W // LANES
    voff = koff + ATTN_W // LANES
    return pl.pallas_call(
        _attn_kernel,
        grid=(N_HEADS // 2,),
        in_specs=[
            pl.BlockSpec((SEQ, LANES), lambda h: (0, qoff + h)),
            pl.BlockSpec((SEQ, LANES), lambda h: (0, koff + h)),
            pl.BlockSpec((SEQ, LANES), lambda h: (0, voff + h)),
            pl.BlockSpec((1, LANES), lambda h: (0, 0)),
            pl.BlockSpec((1, LANES), lambda h: (0, 0)),
            pl.BlockSpec((len(PATTERNS), 2, None, 2 * QBLK, 2 * QBLK), lambda h: (0, 0, h, 0, 0)),
        ],
        out_specs=pl.BlockSpec((SEQ, LANES), lambda h: (0, h)),
        out_shape=jax.ShapeDtypeStruct((SEQ, ATTN_W), BF16),
        scratch_shapes=[pltpu.VMEM((SEQ, LANES), F32) for _ in range(5)],
        compiler_params=_cparams(("arbitrary",)),
        name="dilated_attn",
    )(proj, proj, proj, qg2, kg2, bias_tab)


def _outproj_kernel(x_ref, c_ref, a_ref, wc_ref, wa_ref, o_ref):
    o_ref[...] = (x_ref[...]
                  + jnp.dot(c_ref[...], wc_ref[...], preferred_element_type=F32)
                  + jnp.dot(a_ref[...], wa_ref[...], preferred_element_type=F32))


def _outproj(x, conv_out, attn_out, w_out_bf16):
    return pl.pallas_call(
        _outproj_kernel,
        grid=(SEQ // TM_OUT,),
        in_specs=[
            pl.BlockSpec((TM_OUT, D_MODEL), lambda i: (i, 0)),
            pl.BlockSpec((TM_OUT, CONV_C), lambda i: (i, 0)),
            pl.BlockSpec((TM_OUT, ATTN_W), lambda i: (i, 0)),
            pl.BlockSpec((CONV_C, D_MODEL), lambda i: (0, 0)),
            pl.BlockSpec((ATTN_W, D_MODEL), lambda i: (1, 0)),
        ],
        out_specs=pl.BlockSpec((TM_OUT, D_MODEL), lambda i: (i, 0)),
        out_shape=jax.ShapeDtypeStruct((SEQ, D_MODEL), F32),
        compiler_params=_cparams(("arbitrary",)),
        name="outproj",
    )(x, conv_out, attn_out, w_out_bf16, w_out_bf16)


def _split3(a):
    a1 = a.astype(BF16)
    r1 = a - a1.astype(F32)
    a2 = r1.astype(BF16)
    a3 = (r1 - a2.astype(F32)).astype(BF16)
    return a1, a2, a3


def _router_kernel(x_ref, g_ref, wr_ref, br_ref, eid_ref, wts_ref, rank_ref, cnt_ref):
    i = pl.program_id(0)

    @pl.when(i == 0)
    def _():
        cnt_ref[...] = jnp.zeros_like(cnt_ref)

    x = x_ref[...]
    ms = jnp.mean(x * x, axis=-1, keepdims=True)
    hn = x * lax.rsqrt(ms + EPS) * g_ref[...]
    h1, h2, h3 = _split3(hn)
    w1, w2, w3 = _split3(wr_ref[...])
    dn = (((1,), (1,)), ((), ()))
    lt = None
    for wa, ha in ((w1, h1), (w1, h2), (w2, h1), (w2, h2), (w1, h3), (w3, h1)):
        t = lax.dot_general(wa, ha, dn, preferred_element_type=F32)
        lt = t if lt is None else lt + t
    lt = lt + br_ref[:, 0:1]

    row8 = lax.broadcasted_iota(jnp.int32, (8, TM_R), 0)
    gl = jnp.where(row8 < N_GROUPS, lt[0:8], -jnp.inf)
    gmax = jnp.max(gl, axis=0, keepdims=True)
    gidx = jnp.min(jnp.where(gl == gmax, row8, 8), axis=0, keepdims=True)
    gw = 1.0 / jnp.sum(jnp.exp(gl - gmax), axis=0, keepdims=True)

    esel = lt[8:16]
    for g in range(1, N_GROUPS):
        esel = jnp.where(gidx == g, lt[8 + 8 * g:16 + 8 * g], esel)
    v1 = jnp.max(esel, axis=0, keepdims=True)
    i1 = jnp.min(jnp.where(esel == v1, row8, 8), axis=0, keepdims=True)
    rest = jnp.where(row8 == i1, -jnp.inf, esel)
    v2 = jnp.max(rest, axis=0, keepdims=True)
    i2 = jnp.min(jnp.where(rest == v2, row8, 8), axis=0, keepdims=True)
    e21 = jnp.exp(v2 - v1)
    den = 1.0 + e21
    e1 = gidx * E_PER_G + i1
    e2 = gidx * E_PER_G + i2
    eid_ref[0:1, :] = e1
    eid_ref[1:2, :] = e2
    wts_ref[0:1, :] = gw * (1.0 / den)
    wts_ref[1:2, :] = gw * (e21 / den)

    erow = lax.broadcasted_iota(jnp.int32, (N_EXPERTS, TM_R), 0)
    oh1 = erow == e1
    oh2 = erow == e2
    member = jnp.where(oh1 | oh2, 1.0, 0.0)
    ti = lax.broadcasted_iota(jnp.int32, (TM_R, TM_R), 0)
    tj = lax.broadcasted_iota(jnp.int32, (TM_R, TM_R), 1)
    upper = jnp.where(ti < tj, 1.0, 0.0).astype(BF16)
    before = jnp.dot(member.astype(BF16), upper, preferred_element_type=F32)
    pos = before + cnt_ref[:, 0:1]
    rank_ref[0:1, :] = jnp.sum(jnp.where(oh1, pos, 0.0), axis=0, keepdims=True).astype(jnp.int32)
    rank_ref[1:2, :] = jnp.sum(jnp.where(oh2, pos, 0.0), axis=0, keepdims=True).astype(jnp.int32)
    cnt_ref[...] = cnt_ref[...] + jnp.sum(member, axis=1, keepdims=True)


def _router(x1, g2, wr_t, br):
    return pl.pallas_call(
        _router_kernel,
        grid=(SEQ // TM_R,),
        in_specs=[
            pl.BlockSpec((TM_R, D_MODEL), lambda i: (i, 0)),
            pl.BlockSpec((1, D_MODEL), lambda i: (0, 0)),
            pl.BlockSpec((R_ROWS, D_MODEL), lambda i: (0, 0)),
            pl.BlockSpec((R_ROWS, LANES), lambda i: (0, 0)),
        ],
        out_specs=[
            pl.BlockSpec((2, TM_R), lambda i: (0, i)),
            pl.BlockSpec((2, TM_R), lambda i: (0, i)),
            pl.BlockSpec((2, TM_R), lambda i: (0, i)),
            pl.BlockSpec((N_EXPERTS, LANES), lambda i: (0, 0)),
        ],
        out_shape=[
            jax.ShapeDtypeStruct((2, SEQ), jnp.int32),
            jax.ShapeDtypeStruct((2, SEQ), F32),
            jax.ShapeDtypeStruct((2, SEQ), jnp.int32),
            jax.ShapeDtypeStruct((N_EXPERTS, LANES), F32),
        ],
        compiler_params=_cparams(("arbitrary",)),
        name="router",
    )(x1, g2, wr_t, br)


def _moe_kernel(ie_ref, row0_ref, rows_ref, nitems_ref, sorted_ref,
                x_hbm, g_ref, wg_ref, wu_ref, wd_ref, dest_hbm,
                xg_ref, xb_ref, y_ref, gsem, ssem):
    it = pl.program_id(0)
    f = pl.program_id(1)
    nrows = rows_ref[it]
    nitems = nitems_ref[0]
    slot = it % 2
    half = SUB_E // 2

    def padded(item):
        return pl.multiple_of(((rows_ref[item] + half - 1) // half) * half, half)

    nhalf = padded(it) // half
    nsub = nhalf // 2

    def start_gather(item):
        base = row0_ref[item]
        buf = item % 2

        def start(j8, c):
            for k in range(8):
                j = j8 * 8 + k
                tok = sorted_ref[base + j] & (SEQ - 1)
                pltpu.make_async_copy(x_hbm.at[pl.ds(tok, 1), :], xg_ref.at[buf, pl.ds(j, 1), :],
                                      gsem.at[buf]).start()
            return c
        lax.fori_loop(0, padded(item) // 8, start, 0)

    def wait_gather(item):
        got = pl.ds(0, padded(item))
        buf = item % 2
        pltpu.make_async_copy(x_hbm.at[got, :], xg_ref.at[buf, got, :], gsem.at[buf]).wait()

    def scatter_copy(item, j):
        dst = sorted_ref[row0_ref[item] + j]
        buf = item % 2
        return pltpu.make_async_copy(y_ref.at[buf, pl.ds(j, 1), :], dest_hbm.at[pl.ds(dst, 1), :],
                                     ssem.at[buf])

    def start_scatter(item):
        n = rows_ref[item]

        def start8(j8, c):
            for k in range(8):
                scatter_copy(item, j8 * 8 + k).start()
            return c
        lax.fori_loop(0, n // 8, start8, 0)

        def start(j, c):
            scatter_copy(item, j).start()
            return c
        lax.fori_loop((n // 8) * 8, n, start, 0)

    def wait_scatter(item):
        n = rows_ref[item]
        whole = pl.multiple_of((n // 8) * 8, 8)
        buf = item % 2

        @pl.when(whole > 0)
        def _():
            sent = pl.ds(0, whole)
            pltpu.make_async_copy(y_ref.at[buf, sent, :], dest_hbm.at[sent, :], ssem.at[buf]).wait()

        def wait(j, c):
            scatter_copy(item, j).wait()
            return c
        lax.fori_loop(whole, n, wait, 0)

    @pl.when((f == 0) & (nrows > 0))
    def _():
        @pl.when(it == 0)
        def _():
            start_gather(0)

        wait_gather(it)

        def norm(s, c):
            rows = pl.ds(pl.multiple_of(s * half, half), half)
            x = xg_ref[slot, rows, :]
            ms = jnp.mean(x * x, axis=-1, keepdims=True)
            xb_ref[rows, :] = (x * lax.rsqrt(ms + EPS) * g_ref[...]).astype(BF16)
            y_ref[slot, rows, :] = jnp.zeros((half, D_MODEL), F32)
            return c
        lax.fori_loop(0, nhalf, norm, 0)

        @pl.when(it + 1 < nitems)
        def _():
            start_gather(it + 1)

    @pl.when(nrows > 0)
    def _():
        wg = wg_ref[...].astype(BF16)
        wu = wu_ref[...].astype(BF16)
        wd = wd_ref[...].astype(BF16)

        def block(rows):
            xb = xb_ref[rows, :]
            hg = jnp.dot(xb, wg, preferred_element_type=F32)
            hu = jnp.dot(xb, wu, preferred_element_type=F32)
            h = (hg * jax.nn.sigmoid(hg) * hu).astype(BF16)
            y_ref[slot, rows, :] = y_ref[slot, rows, :] + jnp.dot(h, wd, preferred_element_type=F32)

        def sub(s, c):
            block(pl.ds(pl.multiple_of(s * SUB_E, SUB_E), SUB_E))
            return c
        lax.fori_loop(0, nsub, sub, 0)

        @pl.when(nhalf % 2 == 1)
        def _():
            block(pl.ds(pl.multiple_of(nsub * SUB_E, SUB_E), half))

    @pl.when((f == NF_E - 1) & (nrows > 0))
    def _():
        @pl.when(it > 0)
        def _():
            wait_scatter(it - 1)

        start_scatter(it)

        @pl.when(it + 1 >= nitems)
        def _():
            wait_scatter(it)


def _moe(ie, row0, rows, nitems, sorted_i, x1, g2, wg, wu, wd):
    def wmap_cols(i, f, ie_ref, row0_ref, rows_ref, n_ref, s_ref):
        return (ie_ref[i], 0, jnp.where(i < n_ref[0], f, NF_E - 1))

    def wmap_rows(i, f, ie_ref, row0_ref, rows_ref, n_ref, s_ref):
        return (ie_ref[i], jnp.where(i < n_ref[0], f, NF_E - 1), 0)

    grid_spec = pltpu.PrefetchScalarGridSpec(
        num_scalar_prefetch=5,
        grid=(nitems[0], NF_E),
        in_specs=[
            pl.BlockSpec(memory_space=pl.ANY),
            pl.BlockSpec((1, D_MODEL), lambda i, f, *_: (0, 0)),
            pl.BlockSpec((None, D_MODEL, TF_E), wmap_cols),
            pl.BlockSpec((None, D_MODEL, TF_E), wmap_cols),
            pl.BlockSpec((None, TF_E, D_MODEL), wmap_rows),
        ],
        out_specs=pl.BlockSpec(memory_space=pl.ANY),
        scratch_shapes=[
            pltpu.VMEM((2, TM_E, D_MODEL), F32),
            pltpu.VMEM((TM_E, D_MODEL), BF16),
            pltpu.VMEM((2, TM_E, D_MODEL), F32),
            pltpu.SemaphoreType.DMA((2,)),
            pltpu.SemaphoreType.DMA((2,)),
        ],
    )
    return pl.pallas_call(
        _moe_kernel,
        grid_spec=grid_spec,
        out_shape=jax.ShapeDtypeStruct((N_ASSIGN, D_MODEL), F32),
        compiler_params=_cparams(("arbitrary", "arbitrary")),
        name="moe_experts",
    )(ie, row0, rows, nitems, sorted_i, x1, g2, wg, wu, wd)


def _combine_kernel(x_ref, w_ref, d0_ref, d1_ref, o_ref):
    w = w_ref[...]
    o_ref[...] = x_ref[...] + w[:, 0:1] * d0_ref[...] + w[:, 1:2] * d1_ref[...]


def _combine(x1, wts_t, dest):
    dest3 = dest.reshape(2, SEQ, D_MODEL)
    return pl.pallas_call(
        _combine_kernel,
        grid=(SEQ // TT_COMB,),
        in_specs=[
            pl.BlockSpec((TT_COMB, D_MODEL), lambda i: (i, 0)),
            pl.BlockSpec((TT_COMB, 2), lambda i: (i, 0)),
            pl.BlockSpec((None, TT_COMB, D_MODEL), lambda i: (0, i, 0)),
            pl.BlockSpec((None, TT_COMB, D_MODEL), lambda i: (1, i, 0)),
        ],
        out_specs=pl.BlockSpec((TT_COMB, D_MODEL), lambda i: (i, 0)),
        out_shape=jax.ShapeDtypeStruct((SEQ, D_MODEL), F32),
        compiler_params=_cparams(("arbitrary",)),
        name="moe_combine",
    )(x1, wts_t, dest3, dest3)


def _work_items(eid, rank, counts_f):
    counts = counts_f[:, 0].astype(jnp.int32)
    cum = jnp.cumsum(counts)
    base = cum - counts
    tiles = (counts + TM_E - 1) // TM_E
    tcum = jnp.cumsum(tiles)
    tstart = tcum - tiles
    nitems = tcum[-1]
    ids = jnp.arange(MAX_ITEMS, dtype=jnp.int32)
    ie = jnp.clip(jnp.searchsorted(tcum, ids, side="right"), 0, N_EXPERTS - 1).astype(jnp.int32)
    live = ids < nitems
    ie = jnp.where(live, ie, ie[jnp.maximum(nitems - 1, 0)])
    jt = ids - tstart[ie]
    row0 = jnp.where(live, base[ie] + jt * TM_E, 0)
    rows = jnp.where(live, jnp.clip(counts[ie] - jt * TM_E, 0, TM_E), 0)
    eoh = eid[:, :, None] == jnp.arange(N_EXPERTS, dtype=jnp.int32)
    slot = jnp.sum(jnp.where(eoh, base, 0), axis=-1) + rank
    sorted_i = jnp.zeros((SORTED_LEN,), jnp.int32).at[slot.reshape(-1)].set(
        jnp.arange(N_ASSIGN, dtype=jnp.int32))
    return ie, row0.astype(jnp.int32), rows.astype(jnp.int32), nitems.reshape(1).astype(jnp.int32), sorted_i


def kernel(x, norm1_g, w_in, q_norm_g, k_norm_g, conv_w, conv_b, conv_ln_g, conv_ln_b, rel_bias,
           w_out, norm2_g, w_router_group, b_router_group, w_router_expert, b_router_expert,
           w_gate, w_up, w_down):
    assert x.shape == (1, SEQ, D_MODEL) and w_in.shape[0] == 1
    xs = x[0]
    bias_tab = _attn_bias_tables(rel_bias)
    qg2 = jnp.tile(q_norm_g[0], 2)[None]
    kg2 = jnp.tile(k_norm_g[0], 2)[None]

    proj = _inproj(xs, norm1_g[0][None], w_in[0].astype(BF16))
    conv_out = _conv_mixer(proj, conv_w[0], conv_b[0][None], conv_ln_g[0][None], conv_ln_b[0][None])
    attn_out = _attention(proj, qg2, kg2, bias_tab)
    x1 = _outproj(xs, conv_out, attn_out, w_out[0].astype(BF16))

    wr_t = jnp.concatenate([
        w_router_group[0].T, jnp.zeros((8 - N_GROUPS, D_MODEL), F32),
        jnp.transpose(w_router_expert[0], (0, 2, 1)).reshape(N_EXPERTS, D_MODEL)], axis=0)
    br = jnp.concatenate([b_router_group[0], jnp.zeros((8 - N_GROUPS,), F32),
                          b_router_expert[0].reshape(-1)])
    br = jnp.broadcast_to(br[:, None], (R_ROWS, LANES))
    eid, wts, rank, counts_f = _router(x1, norm2_g[0][None], wr_t, br)

    ie, row0, rows, nitems, sorted_i = _work_items(eid, rank, counts_f)
    dest = _moe(ie, row0, rows, nitems, sorted_i, x1, norm2_g[0][None],
                w_gate[0].reshape(N_EXPERTS, D_MODEL, D_FF),
                w_up[0].reshape(N_EXPERTS, D_MODEL, D_FF),
                w_down[0].reshape(N_EXPERTS, D_FF, D_MODEL))
    out = _combine(x1, wts.T, dest)
    return out[None]
```

```python
import functools
import math

import numpy as np
import jax
import jax.numpy as jnp
from jax import lax
from jax.experimental import pallas as pl
from jax.experimental.pallas import tpu as pltpu

F32 = jnp.float32
BF16 = jnp.bfloat16

D_MODEL = 2048
SEQ = 8192
N_HEADS = 16
HEAD_DIM = 64
ATTN_W = N_HEADS * HEAD_DIM
CONV_C = D_MODEL - ATTN_W
CONV_K = 31
IN_W = 2 * CONV_C + 3 * ATTN_W
PATTERNS = ((128, 1), (512, 4), (2048, 16))
QBLK = 128
NUM_BUCKETS = 32
MAX_DISTANCE = 2048
N_GROUPS = 4
E_PER_G = 8
N_EXPERTS = N_GROUPS * E_PER_G
D_FF = D_MODEL // 2
EPS = 1e-6
NEG_INF = -1e30
LOG2E = math.log2(math.e)

LANES = 128
VMEM_LIMIT = 56 * 1024 * 1024

TM_IN = 1024
TN_IN = 1024
TT_CONV = 512
HALO = 32
R_CONV = 64
R_LN = 16
ATTN_UNROLL = 8
TM_OUT = 512
TM_R = 512
R_ROWS = 8 + N_EXPERTS
TM_E = 768
SUB_E = 256
TF_E = 256
NF_E = D_FF // TF_E
N_ASSIGN = 2 * SEQ
MAX_ITEMS = -(-N_ASSIGN // TM_E) + N_EXPERTS
SORTED_LEN = N_ASSIGN + TM_E
XS_TAIL = 256 + TM_E
XS_ROWS = N_ASSIGN + XS_TAIL
ZROWS = 128
TT_D = 512
TT_G = 256
TT_COMB = 512


def _cparams(sem, vmem=VMEM_LIMIT):
    return pltpu.CompilerParams(dimension_semantics=sem, vmem_limit_bytes=vmem)


def _inproj_kernel(x_ref, g_ref, w_ref, o_ref, xn_ref):
    @pl.when(pl.program_id(1) == 0)
    def _():
        x = x_ref[...]
        ms = jnp.mean(x * x, axis=-1, keepdims=True)
        xn_ref[...] = (x * lax.rsqrt(ms + EPS) * g_ref[...]).astype(BF16)

    o_ref[...] = jnp.dot(xn_ref[...], w_ref[...], preferred_element_type=F32)


def _inproj(x, g, w_bf16):
    return pl.pallas_call(
        _inproj_kernel,
        grid=(SEQ // TM_IN, IN_W // TN_IN),
        in_specs=[
            pl.BlockSpec((TM_IN, D_MODEL), lambda i, j: (i, 0)),
            pl.BlockSpec((1, D_MODEL), lambda i, j: (0, 0)),
            pl.BlockSpec((D_MODEL, TN_IN), lambda i, j: (0, j)),
        ],
        out_specs=pl.BlockSpec((TM_IN, TN_IN), lambda i, j: (i, j)),
        out_shape=jax.ShapeDtypeStruct((SEQ, IN_W), F32),
        scratch_shapes=[pltpu.VMEM((TM_IN, D_MODEL), BF16)],
        compiler_params=_cparams(("arbitrary", "arbitrary")),
        name="inproj",
    )(x, g, w_bf16)


def _conv_kernel(val_ref, gate_ref, hval_ref, hgate_ref, cw_ref, cb_ref, lg_ref, lb_ref,
                 o_ref, ubuf, zbuf, ybuf):
    i = pl.program_id(0)
    u = val_ref[...] * jax.nn.sigmoid(gate_ref[...])
    hu = hval_ref[...] * jax.nn.sigmoid(hgate_ref[...])
    hu = jnp.where(i > 0, hu, 0.0)
    for c in range(CONV_C // LANES):
        cols = slice(c * LANES, (c + 1) * LANES)
        ubuf[c, 0:HALO, :] = hu[:, cols]
        ubuf[c, HALO:HALO + TT_CONV, :] = u[:, cols]

    n_a = -(-CONV_K // 8)
    assert HALO == 8 * n_a
    for c in range(CONV_C // LANES):
        cols = slice(c * LANES, (c + 1) * LANES)

        def taps(r, carry, c=c, cols=cols):
            base = pl.multiple_of(r * R_CONV, R_CONV)
            win = ubuf[c, pl.ds(base, R_CONV + HALO), :]
            for b in range(8):
                z = None
                for a in range(n_a):
                    s = 8 * a + b
                    if s >= CONV_K:
                        continue
                    lo = HALO - 8 - 8 * a
                    t = cw_ref[CONV_K - 1 - s:CONV_K - s, cols] * win[lo:lo + R_CONV + 8, :]
                    z = t if z is None else z + t
                zbuf[b, pl.ds(0, R_CONV + 8, stride=2), :] = z
            acc = None
            for b in range(8):
                t = zbuf[b, pl.ds(2 * (8 - b), R_CONV, stride=2), :]
                acc = t if acc is None else acc + t
            ybuf[pl.ds(base, R_CONV), cols] = acc
            return carry

        lax.fori_loop(0, TT_CONV // R_CONV, taps, 0)

    def norm(r, carry):
        rows = pl.ds(pl.multiple_of(r * R_LN, R_LN), R_LN)
        acc = ybuf[rows, :] + cb_ref[...]
        mu = jnp.mean(acc, axis=-1, keepdims=True)
        xc = acc - mu
        var = jnp.mean(xc * xc, axis=-1, keepdims=True)
        y = xc * lax.rsqrt(var + EPS) * lg_ref[...] + lb_ref[...]
        o_ref[rows, :] = (y * jax.nn.sigmoid(y)).astype(BF16)
        return carry

    lax.fori_loop(0, TT_CONV // R_LN, norm, 0, unroll=4)


def _conv_mixer(proj, cw, cb, lg, lb):
    hb = TT_CONV // HALO
    return pl.pallas_call(
        _conv_kernel,
        grid=(SEQ // TT_CONV,),
        in_specs=[
            pl.BlockSpec((TT_CONV, CONV_C), lambda i: (i, 0)),
            pl.BlockSpec((TT_CONV, CONV_C), lambda i: (i, 1)),
            pl.BlockSpec((HALO, CONV_C), lambda i: (jnp.maximum(i * hb - 1, 0), 0)),
            pl.BlockSpec((HALO, CONV_C), lambda i: (jnp.maximum(i * hb - 1, 0), 1)),
            pl.BlockSpec((CONV_K, CONV_C), lambda i: (0, 0)),
            pl.BlockSpec((1, CONV_C), lambda i: (0, 0)),
            pl.BlockSpec((1, CONV_C), lambda i: (0, 0)),
            pl.BlockSpec((1, CONV_C), lambda i: (0, 0)),
        ],
        out_specs=pl.BlockSpec((TT_CONV, CONV_C), lambda i: (i, 0)),
        out_shape=jax.ShapeDtypeStruct((SEQ, CONV_C), BF16),
        scratch_shapes=[pltpu.VMEM((CONV_C // LANES, HALO + TT_CONV, LANES), F32),
                        pltpu.VMEM((8, 2 * (R_CONV + 8), LANES), F32),
                        pltpu.VMEM((TT_CONV, CONV_C), F32)],
        compiler_params=_cparams(("arbitrary",)),
        name="conv_mixer",
    )(proj, proj, proj, proj, cw, cb, lg, lb)


def _t5_bucket_np(dist):
    max_exact = NUM_BUCKETS // 2
    nf = np.maximum(dist, 1).astype(np.float32)
    large = max_exact + (np.log(nf / np.float32(max_exact)) / np.float32(math.log(MAX_DISTANCE / max_exact))
                         * np.float32(NUM_BUCKETS - max_exact)).astype(np.int32)
    large = np.minimum(large, NUM_BUCKETS - 1)
    return np.where(dist < max_exact, dist, large)


def _attn_bias_tables(rel_bias):
    kj = np.arange(2 * QBLK)[None, None, :]
    period = 3 * QBLK
    tables = []
    for window, dil in PATTERNS:
        span = window // dil
        assert span <= QBLK
        bucket = _t5_bucket_np(np.arange(span + 1) * dil)
        onehot = np.eye(NUM_BUCKETS, dtype=np.float32)[bucket]
        vec = jnp.einsum("rb,bh->hr", onehot, rel_bias.astype(F32),
                         precision=lax.Precision.HIGHEST)
        diag = jnp.full((N_HEADS, period), NEG_INF, F32)
        diag = diag.at[:, 2 * QBLK - 1 - span:2 * QBLK].set(vec[:, ::-1])
        skew = jnp.tile(diag, (1, QBLK))[:, :QBLK * (period - 1)].reshape(N_HEADS, QBLK, period - 1)
        full = skew[:, :, QBLK - 1:3 * QBLK - 1]
        first = jnp.where(kj >= QBLK, full, NEG_INF)
        both = jnp.stack([full, first])
        tables.append(both.reshape(2, N_HEADS // 2, 2 * QBLK, 2 * QBLK))
    return jnp.stack(tables) * LOG2E


def _attn_kernel(q_ref, k_ref, v_ref, qg_ref, kg_ref, bias_ref, o_ref,
                 qn_ref, kn_ref, acc_ref, m_ref, l_ref):
    lane = lax.broadcasted_iota(jnp.int32, (QBLK, LANES), 1)
    head_a = lane < HEAD_DIM
    ri = lax.broadcasted_iota(jnp.int32, (LANES, LANES), 0) // HEAD_DIM
    ci = lax.broadcasted_iota(jnp.int32, (LANES, LANES), 1) // HEAD_DIM
    seg = (ri == ci).astype(BF16)

    def head_rms(x, g):
        sq = x * x
        hi = sq.astype(BF16)
        lo = (sq - hi.astype(F32)).astype(BF16)
        ss = (jnp.dot(hi, seg, preferred_element_type=F32)
              + jnp.dot(lo, seg, preferred_element_type=F32))
        return x * lax.rsqrt(ss * (1.0 / HEAD_DIM) + EPS) * g

    NCH = 512

    def norm_body(c, carry):
        rows = pl.ds(pl.multiple_of(c * NCH, NCH), NCH)
        qn_ref[rows, :] = head_rms(q_ref[rows, :], qg_ref[...]) * (LOG2E / math.sqrt(HEAD_DIM))
        kn_ref[rows, :] = head_rms(k_ref[rows, :], kg_ref[...])
        return carry

    lax.fori_loop(0, SEQ // NCH, norm_body, 0)

    order = sorted(range(len(PATTERNS)), key=lambda i: -PATTERNS[i][1])
    assert PATTERNS[order[-1]][1] == 1
    for step, p in enumerate(order):
        dil = PATTERNS[p][1]
        nb = SEQ // (dil * QBLK)
        is_first = step == 0
        is_last = step == len(order) - 1

        def unit(u, carry, p=p, dil=dil, nb=nb, is_first=is_first, is_last=is_last):
            r = u // nb
            n = u - r * nb
            cur = n * (QBLK * dil) + r
            prev = jnp.maximum(n - 1, 0) * (QBLK * dil) + r

            def rows(start):
                if dil == 1:
                    return pl.ds(start, QBLK)
                return pl.ds(start, QBLK, stride=dil)

            q = qn_ref[rows(cur), :]
            q2 = jnp.concatenate([jnp.where(head_a, q, 0.0), jnp.where(head_a, 0.0, q)],
                                 axis=0).astype(BF16)
            k2 = jnp.concatenate([kn_ref[rows(prev), :], kn_ref[rows(cur), :]], axis=0).astype(BF16)
            v2 = jnp.concatenate([v_ref[rows(prev), :], v_ref[rows(cur), :]], axis=0).astype(BF16)
            s = lax.dot_general(q2, k2, (((1,), (1,)), ((), ())), preferred_element_type=F32)
            first = jnp.where(n == 0, 1, 0)
            s = s + bias_ref[p, first]
            m = jnp.max(s, axis=-1, keepdims=True)
            e = jnp.exp2(s - m)
            l = jnp.sum(e, axis=-1, keepdims=True)
            pv = jnp.dot(e.astype(BF16), v2, preferred_element_type=F32)
            o_new = jnp.where(head_a, pv[:QBLK], pv[QBLK:])
            m_new = jnp.where(head_a, m[:QBLK], m[QBLK:])
            l_new = jnp.where(head_a, l[:QBLK], l[QBLK:])
            if is_first:
                acc_ref[rows(cur), :] = o_new
                m_ref[rows(cur), :] = m_new
                l_ref[rows(cur), :] = l_new
            else:
                m_old = m_ref[rows(cur), :]
                m_tot = jnp.maximum(m_old, m_new)
                a = jnp.exp2(m_old - m_tot)
                b = jnp.exp2(m_new - m_tot)
                acc = acc_ref[rows(cur), :] * a + o_new * b
                den = l_ref[rows(cur), :] * a + l_new * b
                if is_last:
                    o_ref[pl.ds(pl.multiple_of(cur, QBLK), QBLK), :] = (acc / den).astype(BF16)
                else:
                    acc_ref[rows(cur), :] = acc
                    l_ref[rows(cur), :] = den
                    m_ref[rows(cur), :] = m_tot
            return carry

        lax.fori_loop(0, SEQ // QBLK, unit, 0, unroll=ATTN_UNROLL)


def _attention(proj, qg2, kg2, bias_tab):
    qoff = 2 * CONV_C // LANES
    koff = qoff + ATTN_W // LANES
    voff = koff + ATTN_W // LANES
    return pl.pallas_call(
        _attn_kernel,
        grid=(N_HEADS // 2,),
        in_specs=[
            pl.BlockSpec((SEQ, LANES), lambda h: (0, qoff + h)),
            pl.BlockSpec((SEQ, LANES), lambda h: (0, koff + h)),
            pl.BlockSpec((SEQ, LANES), lambda h: (0, voff + h)),
            pl.BlockSpec((1, LANES), lambda h: (0, 0)),
            pl.BlockSpec((1, LANES), lambda h: (0, 0)),
            pl.BlockSpec((len(PATTERNS), 2, None, 2 * QBLK, 2 * QBLK), lambda h: (0, 0, h, 0, 0)),
        ],
        out_specs=pl.BlockSpec((SEQ, LANES), lambda h: (0, h)),
        out_shape=jax.ShapeDtypeStruct((SEQ, ATTN_W), BF16),
        scratch_shapes=[pltpu.VMEM((SEQ, LANES), F32) for _ in range(5)],
        compiler_params=_cparams(("arbitrary",)),
        name="dilated_attn",
    )(proj, proj, proj, qg2, kg2, bias_tab)


def _outproj_kernel(x_ref, c_ref, a_ref, wc_ref, wa_ref, o_ref):
    o_ref[...] = (x_ref[...]
                  + jnp.dot(c_ref[...], wc_ref[...], preferred_element_type=F32)
                  + jnp.dot(a_ref[...], wa_ref[...], preferred_element_type=F32))


def _outproj(x, conv_out, attn_out, w_out_bf16):
    return pl.pallas_call(
        _outproj_kernel,
        grid=(SEQ // TM_OUT,),
        in_specs=[
            pl.BlockSpec((TM_OUT, D_MODEL), lambda i: (i, 0)),
            pl.BlockSpec((TM_OUT, CONV_C), lambda i: (i, 0)),
            pl.BlockSpec((TM_OUT, ATTN_W), lambda i: (i, 0)),
            pl.BlockSpec((CONV_C, D_MODEL), lambda i: (0, 0)),
            pl.BlockSpec((ATTN_W, D_MODEL), lambda i: (1, 0)),
        ],
        out_specs=pl.BlockSpec((TM_OUT, D_MODEL), lambda i: (i, 0)),
        out_shape=jax.ShapeDtypeStruct((SEQ, D_MODEL), F32),
        compiler_params=_cparams(("arbitrary",)),
        name="outproj",
    )(x, conv_out, attn_out, w_out_bf16, w_out_bf16)


def _split3(a):
    a1 = a.astype(BF16)
    r1 = a - a1.astype(F32)
    a2 = r1.astype(BF16)
    a3 = (r1 - a2.astype(F32)).astype(BF16)
    return a1, a2, a3


def _router_kernel(x_ref, g_ref, wr_ref, br_ref, eid_ref, wts_ref, rank_ref, cnt_ref):
    i = pl.program_id(0)

    @pl.when(i == 0)
    def _():
        cnt_ref[...] = jnp.zeros_like(cnt_ref)

    x = x_ref[...]
    ms = jnp.mean(x * x, axis=-1, keepdims=True)
    hn = x * lax.rsqrt(ms + EPS) * g_ref[...]
    h1, h2, h3 = _split3(hn)
    w1, w2, w3 = _split3(wr_ref[...])
    dn = (((1,), (1,)), ((), ()))
    lt = None
    for wa, ha in ((w1, h1), (w1, h2), (w2, h1), (w2, h2), (w1, h3), (w3, h1)):
        t = lax.dot_general(wa, ha, dn, preferred_element_type=F32)
        lt = t if lt is None else lt + t
    lt = lt + br_ref[:, 0:1]

    row8 = lax.broadcasted_iota(jnp.int32, (8, TM_R), 0)
    gl = jnp.where(row8 < N_GROUPS, lt[0:8], -jnp.inf)
    gmax = jnp.max(gl, axis=0, keepdims=True)
    gidx = jnp.min(jnp.where(gl == gmax, row8, 8), axis=0, keepdims=True)
    gw = 1.0 / jnp.sum(jnp.exp(gl - gmax), axis=0, keepdims=True)

    esel = lt[8:16]
    for g in range(1, N_GROUPS):
        esel = jnp.where(gidx == g, lt[8 + 8 * g:16 + 8 * g], esel)
    v1 = jnp.max(esel, axis=0, keepdims=True)
    i1 = jnp.min(jnp.where(esel == v1, row8, 8), axis=0, keepdims=True)
    rest = jnp.where(row8 == i1, -jnp.inf, esel)
    v2 = jnp.max(rest, axis=0, keepdims=True)
    i2 = jnp.min(jnp.where(rest == v2, row8, 8), axis=0, keepdims=True)
    e21 = jnp.exp(v2 - v1)
    den = 1.0 + e21
    e1 = gidx * E_PER_G + i1
    e2 = gidx * E_PER_G + i2
    eid_ref[0:1, :] = e1
    eid_ref[1:2, :] = e2
    wts_ref[0:1, :] = gw * (1.0 / den)
    wts_ref[1:2, :] = gw * (e21 / den)

    erow = lax.broadcasted_iota(jnp.int32, (N_EXPERTS, TM_R), 0)
    oh1 = erow == e1
    oh2 = erow == e2
    member = jnp.where(oh1 | oh2, 1.0, 0.0)
    ti = lax.broadcasted_iota(jnp.int32, (TM_R, TM_R), 0)
    tj = lax.broadcasted_iota(jnp.int32, (TM_R, TM_R), 1)
    upper = jnp.where(ti < tj, 1.0, 0.0).astype(BF16)
    before = jnp.dot(member.astype(BF16), upper, preferred_element_type=F32)
    pos = before + cnt_ref[:, 0:1]
    rank_ref[0:1, :] = jnp.sum(jnp.where(oh1, pos, 0.0), axis=0, keepdims=True).astype(jnp.int32)
    rank_ref[1:2, :] = jnp.sum(jnp.where(oh2, pos, 0.0), axis=0, keepdims=True).astype(jnp.int32)
    cnt_ref[...] = cnt_ref[...] + jnp.sum(member, axis=1, keepdims=True)


def _router(x1, g2, wr_t, br):
    return pl.pallas_call(
        _router_kernel,
        grid=(SEQ // TM_R,),
        in_specs=[
            pl.BlockSpec((TM_R, D_MODEL), lambda i: (i, 0)),
            pl.BlockSpec((1, D_MODEL), lambda i: (0, 0)),
            pl.BlockSpec((R_ROWS, D_MODEL), lambda i: (0, 0)),
            pl.BlockSpec((R_ROWS, LANES), lambda i: (0, 0)),
        ],
        out_specs=[
            pl.BlockSpec((2, TM_R), lambda i: (0, i)),
            pl.BlockSpec((2, TM_R), lambda i: (0, i)),
            pl.BlockSpec((2, TM_R), lambda i: (0, i)),
            pl.BlockSpec((N_EXPERTS, LANES), lambda i: (0, 0)),
        ],
        out_shape=[
            jax.ShapeDtypeStruct((2, SEQ), jnp.int32),
            jax.ShapeDtypeStruct((2, SEQ), F32),
            jax.ShapeDtypeStruct((2, SEQ), jnp.int32),
            jax.ShapeDtypeStruct((N_EXPERTS, LANES), F32),
        ],
        compiler_params=_cparams(("arbitrary",)),
        name="router",
    )(x1, g2, wr_t, br)


def _moe_kernel(ie_ref, row0_ref, rows_ref, nitems_ref, sorted_ref,
                x_hbm, g_ref, wg_ref, wu_ref, wd_ref, dest_hbm,
                xg_ref, xb_ref, y_ref, gsem, ssem):
    it = pl.program_id(0)
    f = pl.program_id(1)
    nrows = rows_ref[it]
    nitems = nitems_ref[0]
    slot = it % 2
    half = SUB_E // 2

    def padded(item):
        return pl.multiple_of(((rows_ref[item] + half - 1) // half) * half, half)

    nhalf = padded(it) // half
    nsub = nhalf // 2

    def start_gather(item):
        base = row0_ref[item]
        buf = item % 2

        def start(j8, c):
            for k in range(8):
                j = j8 * 8 + k
                tok = sorted_ref[base + j] & (SEQ - 1)
                pltpu.make_async_copy(x_hbm.at[pl.ds(tok, 1), :], xg_ref.at[buf, pl.ds(j, 1), :],
                                      gsem.at[buf]).start()
            return c
        lax.fori_loop(0, padded(item) // 8, start, 0)

    def wait_gather(item):
        got = pl.ds(0, padded(item))
        buf = item % 2
        pltpu.make_async_copy(x_hbm.at[got, :], xg_ref.at[buf, got, :], gsem.at[buf]).wait()

    def scatter_copy(item, j):
        dst = sorted_ref[row0_ref[item] + j]
        buf = item % 2
        return pltpu.make_async_copy(y_ref.at[buf, pl.ds(j, 1), :], dest_hbm.at[pl.ds(dst, 1), :],
                                     ssem.at[buf])

    def start_scatter(item):
        n = rows_ref[item]

        def start8(j8, c):
            for k in range(8):
                scatter_copy(item, j8 * 8 + k).start()
            return c
        lax.fori_loop(0, n // 8, start8, 0)

        def start(j, c):
            scatter_copy(item, j).start()
            return c
        lax.fori_loop((n // 8) * 8, n, start, 0)

    def wait_scatter(item):
        n = rows_ref[item]
        whole = pl.multiple_of((n // 8) * 8, 8)
        buf = item % 2

        @pl.when(whole > 0)
        def _():
            sent = pl.ds(0, whole)
            pltpu.make_async_copy(y_ref.at[buf, sent, :], dest_hbm.at[sent, :], ssem.at[buf]).wait()

        def wait(j, c):
            scatter_copy(item, j).wait()
            return c
        lax.fori_loop(whole, n, wait, 0)

    @pl.when((f == 0) & (nrows > 0))
    def _():
        @pl.when(it == 0)
        def _():
            start_gather(0)

        wait_gather(it)

        def norm(s, c):
            rows = pl.ds(pl.multiple_of(s * half, half), half)
            x = xg_ref[slot, rows, :]
            ms = jnp.mean(x * x, axis=-1, keepdims=True)
            xb_ref[rows, :] = (x * lax.rsqrt(ms + EPS) * g_ref[...]).astype(BF16)
            y_ref[slot, rows, :] = jnp.zeros((half, D_MODEL), F32)
            return c
        lax.fori_loop(0, nhalf, norm, 0)

        @pl.when(it + 1 < nitems)
        def _():
            start_gather(it + 1)

    @pl.when(nrows > 0)
    def _():
        wg = wg_ref[...].astype(BF16)
        wu = wu_ref[...].astype(BF16)
        wd = wd_ref[...].astype(BF16)

        def block(rows):
            xb = xb_ref[rows, :]
            hg = jnp.dot(xb, wg, preferred_element_type=F32)
            hu = jnp.dot(xb, wu, preferred_element_type=F32)
            h = (hg * jax.nn.sigmoid(hg) * hu).astype(BF16)
            y_ref[slot, rows, :] = y_ref[slot, rows, :] + jnp.dot(h, wd, preferred_element_type=F32)

        def sub(s, c):
            block(pl.ds(pl.multiple_of(s * SUB_E, SUB_E), SUB_E))
            return c
        lax.fori_loop(0, nsub, sub, 0)

        @pl.when(nhalf % 2 == 1)
        def _():
            block(pl.ds(pl.multiple_of(nsub * SUB_E, SUB_E), half))

    @pl.when((f == NF_E - 1) & (nrows > 0))
    def _():
        @pl.when(it > 0)
        def _():
            wait_scatter(it - 1)

        start_scatter(it)

        @pl.when(it + 1 >= nitems)
        def _():
            wait_scatter(it)


def _moe(ie, row0, rows, nitems, sorted_i, x1, g2, wg, wu, wd):
    def wmap_cols(i, f, ie_ref, row0_ref, rows_ref, n_ref, s_ref):
        return (ie_ref[i], 0, jnp.where(i < n_ref[0], f, NF_E - 1))

    def wmap_rows(i, f, ie_ref, row0_ref, rows_ref, n_ref, s_ref):
        return (ie_ref[i], jnp.where(i < n_ref[0], f, NF_E - 1), 0)

    grid_spec = pltpu.PrefetchScalarGridSpec(
        num_scalar_prefetch=5,
        grid=(nitems[0], NF_E),
        in_specs=[
            pl.BlockSpec(memory_space=pl.ANY),
            pl.BlockSpec((1, D_MODEL), lambda i, f, *_: (0, 0)),
            pl.BlockSpec((None, D_MODEL, TF_E), wmap_cols),
            pl.BlockSpec((None, D_MODEL, TF_E), wmap_cols),
            pl.BlockSpec((None, TF_E, D_MODEL), wmap_rows),
        ],
        out_specs=pl.BlockSpec(memory_space=pl.ANY),
        scratch_shapes=[
            pltpu.VMEM((2, TM_E, D_MODEL), F32),
            pltpu.VMEM((TM_E, D_MODEL), BF16),
            pltpu.VMEM((2, TM_E, D_MODEL), F32),
            pltpu.SemaphoreType.DMA((2,)),
            pltpu.SemaphoreType.DMA((2,)),
        ],
    )
    return pl.pallas_call(
        _moe_kernel,
        grid_spec=grid_spec,
        out_shape=jax.ShapeDtypeStruct((N_ASSIGN, D_MODEL), F32),
        compiler_params=_cparams(("arbitrary", "arbitrary")),
        name="moe_experts",
    )(ie, row0, rows, nitems, sorted_i, x1, g2, wg, wu, wd)


def _combine_kernel(x_ref, w_ref, d0_ref, d1_ref, o_ref):
    w = w_ref[...]
    o_ref[...] = x_ref[...] + w[:, 0:1] * d0_ref[...] + w[:, 1:2] * d1_ref[...]


def _combine(x1, wts_t, dest):
    dest3 = dest.reshape(2, SEQ, D_MODEL)
    return pl.pallas_call(
        _combine_kernel,
        grid=(SEQ // TT_COMB,),
        in_specs=[
            pl.BlockSpec((TT_COMB, D_MODEL), lambda i: (i, 0)),
            pl.BlockSpec((TT_COMB, 2), lambda i: (i, 0)),
            pl.BlockSpec((None, TT_COMB, D_MODEL), lambda i: (0, i, 0)),
            pl.BlockSpec((None, TT_COMB, D_MODEL), lambda i: (1, i, 0)),
        ],
        out_specs=pl.BlockSpec((TT_COMB, D_MODEL), lambda i: (i, 0)),
        out_shape=jax.ShapeDtypeStruct((SEQ, D_MODEL), F32),
        compiler_params=_cparams(("arbitrary",)),
        name="moe_combine",
    )(x1, wts_t, dest3, dest3)


def _dispatch_kernel(slot_ref, padrow_ref, x_ref, g_ref, xs_hbm, hbuf, zbuf, sem, zsem):
    i = pl.program_id(0)
    last = pl.num_programs(0) - 1
    buf = i % 2

    def tile_wait(b):
        for _ in range(2):
            pltpu.make_async_copy(hbuf.at[b], xs_hbm.at[pl.ds(0, TT_D), :], sem.at[b]).wait()

    @pl.when(i == 0)
    def _():
        zbuf[...] = jnp.zeros_like(zbuf)

        for c in range(XS_TAIL // ZROWS):
            pltpu.make_async_copy(zbuf, xs_hbm.at[pl.ds(XS_ROWS - XS_TAIL + c * ZROWS, ZROWS), :], zsem).start()
        for c in range(XS_TAIL // ZROWS):
            pltpu.make_async_copy(zbuf, xs_hbm.at[pl.ds(0, ZROWS), :], zsem).wait()

        def pad(e, c):
            dst = pl.multiple_of(padrow_ref[e], 8)
            pltpu.make_async_copy(zbuf.at[pl.ds(0, 8), :], xs_hbm.at[pl.ds(dst, 8), :], zsem).start()
            return c
        lax.fori_loop(0, N_EXPERTS, pad, 0)

        def padw(e, c):
            pltpu.make_async_copy(zbuf.at[pl.ds(0, 8), :], xs_hbm.at[pl.ds(0, 8), :], zsem).wait()
            return c
        lax.fori_loop(0, N_EXPERTS, padw, 0)

    @pl.when(i >= 2)
    def _():
        tile_wait(buf)

    x = x_ref[...]
    ms = jnp.mean(x * x, axis=-1, keepdims=True)
    hbuf[buf] = x * lax.rsqrt(ms + EPS) * g_ref[...]

    tok0 = i * TT_D

    def send(j8, c):
        for k8 in range(8):
            j = j8 * 8 + k8
            for k in range(2):
                dst = slot_ref[k * SEQ + tok0 + j]
                pltpu.make_async_copy(hbuf.at[buf, pl.ds(j, 1), :], xs_hbm.at[pl.ds(dst, 1), :],
                                      sem.at[buf]).start()
        return c
    lax.fori_loop(0, TT_D // 8, send, 0)

    @pl.when(i == last)
    def _():
        @pl.when(i >= 1)
        def _():
            tile_wait(1 - buf)
        tile_wait(buf)


def _dispatch(slot_flat, padrow, x1, g2):
    grid_spec = pltpu.PrefetchScalarGridSpec(
        num_scalar_prefetch=2,
        grid=(SEQ // TT_D,),
        in_specs=[
            pl.BlockSpec((TT_D, D_MODEL), lambda i, *_: (i, 0)),
            pl.BlockSpec((1, D_MODEL), lambda i, *_: (0, 0)),
        ],
        out_specs=pl.BlockSpec(memory_space=pl.ANY),
        scratch_shapes=[
            pltpu.VMEM((2, TT_D, D_MODEL), F32),
            pltpu.VMEM((ZROWS, D_MODEL), F32),
            pltpu.SemaphoreType.DMA((2,)),
            pltpu.SemaphoreType.DMA,
        ],
    )
    return pl.pallas_call(
        _dispatch_kernel,
        grid_spec=grid_spec,
        out_shape=jax.ShapeDtypeStruct((XS_ROWS, D_MODEL), F32),
        compiler_params=_cparams(("arbitrary",)),
        name="moe_dispatch",
    )(slot_flat, padrow, x1, g2)


def _experts_kernel(ie_ref, row0_ref, rows_ref, nitems_ref,
                    xs_hbm, wg_ref, wu_ref, wd_ref, ys_hbm,
                    xg_ref, y_ref, zbuf, gsem, ssem, zsem):
    it = pl.program_id(0)
    f = pl.program_id(1)
    nrows = rows_ref[it]
    nitems = nitems_ref[0]
    slot = it % 2
    half = SUB_E // 2

    def padded(item):
        return pl.multiple_of(((rows_ref[item] + half - 1) // half) * half, half)

    nhalf = padded(it) // half
    nsub = nhalf // 2

    def chunk_copies(item, start):
        base = pl.multiple_of(row0_ref[item], 8)
        buf = item % 2

        def body(c, carry):
            off = pl.multiple_of(c * half, half)
            src = xs_hbm.at[pl.ds(pl.multiple_of(base + off, 8), half), :]
            dst = ys_hbm.at[pl.ds(pl.multiple_of(base + off, 8), half), :]
            if start == "fetch":
                pltpu.make_async_copy(src, xg_ref.at[buf, pl.ds(off, half), :], gsem.at[buf]).start()
            else:
                pltpu.make_async_copy(y_ref.at[buf, pl.ds(off, half), :], dst, ssem.at[buf]).start()
            return carry
        lax.fori_loop(0, padded(item) // half, body, 0)

    def wait_fetch(item):
        got = pl.ds(0, padded(item))
        buf = item % 2
        pltpu.make_async_copy(xs_hbm.at[got, :], xg_ref.at[buf, got, :], gsem.at[buf]).wait()

    def wait_store(item):
        put = pl.ds(0, padded(item))
        buf = item % 2
        pltpu.make_async_copy(y_ref.at[buf, put, :], ys_hbm.at[put, :], ssem.at[buf]).wait()

    @pl.when((it == 0) & (f == 0))
    def _():
        zbuf[...] = jnp.zeros_like(zbuf)
        for c in range(XS_TAIL // ZROWS):
            pltpu.make_async_copy(zbuf, ys_hbm.at[pl.ds(XS_ROWS - XS_TAIL + c * ZROWS, ZROWS), :], zsem).start()
        for c in range(XS_TAIL // ZROWS):
            pltpu.make_async_copy(zbuf, ys_hbm.at[pl.ds(0, ZROWS), :], zsem).wait()
        chunk_copies(0, "fetch")

    @pl.when(f == 0)
    def _():
        wait_fetch(it)

        def clear(s, c):
            rows = pl.ds(pl.multiple_of(s * half, half), half)
            y_ref[slot, rows, :] = jnp.zeros((half, D_MODEL), F32)
            return c
        lax.fori_loop(0, nhalf, clear, 0)

        @pl.when(it + 1 < nitems)
        def _():
            chunk_copies(it + 1, "fetch")

    wg = wg_ref[...].astype(BF16)
    wu = wu_ref[...].astype(BF16)
    wd = wd_ref[...].astype(BF16)

    def block(rows):
        xb = xg_ref[slot, rows, :].astype(BF16)
        hg = jnp.dot(xb, wg, preferred_element_type=F32)
        hu = jnp.dot(xb, wu, preferred_element_type=F32)
        h = (hg * jax.nn.sigmoid(hg) * hu).astype(BF16)
        y_ref[slot, rows, :] = y_ref[slot, rows, :] + jnp.dot(h, wd, preferred_element_type=F32)

    def sub(s, c):
        block(pl.ds(pl.multiple_of(s * SUB_E, SUB_E), SUB_E))
        return c
    lax.fori_loop(0, nsub, sub, 0)

    @pl.when(nhalf % 2 == 1)
    def _():
        block(pl.ds(pl.multiple_of(nsub * SUB_E, SUB_E), half))

    @pl.when(f == NF_E - 1)
    def _():
        @pl.when(it > 0)
        def _():
            wait_store(it - 1)

        chunk_copies(it, "store")

        @pl.when(it + 1 >= nitems)
        def _():
            wait_store(it)


def _experts(ie, row0, rows, nitems, xs, wg, wu, wd):
    def wmap_cols(i, f, ie_ref, *_):
        return (ie_ref[i], 0, f)

    def wmap_rows(i, f, ie_ref, *_):
        return (ie_ref[i], f, 0)

    grid_spec = pltpu.PrefetchScalarGridSpec(
        num_scalar_prefetch=4,
        grid=(nitems[0], NF_E),
        in_specs=[
            pl.BlockSpec(memory_space=pl.ANY),
            pl.BlockSpec((None, D_MODEL, TF_E), wmap_cols),
            pl.BlockSpec((None, D_MODEL, TF_E), wmap_cols),
            pl.BlockSpec((None, TF_E, D_MODEL), wmap_rows),
        ],
        out_specs=pl.BlockSpec(memory_space=pl.ANY),
        scratch_shapes=[
            pltpu.VMEM((2, TM_E, D_MODEL), F32),
            pltpu.VMEM((2, TM_E, D_MODEL), F32),
            pltpu.VMEM((ZROWS, D_MODEL), F32),
            pltpu.SemaphoreType.DMA((2,)),
            pltpu.SemaphoreType.DMA((2,)),
            pltpu.SemaphoreType.DMA,
        ],
    )
    return pl.pallas_call(
        _experts_kernel,
        grid_spec=grid_spec,
        out_shape=jax.ShapeDtypeStruct((XS_ROWS, D_MODEL), F32),
        compiler_params=_cparams(("arbitrary", "arbitrary")),
        name="moe_experts",
    )(ie, row0, rows, nitems, xs, wg, wu, wd)


def _gcombine_kernel(slot_ref, x_ref, w_ref, ys_hbm, o_ref, dbuf, sem):
    i = pl.program_id(0)
    n = pl.num_programs(0)
    buf = i % 2

    def fetch(tile, b):
        tok0 = tile * TT_G

        def body(j8, c):
            for k8 in range(8):
                j = j8 * 8 + k8
                for k in range(2):
                    src = slot_ref[k * SEQ + tok0 + j]
                    pltpu.make_async_copy(ys_hbm.at[pl.ds(src, 1), :], dbuf.at[b, k, pl.ds(j, 1), :],
                                          sem.at[b]).start()
            return c
        lax.fori_loop(0, TT_G // 8, body, 0)

    @pl.when(i == 0)
    def _():
        fetch(0, 0)

    @pl.when(i + 1 < n)
    def _():
        fetch(i + 1, 1 - buf)

    for k in range(2):
        pltpu.make_async_copy(ys_hbm.at[pl.ds(0, TT_G), :], dbuf.at[buf, k], sem.at[buf]).wait()

    w = w_ref[...]
    o_ref[...] = x_ref[...] + w[:, 0:1] * dbuf[buf, 0] + w[:, 1:2] * dbuf[buf, 1]


def _gcombine(slot_flat, x1, wts_t, ys):
    grid_spec = pltpu.PrefetchScalarGridSpec(
        num_scalar_prefetch=1,
        grid=(SEQ // TT_G,),
        in_specs=[
            pl.BlockSpec((TT_G, D_MODEL), lambda i, *_: (i, 0)),
            pl.BlockSpec((TT_G, 2), lambda i, *_: (i, 0)),
            pl.BlockSpec(memory_space=pl.ANY),
        ],
        out_specs=pl.BlockSpec((TT_G, D_MODEL), lambda i, *_: (i, 0)),
        scratch_shapes=[
            pltpu.VMEM((2, 2, TT_G, D_MODEL), F32),
            pltpu.SemaphoreType.DMA((2,)),
        ],
    )
    return pl.pallas_call(
        _gcombine_kernel,
        grid_spec=grid_spec,
        out_shape=jax.ShapeDtypeStruct((SEQ, D_MODEL), F32),
        compiler_params=_cparams(("arbitrary",)),
        name="moe_combine",
    )(slot_flat, x1, wts_t, ys)


def _routing_tables(eid, rank, counts_f):
    counts = counts_f[:, 0].astype(jnp.int32)
    aligned = ((counts + 7) // 8) * 8
    base = jnp.cumsum(aligned) - aligned
    tiles = (counts + TM_E - 1) // TM_E
    tcum = jnp.cumsum(tiles)
    tstart = tcum - tiles
    nitems = tcum[-1]
    ids = jnp.arange(MAX_ITEMS, dtype=jnp.int32)
    ie = jnp.clip(jnp.searchsorted(tcum, ids, side="right"), 0, N_EXPERTS - 1).astype(jnp.int32)
    live = ids < nitems
    ie = jnp.where(live, ie, ie[jnp.maximum(nitems - 1, 0)])
    jt = ids - tstart[ie]
    row0 = jnp.where(live, base[ie] + jt * TM_E, 0)
    rows = jnp.where(live, jnp.clip(counts[ie] - jt * TM_E, 0, TM_E), 0)
    eoh = eid[:, :, None] == jnp.arange(N_EXPERTS, dtype=jnp.int32)
    slot = jnp.sum(jnp.where(eoh, base, 0), axis=-1) + rank
    spare = XS_ROWS - 8 * (1 + jnp.arange(N_EXPERTS, dtype=jnp.int32))
    padrow = jnp.where(counts % 8 != 0, base + (counts // 8) * 8, spare)
    return (ie, row0.astype(jnp.int32), rows.astype(jnp.int32), nitems.reshape(1).astype(jnp.int32),
            slot.reshape(-1).astype(jnp.int32), padrow.astype(jnp.int32))


def _work_items(eid, rank, counts_f):
    counts = counts_f[:, 0].astype(jnp.int32)
    cum = jnp.cumsum(counts)
    base = cum - counts
    tiles = (counts + TM_E - 1) // TM_E
    tcum = jnp.cumsum(tiles)
    tstart = tcum - tiles
    nitems = tcum[-1]
    ids = jnp.arange(MAX_ITEMS, dtype=jnp.int32)
    ie = jnp.clip(jnp.searchsorted(tcum, ids, side="right"), 0, N_EXPERTS - 1).astype(jnp.int32)
    live = ids < nitems
    ie = jnp.where(live, ie, ie[jnp.maximum(nitems - 1, 0)])
    jt = ids - tstart[ie]
    row0 = jnp.where(live, base[ie] + jt * TM_E, 0)
    rows = jnp.where(live, jnp.clip(counts[ie] - jt * TM_E, 0, TM_E), 0)
    eoh = eid[:, :, None] == jnp.arange(N_EXPERTS, dtype=jnp.int32)
    slot = jnp.sum(jnp.where(eoh, base, 0), axis=-1) + rank
    sorted_i = jnp.zeros((SORTED_LEN,), jnp.int32).at[slot.reshape(-1)].set(
        jnp.arange(N_ASSIGN, dtype=jnp.int32))
    return ie, row0.astype(jnp.int32), rows.astype(jnp.int32), nitems.reshape(1).astype(jnp.int32), sorted_i


def kernel(x, norm1_g, w_in, q_norm_g, k_norm_g, conv_w, conv_b, conv_ln_g, conv_ln_b, rel_bias,
           w_out, norm2_g, w_router_group, b_router_group, w_router_expert, b_router_expert,
           w_gate, w_up, w_down):
    assert x.shape == (1, SEQ, D_MODEL) and w_in.shape[0] == 1
    xs = x[0]
    bias_tab = _attn_bias_tables(rel_bias)
    qg2 = jnp.tile(q_norm_g[0], 2)[None]
    kg2 = jnp.tile(k_norm_g[0], 2)[None]

    proj = _inproj(xs, norm1_g[0][None], w_in[0].astype(BF16))
    conv_out = _conv_mixer(proj, conv_w[0], conv_b[0][None], conv_ln_g[0][None], conv_ln_b[0][None])
    attn_out = _attention(proj, qg2, kg2, bias_tab)
    x1 = _outproj(xs, conv_out, attn_out, w_out[0].astype(BF16))

    wr_t = jnp.concatenate([
        w_router_group[0].T, jnp.zeros((8 - N_GROUPS, D_MODEL), F32),
        jnp.transpose(w_router_expert[0], (0, 2, 1)).reshape(N_EXPERTS, D_MODEL)], axis=0)
    br = jnp.concatenate([b_router_group[0], jnp.zeros((8 - N_GROUPS,), F32),
                          b_router_expert[0].reshape(-1)])
    br = jnp.broadcast_to(br[:, None], (R_ROWS, LANES))
    eid, wts, rank, counts_f = _router(x1, norm2_g[0][None], wr_t, br)

    ie, row0, rows, nitems, slot_flat, padrow = _routing_tables(eid, rank, counts_f)
    xs = _dispatch(slot_flat, padrow, x1, norm2_g[0][None])
    ys = _experts(ie, row0, rows, nitems, xs,
                  w_gate[0].reshape(N_EXPERTS, D_MODEL, D_FF),
                  w_up[0].reshape(N_EXPERTS, D_MODEL, D_FF),
                  w_down[0].reshape(N_EXPERTS, D_FF, D_MODEL))
    out = _gcombine(slot_flat, x1, wts.T, ys)
    return out[None]
```

```python
import functools
import math

import numpy as np
import jax
import jax.numpy as jnp
from jax import lax
from jax.experimental import pallas as pl
from jax.experimental.pallas import tpu as pltpu

F32 = jnp.float32
BF16 = jnp.bfloat16

D_MODEL = 2048
SEQ = 8192
N_HEADS = 16
HEAD_DIM = 64
ATTN_W = N_HEADS * HEAD_DIM
CONV_C = D_MODEL - ATTN_W
CONV_K = 31
IN_W = 2 * CONV_C + 3 * ATTN_W
PATTERNS = ((128, 1), (512, 4), (2048, 16))
QBLK = 128
NUM_BUCKETS = 32
MAX_DISTANCE = 2048
N_GROUPS = 4
E_PER_G = 8
N_EXPERTS = N_GROUPS * E_PER_G
D_FF = D_MODEL // 2
EPS = 1e-6
NEG_INF = -1e30
LOG2E = math.log2(math.e)

LANES = 128
VMEM_LIMIT = 56 * 1024 * 1024

TM_IN = 1024
TN_IN = 1024
TT_CONV = 512
HALO = 32
R_CONV = 64
R_LN = 16
ATTN_GROUP = 4
ATTN_UNROLL = 2
TM_OUT = 512
TM_R = 512
R_ROWS = 8 + N_EXPERTS
KC_R = 512
TM_E = 768
SUB_E = 256
TF_E = 256
NF_E = D_FF // TF_E
N_ASSIGN = 2 * SEQ
MAX_ITEMS = -(-N_ASSIGN // TM_E) + N_EXPERTS
SORTED_LEN = N_ASSIGN + TM_E
XS_TAIL = 256 + TM_E
XS_ROWS = N_ASSIGN + XS_TAIL
ZROWS = 128
TT_D = 512
TT_G = 512
TT_COMB = 512


def _cparams(sem, vmem=VMEM_LIMIT, flags=None):
    return pltpu.CompilerParams(dimension_semantics=sem, vmem_limit_bytes=vmem, flags=flags)


def _inproj_kernel(x_ref, g_ref, w_ref, o_ref, xn_ref):
    @pl.when(pl.program_id(1) == 0)
    def _():
        x = x_ref[...]
        ms = jnp.mean(x * x, axis=-1, keepdims=True)
        xn_ref[...] = (x * lax.rsqrt(ms + EPS) * g_ref[...]).astype(BF16)

    o_ref[...] = jnp.dot(xn_ref[...], w_ref[...].astype(BF16), preferred_element_type=F32)


def _inproj(x, g, w):
    return pl.pallas_call(
        _inproj_kernel,
        grid=(SEQ // TM_IN, IN_W // TN_IN),
        in_specs=[
            pl.BlockSpec((TM_IN, D_MODEL), lambda i, j: (i, 0)),
            pl.BlockSpec((1, D_MODEL), lambda i, j: (0, 0)),
            pl.BlockSpec((D_MODEL, TN_IN), lambda i, j: (0, j)),
        ],
        out_specs=pl.BlockSpec((TM_IN, TN_IN), lambda i, j: (i, j)),
        out_shape=jax.ShapeDtypeStruct((SEQ, IN_W), F32),
        scratch_shapes=[pltpu.VMEM((TM_IN, D_MODEL), BF16)],
        compiler_params=_cparams(("arbitrary", "arbitrary")),
        name="inproj",
    )(x, g, w)


def _conv_kernel(val_ref, gate_ref, hval_ref, hgate_ref, cw_ref, cb_ref, lg_ref, lb_ref,
                 o_ref, ubuf, zbuf, ybuf):
    i = pl.program_id(0)
    u = val_ref[...] * jax.nn.sigmoid(gate_ref[...])
    hu = hval_ref[...] * jax.nn.sigmoid(hgate_ref[...])
    hu = jnp.where(i > 0, hu, 0.0)
    for c in range(CONV_C // LANES):
        cols = slice(c * LANES, (c + 1) * LANES)
        ubuf[c, 0:HALO, :] = hu[:, cols]
        ubuf[c, HALO:HALO + TT_CONV, :] = u[:, cols]

    n_a = -(-CONV_K // 8)
    assert HALO == 8 * n_a
    for c in range(CONV_C // LANES):
        cols = slice(c * LANES, (c + 1) * LANES)

        def taps(r, carry, c=c, cols=cols):
            base = pl.multiple_of(r * R_CONV, R_CONV)
            win = ubuf[c, pl.ds(base, R_CONV + HALO), :]
            for b in range(8):
                z = None
                for a in range(n_a):
                    s = 8 * a + b
                    if s >= CONV_K:
                        continue
                    lo = HALO - 8 - 8 * a
                    t = cw_ref[CONV_K - 1 - s:CONV_K - s, cols] * win[lo:lo + R_CONV + 8, :]
                    z = t if z is None else z + t
                zbuf[b, pl.ds(0, R_CONV + 8, stride=2), :] = z
            acc = None
            for b in range(8):
                t = zbuf[b, pl.ds(2 * (8 - b), R_CONV, stride=2), :]
                acc = t if acc is None else acc + t
            ybuf[pl.ds(base, R_CONV), cols] = acc
            return carry

        lax.fori_loop(0, TT_CONV // R_CONV, taps, 0)

    def norm(r, carry):
        rows = pl.ds(pl.multiple_of(r * R_LN, R_LN), R_LN)
        acc = ybuf[rows, :] + cb_ref[...]
        mu = jnp.mean(acc, axis=-1, keepdims=True)
        xc = acc - mu
        var = jnp.mean(xc * xc, axis=-1, keepdims=True)
        y = xc * lax.rsqrt(var + EPS) * lg_ref[...] + lb_ref[...]
        o_ref[rows, :] = (y * jax.nn.sigmoid(y)).astype(BF16)
        return carry

    lax.fori_loop(0, TT_CONV // R_LN, norm, 0, unroll=4)


def _conv_mixer(proj, cw, cb, lg, lb):
    hb = TT_CONV // HALO
    return pl.pallas_call(
        _conv_kernel,
        grid=(SEQ // TT_CONV,),
        in_specs=[
            pl.BlockSpec((TT_CONV, CONV_C), lambda i: (i, 0)),
            pl.BlockSpec((TT_CONV, CONV_C), lambda i: (i, 1)),
            pl.BlockSpec((HALO, CONV_C), lambda i: (jnp.maximum(i * hb - 1, 0), 0)),
            pl.BlockSpec((HALO, CONV_C), lambda i: (jnp.maximum(i * hb - 1, 0), 1)),
            pl.BlockSpec((CONV_K, CONV_C), lambda i: (0, 0)),
            pl.BlockSpec((1, CONV_C), lambda i: (0, 0)),
            pl.BlockSpec((1, CONV_C), lambda i: (0, 0)),
            pl.BlockSpec((1, CONV_C), lambda i: (0, 0)),
        ],
        out_specs=pl.BlockSpec((TT_CONV, CONV_C), lambda i: (i, 0)),
        out_shape=jax.ShapeDtypeStruct((SEQ, CONV_C), BF16),
        scratch_shapes=[pltpu.VMEM((CONV_C // LANES, HALO + TT_CONV, LANES), F32),
                        pltpu.VMEM((8, 2 * (R_CONV + 8), LANES), F32),
                        pltpu.VMEM((TT_CONV, CONV_C), F32)],
        compiler_params=_cparams(("arbitrary",)),
        name="conv_mixer",
    )(proj, proj, proj, proj, cw, cb, lg, lb)


def _t5_bucket_np(dist):
    max_exact = NUM_BUCKETS // 2
    nf = np.maximum(dist, 1).astype(np.float32)
    large = max_exact + (np.log(nf / np.float32(max_exact)) / np.float32(math.log(MAX_DISTANCE / max_exact))
                         * np.float32(NUM_BUCKETS - max_exact)).astype(np.int32)
    large = np.minimum(large, NUM_BUCKETS - 1)
    return np.where(dist < max_exact, dist, large)


def _attn_bias_tables(rel_bias):
    kj = np.arange(2 * QBLK)[None, None, :]
    period = 3 * QBLK
    tables = []
    for window, dil in PATTERNS:
        span = window // dil
        assert span <= QBLK
        bucket = _t5_bucket_np(np.arange(span + 1) * dil)
        onehot = np.eye(NUM_BUCKETS, dtype=np.float32)[bucket]
        vec = jnp.einsum("rb,bh->hr", onehot, rel_bias.astype(F32),
                         precision=lax.Precision.HIGHEST)
        diag = jnp.full((N_HEADS, period), NEG_INF, F32)
        diag = diag.at[:, 2 * QBLK - 1 - span:2 * QBLK].set(vec[:, ::-1])
        skew = jnp.tile(diag, (1, QBLK))[:, :QBLK * (period - 1)].reshape(N_HEADS, QBLK, period - 1)
        full = skew[:, :, QBLK - 1:3 * QBLK - 1]
        first = jnp.where(kj >= QBLK, full, NEG_INF)
        both = jnp.stack([full, first])
        tables.append(both.reshape(2, N_HEADS // 2, 2 * QBLK, 2 * QBLK))
    return jnp.stack(tables) * LOG2E


def _attn_kernel(q_ref, k_ref, v_ref, qg_ref, kg_ref, bias_ref, o_ref,
                 qn_ref, kn_ref, acc_ref, m_ref, l_ref):
    lane = lax.broadcasted_iota(jnp.int32, (QBLK, LANES), 1)
    head_a = lane < HEAD_DIM
    ri = lax.broadcasted_iota(jnp.int32, (LANES, LANES), 0) // HEAD_DIM
    ci = lax.broadcasted_iota(jnp.int32, (LANES, LANES), 1) // HEAD_DIM
    seg = (ri == ci).astype(BF16)

    def head_rms(x, g):
        sq = x * x
        hi = sq.astype(BF16)
        lo = (sq - hi.astype(F32)).astype(BF16)
        ss = (jnp.dot(hi, seg, preferred_element_type=F32)
              + jnp.dot(lo, seg, preferred_element_type=F32))
        return x * lax.rsqrt(ss * (1.0 / HEAD_DIM) + EPS) * g

    NCH = 512

    def norm_body(c, carry):
        rows = pl.ds(pl.multiple_of(c * NCH, NCH), NCH)
        qn_ref[rows, :] = head_rms(q_ref[rows, :], qg_ref[...]) * (LOG2E / math.sqrt(HEAD_DIM))
        kn_ref[rows, :] = head_rms(k_ref[rows, :], kg_ref[...])
        return carry

    lax.fori_loop(0, SEQ // NCH, norm_body, 0)

    order = sorted(range(len(PATTERNS)), key=lambda i: -PATTERNS[i][1])
    assert PATTERNS[order[-1]][1] == 1
    for step, p in enumerate(order):
        dil = PATTERNS[p][1]
        nb = SEQ // (dil * QBLK)
        is_first = step == 0
        is_last = step == len(order) - 1

        def rows(start, dil=dil):
            if dil == 1:
                return pl.ds(start, QBLK)
            return pl.ds(start, QBLK, stride=dil)

        def unit(cur, k_prev, v_prev, first, p=p, rows=rows, is_first=is_first, is_last=is_last):
            q = qn_ref[rows(cur), :]
            q2 = jnp.concatenate([jnp.where(head_a, q, 0.0), jnp.where(head_a, 0.0, q)],
                                 axis=0).astype(BF16)
            k_cur = kn_ref[rows(cur), :].astype(BF16)
            v_cur = v_ref[rows(cur), :].astype(BF16)
            k2 = jnp.concatenate([k_prev, k_cur], axis=0)
            v2 = jnp.concatenate([v_prev, v_cur], axis=0)
            s = lax.dot_general(q2, k2, (((1,), (1,)), ((), ())), preferred_element_type=F32)
            s = s + bias_ref[p, first]
            m = jnp.max(s, axis=-1, keepdims=True)
            e = jnp.exp2(s - m)
            l = jnp.sum(e, axis=-1, keepdims=True)
            pv = jnp.dot(e.astype(BF16), v2, preferred_element_type=F32)
            o_new = jnp.where(head_a, pv[:QBLK], pv[QBLK:])
            m_new = jnp.where(head_a, m[:QBLK], m[QBLK:])
            l_new = jnp.where(head_a, l[:QBLK], l[QBLK:])
            if is_first:
                acc_ref[rows(cur), :] = o_new
                m_ref[rows(cur), :] = m_new
                l_ref[rows(cur), :] = l_new
            else:
                m_old = m_ref[rows(cur), :]
                m_tot = jnp.maximum(m_old, m_new)
                a = jnp.exp2(m_old - m_tot)
                b = jnp.exp2(m_new - m_tot)
                acc = acc_ref[rows(cur), :] * a + o_new * b
                den = l_ref[rows(cur), :] * a + l_new * b
                if is_last:
                    o_ref[pl.ds(pl.multiple_of(cur, QBLK), QBLK), :] = (acc / den).astype(BF16)
                else:
                    acc_ref[rows(cur), :] = acc
                    l_ref[rows(cur), :] = den
                    m_ref[rows(cur), :] = m_tot
            return k_cur, v_cur

        per = nb // ATTN_GROUP
        assert per * ATTN_GROUP == nb

        def group(g, carry, dil=dil, per=per, rows=rows, unit=unit):
            r = g // per
            n0 = (g - r * per) * ATTN_GROUP
            hist = jnp.maximum(n0 - 1, 0) * (QBLK * dil) + r
            k_prev = kn_ref[rows(hist), :].astype(BF16)
            v_prev = v_ref[rows(hist), :].astype(BF16)
            for i in range(ATTN_GROUP):
                first = jnp.where(n0 == 0, 1, 0) if i == 0 else 0
                k_prev, v_prev = unit((n0 + i) * (QBLK * dil) + r, k_prev, v_prev, first)
            return carry

        lax.fori_loop(0, SEQ // (QBLK * ATTN_GROUP), group, 0, unroll=ATTN_UNROLL)


def _attention(proj, qg2, kg2, bias_tab):
    qoff = 2 * CONV_C // LANES
    koff = qoff + ATTN_W // LANES
    voff = koff + ATTN_W // LANES
    return pl.pallas_call(
        _attn_kernel,
        grid=(N_HEADS // 2,),
        in_specs=[
            pl.BlockSpec((SEQ, LANES), lambda h: (0, qoff + h)),
            pl.BlockSpec((SEQ, LANES), lambda h: (0, koff + h)),
            pl.BlockSpec((SEQ, LANES), lambda h: (0, voff + h)),
            pl.BlockSpec((1, LANES), lambda h: (0, 0)),
            pl.BlockSpec((1, LANES), lambda h: (0, 0)),
            pl.BlockSpec((len(PATTERNS), 2, None, 2 * QBLK, 2 * QBLK), lambda h: (0, 0, h, 0, 0)),
        ],
        out_specs=pl.BlockSpec((SEQ, LANES), lambda h: (0, h)),
        out_shape=jax.ShapeDtypeStruct((SEQ, ATTN_W), BF16),
        scratch_shapes=[pltpu.VMEM((SEQ, LANES), F32) for _ in range(5)],
        compiler_params=_cparams(("arbitrary",)),
        name="dilated_attn",
    )(proj, proj, proj, qg2, kg2, bias_tab)


def _outproj_kernel(x_ref, c_ref, a_ref, wc_ref, wa_ref, o_ref):
    o_ref[...] = (x_ref[...]
                  + jnp.dot(c_ref[...], wc_ref[...], preferred_element_type=F32)
                  + jnp.dot(a_ref[...], wa_ref[...], preferred_element_type=F32))


def _outproj(x, conv_out, attn_out, w_out_bf16):
    return pl.pallas_call(
        _outproj_kernel,
        grid=(SEQ // TM_OUT,),
        in_specs=[
            pl.BlockSpec((TM_OUT, D_MODEL), lambda i: (i, 0)),
            pl.BlockSpec((TM_OUT, CONV_C), lambda i: (i, 0)),
            pl.BlockSpec((TM_OUT, ATTN_W), lambda i: (i, 0)),
            pl.BlockSpec((CONV_C, D_MODEL), lambda i: (0, 0)),
            pl.BlockSpec((ATTN_W, D_MODEL), lambda i: (1, 0)),
        ],
        out_specs=pl.BlockSpec((TM_OUT, D_MODEL), lambda i: (i, 0)),
        out_shape=jax.ShapeDtypeStruct((SEQ, D_MODEL), F32),
        compiler_params=_cparams(("arbitrary",)),
        name="outproj",
    )(x, conv_out, attn_out, w_out_bf16, w_out_bf16)


def _split2(a):
    a1 = a.astype(BF16)
    a2 = (a - a1.astype(F32)).astype(BF16)
    return a1, a2


def _router_kernel(x_ref, g_ref, wr_ref, br_ref, eid_ref, wts_ref, rank_ref, cnt_ref):
    i = pl.program_id(0)

    @pl.when(i == 0)
    def _():
        cnt_ref[...] = jnp.zeros_like(cnt_ref)

    x = x_ref[...]
    inv = lax.rsqrt(jnp.mean(x * x, axis=-1, keepdims=True) + EPS)
    dn = (((1,), (1,)), ((), ()))
    lt = None
    for c in range(D_MODEL // KC_R):
        cols = slice(c * KC_R, (c + 1) * KC_R)
        h1, h2 = _split2(x_ref[:, cols] * inv * g_ref[:, cols])
        w1, w2 = _split2(wr_ref[:, cols])
        for wa, ha in ((w1, h1), (w1, h2), (w2, h1)):
            t = lax.dot_general(wa, ha, dn, preferred_element_type=F32)
            lt = t if lt is None else lt + t
    lt = lt + br_ref[:, 0:1]

    row8 = lax.broadcasted_iota(jnp.int32, (8, TM_R), 0)
    gl = jnp.where(row8 < N_GROUPS, lt[0:8], -jnp.inf)
    gmax = jnp.max(gl, axis=0, keepdims=True)
    gidx = jnp.min(jnp.where(gl == gmax, row8, 8), axis=0, keepdims=True)
    gw = 1.0 / jnp.sum(jnp.exp(gl - gmax), axis=0, keepdims=True)

    esel = lt[8:16]
    for g in range(1, N_GROUPS):
        esel = jnp.where(gidx == g, lt[8 + 8 * g:16 + 8 * g], esel)
    v1 = jnp.max(esel, axis=0, keepdims=True)
    i1 = jnp.min(jnp.where(esel == v1, row8, 8), axis=0, keepdims=True)
    rest = jnp.where(row8 == i1, -jnp.inf, esel)
    v2 = jnp.max(rest, axis=0, keepdims=True)
    i2 = jnp.min(jnp.where(rest == v2, row8, 8), axis=0, keepdims=True)
    e21 = jnp.exp(v2 - v1)
    den = 1.0 + e21
    e1 = gidx * E_PER_G + i1
    e2 = gidx * E_PER_G + i2
    eid_ref[0:1, :] = e1
    eid_ref[1:2, :] = e2
    wts_ref[0:1, :] = gw * (1.0 / den)
    wts_ref[1:2, :] = gw * (e21 / den)

    erow = lax.broadcasted_iota(jnp.int32, (N_EXPERTS, TM_R), 0)
    oh1 = erow == e1
    oh2 = erow == e2
    member = jnp.where(oh1 | oh2, 1.0, 0.0)
    ti = lax.broadcasted_iota(jnp.int32, (TM_R, TM_R), 0)
    tj = lax.broadcasted_iota(jnp.int32, (TM_R, TM_R), 1)
    upper = jnp.where(ti < tj, 1.0, 0.0).astype(BF16)
    before = jnp.dot(member.astype(BF16), upper, preferred_element_type=F32)
    pos = before + cnt_ref[:, 0:1]
    rank_ref[0:1, :] = jnp.sum(jnp.where(oh1, pos, 0.0), axis=0, keepdims=True).astype(jnp.int32)
    rank_ref[1:2, :] = jnp.sum(jnp.where(oh2, pos, 0.0), axis=0, keepdims=True).astype(jnp.int32)
    cnt_ref[...] = cnt_ref[...] + jnp.sum(member, axis=1, keepdims=True)


def _router(x1, g2, wr_t, br):
    return pl.pallas_call(
        _router_kernel,
        grid=(SEQ // TM_R,),
        in_specs=[
            pl.BlockSpec((TM_R, D_MODEL), lambda i: (i, 0)),
            pl.BlockSpec((1, D_MODEL), lambda i: (0, 0)),
            pl.BlockSpec((R_ROWS, D_MODEL), lambda i: (0, 0)),
            pl.BlockSpec((R_ROWS, LANES), lambda i: (0, 0)),
        ],
        out_specs=[
            pl.BlockSpec((2, TM_R), lambda i: (0, i)),
            pl.BlockSpec((2, TM_R), lambda i: (0, i)),
            pl.BlockSpec((2, TM_R), lambda i: (0, i)),
            pl.BlockSpec((N_EXPERTS, LANES), lambda i: (0, 0)),
        ],
        out_shape=[
            jax.ShapeDtypeStruct((2, SEQ), jnp.int32),
            jax.ShapeDtypeStruct((2, SEQ), F32),
            jax.ShapeDtypeStruct((2, SEQ), jnp.int32),
            jax.ShapeDtypeStruct((N_EXPERTS, LANES), F32),
        ],
        compiler_params=_cparams(("arbitrary",)),
        name="router",
    )(x1, g2, wr_t, br)


def _moe_kernel(ie_ref, row0_ref, rows_ref, nitems_ref, sorted_ref,
                x_hbm, g_ref, wg_ref, wu_ref, wd_ref, dest_hbm,
                xg_ref, xb_ref, y_ref, gsem, ssem):
    it = pl.program_id(0)
    f = pl.program_id(1)
    nrows = rows_ref[it]
    nitems = nitems_ref[0]
    slot = it % 2
    half = SUB_E // 2

    def padded(item):
        return pl.multiple_of(((rows_ref[item] + half - 1) // half) * half, half)

    nhalf = padded(it) // half
    nsub = nhalf // 2

    def start_gather(item):
        base = row0_ref[item]
        buf = item % 2

        def start(j8, c):
            for k in range(8):
                j = j8 * 8 + k
                tok = sorted_ref[base + j] & (SEQ - 1)
                pltpu.make_async_copy(x_hbm.at[pl.ds(tok, 1), :], xg_ref.at[buf, pl.ds(j, 1), :],
                                      gsem.at[buf]).start()
            return c
        lax.fori_loop(0, padded(item) // 8, start, 0)

    def wait_gather(item):
        got = pl.ds(0, padded(item))
        buf = item % 2
        pltpu.make_async_copy(x_hbm.at[got, :], xg_ref.at[buf, got, :], gsem.at[buf]).wait()

    def scatter_copy(item, j):
        dst = sorted_ref[row0_ref[item] + j]
        buf = item % 2
        return pltpu.make_async_copy(y_ref.at[buf, pl.ds(j, 1), :], dest_hbm.at[pl.ds(dst, 1), :],
                                     ssem.at[buf])

    def start_scatter(item):
        n = rows_ref[item]

        def start8(j8, c):
            for k in range(8):
                scatter_copy(item, j8 * 8 + k).start()
            return c
        lax.fori_loop(0, n // 8, start8, 0)

        def start(j, c):
            scatter_copy(item, j).start()
            return c
        lax.fori_loop((n // 8) * 8, n, start, 0)

    def wait_scatter(item):
        n = rows_ref[item]
        whole = pl.multiple_of((n // 8) * 8, 8)
        buf = item % 2

        @pl.when(whole > 0)
        def _():
            sent = pl.ds(0, whole)
            pltpu.make_async_copy(y_ref.at[buf, sent, :], dest_hbm.at[sent, :], ssem.at[buf]).wait()

        def wait(j, c):
            scatter_copy(item, j).wait()
            return c
        lax.fori_loop(whole, n, wait, 0)

    @pl.when((f == 0) & (nrows > 0))
    def _():
        @pl.when(it == 0)
        def _():
            start_gather(0)

        wait_gather(it)

        def norm(s, c):
            rows = pl.ds(pl.multiple_of(s * half, half), half)
            x = xg_ref[slot, rows, :]
            ms = jnp.mean(x * x, axis=-1, keepdims=True)
            xb_ref[rows, :] = (x * lax.rsqrt(ms + EPS) * g_ref[...]).astype(BF16)
            y_ref[slot, rows, :] = jnp.zeros((half, D_MODEL), F32)
            return c
        lax.fori_loop(0, nhalf, norm, 0)

        @pl.when(it + 1 < nitems)
        def _():
            start_gather(it + 1)

    @pl.when(nrows > 0)
    def _():
        wg = wg_ref[...].astype(BF16)
        wu = wu_ref[...].astype(BF16)
        wd = wd_ref[...].astype(BF16)

        def block(rows):
            xb = xb_ref[rows, :]
            hg = jnp.dot(xb, wg, preferred_element_type=F32)
            hu = jnp.dot(xb, wu, preferred_element_type=F32)
            h = (hg * jax.nn.sigmoid(hg) * hu).astype(BF16)
            y_ref[slot, rows, :] = y_ref[slot, rows, :] + jnp.dot(h, wd, preferred_element_type=F32)

        def sub(s, c):
            block(pl.ds(pl.multiple_of(s * SUB_E, SUB_E), SUB_E))
            return c
        lax.fori_loop(0, nsub, sub, 0)

        @pl.when(nhalf % 2 == 1)
        def _():
            block(pl.ds(pl.multiple_of(nsub * SUB_E, SUB_E), half))

    @pl.when((f == NF_E - 1) & (nrows > 0))
    def _():
        @pl.when(it > 0)
        def _():
            wait_scatter(it - 1)

        start_scatter(it)

        @pl.when(it + 1 >= nitems)
        def _():
            wait_scatter(it)


def _moe(ie, row0, rows, nitems, sorted_i, x1, g2, wg, wu, wd):
    def wmap_cols(i, f, ie_ref, row0_ref, rows_ref, n_ref, s_ref):
        return (ie_ref[i], 0, jnp.where(i < n_ref[0], f, NF_E - 1))

    def wmap_rows(i, f, ie_ref, row0_ref, rows_ref, n_ref, s_ref):
        return (ie_ref[i], jnp.where(i < n_ref[0], f, NF_E - 1), 0)

    grid_spec = pltpu.PrefetchScalarGridSpec(
        num_scalar_prefetch=5,
        grid=(nitems[0], NF_E),
        in_specs=[
            pl.BlockSpec(memory_space=pl.ANY),
            pl.BlockSpec((1, D_MODEL), lambda i, f, *_: (0, 0)),
            pl.BlockSpec((None, D_MODEL, TF_E), wmap_cols),
            pl.BlockSpec((None, D_MODEL, TF_E), wmap_cols),
            pl.BlockSpec((None, TF_E, D_MODEL), wmap_rows),
        ],
        out_specs=pl.BlockSpec(memory_space=pl.ANY),
        scratch_shapes=[
            pltpu.VMEM((2, TM_E, D_MODEL), F32),
            pltpu.VMEM((TM_E, D_MODEL), BF16),
            pltpu.VMEM((2, TM_E, D_MODEL), F32),
            pltpu.SemaphoreType.DMA((2,)),
            pltpu.SemaphoreType.DMA((2,)),
        ],
    )
    return pl.pallas_call(
        _moe_kernel,
        grid_spec=grid_spec,
        out_shape=jax.ShapeDtypeStruct((N_ASSIGN, D_MODEL), F32),
        compiler_params=_cparams(("arbitrary", "arbitrary")),
        name="moe_experts",
    )(ie, row0, rows, nitems, sorted_i, x1, g2, wg, wu, wd)


def _combine_kernel(x_ref, w_ref, d0_ref, d1_ref, o_ref):
    w = w_ref[...]
    o_ref[...] = x_ref[...] + w[:, 0:1] * d0_ref[...] + w[:, 1:2] * d1_ref[...]


def _combine(x1, wts_t, dest):
    dest3 = dest.reshape(2, SEQ, D_MODEL)
    return pl.pallas_call(
        _combine_kernel,
        grid=(SEQ // TT_COMB,),
        in_specs=[
            pl.BlockSpec((TT_COMB, D_MODEL), lambda i: (i, 0)),
            pl.BlockSpec((TT_COMB, 2), lambda i: (i, 0)),
            pl.BlockSpec((None, TT_COMB, D_MODEL), lambda i: (0, i, 0)),
            pl.BlockSpec((None, TT_COMB, D_MODEL), lambda i: (1, i, 0)),
        ],
        out_specs=pl.BlockSpec((TT_COMB, D_MODEL), lambda i: (i, 0)),
        out_shape=jax.ShapeDtypeStruct((SEQ, D_MODEL), F32),
        compiler_params=_cparams(("arbitrary",)),
        name="moe_combine",
    )(x1, wts_t, dest3, dest3)


def _dispatch_kernel(slot_ref, padrow_ref, x_ref, g_ref, xs_hbm, hbuf, zbuf, sem, zsem):
    i = pl.program_id(0)
    last = pl.num_programs(0) - 1
    buf = i % 2

    def tile_wait(b):
        for _ in range(2):
            pltpu.make_async_copy(hbuf.at[b], xs_hbm.at[pl.ds(0, TT_D), :], sem.at[b]).wait()

    @pl.when(i == 0)
    def _():
        zbuf[...] = jnp.zeros_like(zbuf)

        for c in range(XS_TAIL // ZROWS):
            pltpu.make_async_copy(zbuf, xs_hbm.at[pl.ds(XS_ROWS - XS_TAIL + c * ZROWS, ZROWS), :], zsem).start()
        for c in range(XS_TAIL // ZROWS):
            pltpu.make_async_copy(zbuf, xs_hbm.at[pl.ds(0, ZROWS), :], zsem).wait()

        def pad(e, c):
            dst = pl.multiple_of(padrow_ref[e], 8)
            pltpu.make_async_copy(zbuf.at[pl.ds(0, 8), :], xs_hbm.at[pl.ds(dst, 8), :], zsem).start()
            return c
        lax.fori_loop(0, N_EXPERTS, pad, 0)

        def padw(e, c):
            pltpu.make_async_copy(zbuf.at[pl.ds(0, 8), :], xs_hbm.at[pl.ds(0, 8), :], zsem).wait()
            return c
        lax.fori_loop(0, N_EXPERTS, padw, 0)

    @pl.when(i >= 2)
    def _():
        tile_wait(buf)

    x = x_ref[...]
    ms = jnp.mean(x * x, axis=-1, keepdims=True)
    hbuf[buf] = x * lax.rsqrt(ms + EPS) * g_ref[...]

    tok0 = i * TT_D

    def send(j8, c):
        for k8 in range(8):
            j = j8 * 8 + k8
            for k in range(2):
                dst = slot_ref[k * SEQ + tok0 + j]
                pltpu.make_async_copy(hbuf.at[buf, pl.ds(j, 1), :], xs_hbm.at[pl.ds(dst, 1), :],
                                      sem.at[buf]).start()
        return c
    lax.fori_loop(0, TT_D // 8, send, 0)

    @pl.when(i == last)
    def _():
        @pl.when(i >= 1)
        def _():
            tile_wait(1 - buf)
        tile_wait(buf)


def _dispatch(slot_flat, padrow, x1, g2):
    grid_spec = pltpu.PrefetchScalarGridSpec(
        num_scalar_prefetch=2,
        grid=(SEQ // TT_D,),
        in_specs=[
            pl.BlockSpec((TT_D, D_MODEL), lambda i, *_: (i, 0)),
            pl.BlockSpec((1, D_MODEL), lambda i, *_: (0, 0)),
        ],
        out_specs=pl.BlockSpec(memory_space=pl.ANY),
        scratch_shapes=[
            pltpu.VMEM((2, TT_D, D_MODEL), F32),
            pltpu.VMEM((ZROWS, D_MODEL), F32),
            pltpu.SemaphoreType.DMA((2,)),
            pltpu.SemaphoreType.DMA,
        ],
    )
    return pl.pallas_call(
        _dispatch_kernel,
        grid_spec=grid_spec,
        out_shape=jax.ShapeDtypeStruct((XS_ROWS, D_MODEL), F32),
        compiler_params=_cparams(("arbitrary",)),
        name="moe_dispatch",
    )(slot_flat, padrow, x1, g2)


def _experts_kernel(ie_ref, row0_ref, rows_ref, nitems_ref,
                    xs_hbm, wg_ref, wu_ref, wd_ref, ys_hbm,
                    xg_ref, y_ref, zbuf, gsem, ssem, zsem):
    it = pl.program_id(0)
    f = pl.program_id(1)
    nrows = rows_ref[it]
    nitems = nitems_ref[0]
    slot = it % 2
    half = SUB_E // 2

    def padded(item):
        return pl.multiple_of(((rows_ref[item] + half - 1) // half) * half, half)

    nhalf = padded(it) // half
    nsub = nhalf // 2

    def chunk_copies(item, start):
        base = pl.multiple_of(row0_ref[item], 8)
        buf = item % 2

        def body(c, carry):
            off = pl.multiple_of(c * half, half)
            src = xs_hbm.at[pl.ds(pl.multiple_of(base + off, 8), half), :]
            dst = ys_hbm.at[pl.ds(pl.multiple_of(base + off, 8), half), :]
            if start == "fetch":
                pltpu.make_async_copy(src, xg_ref.at[buf, pl.ds(off, half), :], gsem.at[buf]).start()
            else:
                pltpu.make_async_copy(y_ref.at[buf, pl.ds(off, half), :], dst, ssem.at[buf]).start()
            return carry
        lax.fori_loop(0, padded(item) // half, body, 0)

    def wait_fetch(item):
        got = pl.ds(0, padded(item))
        buf = item % 2
        pltpu.make_async_copy(xs_hbm.at[got, :], xg_ref.at[buf, got, :], gsem.at[buf]).wait()

    def wait_store(item):
        put = pl.ds(0, padded(item))
        buf = item % 2
        pltpu.make_async_copy(y_ref.at[buf, put, :], ys_hbm.at[put, :], ssem.at[buf]).wait()

    @pl.when((it == 0) & (f == 0))
    def _():
        zbuf[...] = jnp.zeros_like(zbuf)
        for c in range(XS_TAIL // ZROWS):
            pltpu.make_async_copy(zbuf, ys_hbm.at[pl.ds(XS_ROWS - XS_TAIL + c * ZROWS, ZROWS), :], zsem).start()
        for c in range(XS_TAIL // ZROWS):
            pltpu.make_async_copy(zbuf, ys_hbm.at[pl.ds(0, ZROWS), :], zsem).wait()
        chunk_copies(0, "fetch")

    @pl.when(f == 0)
    def _():
        wait_fetch(it)

        def clear(s, c):
            rows = pl.ds(pl.multiple_of(s * half, half), half)
            y_ref[slot, rows, :] = jnp.zeros((half, D_MODEL), F32)
            return c
        lax.fori_loop(0, nhalf, clear, 0)

        @pl.when(it + 1 < nitems)
        def _():
            chunk_copies(it + 1, "fetch")

    def block(rows):
        xb = xg_ref[slot, rows, :].astype(BF16)
        hg = jnp.dot(xb, wg_ref[...].astype(BF16), preferred_element_type=F32)
        hu = jnp.dot(xb, wu_ref[...].astype(BF16), preferred_element_type=F32)
        h = (hg * jax.nn.sigmoid(hg) * hu).astype(BF16)
        y_ref[slot, rows, :] = y_ref[slot, rows, :] + jnp.dot(h, wd_ref[...].astype(BF16),
                                                               preferred_element_type=F32)

    def sub(s, c):
        block(pl.ds(pl.multiple_of(s * SUB_E, SUB_E), SUB_E))
        return c
    lax.fori_loop(0, nsub, sub, 0)

    @pl.when(nhalf % 2 == 1)
    def _():
        block(pl.ds(pl.multiple_of(nsub * SUB_E, SUB_E), half))

    @pl.when(f == NF_E - 1)
    def _():
        @pl.when(it > 0)
        def _():
            wait_store(it - 1)

        chunk_copies(it, "store")

        @pl.when(it + 1 >= nitems)
        def _():
            wait_store(it)


def _experts(ie, row0, rows, nitems, xs, wg, wu, wd):
    def wmap_cols(i, f, ie_ref, *_):
        return (ie_ref[i], 0, f)

    def wmap_rows(i, f, ie_ref, *_):
        return (ie_ref[i], f, 0)

    grid_spec = pltpu.PrefetchScalarGridSpec(
        num_scalar_prefetch=4,
        grid=(nitems[0], NF_E),
        in_specs=[
            pl.BlockSpec(memory_space=pl.ANY),
            pl.BlockSpec((None, D_MODEL, TF_E), wmap_cols),
            pl.BlockSpec((None, D_MODEL, TF_E), wmap_cols),
            pl.BlockSpec((None, TF_E, D_MODEL), wmap_rows),
        ],
        out_specs=pl.BlockSpec(memory_space=pl.ANY),
        scratch_shapes=[
            pltpu.VMEM((2, TM_E, D_MODEL), F32),
            pltpu.VMEM((2, TM_E, D_MODEL), F32),
            pltpu.VMEM((ZROWS, D_MODEL), F32),
            pltpu.SemaphoreType.DMA((2,)),
            pltpu.SemaphoreType.DMA((2,)),
            pltpu.SemaphoreType.DMA,
        ],
    )
    return pl.pallas_call(
        _experts_kernel,
        grid_spec=grid_spec,
        out_shape=jax.ShapeDtypeStruct((XS_ROWS, D_MODEL), F32),
        compiler_params=_cparams(("arbitrary", "arbitrary")),
        name="moe_experts",
    )(ie, row0, rows, nitems, xs, wg, wu, wd)


def _gcombine_kernel(slot_ref, x_ref, w_ref, ys_hbm, o_ref, dbuf, sem):
    i = pl.program_id(0)
    n = pl.num_programs(0)
    buf = i % 2

    def fetch(tile, b):
        tok0 = tile * TT_G

        def body(j8, c):
            for k8 in range(8):
                j = j8 * 8 + k8
                for k in range(2):
                    src = slot_ref[k * SEQ + tok0 + j]
                    pltpu.make_async_copy(ys_hbm.at[pl.ds(src, 1), :], dbuf.at[b, k, pl.ds(j, 1), :],
                                          sem.at[b]).start()
            return c
        lax.fori_loop(0, TT_G // 8, body, 0)

    @pl.when(i == 0)
    def _():
        fetch(0, 0)

    @pl.when(i + 1 < n)
    def _():
        fetch(i + 1, 1 - buf)

    for k in range(2):
        pltpu.make_async_copy(ys_hbm.at[pl.ds(0, TT_G), :], dbuf.at[buf, k], sem.at[buf]).wait()

    w = w_ref[...]
    o_ref[...] = x_ref[...] + w[:, 0:1] * dbuf[buf, 0] + w[:, 1:2] * dbuf[buf, 1]


def _gcombine(slot_flat, x1, wts_t, ys):
    grid_spec = pltpu.PrefetchScalarGridSpec(
        num_scalar_prefetch=1,
        grid=(SEQ // TT_G,),
        in_specs=[
            pl.BlockSpec((TT_G, D_MODEL), lambda i, *_: (i, 0)),
            pl.BlockSpec((TT_G, 2), lambda i, *_: (i, 0)),
            pl.BlockSpec(memory_space=pl.ANY),
        ],
        out_specs=pl.BlockSpec((TT_G, D_MODEL), lambda i, *_: (i, 0)),
        scratch_shapes=[
            pltpu.VMEM((2, 2, TT_G, D_MODEL), F32),
            pltpu.SemaphoreType.DMA((2,)),
        ],
    )
    return pl.pallas_call(
        _gcombine_kernel,
        grid_spec=grid_spec,
        out_shape=jax.ShapeDtypeStruct((SEQ, D_MODEL), F32),
        compiler_params=_cparams(("arbitrary",)),
        name="moe_combine",
    )(slot_flat, x1, wts_t, ys)


def _routing_tables(eid, rank, counts_f):
    counts = counts_f[:, 0].astype(jnp.int32)
    aligned = ((counts + 7) // 8) * 8
    base = jnp.cumsum(aligned) - aligned
    tiles = (counts + TM_E - 1) // TM_E
    tcum = jnp.cumsum(tiles)
    tstart = tcum - tiles
    nitems = tcum[-1]
    ids = jnp.arange(MAX_ITEMS, dtype=jnp.int32)
    ie = jnp.clip(jnp.searchsorted(tcum, ids, side="right"), 0, N_EXPERTS - 1).astype(jnp.int32)
    live = ids < nitems
    ie = jnp.where(live, ie, ie[jnp.maximum(nitems - 1, 0)])
    jt = ids - tstart[ie]
    row0 = jnp.where(live, base[ie] + jt * TM_E, 0)
    rows = jnp.where(live, jnp.clip(counts[ie] - jt * TM_E, 0, TM_E), 0)
    eoh = eid[:, :, None] == jnp.arange(N_EXPERTS, dtype=jnp.int32)
    slot = jnp.sum(jnp.where(eoh, base, 0), axis=-1) + rank
    spare = XS_ROWS - 8 * (1 + jnp.arange(N_EXPERTS, dtype=jnp.int32))
    padrow = jnp.where(counts % 8 != 0, base + (counts // 8) * 8, spare)
    return (ie, row0.astype(jnp.int32), rows.astype(jnp.int32), nitems.reshape(1).astype(jnp.int32),
            slot.reshape(-1).astype(jnp.int32), padrow.astype(jnp.int32))


def _work_items(eid, rank, counts_f):
    counts = counts_f[:, 0].astype(jnp.int32)
    cum = jnp.cumsum(counts)
    base = cum - counts
    tiles = (counts + TM_E - 1) // TM_E
    tcum = jnp.cumsum(tiles)
    tstart = tcum - tiles
    nitems = tcum[-1]
    ids = jnp.arange(MAX_ITEMS, dtype=jnp.int32)
    ie = jnp.clip(jnp.searchsorted(tcum, ids, side="right"), 0, N_EXPERTS - 1).astype(jnp.int32)
    live = ids < nitems
    ie = jnp.where(live, ie, ie[jnp.maximum(nitems - 1, 0)])
    jt = ids - tstart[ie]
    row0 = jnp.where(live, base[ie] + jt * TM_E, 0)
    rows = jnp.where(live, jnp.clip(counts[ie] - jt * TM_E, 0, TM_E), 0)
    eoh = eid[:, :, None] == jnp.arange(N_EXPERTS, dtype=jnp.int32)
    slot = jnp.sum(jnp.where(eoh, base, 0), axis=-1) + rank
    sorted_i = jnp.zeros((SORTED_LEN,), jnp.int32).at[slot.reshape(-1)].set(
        jnp.arange(N_ASSIGN, dtype=jnp.int32))
    return ie, row0.astype(jnp.int32), rows.astype(jnp.int32), nitems.reshape(1).astype(jnp.int32), sorted_i


def kernel(x, norm1_g, w_in, q_norm_g, k_norm_g, conv_w, conv_b, conv_ln_g, conv_ln_b, rel_bias,
           w_out, norm2_g, w_router_group, b_router_group, w_router_expert, b_router_expert,
           w_gate, w_up, w_down):
    assert x.shape == (1, SEQ, D_MODEL) and w_in.shape[0] == 1
    xs = x[0]
    bias_tab = _attn_bias_tables(rel_bias)
    qg2 = jnp.tile(q_norm_g[0], 2)[None]
    kg2 = jnp.tile(k_norm_g[0], 2)[None]

    proj = _inproj(xs, norm1_g[0][None], w_in[0])
    conv_out = _conv_mixer(proj, conv_w[0], conv_b[0][None], conv_ln_g[0][None], conv_ln_b[0][None])
    attn_out = _attention(proj, qg2, kg2, bias_tab)
    x1 = _outproj(xs, conv_out, attn_out, w_out[0].astype(BF16))

    wr_t = jnp.concatenate([
        w_router_group[0].T, jnp.zeros((8 - N_GROUPS, D_MODEL), F32),
        jnp.transpose(w_router_expert[0], (0, 2, 1)).reshape(N_EXPERTS, D_MODEL)], axis=0)
    br = jnp.concatenate([b_router_group[0], jnp.zeros((8 - N_GROUPS,), F32),
                          b_router_expert[0].reshape(-1)])
    br = jnp.broadcast_to(br[:, None], (R_ROWS, LANES))
    eid, wts, rank, counts_f = _router(x1, norm2_g[0][None], wr_t, br)

    ie, row0, rows, nitems, slot_flat, padrow = _routing_tables(eid, rank, counts_f)
    xs = _dispatch(slot_flat, padrow, x1, norm2_g[0][None])
    ys = _experts(ie, row0, rows, nitems, xs,
                  w_gate[0].reshape(N_EXPERTS, D_MODEL, D_FF),
                  w_up[0].reshape(N_EXPERTS, D_MODEL, D_FF),
                  w_down[0].reshape(N_EXPERTS, D_FF, D_MODEL))
    out = _gcombine(slot_flat, x1, wts.T, ys)
    return out[None]
```

```python
import functools
import math

import numpy as np
import jax
import jax.numpy as jnp
from jax import lax
from jax.experimental import pallas as pl
from jax.experimental.pallas import tpu as pltpu

F32 = jnp.float32
BF16 = jnp.bfloat16

D_MODEL = 2048
SEQ = 8192
N_HEADS = 16
HEAD_DIM = 64
ATTN_W = N_HEADS * HEAD_DIM
CONV_C = D_MODEL - ATTN_W
CONV_K = 31
IN_W = 2 * CONV_C + 3 * ATTN_W
PATTERNS = ((128, 1), (512, 4), (2048, 16))
QBLK = 128
NUM_BUCKETS = 32
MAX_DISTANCE = 2048
N_GROUPS = 4
E_PER_G = 8
N_EXPERTS = N_GROUPS * E_PER_G
D_FF = D_MODEL // 2
EPS = 1e-6
NEG_INF = -1e30
LOG2E = math.log2(math.e)

LANES = 128
VMEM_LIMIT = 56 * 1024 * 1024

TM_IN = 1024
TN_IN = 1024
TT_CONV = 512
HALO = 32
R_CONV = 64
R_LN = 16
ATTN_GROUP = 4
ATTN_UNROLL = 2
TM_OUT = 512
TM_R = 512
R_ROWS = 8 + N_EXPERTS
KC_R = 512
TM_E = 768
SUB_E = 256
TF_E = 256
NF_E = D_FF // TF_E
W_RING = 3
N_ASSIGN = 2 * SEQ
MAX_ITEMS = -(-N_ASSIGN // TM_E) + N_EXPERTS
SORTED_LEN = N_ASSIGN + TM_E
XS_TAIL = 256 + TM_E
XS_ROWS = N_ASSIGN + XS_TAIL
ZROWS = 128
TT_D = 512
TT_G = 512
TT_COMB = 512


def _cparams(sem, vmem=VMEM_LIMIT, flags=None):
    return pltpu.CompilerParams(dimension_semantics=sem, vmem_limit_bytes=vmem, flags=flags)


def _inproj_kernel(x_ref, g_ref, w_ref, o_ref, xn_ref):
    @pl.when(pl.program_id(1) == 0)
    def _():
        x = x_ref[...]
        ms = jnp.mean(x * x, axis=-1, keepdims=True)
        xn_ref[...] = (x * lax.rsqrt(ms + EPS) * g_ref[...]).astype(BF16)

    o_ref[...] = jnp.dot(xn_ref[...], w_ref[...].astype(BF16), preferred_element_type=F32)


def _inproj(x, g, w):
    return pl.pallas_call(
        _inproj_kernel,
        grid=(SEQ // TM_IN, IN_W // TN_IN),
        in_specs=[
            pl.BlockSpec((TM_IN, D_MODEL), lambda i, j: (i, 0)),
            pl.BlockSpec((1, D_MODEL), lambda i, j: (0, 0)),
            pl.BlockSpec((D_MODEL, TN_IN), lambda i, j: (0, j)),
        ],
        out_specs=pl.BlockSpec((TM_IN, TN_IN), lambda i, j: (i, j)),
        out_shape=jax.ShapeDtypeStruct((SEQ, IN_W), F32),
        scratch_shapes=[pltpu.VMEM((TM_IN, D_MODEL), BF16)],
        compiler_params=_cparams(("arbitrary", "arbitrary")),
        name="inproj",
    )(x, g, w)


def _conv_kernel(val_ref, gate_ref, hval_ref, hgate_ref, cw_ref, cb_ref, lg_ref, lb_ref,
                 o_ref, ubuf, zbuf, ybuf):
    i = pl.program_id(0)
    u = val_ref[...] * jax.nn.sigmoid(gate_ref[...])
    hu = hval_ref[...] * jax.nn.sigmoid(hgate_ref[...])
    hu = jnp.where(i > 0, hu, 0.0)
    for c in range(CONV_C // LANES):
        cols = slice(c * LANES, (c + 1) * LANES)
        ubuf[c, 0:HALO, :] = hu[:, cols]
        ubuf[c, HALO:HALO + TT_CONV, :] = u[:, cols]

    n_a = -(-CONV_K // 8)
    assert HALO == 8 * n_a
    for c in range(CONV_C // LANES):
        cols = slice(c * LANES, (c + 1) * LANES)

        def taps(r, carry, c=c, cols=cols):
            base = pl.multiple_of(r * R_CONV, R_CONV)
            win = ubuf[c, pl.ds(base, R_CONV + HALO), :]
            for b in range(8):
                z = None
                for a in range(n_a):
                    s = 8 * a + b
                    if s >= CONV_K:
                        continue
                    lo = HALO - 8 - 8 * a
                    t = cw_ref[CONV_K - 1 - s:CONV_K - s, cols] * win[lo:lo + R_CONV + 8, :]
                    z = t if z is None else z + t
                zbuf[b, pl.ds(0, R_CONV + 8, stride=2), :] = z
            acc = None
            for b in range(8):
                t = zbuf[b, pl.ds(2 * (8 - b), R_CONV, stride=2), :]
                acc = t if acc is None else acc + t
            ybuf[pl.ds(base, R_CONV), cols] = acc
            return carry

        lax.fori_loop(0, TT_CONV // R_CONV, taps, 0)

    def norm(r, carry):
        rows = pl.ds(pl.multiple_of(r * R_LN, R_LN), R_LN)
        acc = ybuf[rows, :] + cb_ref[...]
        mu = jnp.mean(acc, axis=-1, keepdims=True)
        xc = acc - mu
        var = jnp.mean(xc * xc, axis=-1, keepdims=True)
        y = xc * lax.rsqrt(var + EPS) * lg_ref[...] + lb_ref[...]
        o_ref[rows, :] = (y * jax.nn.sigmoid(y)).astype(BF16)
        return carry

    lax.fori_loop(0, TT_CONV // R_LN, norm, 0, unroll=4)


def _conv_mixer(proj, cw, cb, lg, lb):
    hb = TT_CONV // HALO
    return pl.pallas_call(
        _conv_kernel,
        grid=(SEQ // TT_CONV,),
        in_specs=[
            pl.BlockSpec((TT_CONV, CONV_C), lambda i: (i, 0)),
            pl.BlockSpec((TT_CONV, CONV_C), lambda i: (i, 1)),
            pl.BlockSpec((HALO, CONV_C), lambda i: (jnp.maximum(i * hb - 1, 0), 0)),
            pl.BlockSpec((HALO, CONV_C), lambda i: (jnp.maximum(i * hb - 1, 0), 1)),
            pl.BlockSpec((CONV_K, CONV_C), lambda i: (0, 0)),
            pl.BlockSpec((1, CONV_C), lambda i: (0, 0)),
            pl.BlockSpec((1, CONV_C), lambda i: (0, 0)),
            pl.BlockSpec((1, CONV_C), lambda i: (0, 0)),
        ],
        out_specs=pl.BlockSpec((TT_CONV, CONV_C), lambda i: (i, 0)),
        out_shape=jax.ShapeDtypeStruct((SEQ, CONV_C), BF16),
        scratch_shapes=[pltpu.VMEM((CONV_C // LANES, HALO + TT_CONV, LANES), F32),
                        pltpu.VMEM((8, 2 * (R_CONV + 8), LANES), F32),
                        pltpu.VMEM((TT_CONV, CONV_C), F32)],
        compiler_params=_cparams(("arbitrary",)),
        name="conv_mixer",
    )(proj, proj, proj, proj, cw, cb, lg, lb)


def _t5_bucket_np(dist):
    max_exact = NUM_BUCKETS // 2
    nf = np.maximum(dist, 1).astype(np.float32)
    large = max_exact + (np.log(nf / np.float32(max_exact)) / np.float32(math.log(MAX_DISTANCE / max_exact))
                         * np.float32(NUM_BUCKETS - max_exact)).astype(np.int32)
    large = np.minimum(large, NUM_BUCKETS - 1)
    return np.where(dist < max_exact, dist, large)


def _attn_bias_tables(rel_bias):
    kj = np.arange(2 * QBLK)[None, None, :]
    period = 3 * QBLK
    tables = []
    for window, dil in PATTERNS:
        span = window // dil
        assert span <= QBLK
        bucket = _t5_bucket_np(np.arange(span + 1) * dil)
        onehot = np.eye(NUM_BUCKETS, dtype=np.float32)[bucket]
        vec = jnp.einsum("rb,bh->hr", onehot, rel_bias.astype(F32),
                         precision=lax.Precision.HIGHEST)
        diag = jnp.full((N_HEADS, period), NEG_INF, F32)
        diag = diag.at[:, 2 * QBLK - 1 - span:2 * QBLK].set(vec[:, ::-1])
        skew = jnp.tile(diag, (1, QBLK))[:, :QBLK * (period - 1)].reshape(N_HEADS, QBLK, period - 1)
        full = skew[:, :, QBLK - 1:3 * QBLK - 1]
        first = jnp.where(kj >= QBLK, full, NEG_INF)
        both = jnp.stack([full, first])
        tables.append(both.reshape(2, N_HEADS // 2, 2 * QBLK, 2 * QBLK))
    return jnp.stack(tables) * LOG2E


def _attn_kernel(q_ref, k_ref, v_ref, qg_ref, kg_ref, bias_ref, o_ref,
                 qn_ref, kn_ref, acc_ref, m_ref, l_ref):
    lane = lax.broadcasted_iota(jnp.int32, (QBLK, LANES), 1)
    head_a = lane < HEAD_DIM
    ri = lax.broadcasted_iota(jnp.int32, (LANES, LANES), 0) // HEAD_DIM
    ci = lax.broadcasted_iota(jnp.int32, (LANES, LANES), 1) // HEAD_DIM
    seg = (ri == ci).astype(BF16)

    def head_rms(x, g):
        sq = x * x
        hi = sq.astype(BF16)
        lo = (sq - hi.astype(F32)).astype(BF16)
        ss = (jnp.dot(hi, seg, preferred_element_type=F32)
              + jnp.dot(lo, seg, preferred_element_type=F32))
        return x * lax.rsqrt(ss * (1.0 / HEAD_DIM) + EPS) * g

    NCH = 512

    def norm_body(c, carry):
        rows = pl.ds(pl.multiple_of(c * NCH, NCH), NCH)
        qn_ref[rows, :] = head_rms(q_ref[rows, :], qg_ref[...]) * (LOG2E / math.sqrt(HEAD_DIM))
        kn_ref[rows, :] = head_rms(k_ref[rows, :], kg_ref[...])
        return carry

    lax.fori_loop(0, SEQ // NCH, norm_body, 0)

    order = sorted(range(len(PATTERNS)), key=lambda i: -PATTERNS[i][1])
    assert PATTERNS[order[-1]][1] == 1
    for step, p in enumerate(order):
        dil = PATTERNS[p][1]
        nb = SEQ // (dil * QBLK)
        is_first = step == 0
        is_last = step == len(order) - 1

        def rows(start, dil=dil):
            if dil == 1:
                return pl.ds(start, QBLK)
            return pl.ds(start, QBLK, stride=dil)

        def unit(cur, k_prev, v_prev, first, p=p, rows=rows, is_first=is_first, is_last=is_last):
            q = qn_ref[rows(cur), :]
            q2 = jnp.concatenate([jnp.where(head_a, q, 0.0), jnp.where(head_a, 0.0, q)],
                                 axis=0).astype(BF16)
            k_cur = kn_ref[rows(cur), :].astype(BF16)
            v_cur = v_ref[rows(cur), :].astype(BF16)
            k2 = jnp.concatenate([k_prev, k_cur], axis=0)
            v2 = jnp.concatenate([v_prev, v_cur], axis=0)
            s = lax.dot_general(q2, k2, (((1,), (1,)), ((), ())), preferred_element_type=F32)
            s = s + bias_ref[p, first]
            m = jnp.max(s, axis=-1, keepdims=True)
            e = jnp.exp2(s - m)
            l = jnp.sum(e, axis=-1, keepdims=True)
            pv = jnp.dot(e.astype(BF16), v2, preferred_element_type=F32)
            o_new = jnp.where(head_a, pv[:QBLK], pv[QBLK:])
            m_new = jnp.where(head_a, m[:QBLK], m[QBLK:])
            l_new = jnp.where(head_a, l[:QBLK], l[QBLK:])
            if is_first:
                acc_ref[rows(cur), :] = o_new
                m_ref[rows(cur), :] = m_new
                l_ref[rows(cur), :] = l_new
            else:
                m_old = m_ref[rows(cur), :]
                m_tot = jnp.maximum(m_old, m_new)
                a = jnp.exp2(m_old - m_tot)
                b = jnp.exp2(m_new - m_tot)
                acc = acc_ref[rows(cur), :] * a + o_new * b
                den = l_ref[rows(cur), :] * a + l_new * b
                if is_last:
                    o_ref[pl.ds(pl.multiple_of(cur, QBLK), QBLK), :] = (acc / den).astype(BF16)
                else:
                    acc_ref[rows(cur), :] = acc
                    l_ref[rows(cur), :] = den
                    m_ref[rows(cur), :] = m_tot
            return k_cur, v_cur

        per = nb // ATTN_GROUP
        assert per * ATTN_GROUP == nb

        def group(g, carry, dil=dil, per=per, rows=rows, unit=unit):
            r = g // per
            n0 = (g - r * per) * ATTN_GROUP
            hist = jnp.maximum(n0 - 1, 0) * (QBLK * dil) + r
            k_prev = kn_ref[rows(hist), :].astype(BF16)
            v_prev = v_ref[rows(hist), :].astype(BF16)
            for i in range(ATTN_GROUP):
                first = jnp.where(n0 == 0, 1, 0) if i == 0 else 0
                k_prev, v_prev = unit((n0 + i) * (QBLK * dil) + r, k_prev, v_prev, first)
            return carry

        lax.fori_loop(0, SEQ // (QBLK * ATTN_GROUP), group, 0, unroll=ATTN_UNROLL)


def _attention(proj, qg2, kg2, bias_tab):
    qoff = 2 * CONV_C // LANES
    koff = qoff + ATTN_W // LANES
    voff = koff + ATTN_W // LANES
    return pl.pallas_call(
        _attn_kernel,
        grid=(N_HEADS // 2,),
        in_specs=[
            pl.BlockSpec((SEQ, LANES), lambda h: (0, qoff + h)),
            pl.BlockSpec((SEQ, LANES), lambda h: (0, koff + h)),
            pl.BlockSpec((SEQ, LANES), lambda h: (0, voff + h)),
            pl.BlockSpec((1, LANES), lambda h: (0, 0)),
            pl.BlockSpec((1, LANES), lambda h: (0, 0)),
            pl.BlockSpec((len(PATTERNS), 2, None, 2 * QBLK, 2 * QBLK), lambda h: (0, 0, h, 0, 0)),
        ],
        out_specs=pl.BlockSpec((SEQ, LANES), lambda h: (0, h)),
        out_shape=jax.ShapeDtypeStruct((SEQ, ATTN_W), BF16),
        scratch_shapes=[pltpu.VMEM((SEQ, LANES), F32) for _ in range(5)],
        compiler_params=_cparams(("arbitrary",)),
        name="dilated_attn",
    )(proj, proj, proj, qg2, kg2, bias_tab)


def _outproj_kernel(x_ref, c_ref, a_ref, wc_ref, wa_ref, o_ref):
    o_ref[...] = (x_ref[...]
                  + jnp.dot(c_ref[...], wc_ref[...], preferred_element_type=F32)
                  + jnp.dot(a_ref[...], wa_ref[...], preferred_element_type=F32))


def _outproj(x, conv_out, attn_out, w_out_bf16):
    return pl.pallas_call(
        _outproj_kernel,
        grid=(SEQ // TM_OUT,),
        in_specs=[
            pl.BlockSpec((TM_OUT, D_MODEL), lambda i: (i, 0)),
            pl.BlockSpec((TM_OUT, CONV_C), lambda i: (i, 0)),
            pl.BlockSpec((TM_OUT, ATTN_W), lambda i: (i, 0)),
            pl.BlockSpec((CONV_C, D_MODEL), lambda i: (0, 0)),
            pl.BlockSpec((ATTN_W, D_MODEL), lambda i: (1, 0)),
        ],
        out_specs=pl.BlockSpec((TM_OUT, D_MODEL), lambda i: (i, 0)),
        out_shape=jax.ShapeDtypeStruct((SEQ, D_MODEL), F32),
        compiler_params=_cparams(("arbitrary",)),
        name="outproj",
    )(x, conv_out, attn_out, w_out_bf16, w_out_bf16)


def _split2(a):
    a1 = a.astype(BF16)
    a2 = (a - a1.astype(F32)).astype(BF16)
    return a1, a2


def _router_kernel(x_ref, g_ref, wr_ref, br_ref, eid_ref, wts_ref, rank_ref, cnt_ref):
    i = pl.program_id(0)

    @pl.when(i == 0)
    def _():
        cnt_ref[...] = jnp.zeros_like(cnt_ref)

    x = x_ref[...]
    inv = lax.rsqrt(jnp.mean(x * x, axis=-1, keepdims=True) + EPS)
    dn = (((1,), (1,)), ((), ()))
    lt = None
    for c in range(D_MODEL // KC_R):
        cols = slice(c * KC_R, (c + 1) * KC_R)
        h1, h2 = _split2(x_ref[:, cols] * inv * g_ref[:, cols])
        w1, w2 = _split2(wr_ref[:, cols])
        for wa, ha in ((w1, h1), (w1, h2), (w2, h1)):
            t = lax.dot_general(wa, ha, dn, preferred_element_type=F32)
            lt = t if lt is None else lt + t
    lt = lt + br_ref[:, 0:1]

    row8 = lax.broadcasted_iota(jnp.int32, (8, TM_R), 0)
    gl = jnp.where(row8 < N_GROUPS, lt[0:8], -jnp.inf)
    gmax = jnp.max(gl, axis=0, keepdims=True)
    gidx = jnp.min(jnp.where(gl == gmax, row8, 8), axis=0, keepdims=True)
    gw = 1.0 / jnp.sum(jnp.exp(gl - gmax), axis=0, keepdims=True)

    esel = lt[8:16]
    for g in range(1, N_GROUPS):
        esel = jnp.where(gidx == g, lt[8 + 8 * g:16 + 8 * g], esel)
    v1 = jnp.max(esel, axis=0, keepdims=True)
    i1 = jnp.min(jnp.where(esel == v1, row8, 8), axis=0, keepdims=True)
    rest = jnp.where(row8 == i1, -jnp.inf, esel)
    v2 = jnp.max(rest, axis=0, keepdims=True)
    i2 = jnp.min(jnp.where(rest == v2, row8, 8), axis=0, keepdims=True)
    e21 = jnp.exp(v2 - v1)
    den = 1.0 + e21
    e1 = gidx * E_PER_G + i1
    e2 = gidx * E_PER_G + i2
    eid_ref[0:1, :] = e1
    eid_ref[1:2, :] = e2
    wts_ref[0:1, :] = gw * (1.0 / den)
    wts_ref[1:2, :] = gw * (e21 / den)

    erow = lax.broadcasted_iota(jnp.int32, (N_EXPERTS, TM_R), 0)
    oh1 = erow == e1
    oh2 = erow == e2
    member = jnp.where(oh1 | oh2, 1.0, 0.0)
    ti = lax.broadcasted_iota(jnp.int32, (TM_R, TM_R), 0)
    tj = lax.broadcasted_iota(jnp.int32, (TM_R, TM_R), 1)
    upper = jnp.where(ti < tj, 1.0, 0.0).astype(BF16)
    before = jnp.dot(member.astype(BF16), upper, preferred_element_type=F32)
    pos = before + cnt_ref[:, 0:1]
    rank_ref[0:1, :] = jnp.sum(jnp.where(oh1, pos, 0.0), axis=0, keepdims=True).astype(jnp.int32)
    rank_ref[1:2, :] = jnp.sum(jnp.where(oh2, pos, 0.0), axis=0, keepdims=True).astype(jnp.int32)
    cnt_ref[...] = cnt_ref[...] + jnp.sum(member, axis=1, keepdims=True)


def _router(x1, g2, wr_t, br):
    return pl.pallas_call(
        _router_kernel,
        grid=(SEQ // TM_R,),
        in_specs=[
            pl.BlockSpec((TM_R, D_MODEL), lambda i: (i, 0)),
            pl.BlockSpec((1, D_MODEL), lambda i: (0, 0)),
            pl.BlockSpec((R_ROWS, D_MODEL), lambda i: (0, 0)),
            pl.BlockSpec((R_ROWS, LANES), lambda i: (0, 0)),
        ],
        out_specs=[
            pl.BlockSpec((2, TM_R), lambda i: (0, i)),
            pl.BlockSpec((2, TM_R), lambda i: (0, i)),
            pl.BlockSpec((2, TM_R), lambda i: (0, i)),
            pl.BlockSpec((N_EXPERTS, LANES), lambda i: (0, 0)),
        ],
        out_shape=[
            jax.ShapeDtypeStruct((2, SEQ), jnp.int32),
            jax.ShapeDtypeStruct((2, SEQ), F32),
            jax.ShapeDtypeStruct((2, SEQ), jnp.int32),
            jax.ShapeDtypeStruct((N_EXPERTS, LANES), F32),
        ],
        compiler_params=_cparams(("arbitrary",)),
        name="router",
    )(x1, g2, wr_t, br)


def _moe_kernel(ie_ref, row0_ref, rows_ref, nitems_ref, sorted_ref,
                x_hbm, g_ref, wg_ref, wu_ref, wd_ref, dest_hbm,
                xg_ref, xb_ref, y_ref, gsem, ssem):
    it = pl.program_id(0)
    f = pl.program_id(1)
    nrows = rows_ref[it]
    nitems = nitems_ref[0]
    slot = it % 2
    half = SUB_E // 2

    def padded(item):
        return pl.multiple_of(((rows_ref[item] + half - 1) // half) * half, half)

    nhalf = padded(it) // half
    nsub = nhalf // 2

    def start_gather(item):
        base = row0_ref[item]
        buf = item % 2

        def start(j8, c):
            for k in range(8):
                j = j8 * 8 + k
                tok = sorted_ref[base + j] & (SEQ - 1)
                pltpu.make_async_copy(x_hbm.at[pl.ds(tok, 1), :], xg_ref.at[buf, pl.ds(j, 1), :],
                                      gsem.at[buf]).start()
            return c
        lax.fori_loop(0, padded(item) // 8, start, 0)

    def wait_gather(item):
        got = pl.ds(0, padded(item))
        buf = item % 2
        pltpu.make_async_copy(x_hbm.at[got, :], xg_ref.at[buf, got, :], gsem.at[buf]).wait()

    def scatter_copy(item, j):
        dst = sorted_ref[row0_ref[item] + j]
        buf = item % 2
        return pltpu.make_async_copy(y_ref.at[buf, pl.ds(j, 1), :], dest_hbm.at[pl.ds(dst, 1), :],
                                     ssem.at[buf])

    def start_scatter(item):
        n = rows_ref[item]

        def start8(j8, c):
            for k in range(8):
                scatter_copy(item, j8 * 8 + k).start()
            return c
        lax.fori_loop(0, n // 8, start8, 0)

        def start(j, c):
            scatter_copy(item, j).start()
            return c
        lax.fori_loop((n // 8) * 8, n, start, 0)

    def wait_scatter(item):
        n = rows_ref[item]
        whole = pl.multiple_of((n // 8) * 8, 8)
        buf = item % 2

        @pl.when(whole > 0)
        def _():
            sent = pl.ds(0, whole)
            pltpu.make_async_copy(y_ref.at[buf, sent, :], dest_hbm.at[sent, :], ssem.at[buf]).wait()

        def wait(j, c):
            scatter_copy(item, j).wait()
            return c
        lax.fori_loop(whole, n, wait, 0)

    @pl.when((f == 0) & (nrows > 0))
    def _():
        @pl.when(it == 0)
        def _():
            start_gather(0)

        wait_gather(it)

        def norm(s, c):
            rows = pl.ds(pl.multiple_of(s * half, half), half)
            x = xg_ref[slot, rows, :]
            ms = jnp.mean(x * x, axis=-1, keepdims=True)
            xb_ref[rows, :] = (x * lax.rsqrt(ms + EPS) * g_ref[...]).astype(BF16)
            y_ref[slot, rows, :] = jnp.zeros((half, D_MODEL), F32)
            return c
        lax.fori_loop(0, nhalf, norm, 0)

        @pl.when(it + 1 < nitems)
        def _():
            start_gather(it + 1)

    @pl.when(nrows > 0)
    def _():
        wg = wg_ref[...].astype(BF16)
        wu = wu_ref[...].astype(BF16)
        wd = wd_ref[...].astype(BF16)

        def block(rows):
            xb = xb_ref[rows, :]
            hg = jnp.dot(xb, wg, preferred_element_type=F32)
            hu = jnp.dot(xb, wu, preferred_element_type=F32)
            h = (hg * jax.nn.sigmoid(hg) * hu).astype(BF16)
            y_ref[slot, rows, :] = y_ref[slot, rows, :] + jnp.dot(h, wd, preferred_element_type=F32)

        def sub(s, c):
            block(pl.ds(pl.multiple_of(s * SUB_E, SUB_E), SUB_E))
            return c
        lax.fori_loop(0, nsub, sub, 0)

        @pl.when(nhalf % 2 == 1)
        def _():
            block(pl.ds(pl.multiple_of(nsub * SUB_E, SUB_E), half))

    @pl.when((f == NF_E - 1) & (nrows > 0))
    def _():
        @pl.when(it > 0)
        def _():
            wait_scatter(it - 1)

        start_scatter(it)

        @pl.when(it + 1 >= nitems)
        def _():
            wait_scatter(it)


def _moe(ie, row0, rows, nitems, sorted_i, x1, g2, wg, wu, wd):
    def wmap_cols(i, f, ie_ref, row0_ref, rows_ref, n_ref, s_ref):
        return (ie_ref[i], 0, jnp.where(i < n_ref[0], f, NF_E - 1))

    def wmap_rows(i, f, ie_ref, row0_ref, rows_ref, n_ref, s_ref):
        return (ie_ref[i], jnp.where(i < n_ref[0], f, NF_E - 1), 0)

    grid_spec = pltpu.PrefetchScalarGridSpec(
        num_scalar_prefetch=5,
        grid=(nitems[0], NF_E),
        in_specs=[
            pl.BlockSpec(memory_space=pl.ANY),
            pl.BlockSpec((1, D_MODEL), lambda i, f, *_: (0, 0)),
            pl.BlockSpec((None, D_MODEL, TF_E), wmap_cols),
            pl.BlockSpec((None, D_MODEL, TF_E), wmap_cols),
            pl.BlockSpec((None, TF_E, D_MODEL), wmap_rows),
        ],
        out_specs=pl.BlockSpec(memory_space=pl.ANY),
        scratch_shapes=[
            pltpu.VMEM((2, TM_E, D_MODEL), F32),
            pltpu.VMEM((TM_E, D_MODEL), BF16),
            pltpu.VMEM((2, TM_E, D_MODEL), F32),
            pltpu.SemaphoreType.DMA((2,)),
            pltpu.SemaphoreType.DMA((2,)),
        ],
    )
    return pl.pallas_call(
        _moe_kernel,
        grid_spec=grid_spec,
        out_shape=jax.ShapeDtypeStruct((N_ASSIGN, D_MODEL), F32),
        compiler_params=_cparams(("arbitrary", "arbitrary")),
        name="moe_experts",
    )(ie, row0, rows, nitems, sorted_i, x1, g2, wg, wu, wd)


def _combine_kernel(x_ref, w_ref, d0_ref, d1_ref, o_ref):
    w = w_ref[...]
    o_ref[...] = x_ref[...] + w[:, 0:1] * d0_ref[...] + w[:, 1:2] * d1_ref[...]


def _combine(x1, wts_t, dest):
    dest3 = dest.reshape(2, SEQ, D_MODEL)
    return pl.pallas_call(
        _combine_kernel,
        grid=(SEQ // TT_COMB,),
        in_specs=[
            pl.BlockSpec((TT_COMB, D_MODEL), lambda i: (i, 0)),
            pl.BlockSpec((TT_COMB, 2), lambda i: (i, 0)),
            pl.BlockSpec((None, TT_COMB, D_MODEL), lambda i: (0, i, 0)),
            pl.BlockSpec((None, TT_COMB, D_MODEL), lambda i: (1, i, 0)),
        ],
        out_specs=pl.BlockSpec((TT_COMB, D_MODEL), lambda i: (i, 0)),
        out_shape=jax.ShapeDtypeStruct((SEQ, D_MODEL), F32),
        compiler_params=_cparams(("arbitrary",)),
        name="moe_combine",
    )(x1, wts_t, dest3, dest3)


def _dispatch_kernel(slot_ref, padrow_ref, x_ref, g_ref, xs_hbm, hbuf, zbuf, sem, zsem):
    i = pl.program_id(0)
    last = pl.num_programs(0) - 1
    buf = i % 2

    def tile_wait(b):
        for _ in range(2):
            pltpu.make_async_copy(hbuf.at[b], xs_hbm.at[pl.ds(0, TT_D), :], sem.at[b]).wait()

    @pl.when(i == 0)
    def _():
        zbuf[...] = jnp.zeros_like(zbuf)

        for c in range(XS_TAIL // ZROWS):
            pltpu.make_async_copy(zbuf, xs_hbm.at[pl.ds(XS_ROWS - XS_TAIL + c * ZROWS, ZROWS), :], zsem).start()
        for c in range(XS_TAIL // ZROWS):
            pltpu.make_async_copy(zbuf, xs_hbm.at[pl.ds(0, ZROWS), :], zsem).wait()

        def pad(e, c):
            dst = pl.multiple_of(padrow_ref[e], 8)
            pltpu.make_async_copy(zbuf.at[pl.ds(0, 8), :], xs_hbm.at[pl.ds(dst, 8), :], zsem).start()
            return c
        lax.fori_loop(0, N_EXPERTS, pad, 0)

        def padw(e, c):
            pltpu.make_async_copy(zbuf.at[pl.ds(0, 8), :], xs_hbm.at[pl.ds(0, 8), :], zsem).wait()
            return c
        lax.fori_loop(0, N_EXPERTS, padw, 0)

    @pl.when(i >= 2)
    def _():
        tile_wait(buf)

    x = x_ref[...]
    ms = jnp.mean(x * x, axis=-1, keepdims=True)
    hbuf[buf] = x * lax.rsqrt(ms + EPS) * g_ref[...]

    tok0 = i * TT_D

    def send(j8, c):
        for k8 in range(8):
            j = j8 * 8 + k8
            for k in range(2):
                dst = slot_ref[k * SEQ + tok0 + j]
                pltpu.make_async_copy(hbuf.at[buf, pl.ds(j, 1), :], xs_hbm.at[pl.ds(dst, 1), :],
                                      sem.at[buf]).start()
        return c
    lax.fori_loop(0, TT_D // 8, send, 0)

    @pl.when(i == last)
    def _():
        @pl.when(i >= 1)
        def _():
            tile_wait(1 - buf)
        tile_wait(buf)


def _dispatch(slot_flat, padrow, x1, g2):
    grid_spec = pltpu.PrefetchScalarGridSpec(
        num_scalar_prefetch=2,
        grid=(SEQ // TT_D,),
        in_specs=[
            pl.BlockSpec((TT_D, D_MODEL), lambda i, *_: (i, 0)),
            pl.BlockSpec((1, D_MODEL), lambda i, *_: (0, 0)),
        ],
        out_specs=pl.BlockSpec(memory_space=pl.ANY),
        scratch_shapes=[
            pltpu.VMEM((2, TT_D, D_MODEL), F32),
            pltpu.VMEM((ZROWS, D_MODEL), F32),
            pltpu.SemaphoreType.DMA((2,)),
            pltpu.SemaphoreType.DMA,
        ],
    )
    return pl.pallas_call(
        _dispatch_kernel,
        grid_spec=grid_spec,
        out_shape=jax.ShapeDtypeStruct((XS_ROWS, D_MODEL), F32),
        compiler_params=_cparams(("arbitrary",)),
        name="moe_dispatch",
    )(slot_flat, padrow, x1, g2)


def _experts_kernel(ie_ref, row0_ref, rows_ref, nitems_ref,
                    xs_hbm, wg_hbm, wu_hbm, wd_hbm, ys_hbm,
                    xg_ref, y_ref, zbuf, wg_ref, wu_ref, wd_ref, gsem, ssem, zsem, wsem):
    it = pl.program_id(0)
    f = pl.program_id(1)
    nrows = rows_ref[it]
    nitems = nitems_ref[0]
    slot = it % 2
    half = SUB_E // 2
    step = it * NF_E + f
    nsteps = nitems * NF_E
    wslot = step % W_RING

    def weight_copies(s):
        e = ie_ref[s // NF_E]
        cols = pl.ds(pl.multiple_of((s % NF_E) * TF_E, TF_E), TF_E)
        b = s % W_RING
        return (pltpu.make_async_copy(wg_hbm.at[e, :, cols], wg_ref.at[b], wsem.at[b]),
                pltpu.make_async_copy(wu_hbm.at[e, :, cols], wu_ref.at[b], wsem.at[b]),
                pltpu.make_async_copy(wd_hbm.at[e, cols, :], wd_ref.at[b], wsem.at[b]))

    @pl.when(step == 0)
    def _():
        for s in range(W_RING - 1):
            for cp in weight_copies(s):
                cp.start()

    @pl.when(step + W_RING - 1 < nsteps)
    def _():
        for cp in weight_copies(step + W_RING - 1):
            cp.start()

    def padded(item):
        return pl.multiple_of(((rows_ref[item] + half - 1) // half) * half, half)

    nhalf = padded(it) // half
    nsub = nhalf // 2

    def chunk_copies(item, start):
        base = pl.multiple_of(row0_ref[item], 8)
        buf = item % 2

        def body(c, carry):
            off = pl.multiple_of(c * half, half)
            src = xs_hbm.at[pl.ds(pl.multiple_of(base + off, 8), half), :]
            dst = ys_hbm.at[pl.ds(pl.multiple_of(base + off, 8), half), :]
            if start == "fetch":
                pltpu.make_async_copy(src, xg_ref.at[buf, pl.ds(off, half), :], gsem.at[buf]).start()
            else:
                pltpu.make_async_copy(y_ref.at[buf, pl.ds(off, half), :], dst, ssem.at[buf]).start()
            return carry
        lax.fori_loop(0, padded(item) // half, body, 0)

    def wait_fetch(item):
        got = pl.ds(0, padded(item))
        buf = item % 2
        pltpu.make_async_copy(xs_hbm.at[got, :], xg_ref.at[buf, got, :], gsem.at[buf]).wait()

    def wait_store(item):
        put = pl.ds(0, padded(item))
        buf = item % 2
        pltpu.make_async_copy(y_ref.at[buf, put, :], ys_hbm.at[put, :], ssem.at[buf]).wait()

    @pl.when((it == 0) & (f == 0))
    def _():
        zbuf[...] = jnp.zeros_like(zbuf)
        for c in range(XS_TAIL // ZROWS):
            pltpu.make_async_copy(zbuf, ys_hbm.at[pl.ds(XS_ROWS - XS_TAIL + c * ZROWS, ZROWS), :], zsem).start()
        for c in range(XS_TAIL // ZROWS):
            pltpu.make_async_copy(zbuf, ys_hbm.at[pl.ds(0, ZROWS), :], zsem).wait()
        chunk_copies(0, "fetch")

    @pl.when(f == 0)
    def _():
        wait_fetch(it)

        def clear(s, c):
            rows = pl.ds(pl.multiple_of(s * half, half), half)
            y_ref[slot, rows, :] = jnp.zeros((half, D_MODEL), F32)
            return c
        lax.fori_loop(0, nhalf, clear, 0)

        @pl.when(it + 1 < nitems)
        def _():
            chunk_copies(it + 1, "fetch")

    for cp in weight_copies(step):
        cp.wait()

    def block(rows):
        xb = xg_ref[slot, rows, :].astype(BF16)
        hg = jnp.dot(xb, wg_ref[wslot].astype(BF16), preferred_element_type=F32)
        hu = jnp.dot(xb, wu_ref[wslot].astype(BF16), preferred_element_type=F32)
        h = (hg * jax.nn.sigmoid(hg) * hu).astype(BF16)
        y_ref[slot, rows, :] = y_ref[slot, rows, :] + jnp.dot(h, wd_ref[wslot].astype(BF16),
                                                               preferred_element_type=F32)

    def sub(s, c):
        block(pl.ds(pl.multiple_of(s * SUB_E, SUB_E), SUB_E))
        return c
    lax.fori_loop(0, nsub, sub, 0)

    @pl.when(nhalf % 2 == 1)
    def _():
        block(pl.ds(pl.multiple_of(nsub * SUB_E, SUB_E), half))

    @pl.when(f == NF_E - 1)
    def _():
        @pl.when(it > 0)
        def _():
            wait_store(it - 1)

        chunk_copies(it, "store")

        @pl.when(it + 1 >= nitems)
        def _():
            wait_store(it)


def _experts(ie, row0, rows, nitems, xs, wg, wu, wd):
    grid_spec = pltpu.PrefetchScalarGridSpec(
        num_scalar_prefetch=4,
        grid=(nitems[0], NF_E),
        in_specs=[pl.BlockSpec(memory_space=pl.ANY) for _ in range(4)],
        out_specs=pl.BlockSpec(memory_space=pl.ANY),
        scratch_shapes=[
            pltpu.VMEM((2, TM_E, D_MODEL), F32),
            pltpu.VMEM((2, TM_E, D_MODEL), F32),
            pltpu.VMEM((ZROWS, D_MODEL), F32),
            pltpu.VMEM((W_RING, D_MODEL, TF_E), F32),
            pltpu.VMEM((W_RING, D_MODEL, TF_E), F32),
            pltpu.VMEM((W_RING, TF_E, D_MODEL), F32),
            pltpu.SemaphoreType.DMA((2,)),
            pltpu.SemaphoreType.DMA((2,)),
            pltpu.SemaphoreType.DMA,
            pltpu.SemaphoreType.DMA((W_RING,)),
        ],
    )
    return pl.pallas_call(
        _experts_kernel,
        grid_spec=grid_spec,
        out_shape=jax.ShapeDtypeStruct((XS_ROWS, D_MODEL), F32),
        compiler_params=_cparams(("arbitrary", "arbitrary")),
        name="moe_experts",
    )(ie, row0, rows, nitems, xs, wg, wu, wd)


def _gcombine_kernel(slot_ref, x_ref, w_ref, ys_hbm, o_ref, dbuf, sem):
    i = pl.program_id(0)
    n = pl.num_programs(0)
    buf = i % 2

    def fetch(tile, b):
        tok0 = tile * TT_G

        def body(j8, c):
            for k8 in range(8):
                j = j8 * 8 + k8
                for k in range(2):
                    src = slot_ref[k * SEQ + tok0 + j]
                    pltpu.make_async_copy(ys_hbm.at[pl.ds(src, 1), :], dbuf.at[b, k, pl.ds(j, 1), :],
                                          sem.at[b]).start()
            return c
        lax.fori_loop(0, TT_G // 8, body, 0)

    @pl.when(i == 0)
    def _():
        fetch(0, 0)

    @pl.when(i + 1 < n)
    def _():
        fetch(i + 1, 1 - buf)

    for k in range(2):
        pltpu.make_async_copy(ys_hbm.at[pl.ds(0, TT_G), :], dbuf.at[buf, k], sem.at[buf]).wait()

    w = w_ref[...]
    o_ref[...] = x_ref[...] + w[:, 0:1] * dbuf[buf, 0] + w[:, 1:2] * dbuf[buf, 1]


def _gcombine(slot_flat, x1, wts_t, ys):
    grid_spec = pltpu.PrefetchScalarGridSpec(
        num_scalar_prefetch=1,
        grid=(SEQ // TT_G,),
        in_specs=[
            pl.BlockSpec((TT_G, D_MODEL), lambda i, *_: (i, 0)),
            pl.BlockSpec((TT_G, 2), lambda i, *_: (i, 0)),
            pl.BlockSpec(memory_space=pl.ANY),
        ],
        out_specs=pl.BlockSpec((TT_G, D_MODEL), lambda i, *_: (i, 0)),
        scratch_shapes=[
            pltpu.VMEM((2, 2, TT_G, D_MODEL), F32),
            pltpu.SemaphoreType.DMA((2,)),
        ],
    )
    return pl.pallas_call(
        _gcombine_kernel,
        grid_spec=grid_spec,
        out_shape=jax.ShapeDtypeStruct((SEQ, D_MODEL), F32),
        compiler_params=_cparams(("arbitrary",)),
        name="moe_combine",
    )(slot_flat, x1, wts_t, ys)


def _routing_tables(eid, rank, counts_f):
    counts = counts_f[:, 0].astype(jnp.int32)
    aligned = ((counts + 7) // 8) * 8
    base = jnp.cumsum(aligned) - aligned
    tiles = (counts + TM_E - 1) // TM_E
    tcum = jnp.cumsum(tiles)
    tstart = tcum - tiles
    nitems = tcum[-1]
    ids = jnp.arange(MAX_ITEMS, dtype=jnp.int32)
    ie = jnp.clip(jnp.searchsorted(tcum, ids, side="right"), 0, N_EXPERTS - 1).astype(jnp.int32)
    live = ids < nitems
    ie = jnp.where(live, ie, ie[jnp.maximum(nitems - 1, 0)])
    jt = ids - tstart[ie]
    row0 = jnp.where(live, base[ie] + jt * TM_E, 0)
    rows = jnp.where(live, jnp.clip(counts[ie] - jt * TM_E, 0, TM_E), 0)
    eoh = eid[:, :, None] == jnp.arange(N_EXPERTS, dtype=jnp.int32)
    slot = jnp.sum(jnp.where(eoh, base, 0), axis=-1) + rank
    spare = XS_ROWS - 8 * (1 + jnp.arange(N_EXPERTS, dtype=jnp.int32))
    padrow = jnp.where(counts % 8 != 0, base + (counts // 8) * 8, spare)
    return (ie, row0.astype(jnp.int32), rows.astype(jnp.int32), nitems.reshape(1).astype(jnp.int32),
            slot.reshape(-1).astype(jnp.int32), padrow.astype(jnp.int32))


def _work_items(eid, rank, counts_f):
    counts = counts_f[:, 0].astype(jnp.int32)
    cum = jnp.cumsum(counts)
    base = cum - counts
    tiles = (counts + TM_E - 1) // TM_E
    tcum = jnp.cumsum(tiles)
    tstart = tcum - tiles
    nitems = tcum[-1]
    ids = jnp.arange(MAX_ITEMS, dtype=jnp.int32)
    ie = jnp.clip(jnp.searchsorted(tcum, ids, side="right"), 0, N_EXPERTS - 1).astype(jnp.int32)
    live = ids < nitems
    ie = jnp.where(live, ie, ie[jnp.maximum(nitems - 1, 0)])
    jt = ids - tstart[ie]
    row0 = jnp.where(live, base[ie] + jt * TM_E, 0)
    rows = jnp.where(live, jnp.clip(counts[ie] - jt * TM_E, 0, TM_E), 0)
    eoh = eid[:, :, None] == jnp.arange(N_EXPERTS, dtype=jnp.int32)
    slot = jnp.sum(jnp.where(eoh, base, 0), axis=-1) + rank
    sorted_i = jnp.zeros((SORTED_LEN,), jnp.int32).at[slot.reshape(-1)].set(
        jnp.arange(N_ASSIGN, dtype=jnp.int32))
    return ie, row0.astype(jnp.int32), rows.astype(jnp.int32), nitems.reshape(1).astype(jnp.int32), sorted_i


def kernel(x, norm1_g, w_in, q_norm_g, k_norm_g, conv_w, conv_b, conv_ln_g, conv_ln_b, rel_bias,
           w_out, norm2_g, w_router_group, b_router_group, w_router_expert, b_router_expert,
           w_gate, w_up, w_down):
    assert x.shape == (1, SEQ, D_MODEL) and w_in.shape[0] == 1
    xs = x[0]
    bias_tab = _attn_bias_tables(rel_bias)
    qg2 = jnp.tile(q_norm_g[0], 2)[None]
    kg2 = jnp.tile(k_norm_g[0], 2)[None]

    proj = _inproj(xs, norm1_g[0][None], w_in[0])
    conv_out = _conv_mixer(proj, conv_w[0], conv_b[0][None], conv_ln_g[0][None], conv_ln_b[0][None])
    attn_out = _attention(proj, qg2, kg2, bias_tab)
    x1 = _outproj(xs, conv_out, attn_out, w_out[0].astype(BF16))

    wr_t = jnp.concatenate([
        w_router_group[0].T, jnp.zeros((8 - N_GROUPS, D_MODEL), F32),
        jnp.transpose(w_router_expert[0], (0, 2, 1)).reshape(N_EXPERTS, D_MODEL)], axis=0)
    br = jnp.concatenate([b_router_group[0], jnp.zeros((8 - N_GROUPS,), F32),
                          b_router_expert[0].reshape(-1)])
    br = jnp.broadcast_to(br[:, None], (R_ROWS, LANES))
    eid, wts, rank, counts_f = _router(x1, norm2_g[0][None], wr_t, br)

    ie, row0, rows, nitems, slot_flat, padrow = _routing_tables(eid, rank, counts_f)
    xs = _dispatch(slot_flat, padrow, x1, norm2_g[0][None])
    ys = _experts(ie, row0, rows, nitems, xs,
                  w_gate[0].reshape(N_EXPERTS, D_MODEL, D_FF),
                  w_up[0].reshape(N_EXPERTS, D_MODEL, D_FF),
                  w_down[0].reshape(N_EXPERTS, D_FF, D_MODEL))
    out = _gcombine(slot_flat, x1, wts.T, ys)
    return out[None]
```

```python
import functools
import math

import numpy as np
import jax
import jax.numpy as jnp
from jax import lax
from jax.experimental import pallas as pl
from jax.experimental.pallas import tpu as pltpu

F32 = jnp.float32
BF16 = jnp.bfloat16

D_MODEL = 2048
SEQ = 8192
N_HEADS = 16
HEAD_DIM = 64
ATTN_W = N_HEADS * HEAD_DIM
CONV_C = D_MODEL - ATTN_W
CONV_K = 31
IN_W = 2 * CONV_C + 3 * ATTN_W
PATTERNS = ((128, 1), (512, 4), (2048, 16))
QBLK = 128
NUM_BUCKETS = 32
MAX_DISTANCE = 2048
N_GROUPS = 4
E_PER_G = 8
N_EXPERTS = N_GROUPS * E_PER_G
D_FF = D_MODEL // 2
EPS = 1e-6
NEG_INF = -1e30
LOG2E = math.log2(math.e)

LANES = 128
VMEM_LIMIT = 56 * 1024 * 1024

TM_IN = 1024
TN_IN = 1024
TT_CONV = 512
HALO = 32
R_CONV = 64
R_LN = 16
ATTN_GROUP = 4
ATTN_UNROLL = 2
TM_OUT = 512
TM_R = 512
R_ROWS = 8 + N_EXPERTS
KC_R = 512
TM_E = 768
SUB_E = 256
TF_E = 256
NF_E = D_FF // TF_E
W_RING = 4
N_ASSIGN = 2 * SEQ
MAX_ITEMS = -(-N_ASSIGN // TM_E) + N_EXPERTS
SORTED_LEN = N_ASSIGN + TM_E
XS_TAIL = 256 + TM_E
XS_ROWS = N_ASSIGN + XS_TAIL
ZROWS = 128
TT_D = 512
TT_G = 512
TT_COMB = 512


def _cparams(sem, vmem=VMEM_LIMIT, flags=None):
    return pltpu.CompilerParams(dimension_semantics=sem, vmem_limit_bytes=vmem, flags=flags)


def _inproj_kernel(x_ref, g_ref, w_ref, o_ref, xn_ref):
    @pl.when(pl.program_id(1) == 0)
    def _():
        x = x_ref[...]
        ms = jnp.mean(x * x, axis=-1, keepdims=True)
        xn_ref[...] = (x * lax.rsqrt(ms + EPS) * g_ref[...]).astype(BF16)

    o_ref[...] = jnp.dot(xn_ref[...], w_ref[...].astype(BF16), preferred_element_type=F32)


def _inproj(x, g, w):
    return pl.pallas_call(
        _inproj_kernel,
        grid=(SEQ // TM_IN, IN_W // TN_IN),
        in_specs=[
            pl.BlockSpec((TM_IN, D_MODEL), lambda i, j: (i, 0)),
            pl.BlockSpec((1, D_MODEL), lambda i, j: (0, 0)),
            pl.BlockSpec((D_MODEL, TN_IN), lambda i, j: (0, j)),
        ],
        out_specs=pl.BlockSpec((TM_IN, TN_IN), lambda i, j: (i, j)),
        out_shape=jax.ShapeDtypeStruct((SEQ, IN_W), F32),
        scratch_shapes=[pltpu.VMEM((TM_IN, D_MODEL), BF16)],
        compiler_params=_cparams(("arbitrary", "arbitrary")),
        name="inproj",
    )(x, g, w)


def _conv_kernel(val_ref, gate_ref, hval_ref, hgate_ref, cw_ref, cb_ref, lg_ref, lb_ref,
                 o_ref, ubuf, zbuf, ybuf):
    i = pl.program_id(0)
    u = val_ref[...] * jax.nn.sigmoid(gate_ref[...])
    hu = hval_ref[...] * jax.nn.sigmoid(hgate_ref[...])
    hu = jnp.where(i > 0, hu, 0.0)
    for c in range(CONV_C // LANES):
        cols = slice(c * LANES, (c + 1) * LANES)
        ubuf[c, 0:HALO, :] = hu[:, cols]
        ubuf[c, HALO:HALO + TT_CONV, :] = u[:, cols]

    n_a = -(-CONV_K // 8)
    assert HALO == 8 * n_a
    for c in range(CONV_C // LANES):
        cols = slice(c * LANES, (c + 1) * LANES)

        def taps(r, carry, c=c, cols=cols):
            base = pl.multiple_of(r * R_CONV, R_CONV)
            win = ubuf[c, pl.ds(base, R_CONV + HALO), :]
            for b in range(8):
                z = None
                for a in range(n_a):
                    s = 8 * a + b
                    if s >= CONV_K:
                        continue
                    lo = HALO - 8 - 8 * a
                    t = cw_ref[CONV_K - 1 - s:CONV_K - s, cols] * win[lo:lo + R_CONV + 8, :]
                    z = t if z is None else z + t
                zbuf[b, pl.ds(0, R_CONV + 8, stride=2), :] = z
            acc = None
            for b in range(8):
                t = zbuf[b, pl.ds(2 * (8 - b), R_CONV, stride=2), :]
                acc = t if acc is None else acc + t
            ybuf[pl.ds(base, R_CONV), cols] = acc
            return carry

        lax.fori_loop(0, TT_CONV // R_CONV, taps, 0)

    def norm(r, carry):
        rows = pl.ds(pl.multiple_of(r * R_LN, R_LN), R_LN)
        acc = ybuf[rows, :] + cb_ref[...]
        mu = jnp.mean(acc, axis=-1, keepdims=True)
        xc = acc - mu
        var = jnp.mean(xc * xc, axis=-1, keepdims=True)
        y = xc * lax.rsqrt(var + EPS) * lg_ref[...] + lb_ref[...]
        o_ref[rows, :] = (y * jax.nn.sigmoid(y)).astype(BF16)
        return carry

    lax.fori_loop(0, TT_CONV // R_LN, norm, 0, unroll=4)


def _conv_mixer(proj, cw, cb, lg, lb):
    hb = TT_CONV // HALO
    return pl.pallas_call(
        _conv_kernel,
        grid=(SEQ // TT_CONV,),
        in_specs=[
            pl.BlockSpec((TT_CONV, CONV_C), lambda i: (i, 0)),
            pl.BlockSpec((TT_CONV, CONV_C), lambda i: (i, 1)),
            pl.BlockSpec((HALO, CONV_C), lambda i: (jnp.maximum(i * hb - 1, 0), 0)),
            pl.BlockSpec((HALO, CONV_C), lambda i: (jnp.maximum(i * hb - 1, 0), 1)),
            pl.BlockSpec((CONV_K, CONV_C), lambda i: (0, 0)),
            pl.BlockSpec((1, CONV_C), lambda i: (0, 0)),
            pl.BlockSpec((1, CONV_C), lambda i: (0, 0)),
            pl.BlockSpec((1, CONV_C), lambda i: (0, 0)),
        ],
        out_specs=pl.BlockSpec((TT_CONV, CONV_C), lambda i: (i, 0)),
        out_shape=jax.ShapeDtypeStruct((SEQ, CONV_C), BF16),
        scratch_shapes=[pltpu.VMEM((CONV_C // LANES, HALO + TT_CONV, LANES), F32),
                        pltpu.VMEM((8, 2 * (R_CONV + 8), LANES), F32),
                        pltpu.VMEM((TT_CONV, CONV_C), F32)],
        compiler_params=_cparams(("arbitrary",)),
        name="conv_mixer",
    )(proj, proj, proj, proj, cw, cb, lg, lb)


def _t5_bucket_np(dist):
    max_exact = NUM_BUCKETS // 2
    nf = np.maximum(dist, 1).astype(np.float32)
    large = max_exact + (np.log(nf / np.float32(max_exact)) / np.float32(math.log(MAX_DISTANCE / max_exact))
                         * np.float32(NUM_BUCKETS - max_exact)).astype(np.int32)
    large = np.minimum(large, NUM_BUCKETS - 1)
    return np.where(dist < max_exact, dist, large)


def _attn_bias_tables(rel_bias):
    kj = np.arange(2 * QBLK)[None, None, :]
    period = 3 * QBLK
    tables = []
    for window, dil in PATTERNS:
        span = window // dil
        assert span <= QBLK
        bucket = _t5_bucket_np(np.arange(span + 1) * dil)
        onehot = np.eye(NUM_BUCKETS, dtype=np.float32)[bucket]
        vec = jnp.einsum("rb,bh->hr", onehot, rel_bias.astype(F32),
                         precision=lax.Precision.HIGHEST)
        diag = jnp.full((N_HEADS, period), NEG_INF, F32)
        diag = diag.at[:, 2 * QBLK - 1 - span:2 * QBLK].set(vec[:, ::-1])
        skew = jnp.tile(diag, (1, QBLK))[:, :QBLK * (period - 1)].reshape(N_HEADS, QBLK, period - 1)
        full = skew[:, :, QBLK - 1:3 * QBLK - 1]
        first = jnp.where(kj >= QBLK, full, NEG_INF)
        both = jnp.stack([full, first])
        tables.append(both.reshape(2, N_HEADS // 2, 2 * QBLK, 2 * QBLK))
    return jnp.stack(tables) * LOG2E


def _attn_kernel(q_ref, k_ref, v_ref, qg_ref, kg_ref, bias_ref, o_ref,
                 qn_ref, kn_ref, acc_ref, m_ref, l_ref):
    lane = lax.broadcasted_iota(jnp.int32, (QBLK, LANES), 1)
    head_a = lane < HEAD_DIM
    ri = lax.broadcasted_iota(jnp.int32, (LANES, LANES), 0) // HEAD_DIM
    ci = lax.broadcasted_iota(jnp.int32, (LANES, LANES), 1) // HEAD_DIM
    seg = (ri == ci).astype(BF16)

    def head_rms(x, g):
        sq = x * x
        hi = sq.astype(BF16)
        lo = (sq - hi.astype(F32)).astype(BF16)
        ss = (jnp.dot(hi, seg, preferred_element_type=F32)
              + jnp.dot(lo, seg, preferred_element_type=F32))
        return x * lax.rsqrt(ss * (1.0 / HEAD_DIM) + EPS) * g

    NCH = 512

    def norm_body(c, carry):
        rows = pl.ds(pl.multiple_of(c * NCH, NCH), NCH)
        qn_ref[rows, :] = head_rms(q_ref[rows, :], qg_ref[...]) * (LOG2E / math.sqrt(HEAD_DIM))
        kn_ref[rows, :] = head_rms(k_ref[rows, :], kg_ref[...])
        return carry

    lax.fori_loop(0, SEQ // NCH, norm_body, 0)

    order = sorted(range(len(PATTERNS)), key=lambda i: -PATTERNS[i][1])
    assert PATTERNS[order[-1]][1] == 1
    for step, p in enumerate(order):
        dil = PATTERNS[p][1]
        nb = SEQ // (dil * QBLK)
        is_first = step == 0
        is_last = step == len(order) - 1

        def rows(start, dil=dil):
            if dil == 1:
                return pl.ds(start, QBLK)
            return pl.ds(start, QBLK, stride=dil)

        def unit(cur, k_prev, v_prev, first, p=p, rows=rows, is_first=is_first, is_last=is_last):
            q = qn_ref[rows(cur), :]
            q2 = jnp.concatenate([jnp.where(head_a, q, 0.0), jnp.where(head_a, 0.0, q)],
                                 axis=0).astype(BF16)
            k_cur = kn_ref[rows(cur), :].astype(BF16)
            v_cur = v_ref[rows(cur), :].astype(BF16)
            k2 = jnp.concatenate([k_prev, k_cur], axis=0)
            v2 = jnp.concatenate([v_prev, v_cur], axis=0)
            s = lax.dot_general(q2, k2, (((1,), (1,)), ((), ())), preferred_element_type=F32)
            s = s + bias_ref[p, first]
            m = jnp.max(s, axis=-1, keepdims=True)
            e = jnp.exp2(s - m)
            l = jnp.sum(e, axis=-1, keepdims=True)
            pv = jnp.dot(e.astype(BF16), v2, preferred_element_type=F32)
            o_new = jnp.where(head_a, pv[:QBLK], pv[QBLK:])
            m_new = jnp.where(head_a, m[:QBLK], m[QBLK:])
            l_new = jnp.where(head_a, l[:QBLK], l[QBLK:])
            if is_first:
                acc_ref[rows(cur), :] = o_new
                m_ref[rows(cur), :] = m_new
                l_ref[rows(cur), :] = l_new
            else:
                m_old = m_ref[rows(cur), :]
                m_tot = jnp.maximum(m_old, m_new)
                a = jnp.exp2(m_old - m_tot)
                b = jnp.exp2(m_new - m_tot)
                acc = acc_ref[rows(cur), :] * a + o_new * b
                den = l_ref[rows(cur), :] * a + l_new * b
                if is_last:
                    o_ref[pl.ds(pl.multiple_of(cur, QBLK), QBLK), :] = (acc / den).astype(BF16)
                else:
                    acc_ref[rows(cur), :] = acc
                    l_ref[rows(cur), :] = den
                    m_ref[rows(cur), :] = m_tot
            return k_cur, v_cur

        per = nb // ATTN_GROUP
        assert per * ATTN_GROUP == nb

        def group(g, carry, dil=dil, per=per, rows=rows, unit=unit):
            r = g // per
            n0 = (g - r * per) * ATTN_GROUP
            hist = jnp.maximum(n0 - 1, 0) * (QBLK * dil) + r
            k_prev = kn_ref[rows(hist), :].astype(BF16)
            v_prev = v_ref[rows(hist), :].astype(BF16)
            for i in range(ATTN_GROUP):
                first = jnp.where(n0 == 0, 1, 0) if i == 0 else 0
                k_prev, v_prev = unit((n0 + i) * (QBLK * dil) + r, k_prev, v_prev, first)
            return carry

        lax.fori_loop(0, SEQ // (QBLK * ATTN_GROUP), group, 0, unroll=ATTN_UNROLL)


def _attention(proj, qg2, kg2, bias_tab):
    qoff = 2 * CONV_C // LANES
    koff = qoff + ATTN_W // LANES
    voff = koff + ATTN_W // LANES
    return pl.pallas_call(
        _attn_kernel,
        grid=(N_HEADS // 2,),
        in_specs=[
            pl.BlockSpec((SEQ, LANES), lambda h: (0, qoff + h)),
            pl.BlockSpec((SEQ, LANES), lambda h: (0, koff + h)),
            pl.BlockSpec((SEQ, LANES), lambda h: (0, voff + h)),
            pl.BlockSpec((1, LANES), lambda h: (0, 0)),
            pl.BlockSpec((1, LANES), lambda h: (0, 0)),
            pl.BlockSpec((len(PATTERNS), 2, None, 2 * QBLK, 2 * QBLK), lambda h: (0, 0, h, 0, 0)),
        ],
        out_specs=pl.BlockSpec((SEQ, LANES), lambda h: (0, h)),
        out_shape=jax.ShapeDtypeStruct((SEQ, ATTN_W), BF16),
        scratch_shapes=[pltpu.VMEM((SEQ, LANES), F32) for _ in range(5)],
        compiler_params=_cparams(("arbitrary",)),
        name="dilated_attn",
    )(proj, proj, proj, qg2, kg2, bias_tab)


def _outproj_kernel(x_ref, c_ref, a_ref, wc_ref, wa_ref, o_ref):
    o_ref[...] = (x_ref[...]
                  + jnp.dot(c_ref[...], wc_ref[...], preferred_element_type=F32)
                  + jnp.dot(a_ref[...], wa_ref[...], preferred_element_type=F32))


def _outproj(x, conv_out, attn_out, w_out_bf16):
    return pl.pallas_call(
        _outproj_kernel,
        grid=(SEQ // TM_OUT,),
        in_specs=[
            pl.BlockSpec((TM_OUT, D_MODEL), lambda i: (i, 0)),
            pl.BlockSpec((TM_OUT, CONV_C), lambda i: (i, 0)),
            pl.BlockSpec((TM_OUT, ATTN_W), lambda i: (i, 0)),
            pl.BlockSpec((CONV_C, D_MODEL), lambda i: (0, 0)),
            pl.BlockSpec((ATTN_W, D_MODEL), lambda i: (1, 0)),
        ],
        out_specs=pl.BlockSpec((TM_OUT, D_MODEL), lambda i: (i, 0)),
        out_shape=jax.ShapeDtypeStruct((SEQ, D_MODEL), F32),
        compiler_params=_cparams(("arbitrary",)),
        name="outproj",
    )(x, conv_out, attn_out, w_out_bf16, w_out_bf16)


def _split2(a):
    a1 = a.astype(BF16)
    a2 = (a - a1.astype(F32)).astype(BF16)
    return a1, a2


def _router_kernel(x_ref, g_ref, wr_ref, br_ref, eid_ref, wts_ref, rank_ref, cnt_ref):
    i = pl.program_id(0)

    @pl.when(i == 0)
    def _():
        cnt_ref[...] = jnp.zeros_like(cnt_ref)

    x = x_ref[...]
    inv = lax.rsqrt(jnp.mean(x * x, axis=-1, keepdims=True) + EPS)
    dn = (((1,), (1,)), ((), ()))
    lt = None
    for c in range(D_MODEL // KC_R):
        cols = slice(c * KC_R, (c + 1) * KC_R)
        h1, h2 = _split2(x_ref[:, cols] * inv * g_ref[:, cols])
        w1, w2 = _split2(wr_ref[:, cols])
        for wa, ha in ((w1, h1), (w1, h2), (w2, h1)):
            t = lax.dot_general(wa, ha, dn, preferred_element_type=F32)
            lt = t if lt is None else lt + t
    lt = lt + br_ref[:, 0:1]

    row8 = lax.broadcasted_iota(jnp.int32, (8, TM_R), 0)
    gl = jnp.where(row8 < N_GROUPS, lt[0:8], -jnp.inf)
    gmax = jnp.max(gl, axis=0, keepdims=True)
    gidx = jnp.min(jnp.where(gl == gmax, row8, 8), axis=0, keepdims=True)
    gw = 1.0 / jnp.sum(jnp.exp(gl - gmax), axis=0, keepdims=True)

    esel = lt[8:16]
    for g in range(1, N_GROUPS):
        esel = jnp.where(gidx == g, lt[8 + 8 * g:16 + 8 * g], esel)
    v1 = jnp.max(esel, axis=0, keepdims=True)
    i1 = jnp.min(jnp.where(esel == v1, row8, 8), axis=0, keepdims=True)
    rest = jnp.where(row8 == i1, -jnp.inf, esel)
    v2 = jnp.max(rest, axis=0, keepdims=True)
    i2 = jnp.min(jnp.where(rest == v2, row8, 8), axis=0, keepdims=True)
    e21 = jnp.exp(v2 - v1)
    den = 1.0 + e21
    e1 = gidx * E_PER_G + i1
    e2 = gidx * E_PER_G + i2
    eid_ref[0:1, :] = e1
    eid_ref[1:2, :] = e2
    wts_ref[0:1, :] = gw * (1.0 / den)
    wts_ref[1:2, :] = gw * (e21 / den)

    erow = lax.broadcasted_iota(jnp.int32, (N_EXPERTS, TM_R), 0)
    oh1 = erow == e1
    oh2 = erow == e2
    member = jnp.where(oh1 | oh2, 1.0, 0.0)
    ti = lax.broadcasted_iota(jnp.int32, (TM_R, TM_R), 0)
    tj = lax.broadcasted_iota(jnp.int32, (TM_R, TM_R), 1)
    upper = jnp.where(ti < tj, 1.0, 0.0).astype(BF16)
    before = jnp.dot(member.astype(BF16), upper, preferred_element_type=F32)
    pos = before + cnt_ref[:, 0:1]
    rank_ref[0:1, :] = jnp.sum(jnp.where(oh1, pos, 0.0), axis=0, keepdims=True).astype(jnp.int32)
    rank_ref[1:2, :] = jnp.sum(jnp.where(oh2, pos, 0.0), axis=0, keepdims=True).astype(jnp.int32)
    cnt_ref[...] = cnt_ref[...] + jnp.sum(member, axis=1, keepdims=True)


def _router(x1, g2, wr_t, br):
    return pl.pallas_call(
        _router_kernel,
        grid=(SEQ // TM_R,),
        in_specs=[
            pl.BlockSpec((TM_R, D_MODEL), lambda i: (i, 0)),
            pl.BlockSpec((1, D_MODEL), lambda i: (0, 0)),
            pl.BlockSpec((R_ROWS, D_MODEL), lambda i: (0, 0)),
            pl.BlockSpec((R_ROWS, LANES), lambda i: (0, 0)),
        ],
        out_specs=[
            pl.BlockSpec((2, TM_R), lambda i: (0, i)),
            pl.BlockSpec((2, TM_R), lambda i: (0, i)),
            pl.BlockSpec((2, TM_R), lambda i: (0, i)),
            pl.BlockSpec((N_EXPERTS, LANES), lambda i: (0, 0)),
        ],
        out_shape=[
            jax.ShapeDtypeStruct((2, SEQ), jnp.int32),
            jax.ShapeDtypeStruct((2, SEQ), F32),
            jax.ShapeDtypeStruct((2, SEQ), jnp.int32),
            jax.ShapeDtypeStruct((N_EXPERTS, LANES), F32),
        ],
        compiler_params=_cparams(("arbitrary",)),
        name="router",
    )(x1, g2, wr_t, br)


def _moe_kernel(ie_ref, row0_ref, rows_ref, nitems_ref, sorted_ref,
                x_hbm, g_ref, wg_ref, wu_ref, wd_ref, dest_hbm,
                xg_ref, xb_ref, y_ref, gsem, ssem):
    it = pl.program_id(0)
    f = pl.program_id(1)
    nrows = rows_ref[it]
    nitems = nitems_ref[0]
    slot = it % 2
    half = SUB_E // 2

    def padded(item):
        return pl.multiple_of(((rows_ref[item] + half - 1) // half) * half, half)

    nhalf = padded(it) // half
    nsub = nhalf // 2

    def start_gather(item):
        base = row0_ref[item]
        buf = item % 2

        def start(j8, c):
            for k in range(8):
                j = j8 * 8 + k
                tok = sorted_ref[base + j] & (SEQ - 1)
                pltpu.make_async_copy(x_hbm.at[pl.ds(tok, 1), :], xg_ref.at[buf, pl.ds(j, 1), :],
                                      gsem.at[buf]).start()
            return c
        lax.fori_loop(0, padded(item) // 8, start, 0)

    def wait_gather(item):
        got = pl.ds(0, padded(item))
        buf = item % 2
        pltpu.make_async_copy(x_hbm.at[got, :], xg_ref.at[buf, got, :], gsem.at[buf]).wait()

    def scatter_copy(item, j):
        dst = sorted_ref[row0_ref[item] + j]
        buf = item % 2
        return pltpu.make_async_copy(y_ref.at[buf, pl.ds(j, 1), :], dest_hbm.at[pl.ds(dst, 1), :],
                                     ssem.at[buf])

    def start_scatter(item):
        n = rows_ref[item]

        def start8(j8, c):
            for k in range(8):
                scatter_copy(item, j8 * 8 + k).start()
            return c
        lax.fori_loop(0, n // 8, start8, 0)

        def start(j, c):
            scatter_copy(item, j).start()
            return c
        lax.fori_loop((n // 8) * 8, n, start, 0)

    def wait_scatter(item):
        n = rows_ref[item]
        whole = pl.multiple_of((n // 8) * 8, 8)
        buf = item % 2

        @pl.when(whole > 0)
        def _():
            sent = pl.ds(0, whole)
            pltpu.make_async_copy(y_ref.at[buf, sent, :], dest_hbm.at[sent, :], ssem.at[buf]).wait()

        def wait(j, c):
            scatter_copy(item, j).wait()
            return c
        lax.fori_loop(whole, n, wait, 0)

    @pl.when((f == 0) & (nrows > 0))
    def _():
        @pl.when(it == 0)
        def _():
            start_gather(0)

        wait_gather(it)

        def norm(s, c):
            rows = pl.ds(pl.multiple_of(s * half, half), half)
            x = xg_ref[slot, rows, :]
            ms = jnp.mean(x * x, axis=-1, keepdims=True)
            xb_ref[rows, :] = (x * lax.rsqrt(ms + EPS) * g_ref[...]).astype(BF16)
            y_ref[slot, rows, :] = jnp.zeros((half, D_MODEL), F32)
            return c
        lax.fori_loop(0, nhalf, norm, 0)

        @pl.when(it + 1 < nitems)
        def _():
            start_gather(it + 1)

    @pl.when(nrows > 0)
    def _():
        wg = wg_ref[...].astype(BF16)
        wu = wu_ref[...].astype(BF16)
        wd = wd_ref[...].astype(BF16)

        def block(rows):
            xb = xb_ref[rows, :]
            hg = jnp.dot(xb, wg, preferred_element_type=F32)
            hu = jnp.dot(xb, wu, preferred_element_type=F32)
            h = (hg * jax.nn.sigmoid(hg) * hu).astype(BF16)
            y_ref[slot, rows, :] = y_ref[slot, rows, :] + jnp.dot(h, wd, preferred_element_type=F32)

        def sub(s, c):
            block(pl.ds(pl.multiple_of(s * SUB_E, SUB_E), SUB_E))
            return c
        lax.fori_loop(0, nsub, sub, 0)

        @pl.when(nhalf % 2 == 1)
        def _():
            block(pl.ds(pl.multiple_of(nsub * SUB_E, SUB_E), half))

    @pl.when((f == NF_E - 1) & (nrows > 0))
    def _():
        @pl.when(it > 0)
        def _():
            wait_scatter(it - 1)

        start_scatter(it)

        @pl.when(it + 1 >= nitems)
        def _():
            wait_scatter(it)


def _moe(ie, row0, rows, nitems, sorted_i, x1, g2, wg, wu, wd):
    def wmap_cols(i, f, ie_ref, row0_ref, rows_ref, n_ref, s_ref):
        return (ie_ref[i], 0, jnp.where(i < n_ref[0], f, NF_E - 1))

    def wmap_rows(i, f, ie_ref, row0_ref, rows_ref, n_ref, s_ref):
        return (ie_ref[i], jnp.where(i < n_ref[0], f, NF_E - 1), 0)

    grid_spec = pltpu.PrefetchScalarGridSpec(
        num_scalar_prefetch=5,
        grid=(nitems[0], NF_E),
        in_specs=[
            pl.BlockSpec(memory_space=pl.ANY),
            pl.BlockSpec((1, D_MODEL), lambda i, f, *_: (0, 0)),
            pl.BlockSpec((None, D_MODEL, TF_E), wmap_cols),
            pl.BlockSpec((None, D_MODEL, TF_E), wmap_cols),
            pl.BlockSpec((None, TF_E, D_MODEL), wmap_rows),
        ],
        out_specs=pl.BlockSpec(memory_space=pl.ANY),
        scratch_shapes=[
            pltpu.VMEM((2, TM_E, D_MODEL), F32),
            pltpu.VMEM((TM_E, D_MODEL), BF16),
            pltpu.VMEM((2, TM_E, D_MODEL), F32),
            pltpu.SemaphoreType.DMA((2,)),
            pltpu.SemaphoreType.DMA((2,)),
        ],
    )
    return pl.pallas_call(
        _moe_kernel,
        grid_spec=grid_spec,
        out_shape=jax.ShapeDtypeStruct((N_ASSIGN, D_MODEL), F32),
        compiler_params=_cparams(("arbitrary", "arbitrary")),
        name="moe_experts",
    )(ie, row0, rows, nitems, sorted_i, x1, g2, wg, wu, wd)


def _combine_kernel(x_ref, w_ref, d0_ref, d1_ref, o_ref):
    w = w_ref[...]
    o_ref[...] = x_ref[...] + w[:, 0:1] * d0_ref[...] + w[:, 1:2] * d1_ref[...]


def _combine(x1, wts_t, dest):
    dest3 = dest.reshape(2, SEQ, D_MODEL)
    return pl.pallas_call(
        _combine_kernel,
        grid=(SEQ // TT_COMB,),
        in_specs=[
            pl.BlockSpec((TT_COMB, D_MODEL), lambda i: (i, 0)),
            pl.BlockSpec((TT_COMB, 2), lambda i: (i, 0)),
            pl.BlockSpec((None, TT_COMB, D_MODEL), lambda i: (0, i, 0)),
            pl.BlockSpec((None, TT_COMB, D_MODEL), lambda i: (1, i, 0)),
        ],
        out_specs=pl.BlockSpec((TT_COMB, D_MODEL), lambda i: (i, 0)),
        out_shape=jax.ShapeDtypeStruct((SEQ, D_MODEL), F32),
        compiler_params=_cparams(("arbitrary",)),
        name="moe_combine",
    )(x1, wts_t, dest3, dest3)


def _dispatch_kernel(slot_ref, padrow_ref, x_ref, g_ref, xs_hbm, hbuf, zbuf, sem, zsem):
    i = pl.program_id(0)
    last = pl.num_programs(0) - 1
    buf = i % 2

    def tile_wait(b):
        for _ in range(2):
            pltpu.make_async_copy(hbuf.at[b], xs_hbm.at[pl.ds(0, TT_D), :], sem.at[b]).wait()

    @pl.when(i == 0)
    def _():
        zbuf[...] = jnp.zeros_like(zbuf)

        for c in range(XS_TAIL // ZROWS):
            pltpu.make_async_copy(zbuf, xs_hbm.at[pl.ds(XS_ROWS - XS_TAIL + c * ZROWS, ZROWS), :], zsem).start()
        for c in range(XS_TAIL // ZROWS):
            pltpu.make_async_copy(zbuf, xs_hbm.at[pl.ds(0, ZROWS), :], zsem).wait()

        def pad(e, c):
            dst = pl.multiple_of(padrow_ref[e], 8)
            pltpu.make_async_copy(zbuf.at[pl.ds(0, 8), :], xs_hbm.at[pl.ds(dst, 8), :], zsem).start()
            return c
        lax.fori_loop(0, N_EXPERTS, pad, 0)

        def padw(e, c):
            pltpu.make_async_copy(zbuf.at[pl.ds(0, 8), :], xs_hbm.at[pl.ds(0, 8), :], zsem).wait()
            return c
        lax.fori_loop(0, N_EXPERTS, padw, 0)

    @pl.when(i >= 2)
    def _():
        tile_wait(buf)

    x = x_ref[...]
    ms = jnp.mean(x * x, axis=-1, keepdims=True)
    hbuf[buf] = x * lax.rsqrt(ms + EPS) * g_ref[...]

    tok0 = i * TT_D

    def send(j8, c):
        for k8 in range(8):
            j = j8 * 8 + k8
            for k in range(2):
                dst = slot_ref[k * SEQ + tok0 + j]
                pltpu.make_async_copy(hbuf.at[buf, pl.ds(j, 1), :], xs_hbm.at[pl.ds(dst, 1), :],
                                      sem.at[buf]).start()
        return c
    lax.fori_loop(0, TT_D // 8, send, 0)

    @pl.when(i == last)
    def _():
        @pl.when(i >= 1)
        def _():
            tile_wait(1 - buf)
        tile_wait(buf)


def _dispatch(slot_flat, padrow, x1, g2):
    grid_spec = pltpu.PrefetchScalarGridSpec(
        num_scalar_prefetch=2,
        grid=(SEQ // TT_D,),
        in_specs=[
            pl.BlockSpec((TT_D, D_MODEL), lambda i, *_: (i, 0)),
            pl.BlockSpec((1, D_MODEL), lambda i, *_: (0, 0)),
        ],
        out_specs=pl.BlockSpec(memory_space=pl.ANY),
        scratch_shapes=[
            pltpu.VMEM((2, TT_D, D_MODEL), F32),
            pltpu.VMEM((ZROWS, D_MODEL), F32),
            pltpu.SemaphoreType.DMA((2,)),
            pltpu.SemaphoreType.DMA,
        ],
    )
    return pl.pallas_call(
        _dispatch_kernel,
        grid_spec=grid_spec,
        out_shape=jax.ShapeDtypeStruct((XS_ROWS, D_MODEL), F32),
        compiler_params=_cparams(("arbitrary",)),
        name="moe_dispatch",
    )(slot_flat, padrow, x1, g2)


def _experts_kernel(ie_ref, row0_ref, rows_ref, nitems_ref,
                    xs_hbm, wg_hbm, wu_hbm, wd_hbm, ys_hbm,
                    xg_ref, y_ref, zbuf, wg_ref, wu_ref, wd_ref, gsem, ssem, zsem, wsem):
    it = pl.program_id(0)
    nitems = nitems_ref[0]
    slot = it % 2
    half = SUB_E // 2
    nsteps = nitems * NF_E

    def weight_copies(s):
        e = ie_ref[s // NF_E]
        cols = pl.ds(pl.multiple_of((s % NF_E) * TF_E, TF_E), TF_E)
        b = s % W_RING
        return (pltpu.make_async_copy(wg_hbm.at[e, :, cols], wg_ref.at[b], wsem.at[b]),
                pltpu.make_async_copy(wu_hbm.at[e, :, cols], wu_ref.at[b], wsem.at[b]),
                pltpu.make_async_copy(wd_hbm.at[e, cols, :], wd_ref.at[b], wsem.at[b]))

    @pl.when(it == 0)
    def _():
        for s in range(W_RING - 1):
            for cp in weight_copies(s):
                cp.start()

    def padded(item):
        return pl.multiple_of(((rows_ref[item] + half - 1) // half) * half, half)

    nhalf = padded(it) // half
    nsub = nhalf // 2

    def chunk_copies(item, start):
        base = pl.multiple_of(row0_ref[item], 8)
        buf = item % 2

        def body(c, carry):
            off = pl.multiple_of(c * half, half)
            src = xs_hbm.at[pl.ds(pl.multiple_of(base + off, 8), half), :]
            dst = ys_hbm.at[pl.ds(pl.multiple_of(base + off, 8), half), :]
            if start == "fetch":
                pltpu.make_async_copy(src, xg_ref.at[buf, pl.ds(off, half), :], gsem.at[buf]).start()
            else:
                pltpu.make_async_copy(y_ref.at[buf, pl.ds(off, half), :], dst, ssem.at[buf]).start()
            return carry
        lax.fori_loop(0, padded(item) // half, body, 0)

    def wait_fetch(item):
        got = pl.ds(0, padded(item))
        buf = item % 2
        pltpu.make_async_copy(xs_hbm.at[got, :], xg_ref.at[buf, got, :], gsem.at[buf]).wait()

    def wait_store(item):
        put = pl.ds(0, padded(item))
        buf = item % 2
        pltpu.make_async_copy(y_ref.at[buf, put, :], ys_hbm.at[put, :], ssem.at[buf]).wait()

    @pl.when(it == 0)
    def _():
        zbuf[...] = jnp.zeros_like(zbuf)
        for c in range(XS_TAIL // ZROWS):
            pltpu.make_async_copy(zbuf, ys_hbm.at[pl.ds(XS_ROWS - XS_TAIL + c * ZROWS, ZROWS), :], zsem).start()
        for c in range(XS_TAIL // ZROWS):
            pltpu.make_async_copy(zbuf, ys_hbm.at[pl.ds(0, ZROWS), :], zsem).wait()
        chunk_copies(0, "fetch")

    wait_fetch(it)

    def clear(s, c):
        rows = pl.ds(pl.multiple_of(s * half, half), half)
        y_ref[slot, rows, :] = jnp.zeros((half, D_MODEL), F32)
        return c
    lax.fori_loop(0, nhalf, clear, 0)

    @pl.when(it + 1 < nitems)
    def _():
        chunk_copies(it + 1, "fetch")

    def chunk_step(f, carry):
        step = it * NF_E + f
        wslot = step % W_RING

        @pl.when(step + W_RING - 1 < nsteps)
        def _():
            for cp in weight_copies(step + W_RING - 1):
                cp.start()

        for cp in weight_copies(step):
            cp.wait()

        def block(rows):
            xb = xg_ref[slot, rows, :].astype(BF16)
            hg = jnp.dot(xb, wg_ref[wslot].astype(BF16), preferred_element_type=F32)
            hu = jnp.dot(xb, wu_ref[wslot].astype(BF16), preferred_element_type=F32)
            h = (hg * jax.nn.sigmoid(hg) * hu).astype(BF16)
            y_ref[slot, rows, :] = y_ref[slot, rows, :] + jnp.dot(h, wd_ref[wslot].astype(BF16),
                                                                   preferred_element_type=F32)

        def sub(s, c):
            block(pl.ds(pl.multiple_of(s * SUB_E, SUB_E), SUB_E))
            return c
        lax.fori_loop(0, nsub, sub, 0)

        @pl.when(nhalf % 2 == 1)
        def _():
            block(pl.ds(pl.multiple_of(nsub * SUB_E, SUB_E), half))
        return carry

    lax.fori_loop(0, NF_E, chunk_step, 0)

    @pl.when(it > 0)
    def _():
        wait_store(it - 1)

    chunk_copies(it, "store")

    @pl.when(it + 1 >= nitems)
    def _():
        wait_store(it)


def _experts(ie, row0, rows, nitems, xs, wg, wu, wd):
    grid_spec = pltpu.PrefetchScalarGridSpec(
        num_scalar_prefetch=4,
        grid=(nitems[0],),
        in_specs=[pl.BlockSpec(memory_space=pl.ANY) for _ in range(4)],
        out_specs=pl.BlockSpec(memory_space=pl.ANY),
        scratch_shapes=[
            pltpu.VMEM((2, TM_E, D_MODEL), F32),
            pltpu.VMEM((2, TM_E, D_MODEL), F32),
            pltpu.VMEM((ZROWS, D_MODEL), F32),
            pltpu.VMEM((W_RING, D_MODEL, TF_E), F32),
            pltpu.VMEM((W_RING, D_MODEL, TF_E), F32),
            pltpu.VMEM((W_RING, TF_E, D_MODEL), F32),
            pltpu.SemaphoreType.DMA((2,)),
            pltpu.SemaphoreType.DMA((2,)),
            pltpu.SemaphoreType.DMA,
            pltpu.SemaphoreType.DMA((W_RING,)),
        ],
    )
    return pl.pallas_call(
        _experts_kernel,
        grid_spec=grid_spec,
        out_shape=jax.ShapeDtypeStruct((XS_ROWS, D_MODEL), F32),
        compiler_params=_cparams(("arbitrary",)),
        name="moe_experts",
    )(ie, row0, rows, nitems, xs, wg, wu, wd)


def _gcombine_kernel(slot_ref, x_ref, w_ref, ys_hbm, o_ref, dbuf, sem):
    i = pl.program_id(0)
    n = pl.num_programs(0)
    buf = i % 2

    def fetch(tile, b):
        tok0 = tile * TT_G

        def body(j8, c):
            for k8 in range(8):
                j = j8 * 8 + k8
                for k in range(2):
                    src = slot_ref[k * SEQ + tok0 + j]
                    pltpu.make_async_copy(ys_hbm.at[pl.ds(src, 1), :], dbuf.at[b, k, pl.ds(j, 1), :],
                                          sem.at[b]).start()
            return c
        lax.fori_loop(0, TT_G // 8, body, 0)

    @pl.when(i == 0)
    def _():
        fetch(0, 0)

    @pl.when(i + 1 < n)
    def _():
        fetch(i + 1, 1 - buf)

    for k in range(2):
        pltpu.make_async_copy(ys_hbm.at[pl.ds(0, TT_G), :], dbuf.at[buf, k], sem.at[buf]).wait()

    w = w_ref[...]
    o_ref[...] = x_ref[...] + w[:, 0:1] * dbuf[buf, 0] + w[:, 1:2] * dbuf[buf, 1]


def _gcombine(slot_flat, x1, wts_t, ys):
    grid_spec = pltpu.PrefetchScalarGridSpec(
        num_scalar_prefetch=1,
        grid=(SEQ // TT_G,),
        in_specs=[
            pl.BlockSpec((TT_G, D_MODEL), lambda i, *_: (i, 0)),
            pl.BlockSpec((TT_G, 2), lambda i, *_: (i, 0)),
            pl.BlockSpec(memory_space=pl.ANY),
        ],
        out_specs=pl.BlockSpec((TT_G, D_MODEL), lambda i, *_: (i, 0)),
        scratch_shapes=[
            pltpu.VMEM((2, 2, TT_G, D_MODEL), F32),
            pltpu.SemaphoreType.DMA((2,)),
        ],
    )
    return pl.pallas_call(
        _gcombine_kernel,
        grid_spec=grid_spec,
        out_shape=jax.ShapeDtypeStruct((SEQ, D_MODEL), F32),
        compiler_params=_cparams(("arbitrary",)),
        name="moe_combine",
    )(slot_flat, x1, wts_t, ys)


def _routing_tables(eid, rank, counts_f):
    counts = counts_f[:, 0].astype(jnp.int32)
    aligned = ((counts + 7) // 8) * 8
    base = jnp.cumsum(aligned) - aligned
    tiles = (counts + TM_E - 1) // TM_E
    tcum = jnp.cumsum(tiles)
    tstart = tcum - tiles
    nitems = tcum[-1]
    ids = jnp.arange(MAX_ITEMS, dtype=jnp.int32)
    ie = jnp.clip(jnp.searchsorted(tcum, ids, side="right"), 0, N_EXPERTS - 1).astype(jnp.int32)
    live = ids < nitems
    ie = jnp.where(live, ie, ie[jnp.maximum(nitems - 1, 0)])
    jt = ids - tstart[ie]
    row0 = jnp.where(live, base[ie] + jt * TM_E, 0)
    rows = jnp.where(live, jnp.clip(counts[ie] - jt * TM_E, 0, TM_E), 0)
    eoh = eid[:, :, None] == jnp.arange(N_EXPERTS, dtype=jnp.int32)
    slot = jnp.sum(jnp.where(eoh, base, 0), axis=-1) + rank
    spare = XS_ROWS - 8 * (1 + jnp.arange(N_EXPERTS, dtype=jnp.int32))
    padrow = jnp.where(counts % 8 != 0, base + (counts // 8) * 8, spare)
    return (ie, row0.astype(jnp.int32), rows.astype(jnp.int32), nitems.reshape(1).astype(jnp.int32),
            slot.reshape(-1).astype(jnp.int32), padrow.astype(jnp.int32))


def _work_items(eid, rank, counts_f):
    counts = counts_f[:, 0].astype(jnp.int32)
    cum = jnp.cumsum(counts)
    base = cum - counts
    tiles = (counts + TM_E - 1) // TM_E
    tcum = jnp.cumsum(tiles)
    tstart = tcum - tiles
    nitems = tcum[-1]
    ids = jnp.arange(MAX_ITEMS, dtype=jnp.int32)
    ie = jnp.clip(jnp.searchsorted(tcum, ids, side="right"), 0, N_EXPERTS - 1).astype(jnp.int32)
    live = ids < nitems
    ie = jnp.where(live, ie, ie[jnp.maximum(nitems - 1, 0)])
    jt = ids - tstart[ie]
    row0 = jnp.where(live, base[ie] + jt * TM_E, 0)
    rows = jnp.where(live, jnp.clip(counts[ie] - jt * TM_E, 0, TM_E), 0)
    eoh = eid[:, :, None] == jnp.arange(N_EXPERTS, dtype=jnp.int32)
    slot = jnp.sum(jnp.where(eoh, base, 0), axis=-1) + rank
    sorted_i = jnp.zeros((SORTED_LEN,), jnp.int32).at[slot.reshape(-1)].set(
        jnp.arange(N_ASSIGN, dtype=jnp.int32))
    return ie, row0.astype(jnp.int32), rows.astype(jnp.int32), nitems.reshape(1).astype(jnp.int32), sorted_i


def kernel(x, norm1_g, w_in, q_norm_g, k_norm_g, conv_w, conv_b, conv_ln_g, conv_ln_b, rel_bias,
           w_out, norm2_g, w_router_group, b_router_group, w_router_expert, b_router_expert,
           w_gate, w_up, w_down):
    assert x.shape == (1, SEQ, D_MODEL) and w_in.shape[0] == 1
    xs = x[0]
    bias_tab = _attn_bias_tables(rel_bias)
    qg2 = jnp.tile(q_norm_g[0], 2)[None]
    kg2 = jnp.tile(k_norm_g[0], 2)[None]

    proj = _inproj(xs, norm1_g[0][None], w_in[0])
    conv_out = _conv_mixer(proj, conv_w[0], conv_b[0][None], conv_ln_g[0][None], conv_ln_b[0][None])
    attn_out = _attention(proj, qg2, kg2, bias_tab)
    x1 = _outproj(xs, conv_out, attn_out, w_out[0].astype(BF16))

    wr_t = jnp.concatenate([
        w_router_group[0].T, jnp.zeros((8 - N_GROUPS, D_MODEL), F32),
        jnp.transpose(w_router_expert[0], (0, 2, 1)).reshape(N_EXPERTS, D_MODEL)], axis=0)
    br = jnp.concatenate([b_router_group[0], jnp.zeros((8 - N_GROUPS,), F32),
                          b_router_expert[0].reshape(-1)])
    br = jnp.broadcast_to(br[:, None], (R_ROWS, LANES))
    eid, wts, rank, counts_f = _router(x1, norm2_g[0][None], wr_t, br)

    ie, row0, rows, nitems, slot_flat, padrow = _routing_tables(eid, rank, counts_f)
    xs = _dispatch(slot_flat, padrow, x1, norm2_g[0][None])
    ys = _experts(ie, row0, rows, nitems, xs,
                  w_gate[0].reshape(N_EXPERTS, D_MODEL, D_FF),
                  w_up[0].reshape(N_EXPERTS, D_MODEL, D_FF),
                  w_down[0].reshape(N_EXPERTS, D_FF, D_MODEL))
    out = _gcombine(slot_flat, x1, wts.T, ys)
    return out[None]
```

```python
import functools
import math

import numpy as np
import jax
import jax.numpy as jnp
from jax import lax
from jax.experimental import pallas as pl
from jax.experimental.pallas import tpu as pltpu

F32 = jnp.float32
BF16 = jnp.bfloat16

D_MODEL = 2048
SEQ = 8192
N_HEADS = 16
HEAD_DIM = 64
ATTN_W = N_HEADS * HEAD_DIM
CONV_C = D_MODEL - ATTN_W
CONV_K = 31
IN_W = 2 * CONV_C + 3 * ATTN_W
PATTERNS = ((128, 1), (512, 4), (2048, 16))
QBLK = 128
NUM_BUCKETS = 32
MAX_DISTANCE = 2048
N_GROUPS = 4
E_PER_G = 8
N_EXPERTS = N_GROUPS * E_PER_G
D_FF = D_MODEL // 2
EPS = 1e-6
NEG_INF = -1e30
LOG2E = math.log2(math.e)

LANES = 128
VMEM_LIMIT = 56 * 1024 * 1024

TM_IN = 1024
TN_IN = 1024
TT_CONV = 512
HALO = 32
R_CONV = 64
R_LN = 16
ATTN_GROUP = 4
ATTN_UNROLL = 2
TM_OUT = 512
TM_R = 512
R_ROWS = 8 + N_EXPERTS
KC_R = 512
TM_E = 768
SUB_E = 256
TF_E = 256
NF_E = D_FF // TF_E
W_RING = 4
N_ASSIGN = 2 * SEQ
MAX_ITEMS = -(-N_ASSIGN // TM_E) + N_EXPERTS
SORTED_LEN = N_ASSIGN + TM_E
XS_TAIL = 256 + TM_E
XS_ROWS = N_ASSIGN + XS_TAIL
ZROWS = 128
TT_D = 512
TT_G = 512
TT_COMB = 512


def _cparams(sem, vmem=VMEM_LIMIT, flags=None):
    return pltpu.CompilerParams(dimension_semantics=sem, vmem_limit_bytes=vmem, flags=flags)


def _inproj_kernel(x_ref, g_ref, w_ref, o_ref, xn_ref):
    @pl.when(pl.program_id(1) == 0)
    def _():
        x = x_ref[...]
        ms = jnp.mean(x * x, axis=-1, keepdims=True)
        xn_ref[...] = (x * lax.rsqrt(ms + EPS) * g_ref[...]).astype(BF16)

    o_ref[...] = jnp.dot(xn_ref[...], w_ref[...].astype(BF16), preferred_element_type=F32)


def _inproj(x, g, w):
    return pl.pallas_call(
        _inproj_kernel,
        grid=(SEQ // TM_IN, IN_W // TN_IN),
        in_specs=[
            pl.BlockSpec((TM_IN, D_MODEL), lambda i, j: (i, 0)),
            pl.BlockSpec((1, D_MODEL), lambda i, j: (0, 0)),
            pl.BlockSpec((D_MODEL, TN_IN), lambda i, j: (0, j)),
        ],
        out_specs=pl.BlockSpec((TM_IN, TN_IN), lambda i, j: (i, j)),
        out_shape=jax.ShapeDtypeStruct((SEQ, IN_W), F32),
        scratch_shapes=[pltpu.VMEM((TM_IN, D_MODEL), BF16)],
        compiler_params=_cparams(("arbitrary", "arbitrary")),
        name="inproj",
    )(x, g, w)


def _conv_kernel(val_ref, gate_ref, hval_ref, hgate_ref, cw_ref, cb_ref, lg_ref, lb_ref,
                 o_ref, ubuf, zbuf, ybuf):
    i = pl.program_id(0)
    u = val_ref[...] * jax.nn.sigmoid(gate_ref[...])
    hu = hval_ref[...] * jax.nn.sigmoid(hgate_ref[...])
    hu = jnp.where(i > 0, hu, 0.0)
    for c in range(CONV_C // LANES):
        cols = slice(c * LANES, (c + 1) * LANES)
        ubuf[c, 0:HALO, :] = hu[:, cols]
        ubuf[c, HALO:HALO + TT_CONV, :] = u[:, cols]

    n_a = -(-CONV_K // 8)
    assert HALO == 8 * n_a
    for c in range(CONV_C // LANES):
        cols = slice(c * LANES, (c + 1) * LANES)

        def taps(r, carry, c=c, cols=cols):
            base = pl.multiple_of(r * R_CONV, R_CONV)
            win = ubuf[c, pl.ds(base, R_CONV + HALO), :]
            for b in range(8):
                z = None
                for a in range(n_a):
                    s = 8 * a + b
                    if s >= CONV_K:
                        continue
                    lo = HALO - 8 - 8 * a
                    t = cw_ref[CONV_K - 1 - s:CONV_K - s, cols] * win[lo:lo + R_CONV + 8, :]
                    z = t if z is None else z + t
                zbuf[b, pl.ds(0, R_CONV + 8, stride=2), :] = z
            acc = None
            for b in range(8):
                t = zbuf[b, pl.ds(2 * (8 - b), R_CONV, stride=2), :]
                acc = t if acc is None else acc + t
            ybuf[pl.ds(base, R_CONV), cols] = acc
            return carry

        lax.fori_loop(0, TT_CONV // R_CONV, taps, 0)

    def norm(r, carry):
        rows = pl.ds(pl.multiple_of(r * R_LN, R_LN), R_LN)
        acc = ybuf[rows, :] + cb_ref[...]
        mu = jnp.mean(acc, axis=-1, keepdims=True)
        xc = acc - mu
        var = jnp.mean(xc * xc, axis=-1, keepdims=True)
        y = xc * lax.rsqrt(var + EPS) * lg_ref[...] + lb_ref[...]
        o_ref[rows, :] = (y * jax.nn.sigmoid(y)).astype(BF16)
        return carry

    lax.fori_loop(0, TT_CONV // R_LN, norm, 0, unroll=4)


def _conv_mixer(proj, cw, cb, lg, lb):
    hb = TT_CONV // HALO
    return pl.pallas_call(
        _conv_kernel,
        grid=(SEQ // TT_CONV,),
        in_specs=[
            pl.BlockSpec((TT_CONV, CONV_C), lambda i: (i, 0)),
            pl.BlockSpec((TT_CONV, CONV_C), lambda i: (i, 1)),
            pl.BlockSpec((HALO, CONV_C), lambda i: (jnp.maximum(i * hb - 1, 0), 0)),
            pl.BlockSpec((HALO, CONV_C), lambda i: (jnp.maximum(i * hb - 1, 0), 1)),
            pl.BlockSpec((CONV_K, CONV_C), lambda i: (0, 0)),
            pl.BlockSpec((1, CONV_C), lambda i: (0, 0)),
            pl.BlockSpec((1, CONV_C), lambda i: (0, 0)),
            pl.BlockSpec((1, CONV_C), lambda i: (0, 0)),
        ],
        out_specs=pl.BlockSpec((TT_CONV, CONV_C), lambda i: (i, 0)),
        out_shape=jax.ShapeDtypeStruct((SEQ, CONV_C), BF16),
        scratch_shapes=[pltpu.VMEM((CONV_C // LANES, HALO + TT_CONV, LANES), F32),
                        pltpu.VMEM((8, 2 * (R_CONV + 8), LANES), F32),
                        pltpu.VMEM((TT_CONV, CONV_C), F32)],
        compiler_params=_cparams(("arbitrary",)),
        name="conv_mixer",
    )(proj, proj, proj, proj, cw, cb, lg, lb)


def _t5_bucket_np(dist):
    max_exact = NUM_BUCKETS // 2
    nf = np.maximum(dist, 1).astype(np.float32)
    large = max_exact + (np.log(nf / np.float32(max_exact)) / np.float32(math.log(MAX_DISTANCE / max_exact))
                         * np.float32(NUM_BUCKETS - max_exact)).astype(np.int32)
    large = np.minimum(large, NUM_BUCKETS - 1)
    return np.where(dist < max_exact, dist, large)


def _attn_bias_tables(rel_bias):
    period = 3 * QBLK
    diags = []
    for window, dil in PATTERNS:
        span = window // dil
        assert span <= QBLK
        bucket = _t5_bucket_np(np.arange(span + 1) * dil)
        onehot = np.eye(NUM_BUCKETS, dtype=np.float32)[bucket]
        vec = jnp.einsum("rb,bh->hr", onehot, rel_bias.astype(F32),
                         precision=lax.Precision.HIGHEST)
        diag = jnp.full((N_HEADS, period), NEG_INF, F32)
        diags.append(diag.at[:, 2 * QBLK - 1 - span:2 * QBLK].set(vec[:, ::-1]))
    diag = jnp.stack(diags) * LOG2E
    skew = jnp.tile(diag, (1, 1, QBLK))[:, :, :QBLK * (period - 1)]
    skew = skew.reshape(len(PATTERNS), N_HEADS, QBLK, period - 1)[..., QBLK - 1:3 * QBLK - 1]
    return skew.reshape(len(PATTERNS), N_HEADS // 2, 2 * QBLK, 2 * QBLK)


def _attn_kernel(q_ref, k_ref, v_ref, qg_ref, kg_ref, bias_ref, o_ref,
                 qn_ref, kn_ref, acc_ref, m_ref, l_ref):
    lane = lax.broadcasted_iota(jnp.int32, (QBLK, LANES), 1)
    head_a = lane < HEAD_DIM
    hist_keys = lax.broadcasted_iota(jnp.int32, (2 * QBLK, 2 * QBLK), 1) < QBLK
    ri = lax.broadcasted_iota(jnp.int32, (LANES, LANES), 0) // HEAD_DIM
    ci = lax.broadcasted_iota(jnp.int32, (LANES, LANES), 1) // HEAD_DIM
    seg = (ri == ci).astype(BF16)

    def head_rms(x, g):
        sq = x * x
        hi = sq.astype(BF16)
        lo = (sq - hi.astype(F32)).astype(BF16)
        ss = (jnp.dot(hi, seg, preferred_element_type=F32)
              + jnp.dot(lo, seg, preferred_element_type=F32))
        return x * lax.rsqrt(ss * (1.0 / HEAD_DIM) + EPS) * g

    NCH = 512

    def norm_body(c, carry):
        rows = pl.ds(pl.multiple_of(c * NCH, NCH), NCH)
        qn_ref[rows, :] = head_rms(q_ref[rows, :], qg_ref[...]) * (LOG2E / math.sqrt(HEAD_DIM))
        kn_ref[rows, :] = head_rms(k_ref[rows, :], kg_ref[...])
        return carry

    lax.fori_loop(0, SEQ // NCH, norm_body, 0)

    order = sorted(range(len(PATTERNS)), key=lambda i: -PATTERNS[i][1])
    assert PATTERNS[order[-1]][1] == 1
    for step, p in enumerate(order):
        dil = PATTERNS[p][1]
        nb = SEQ // (dil * QBLK)
        is_first = step == 0
        is_last = step == len(order) - 1

        def rows(start, dil=dil):
            if dil == 1:
                return pl.ds(start, QBLK)
            return pl.ds(start, QBLK, stride=dil)

        def unit(cur, k_prev, v_prev, first, p=p, rows=rows, is_first=is_first, is_last=is_last):
            q = qn_ref[rows(cur), :]
            q2 = jnp.concatenate([jnp.where(head_a, q, 0.0), jnp.where(head_a, 0.0, q)],
                                 axis=0).astype(BF16)
            k_cur = kn_ref[rows(cur), :].astype(BF16)
            v_cur = v_ref[rows(cur), :].astype(BF16)
            k2 = jnp.concatenate([k_prev, k_cur], axis=0)
            v2 = jnp.concatenate([v_prev, v_cur], axis=0)
            s = lax.dot_general(q2, k2, (((1,), (1,)), ((), ())), preferred_element_type=F32)
            s = s + bias_ref[p]
            if first is not None:
                s = jnp.where(hist_keys, jnp.where(first, NEG_INF * LOG2E, s), s)
            m = jnp.max(s, axis=-1, keepdims=True)
            e = jnp.exp2(s - m)
            l = jnp.sum(e, axis=-1, keepdims=True)
            pv = jnp.dot(e.astype(BF16), v2, preferred_element_type=F32)
            o_new = jnp.where(head_a, pv[:QBLK], pv[QBLK:])
            m_new = jnp.where(head_a, m[:QBLK], m[QBLK:])
            l_new = jnp.where(head_a, l[:QBLK], l[QBLK:])
            if is_first:
                acc_ref[rows(cur), :] = o_new
                m_ref[rows(cur), :] = m_new
                l_ref[rows(cur), :] = l_new
            else:
                m_old = m_ref[rows(cur), :]
                m_tot = jnp.maximum(m_old, m_new)
                a = jnp.exp2(m_old - m_tot)
                b = jnp.exp2(m_new - m_tot)
                acc = acc_ref[rows(cur), :] * a + o_new * b
                den = l_ref[rows(cur), :] * a + l_new * b
                if is_last:
                    o_ref[pl.ds(pl.multiple_of(cur, QBLK), QBLK), :] = (acc / den).astype(BF16)
                else:
                    acc_ref[rows(cur), :] = acc
                    l_ref[rows(cur), :] = den
                    m_ref[rows(cur), :] = m_tot
            return k_cur, v_cur

        per = nb // ATTN_GROUP
        assert per * ATTN_GROUP == nb

        def group(g, carry, dil=dil, per=per, rows=rows, unit=unit):
            r = g // per
            n0 = (g - r * per) * ATTN_GROUP
            hist = jnp.maximum(n0 - 1, 0) * (QBLK * dil) + r
            k_prev = kn_ref[rows(hist), :].astype(BF16)
            v_prev = v_ref[rows(hist), :].astype(BF16)
            for i in range(ATTN_GROUP):
                first = (n0 == 0) if i == 0 else None
                k_prev, v_prev = unit((n0 + i) * (QBLK * dil) + r, k_prev, v_prev, first)
            return carry

        lax.fori_loop(0, SEQ // (QBLK * ATTN_GROUP), group, 0, unroll=ATTN_UNROLL)


def _attention(proj, qg2, kg2, bias_tab):
    qoff = 2 * CONV_C // LANES
    koff = qoff + ATTN_W // LANES
    voff = koff + ATTN_W // LANES
    return pl.pallas_call(
        _attn_kernel,
        grid=(N_HEADS // 2,),
        in_specs=[
            pl.BlockSpec((SEQ, LANES), lambda h: (0, qoff + h)),
            pl.BlockSpec((SEQ, LANES), lambda h: (0, koff + h)),
            pl.BlockSpec((SEQ, LANES), lambda h: (0, voff + h)),
            pl.BlockSpec((1, LANES), lambda h: (0, 0)),
            pl.BlockSpec((1, LANES), lambda h: (0, 0)),
            pl.BlockSpec((len(PATTERNS), None, 2 * QBLK, 2 * QBLK), lambda h: (0, h, 0, 0)),
        ],
        out_specs=pl.BlockSpec((SEQ, LANES), lambda h: (0, h)),
        out_shape=jax.ShapeDtypeStruct((SEQ, ATTN_W), BF16),
        scratch_shapes=[pltpu.VMEM((SEQ, LANES), F32) for _ in range(5)],
        compiler_params=_cparams(("arbitrary",)),
        name="dilated_attn",
    )(proj, proj, proj, qg2, kg2, bias_tab)


def _outproj_kernel(x_ref, c_ref, a_ref, wc_ref, wa_ref, o_ref):
    o_ref[...] = (x_ref[...]
                  + jnp.dot(c_ref[...], wc_ref[...].astype(BF16), preferred_element_type=F32)
                  + jnp.dot(a_ref[...], wa_ref[...].astype(BF16), preferred_element_type=F32))


def _outproj(x, conv_out, attn_out, w_out):
    return pl.pallas_call(
        _outproj_kernel,
        grid=(SEQ // TM_OUT,),
        in_specs=[
            pl.BlockSpec((TM_OUT, D_MODEL), lambda i: (i, 0)),
            pl.BlockSpec((TM_OUT, CONV_C), lambda i: (i, 0)),
            pl.BlockSpec((TM_OUT, ATTN_W), lambda i: (i, 0)),
            pl.BlockSpec((CONV_C, D_MODEL), lambda i: (0, 0), pipeline_mode=pl.Buffered(1)),
            pl.BlockSpec((ATTN_W, D_MODEL), lambda i: (1, 0), pipeline_mode=pl.Buffered(1)),
        ],
        out_specs=pl.BlockSpec((TM_OUT, D_MODEL), lambda i: (i, 0)),
        out_shape=jax.ShapeDtypeStruct((SEQ, D_MODEL), F32),
        compiler_params=_cparams(("arbitrary",)),
        name="outproj",
    )(x, conv_out, attn_out, w_out, w_out)


def _split2(a):
    a1 = a.astype(BF16)
    a2 = (a - a1.astype(F32)).astype(BF16)
    return a1, a2


def _router_kernel(x_ref, g_ref, wr_ref, br_ref, eid_ref, wts_ref, rank_ref, cnt_ref):
    i = pl.program_id(0)

    @pl.when(i == 0)
    def _():
        cnt_ref[...] = jnp.zeros_like(cnt_ref)

    x = x_ref[...]
    inv = lax.rsqrt(jnp.mean(x * x, axis=-1, keepdims=True) + EPS)
    dn = (((1,), (1,)), ((), ()))
    lt = None
    for c in range(D_MODEL // KC_R):
        cols = slice(c * KC_R, (c + 1) * KC_R)
        h1, h2 = _split2(x_ref[:, cols] * inv * g_ref[:, cols])
        w1, w2 = _split2(wr_ref[:, cols])
        for wa, ha in ((w1, h1), (w1, h2), (w2, h1)):
            t = lax.dot_general(wa, ha, dn, preferred_element_type=F32)
            lt = t if lt is None else lt + t
    lt = lt + br_ref[:, 0:1]

    row8 = lax.broadcasted_iota(jnp.int32, (8, TM_R), 0)
    gl = jnp.where(row8 < N_GROUPS, lt[0:8], -jnp.inf)
    gmax = jnp.max(gl, axis=0, keepdims=True)
    gidx = jnp.min(jnp.where(gl == gmax, row8, 8), axis=0, keepdims=True)
    gw = 1.0 / jnp.sum(jnp.exp(gl - gmax), axis=0, keepdims=True)

    esel = lt[8:16]
    for g in range(1, N_GROUPS):
        esel = jnp.where(gidx == g, lt[8 + 8 * g:16 + 8 * g], esel)
    v1 = jnp.max(esel, axis=0, keepdims=True)
    i1 = jnp.min(jnp.where(esel == v1, row8, 8), axis=0, keepdims=True)
    rest = jnp.where(row8 == i1, -jnp.inf, esel)
    v2 = jnp.max(rest, axis=0, keepdims=True)
    i2 = jnp.min(jnp.where(rest == v2, row8, 8), axis=0, keepdims=True)
    e21 = jnp.exp(v2 - v1)
    den = 1.0 + e21
    e1 = gidx * E_PER_G + i1
    e2 = gidx * E_PER_G + i2
    eid_ref[0:1, :] = e1
    eid_ref[1:2, :] = e2
    wts_ref[0:1, :] = gw * (1.0 / den)
    wts_ref[1:2, :] = gw * (e21 / den)

    erow = lax.broadcasted_iota(jnp.int32, (N_EXPERTS, TM_R), 0)
    oh1 = erow == e1
    oh2 = erow == e2
    member = jnp.where(oh1 | oh2, 1.0, 0.0)
    ti = lax.broadcasted_iota(jnp.int32, (TM_R, TM_R), 0)
    tj = lax.broadcasted_iota(jnp.int32, (TM_R, TM_R), 1)
    upper = jnp.where(ti < tj, 1.0, 0.0).astype(BF16)
    before = jnp.dot(member.astype(BF16), upper, preferred_element_type=F32)
    pos = before + cnt_ref[:, 0:1]
    rank_ref[0:1, :] = jnp.sum(jnp.where(oh1, pos, 0.0), axis=0, keepdims=True).astype(jnp.int32)
    rank_ref[1:2, :] = jnp.sum(jnp.where(oh2, pos, 0.0), axis=0, keepdims=True).astype(jnp.int32)
    cnt_ref[...] = cnt_ref[...] + jnp.sum(member, axis=1, keepdims=True)


def _router(x1, g2, wr_t, br):
    return pl.pallas_call(
        _router_kernel,
        grid=(SEQ // TM_R,),
        in_specs=[
            pl.BlockSpec((TM_R, D_MODEL), lambda i: (i, 0)),
            pl.BlockSpec((1, D_MODEL), lambda i: (0, 0)),
            pl.BlockSpec((R_ROWS, D_MODEL), lambda i: (0, 0)),
            pl.BlockSpec((R_ROWS, LANES), lambda i: (0, 0)),
        ],
        out_specs=[
            pl.BlockSpec((2, TM_R), lambda i: (0, i)),
            pl.BlockSpec((2, TM_R), lambda i: (0, i)),
            pl.BlockSpec((2, TM_R), lambda i: (0, i)),
            pl.BlockSpec((N_EXPERTS, LANES), lambda i: (0, 0)),
        ],
        out_shape=[
            jax.ShapeDtypeStruct((2, SEQ), jnp.int32),
            jax.ShapeDtypeStruct((2, SEQ), F32),
            jax.ShapeDtypeStruct((2, SEQ), jnp.int32),
            jax.ShapeDtypeStruct((N_EXPERTS, LANES), F32),
        ],
        compiler_params=_cparams(("arbitrary",)),
        name="router",
    )(x1, g2, wr_t, br)


def _moe_kernel(ie_ref, row0_ref, rows_ref, nitems_ref, sorted_ref,
                x_hbm, g_ref, wg_ref, wu_ref, wd_ref, dest_hbm,
                xg_ref, xb_ref, y_ref, gsem, ssem):
    it = pl.program_id(0)
    f = pl.program_id(1)
    nrows = rows_ref[it]
    nitems = nitems_ref[0]
    slot = it % 2
    half = SUB_E // 2

    def padded(item):
        return pl.multiple_of(((rows_ref[item] + half - 1) // half) * half, half)

    nhalf = padded(it) // half
    nsub = nhalf // 2

    def start_gather(item):
        base = row0_ref[item]
        buf = item % 2

        def start(j8, c):
            for k in range(8):
                j = j8 * 8 + k
                tok = sorted_ref[base + j] & (SEQ - 1)
                pltpu.make_async_copy(x_hbm.at[pl.ds(tok, 1), :], xg_ref.at[buf, pl.ds(j, 1), :],
                                      gsem.at[buf]).start()
            return c
        lax.fori_loop(0, padded(item) // 8, start, 0)

    def wait_gather(item):
        got = pl.ds(0, padded(item))
        buf = item % 2
        pltpu.make_async_copy(x_hbm.at[got, :], xg_ref.at[buf, got, :], gsem.at[buf]).wait()

    def scatter_copy(item, j):
        dst = sorted_ref[row0_ref[item] + j]
        buf = item % 2
        return pltpu.make_async_copy(y_ref.at[buf, pl.ds(j, 1), :], dest_hbm.at[pl.ds(dst, 1), :],
                                     ssem.at[buf])

    def start_scatter(item):
        n = rows_ref[item]

        def start8(j8, c):
            for k in range(8):
                scatter_copy(item, j8 * 8 + k).start()
            return c
        lax.fori_loop(0, n // 8, start8, 0)

        def start(j, c):
            scatter_copy(item, j).start()
            return c
        lax.fori_loop((n // 8) * 8, n, start, 0)

    def wait_scatter(item):
        n = rows_ref[item]
        whole = pl.multiple_of((n // 8) * 8, 8)
        buf = item % 2

        @pl.when(whole > 0)
        def _():
            sent = pl.ds(0, whole)
            pltpu.make_async_copy(y_ref.at[buf, sent, :], dest_hbm.at[sent, :], ssem.at[buf]).wait()

        def wait(j, c):
            scatter_copy(item, j).wait()
            return c
        lax.fori_loop(whole, n, wait, 0)

    @pl.when((f == 0) & (nrows > 0))
    def _():
        @pl.when(it == 0)
        def _():
            start_gather(0)

        wait_gather(it)

        def norm(s, c):
            rows = pl.ds(pl.multiple_of(s * half, half), half)
            x = xg_ref[slot, rows, :]
            ms = jnp.mean(x * x, axis=-1, keepdims=True)
            xb_ref[rows, :] = (x * lax.rsqrt(ms + EPS) * g_ref[...]).astype(BF16)
            y_ref[slot, rows, :] = jnp.zeros((half, D_MODEL), F32)
            return c
        lax.fori_loop(0, nhalf, norm, 0)

        @pl.when(it + 1 < nitems)
        def _():
            start_gather(it + 1)

    @pl.when(nrows > 0)
    def _():
        wg = wg_ref[...].astype(BF16)
        wu = wu_ref[...].astype(BF16)
        wd = wd_ref[...].astype(BF16)

        def block(rows):
            xb = xb_ref[rows, :]
            hg = jnp.dot(xb, wg, preferred_element_type=F32)
            hu = jnp.dot(xb, wu, preferred_element_type=F32)
            h = (hg * jax.nn.sigmoid(hg) * hu).astype(BF16)
            y_ref[slot, rows, :] = y_ref[slot, rows, :] + jnp.dot(h, wd, preferred_element_type=F32)

        def sub(s, c):
            block(pl.ds(pl.multiple_of(s * SUB_E, SUB_E), SUB_E))
            return c
        lax.fori_loop(0, nsub, sub, 0)

        @pl.when(nhalf % 2 == 1)
        def _():
            block(pl.ds(pl.multiple_of(nsub * SUB_E, SUB_E), half))

    @pl.when((f == NF_E - 1) & (nrows > 0))
    def _():
        @pl.when(it > 0)
        def _():
            wait_scatter(it - 1)

        start_scatter(it)

        @pl.when(it + 1 >= nitems)
        def _():
            wait_scatter(it)


def _moe(ie, row0, rows, nitems, sorted_i, x1, g2, wg, wu, wd):
    def wmap_cols(i, f, ie_ref, row0_ref, rows_ref, n_ref, s_ref):
        return (ie_ref[i], 0, jnp.where(i < n_ref[0], f, NF_E - 1))

    def wmap_rows(i, f, ie_ref, row0_ref, rows_ref, n_ref, s_ref):
        return (ie_ref[i], jnp.where(i < n_ref[0], f, NF_E - 1), 0)

    grid_spec = pltpu.PrefetchScalarGridSpec(
        num_scalar_prefetch=5,
        grid=(nitems[0], NF_E),
        in_specs=[
            pl.BlockSpec(memory_space=pl.ANY),
            pl.BlockSpec((1, D_MODEL), lambda i, f, *_: (0, 0)),
            pl.BlockSpec((None, D_MODEL, TF_E), wmap_cols),
            pl.BlockSpec((None, D_MODEL, TF_E), wmap_cols),
            pl.BlockSpec((None, TF_E, D_MODEL), wmap_rows),
        ],
        out_specs=pl.BlockSpec(memory_space=pl.ANY),
        scratch_shapes=[
            pltpu.VMEM((2, TM_E, D_MODEL), F32),
            pltpu.VMEM((TM_E, D_MODEL), BF16),
            pltpu.VMEM((2, TM_E, D_MODEL), F32),
            pltpu.SemaphoreType.DMA((2,)),
            pltpu.SemaphoreType.DMA((2,)),
        ],
    )
    return pl.pallas_call(
        _moe_kernel,
        grid_spec=grid_spec,
        out_shape=jax.ShapeDtypeStruct((N_ASSIGN, D_MODEL), F32),
        compiler_params=_cparams(("arbitrary", "arbitrary")),
        name="moe_experts",
    )(ie, row0, rows, nitems, sorted_i, x1, g2, wg, wu, wd)


def _combine_kernel(x_ref, w_ref, d0_ref, d1_ref, o_ref):
    w = w_ref[...]
    o_ref[...] = x_ref[...] + w[:, 0:1] * d0_ref[...] + w[:, 1:2] * d1_ref[...]


def _combine(x1, wts_t, dest):
    dest3 = dest.reshape(2, SEQ, D_MODEL)
    return pl.pallas_call(
        _combine_kernel,
        grid=(SEQ // TT_COMB,),
        in_specs=[
            pl.BlockSpec((TT_COMB, D_MODEL), lambda i: (i, 0)),
            pl.BlockSpec((TT_COMB, 2), lambda i: (i, 0)),
            pl.BlockSpec((None, TT_COMB, D_MODEL), lambda i: (0, i, 0)),
            pl.BlockSpec((None, TT_COMB, D_MODEL), lambda i: (1, i, 0)),
        ],
        out_specs=pl.BlockSpec((TT_COMB, D_MODEL), lambda i: (i, 0)),
        out_shape=jax.ShapeDtypeStruct((SEQ, D_MODEL), F32),
        compiler_params=_cparams(("arbitrary",)),
        name="moe_combine",
    )(x1, wts_t, dest3, dest3)


def _dispatch_kernel(slot_ref, padrow_ref, x_ref, g_ref, xs_hbm, hbuf, zbuf, sem, zsem):
    i = pl.program_id(0)
    last = pl.num_programs(0) - 1
    buf = i % 2

    def tile_wait(b):
        for _ in range(2):
            pltpu.make_async_copy(hbuf.at[b], xs_hbm.at[pl.ds(0, TT_D), :], sem.at[b]).wait()

    @pl.when(i == 0)
    def _():
        zbuf[...] = jnp.zeros_like(zbuf)

        for c in range(XS_TAIL // ZROWS):
            pltpu.make_async_copy(zbuf, xs_hbm.at[pl.ds(XS_ROWS - XS_TAIL + c * ZROWS, ZROWS), :], zsem).start()
        for c in range(XS_TAIL // ZROWS):
            pltpu.make_async_copy(zbuf, xs_hbm.at[pl.ds(0, ZROWS), :], zsem).wait()

        def pad(e, c):
            dst = pl.multiple_of(padrow_ref[e], 8)
            pltpu.make_async_copy(zbuf.at[pl.ds(0, 8), :], xs_hbm.at[pl.ds(dst, 8), :], zsem).start()
            return c
        lax.fori_loop(0, N_EXPERTS, pad, 0)

        def padw(e, c):
            pltpu.make_async_copy(zbuf.at[pl.ds(0, 8), :], xs_hbm.at[pl.ds(0, 8), :], zsem).wait()
            return c
        lax.fori_loop(0, N_EXPERTS, padw, 0)

    @pl.when(i >= 2)
    def _():
        tile_wait(buf)

    x = x_ref[...]
    ms = jnp.mean(x * x, axis=-1, keepdims=True)
    hbuf[buf] = x * lax.rsqrt(ms + EPS) * g_ref[...]

    tok0 = i * TT_D

    def send(j8, c):
        for k8 in range(8):
            j = j8 * 8 + k8
            for k in range(2):
                dst = slot_ref[k * SEQ + tok0 + j]
                pltpu.make_async_copy(hbuf.at[buf, pl.ds(j, 1), :], xs_hbm.at[pl.ds(dst, 1), :],
                                      sem.at[buf]).start()
        return c
    lax.fori_loop(0, TT_D // 8, send, 0)

    @pl.when(i == last)
    def _():
        @pl.when(i >= 1)
        def _():
            tile_wait(1 - buf)
        tile_wait(buf)


def _dispatch(slot_flat, padrow, x1, g2):
    grid_spec = pltpu.PrefetchScalarGridSpec(
        num_scalar_prefetch=2,
        grid=(SEQ // TT_D,),
        in_specs=[
            pl.BlockSpec((TT_D, D_MODEL), lambda i, *_: (i, 0)),
            pl.BlockSpec((1, D_MODEL), lambda i, *_: (0, 0)),
        ],
        out_specs=pl.BlockSpec(memory_space=pl.ANY),
        scratch_shapes=[
            pltpu.VMEM((2, TT_D, D_MODEL), F32),
            pltpu.VMEM((ZROWS, D_MODEL), F32),
            pltpu.SemaphoreType.DMA((2,)),
            pltpu.SemaphoreType.DMA,
        ],
    )
    return pl.pallas_call(
        _dispatch_kernel,
        grid_spec=grid_spec,
        out_shape=jax.ShapeDtypeStruct((XS_ROWS, D_MODEL), F32),
        compiler_params=_cparams(("arbitrary",)),
        name="moe_dispatch",
    )(slot_flat, padrow, x1, g2)


def _experts_kernel(ie_ref, row0_ref, rows_ref, nitems_ref,
                    xs_hbm, wg_hbm, wu_hbm, wd_hbm, ys_hbm,
                    xg_ref, y_ref, zbuf, wg_ref, wu_ref, wd_ref, gsem, ssem, zsem, wsem):
    it = pl.program_id(0)
    nitems = nitems_ref[0]
    slot = it % 2
    half = SUB_E // 2
    nsteps = nitems * NF_E

    def weight_copies(s):
        e = ie_ref[s // NF_E]
        cols = pl.ds(pl.multiple_of((s % NF_E) * TF_E, TF_E), TF_E)
        b = s % W_RING
        return (pltpu.make_async_copy(wg_hbm.at[e, :, cols], wg_ref.at[b], wsem.at[b]),
                pltpu.make_async_copy(wu_hbm.at[e, :, cols], wu_ref.at[b], wsem.at[b]),
                pltpu.make_async_copy(wd_hbm.at[e, cols, :], wd_ref.at[b], wsem.at[b]))

    @pl.when(it == 0)
    def _():
        for s in range(W_RING - 1):
            for cp in weight_copies(s):
                cp.start()

    def padded(item):
        return pl.multiple_of(((rows_ref[item] + half - 1) // half) * half, half)

    nhalf = padded(it) // half
    nsub = nhalf // 2

    def chunk_copies(item, start):
        base = pl.multiple_of(row0_ref[item], 8)
        buf = item % 2

        def body(c, carry):
            off = pl.multiple_of(c * half, half)
            src = xs_hbm.at[pl.ds(pl.multiple_of(base + off, 8), half), :]
            dst = ys_hbm.at[pl.ds(pl.multiple_of(base + off, 8), half), :]
            if start == "fetch":
                pltpu.make_async_copy(src, xg_ref.at[buf, pl.ds(off, half), :], gsem.at[buf]).start()
            else:
                pltpu.make_async_copy(y_ref.at[buf, pl.ds(off, half), :], dst, ssem.at[buf]).start()
            return carry
        lax.fori_loop(0, padded(item) // half, body, 0)

    def wait_fetch(item):
        got = pl.ds(0, padded(item))
        buf = item % 2
        pltpu.make_async_copy(xs_hbm.at[got, :], xg_ref.at[buf, got, :], gsem.at[buf]).wait()

    def wait_store(item):
        put = pl.ds(0, padded(item))
        buf = item % 2
        pltpu.make_async_copy(y_ref.at[buf, put, :], ys_hbm.at[put, :], ssem.at[buf]).wait()

    @pl.when(it == 0)
    def _():
        zbuf[...] = jnp.zeros_like(zbuf)
        for c in range(XS_TAIL // ZROWS):
            pltpu.make_async_copy(zbuf, ys_hbm.at[pl.ds(XS_ROWS - XS_TAIL + c * ZROWS, ZROWS), :], zsem).start()
        for c in range(XS_TAIL // ZROWS):
            pltpu.make_async_copy(zbuf, ys_hbm.at[pl.ds(0, ZROWS), :], zsem).wait()
        chunk_copies(0, "fetch")

    wait_fetch(it)

    def clear(s, c):
        rows = pl.ds(pl.multiple_of(s * half, half), half)
        y_ref[slot, rows, :] = jnp.zeros((half, D_MODEL), F32)
        return c
    lax.fori_loop(0, nhalf, clear, 0)

    @pl.when(it + 1 < nitems)
    def _():
        chunk_copies(it + 1, "fetch")

    def chunk_step(f, carry):
        step = it * NF_E + f
        wslot = step % W_RING

        @pl.when(step + W_RING - 1 < nsteps)
        def _():
            for cp in weight_copies(step + W_RING - 1):
                cp.start()

        for cp in weight_copies(step):
            cp.wait()

        def block(rows):
            xb = xg_ref[slot, rows, :].astype(BF16)
            hg = jnp.dot(xb, wg_ref[wslot].astype(BF16), preferred_element_type=F32)
            hu = jnp.dot(xb, wu_ref[wslot].astype(BF16), preferred_element_type=F32)
            h = (hg * jax.nn.sigmoid(hg) * hu).astype(BF16)
            y_ref[slot, rows, :] = y_ref[slot, rows, :] + jnp.dot(h, wd_ref[wslot].astype(BF16),
                                                                   preferred_element_type=F32)

        def sub(s, c):
            block(pl.ds(pl.multiple_of(s * SUB_E, SUB_E), SUB_E))
            return c
        lax.fori_loop(0, nsub, sub, 0)

        @pl.when(nhalf % 2 == 1)
        def _():
            block(pl.ds(pl.multiple_of(nsub * SUB_E, SUB_E), half))
        return carry

    lax.fori_loop(0, NF_E, chunk_step, 0)

    @pl.when(it > 0)
    def _():
        wait_store(it - 1)

    chunk_copies(it, "store")

    @pl.when(it + 1 >= nitems)
    def _():
        wait_store(it)


def _experts(ie, row0, rows, nitems, xs, wg, wu, wd):
    grid_spec = pltpu.PrefetchScalarGridSpec(
        num_scalar_prefetch=4,
        grid=(nitems[0],),
        in_specs=[pl.BlockSpec(memory_space=pl.ANY) for _ in range(4)],
        out_specs=pl.BlockSpec(memory_space=pl.ANY),
        scratch_shapes=[
            pltpu.VMEM((2, TM_E, D_MODEL), F32),
            pltpu.VMEM((2, TM_E, D_MODEL), F32),
            pltpu.VMEM((ZROWS, D_MODEL), F32),
            pltpu.VMEM((W_RING, D_MODEL, TF_E), F32),
            pltpu.VMEM((W_RING, D_MODEL, TF_E), F32),
            pltpu.VMEM((W_RING, TF_E, D_MODEL), F32),
            pltpu.SemaphoreType.DMA((2,)),
            pltpu.SemaphoreType.DMA((2,)),
            pltpu.SemaphoreType.DMA,
            pltpu.SemaphoreType.DMA((W_RING,)),
        ],
    )
    return pl.pallas_call(
        _experts_kernel,
        grid_spec=grid_spec,
        out_shape=jax.ShapeDtypeStruct((XS_ROWS, D_MODEL), F32),
        compiler_params=_cparams(("arbitrary",)),
        name="moe_experts",
    )(ie, row0, rows, nitems, xs, wg, wu, wd)


def _gcombine_kernel(slot_ref, x_ref, w_ref, ys_hbm, o_ref, dbuf, sem):
    i = pl.program_id(0)
    n = pl.num_programs(0)
    buf = i % 2

    def fetch(tile, b):
        tok0 = tile * TT_G

        def body(j8, c):
            for k8 in range(8):
                j = j8 * 8 + k8
                for k in range(2):
                    src = slot_ref[k * SEQ + tok0 + j]
                    pltpu.make_async_copy(ys_hbm.at[pl.ds(src, 1), :], dbuf.at[b, k, pl.ds(j, 1), :],
                                          sem.at[b]).start()
            return c
        lax.fori_loop(0, TT_G // 8, body, 0)

    @pl.when(i == 0)
    def _():
        fetch(0, 0)

    @pl.when(i + 1 < n)
    def _():
        fetch(i + 1, 1 - buf)

    for k in range(2):
        pltpu.make_async_copy(ys_hbm.at[pl.ds(0, TT_G), :], dbuf.at[buf, k], sem.at[buf]).wait()

    w = w_ref[...]
    o_ref[...] = x_ref[...] + w[:, 0:1] * dbuf[buf, 0] + w[:, 1:2] * dbuf[buf, 1]


def _gcombine(slot_flat, x1, wts_t, ys):
    grid_spec = pltpu.PrefetchScalarGridSpec(
        num_scalar_prefetch=1,
        grid=(SEQ // TT_G,),
        in_specs=[
            pl.BlockSpec((TT_G, D_MODEL), lambda i, *_: (i, 0)),
            pl.BlockSpec((TT_G, 2), lambda i, *_: (i, 0)),
            pl.BlockSpec(memory_space=pl.ANY),
        ],
        out_specs=pl.BlockSpec((TT_G, D_MODEL), lambda i, *_: (i, 0)),
        scratch_shapes=[
            pltpu.VMEM((2, 2, TT_G, D_MODEL), F32),
            pltpu.SemaphoreType.DMA((2,)),
        ],
    )
    return pl.pallas_call(
        _gcombine_kernel,
        grid_spec=grid_spec,
        out_shape=jax.ShapeDtypeStruct((SEQ, D_MODEL), F32),
        compiler_params=_cparams(("arbitrary",)),
        name="moe_combine",
    )(slot_flat, x1, wts_t, ys)


def _routing_tables(eid, rank, counts_f):
    counts = counts_f[:, 0].astype(jnp.int32)
    aligned = ((counts + 7) // 8) * 8
    base = jnp.cumsum(aligned) - aligned
    tiles = (counts + TM_E - 1) // TM_E
    tcum = jnp.cumsum(tiles)
    tstart = tcum - tiles
    nitems = tcum[-1]
    ids = jnp.arange(MAX_ITEMS, dtype=jnp.int32)
    ie = jnp.clip(jnp.searchsorted(tcum, ids, side="right"), 0, N_EXPERTS - 1).astype(jnp.int32)
    live = ids < nitems
    ie = jnp.where(live, ie, ie[jnp.maximum(nitems - 1, 0)])
    jt = ids - tstart[ie]
    row0 = jnp.where(live, base[ie] + jt * TM_E, 0)
    rows = jnp.where(live, jnp.clip(counts[ie] - jt * TM_E, 0, TM_E), 0)
    eoh = eid[:, :, None] == jnp.arange(N_EXPERTS, dtype=jnp.int32)
    slot = jnp.sum(jnp.where(eoh, base, 0), axis=-1) + rank
    spare = XS_ROWS - 8 * (1 + jnp.arange(N_EXPERTS, dtype=jnp.int32))
    padrow = jnp.where(counts % 8 != 0, base + (counts // 8) * 8, spare)
    return (ie, row0.astype(jnp.int32), rows.astype(jnp.int32), nitems.reshape(1).astype(jnp.int32),
            slot.reshape(-1).astype(jnp.int32), padrow.astype(jnp.int32))


def _work_items(eid, rank, counts_f):
    counts = counts_f[:, 0].astype(jnp.int32)
    cum = jnp.cumsum(counts)
    base = cum - counts
    tiles = (counts + TM_E - 1) // TM_E
    tcum = jnp.cumsum(tiles)
    tstart = tcum - tiles
    nitems = tcum[-1]
    ids = jnp.arange(MAX_ITEMS, dtype=jnp.int32)
    ie = jnp.clip(jnp.searchsorted(tcum, ids, side="right"), 0, N_EXPERTS - 1).astype(jnp.int32)
    live = ids < nitems
    ie = jnp.where(live, ie, ie[jnp.maximum(nitems - 1, 0)])
    jt = ids - tstart[ie]
    row0 = jnp.where(live, base[ie] + jt * TM_E, 0)
    rows = jnp.where(live, jnp.clip(counts[ie] - jt * TM_E, 0, TM_E), 0)
    eoh = eid[:, :, None] == jnp.arange(N_EXPERTS, dtype=jnp.int32)
    slot = jnp.sum(jnp.where(eoh, base, 0), axis=-1) + rank
    sorted_i = jnp.zeros((SORTED_LEN,), jnp.int32).at[slot.reshape(-1)].set(
        jnp.arange(N_ASSIGN, dtype=jnp.int32))
    return ie, row0.astype(jnp.int32), rows.astype(jnp.int32), nitems.reshape(1).astype(jnp.int32), sorted_i


def kernel(x, norm1_g, w_in, q_norm_g, k_norm_g, conv_w, conv_b, conv_ln_g, conv_ln_b, rel_bias,
           w_out, norm2_g, w_router_group, b_router_group, w_router_expert, b_router_expert,
           w_gate, w_up, w_down):
    assert x.shape == (1, SEQ, D_MODEL) and w_in.shape[0] == 1
    xs = x[0]
    bias_tab = _attn_bias_tables(rel_bias)
    qg2 = jnp.tile(q_norm_g[0], 2)[None]
    kg2 = jnp.tile(k_norm_g[0], 2)[None]

    proj = _inproj(xs, norm1_g[0][None], w_in[0])
    conv_out = _conv_mixer(proj, conv_w[0], conv_b[0][None], conv_ln_g[0][None], conv_ln_b[0][None])
    attn_out = _attention(proj, qg2, kg2, bias_tab)
    x1 = _outproj(xs, conv_out, attn_out, w_out[0])

    wr_t = jnp.concatenate([
        w_router_group[0].T, jnp.zeros((8 - N_GROUPS, D_MODEL), F32),
        jnp.transpose(w_router_expert[0], (0, 2, 1)).reshape(N_EXPERTS, D_MODEL)], axis=0)
    br = jnp.concatenate([b_router_group[0], jnp.zeros((8 - N_GROUPS,), F32),
                          b_router_expert[0].reshape(-1)])
    br = jnp.broadcast_to(br[:, None], (R_ROWS, LANES))
    eid, wts, rank, counts_f = _router(x1, norm2_g[0][None], wr_t, br)

    ie, row0, rows, nitems, slot_flat, padrow = _routing_tables(eid, rank, counts_f)
    xs = _dispatch(slot_flat, padrow, x1, norm2_g[0][None])
    ys = _experts(ie, row0, rows, nitems, xs,
                  w_gate[0].reshape(N_EXPERTS, D_MODEL, D_FF),
                  w_up[0].reshape(N_EXPERTS, D_MODEL, D_FF),
                  w_down[0].reshape(N_EXPERTS, D_FF, D_MODEL))
    out = _gcombine(slot_flat, x1, wts.T, ys)
    return out[None]
```

```python
import functools
import math

import numpy as np
import jax
import jax.numpy as jnp
from jax import lax
from jax.experimental import pallas as pl
from jax.experimental.pallas import tpu as pltpu

F32 = jnp.float32
BF16 = jnp.bfloat16

D_MODEL = 2048
SEQ = 8192
N_HEADS = 16
HEAD_DIM = 64
ATTN_W = N_HEADS * HEAD_DIM
CONV_C = D_MODEL - ATTN_W
CONV_K = 31
IN_W = 2 * CONV_C + 3 * ATTN_W
PATTERNS = ((128, 1), (512, 4), (2048, 16))
QBLK = 128
NUM_BUCKETS = 32
MAX_DISTANCE = 2048
N_GROUPS = 4
E_PER_G = 8
N_EXPERTS = N_GROUPS * E_PER_G
D_FF = D_MODEL // 2
EPS = 1e-6
NEG_INF = -1e30
LOG2E = math.log2(math.e)

LANES = 128
VMEM_LIMIT = 56 * 1024 * 1024

TM_IN = 1024
TN_IN = 1024
TT_CONV = 512
HALO = 32
R_CONV = 64
R_LN = 16
ATTN_GROUP = 4
ATTN_UNROLL = 2
TM_OUT = 512
TM_R = 512
R_ROWS = 8 + N_EXPERTS
KC_R = 512
TM_E = 768
SUB_E = 256
TF_E = 256
NF_E = D_FF // TF_E
W_RING = 4
N_ASSIGN = 2 * SEQ
MAX_ITEMS = -(-N_ASSIGN // TM_E) + N_EXPERTS
SORTED_LEN = N_ASSIGN + TM_E
XS_TAIL = 256 + TM_E
XS_ROWS = N_ASSIGN + XS_TAIL
ZROWS = 128
TT_D = 512
TT_G = 512
TT_COMB = 512


def _cparams(sem, vmem=VMEM_LIMIT, flags=None):
    return pltpu.CompilerParams(dimension_semantics=sem, vmem_limit_bytes=vmem, flags=flags)


def _inproj_kernel(x_ref, g_ref, w_ref, o_ref, xn_ref):
    @pl.when(pl.program_id(1) == 0)
    def _():
        x = x_ref[...]
        ms = jnp.mean(x * x, axis=-1, keepdims=True)
        xn_ref[...] = (x * lax.rsqrt(ms + EPS) * g_ref[...]).astype(BF16)

    o_ref[...] = jnp.dot(xn_ref[...], w_ref[...].astype(BF16), preferred_element_type=F32)


def _inproj(x, g, w):
    return pl.pallas_call(
        _inproj_kernel,
        grid=(SEQ // TM_IN, IN_W // TN_IN),
        in_specs=[
            pl.BlockSpec((TM_IN, D_MODEL), lambda i, j: (i, 0)),
            pl.BlockSpec((1, D_MODEL), lambda i, j: (0, 0)),
            pl.BlockSpec((D_MODEL, TN_IN), lambda i, j: (0, j)),
        ],
        out_specs=pl.BlockSpec((TM_IN, TN_IN), lambda i, j: (i, j)),
        out_shape=jax.ShapeDtypeStruct((SEQ, IN_W), F32),
        scratch_shapes=[pltpu.VMEM((TM_IN, D_MODEL), BF16)],
        compiler_params=_cparams(("arbitrary", "arbitrary")),
        name="inproj",
    )(x, g, w)


def _conv_kernel(val_ref, gate_ref, hval_ref, hgate_ref, cw_ref, cb_ref, lg_ref, lb_ref,
                 o_ref, ubuf, zbuf, ybuf):
    i = pl.program_id(0)
    u = val_ref[...] * jax.nn.sigmoid(gate_ref[...])
    hu = hval_ref[...] * jax.nn.sigmoid(hgate_ref[...])
    hu = jnp.where(i > 0, hu, 0.0)
    for c in range(CONV_C // LANES):
        cols = slice(c * LANES, (c + 1) * LANES)
        ubuf[c, 0:HALO, :] = hu[:, cols]
        ubuf[c, HALO:HALO + TT_CONV, :] = u[:, cols]

    n_a = -(-CONV_K // 8)
    assert HALO == 8 * n_a
    for c in range(CONV_C // LANES):
        cols = slice(c * LANES, (c + 1) * LANES)

        def taps(r, carry, c=c, cols=cols):
            base = pl.multiple_of(r * R_CONV, R_CONV)
            win = ubuf[c, pl.ds(base, R_CONV + HALO), :]
            for b in range(8):
                z = None
                for a in range(n_a):
                    s = 8 * a + b
                    if s >= CONV_K:
                        continue
                    lo = HALO - 8 - 8 * a
                    t = cw_ref[CONV_K - 1 - s:CONV_K - s, cols] * win[lo:lo + R_CONV + 8, :]
                    z = t if z is None else z + t
                zbuf[b, pl.ds(0, R_CONV + 8, stride=2), :] = z
            acc = None
            for b in range(8):
                t = zbuf[b, pl.ds(2 * (8 - b), R_CONV, stride=2), :]
                acc = t if acc is None else acc + t
            ybuf[pl.ds(base, R_CONV), cols] = acc
            return carry

        lax.fori_loop(0, TT_CONV // R_CONV, taps, 0)

    def norm(r, carry):
        rows = pl.ds(pl.multiple_of(r * R_LN, R_LN), R_LN)
        acc = ybuf[rows, :] + cb_ref[...]
        mu = jnp.mean(acc, axis=-1, keepdims=True)
        xc = acc - mu
        var = jnp.mean(xc * xc, axis=-1, keepdims=True)
        y = xc * lax.rsqrt(var + EPS) * lg_ref[...] + lb_ref[...]
        o_ref[rows, :] = (y * jax.nn.sigmoid(y)).astype(BF16)
        return carry

    lax.fori_loop(0, TT_CONV // R_LN, norm, 0, unroll=4)


def _conv_mixer(proj, cw, cb, lg, lb):
    hb = TT_CONV // HALO
    return pl.pallas_call(
        _conv_kernel,
        grid=(SEQ // TT_CONV,),
        in_specs=[
            pl.BlockSpec((TT_CONV, CONV_C), lambda i: (i, 0)),
            pl.BlockSpec((TT_CONV, CONV_C), lambda i: (i, 1)),
            pl.BlockSpec((HALO, CONV_C), lambda i: (jnp.maximum(i * hb - 1, 0), 0)),
            pl.BlockSpec((HALO, CONV_C), lambda i: (jnp.maximum(i * hb - 1, 0), 1)),
            pl.BlockSpec((CONV_K, CONV_C), lambda i: (0, 0)),
            pl.BlockSpec((1, CONV_C), lambda i: (0, 0)),
            pl.BlockSpec((1, CONV_C), lambda i: (0, 0)),
            pl.BlockSpec((1, CONV_C), lambda i: (0, 0)),
        ],
        out_specs=pl.BlockSpec((TT_CONV, CONV_C), lambda i: (i, 0)),
        out_shape=jax.ShapeDtypeStruct((SEQ, CONV_C), BF16),
        scratch_shapes=[pltpu.VMEM((CONV_C // LANES, HALO + TT_CONV, LANES), F32),
                        pltpu.VMEM((8, 2 * (R_CONV + 8), LANES), F32),
                        pltpu.VMEM((TT_CONV, CONV_C), F32)],
        compiler_params=_cparams(("arbitrary",)),
        name="conv_mixer",
    )(proj, proj, proj, proj, cw, cb, lg, lb)


def _t5_bucket_np(dist):
    max_exact = NUM_BUCKETS // 2
    nf = np.maximum(dist, 1).astype(np.float32)
    large = max_exact + (np.log(nf / np.float32(max_exact)) / np.float32(math.log(MAX_DISTANCE / max_exact))
                         * np.float32(NUM_BUCKETS - max_exact)).astype(np.int32)
    large = np.minimum(large, NUM_BUCKETS - 1)
    return np.where(dist < max_exact, dist, large)


def _attn_bias_tables(rel_bias):
    period = 3 * QBLK
    diags = []
    for window, dil in PATTERNS:
        span = window // dil
        assert span <= QBLK
        bucket = _t5_bucket_np(np.arange(span + 1) * dil)
        onehot = np.eye(NUM_BUCKETS, dtype=np.float32)[bucket]
        vec = jnp.einsum("rb,bh->hr", onehot, rel_bias.astype(F32),
                         precision=lax.Precision.HIGHEST)
        diag = jnp.full((N_HEADS, period), NEG_INF, F32)
        diags.append(diag.at[:, 2 * QBLK - 1 - span:2 * QBLK].set(vec[:, ::-1]))
    diag = jnp.stack(diags) * LOG2E
    shifted = jnp.stack([jnp.roll(diag, b, axis=-1) for b in range(8)], axis=-2)
    return shifted.reshape(len(PATTERNS), N_HEADS // 2, 2, 8, period)


def _attn_kernel(q_ref, k_ref, v_ref, qg_ref, kg_ref, diag_ref, o_ref,
                 qn_ref, kn_ref, acc_ref, m_ref, l_ref, bias_s):
    lane = lax.broadcasted_iota(jnp.int32, (QBLK, LANES), 1)
    head_a = lane < HEAD_DIM
    hist_keys = lax.broadcasted_iota(jnp.int32, (2 * QBLK, 2 * QBLK), 1) < QBLK
    ri = lax.broadcasted_iota(jnp.int32, (LANES, LANES), 0) // HEAD_DIM
    ci = lax.broadcasted_iota(jnp.int32, (LANES, LANES), 1) // HEAD_DIM
    seg = (ri == ci).astype(BF16)

    def head_rms(x, g):
        sq = x * x
        hi = sq.astype(BF16)
        lo = (sq - hi.astype(F32)).astype(BF16)
        ss = (jnp.dot(hi, seg, preferred_element_type=F32)
              + jnp.dot(lo, seg, preferred_element_type=F32))
        return x * lax.rsqrt(ss * (1.0 / HEAD_DIM) + EPS) * g

    NCH = 512

    def norm_body(c, carry):
        rows = pl.ds(pl.multiple_of(c * NCH, NCH), NCH)
        qn_ref[rows, :] = head_rms(q_ref[rows, :], qg_ref[...]) * (LOG2E / math.sqrt(HEAD_DIM))
        kn_ref[rows, :] = head_rms(k_ref[rows, :], kg_ref[...])
        return carry

    lax.fori_loop(0, SEQ // NCH, norm_body, 0)

    for p in range(len(PATTERNS)):
        for h in range(2):
            offs = diag_ref[p, h]
            for a in range(QBLK // 8):
                lo = QBLK - 1 - 8 * a
                bias_s[p, h * QBLK + 8 * a:h * QBLK + 8 * a + 8, :] = offs[:, lo:lo + 2 * QBLK]

    order = sorted(range(len(PATTERNS)), key=lambda i: -PATTERNS[i][1])
    assert PATTERNS[order[-1]][1] == 1
    for step, p in enumerate(order):
        dil = PATTERNS[p][1]
        nb = SEQ // (dil * QBLK)
        is_first = step == 0
        is_last = step == len(order) - 1

        def rows(start, dil=dil):
            if dil == 1:
                return pl.ds(start, QBLK)
            return pl.ds(start, QBLK, stride=dil)

        def unit(cur, k_prev, v_prev, first, p=p, rows=rows, is_first=is_first, is_last=is_last):
            q = qn_ref[rows(cur), :]
            q2 = jnp.concatenate([jnp.where(head_a, q, 0.0), jnp.where(head_a, 0.0, q)],
                                 axis=0).astype(BF16)
            k_cur = kn_ref[rows(cur), :].astype(BF16)
            v_cur = v_ref[rows(cur), :].astype(BF16)
            k2 = jnp.concatenate([k_prev, k_cur], axis=0)
            v2 = jnp.concatenate([v_prev, v_cur], axis=0)
            s = lax.dot_general(q2, k2, (((1,), (1,)), ((), ())), preferred_element_type=F32)
            s = s + bias_s[p]
            if first is not None:
                s = jnp.where(hist_keys, jnp.where(first, NEG_INF * LOG2E, s), s)
            m = jnp.max(s, axis=-1, keepdims=True)
            e = jnp.exp2(s - m)
            l = jnp.sum(e, axis=-1, keepdims=True)
            pv = jnp.dot(e.astype(BF16), v2, preferred_element_type=F32)
            o_new = jnp.where(head_a, pv[:QBLK], pv[QBLK:])
            m_new = jnp.where(head_a, m[:QBLK], m[QBLK:])
            l_new = jnp.where(head_a, l[:QBLK], l[QBLK:])
            if is_first:
                acc_ref[rows(cur), :] = o_new
                m_ref[rows(cur), :] = m_new
                l_ref[rows(cur), :] = l_new
            else:
                m_old = m_ref[rows(cur), :]
                m_tot = jnp.maximum(m_old, m_new)
                a = jnp.exp2(m_old - m_tot)
                b = jnp.exp2(m_new - m_tot)
                acc = acc_ref[rows(cur), :] * a + o_new * b
                den = l_ref[rows(cur), :] * a + l_new * b
                if is_last:
                    o_ref[pl.ds(pl.multiple_of(cur, QBLK), QBLK), :] = (acc / den).astype(BF16)
                else:
                    acc_ref[rows(cur), :] = acc
                    l_ref[rows(cur), :] = den
                    m_ref[rows(cur), :] = m_tot
            return k_cur, v_cur

        per = nb // ATTN_GROUP
        assert per * ATTN_GROUP == nb

        def group(g, carry, dil=dil, per=per, rows=rows, unit=unit):
            r = g // per
            n0 = (g - r * per) * ATTN_GROUP
            hist = jnp.maximum(n0 - 1, 0) * (QBLK * dil) + r
            k_prev = kn_ref[rows(hist), :].astype(BF16)
            v_prev = v_ref[rows(hist), :].astype(BF16)
            for i in range(ATTN_GROUP):
                first = (n0 == 0) if i == 0 else None
                k_prev, v_prev = unit((n0 + i) * (QBLK * dil) + r, k_prev, v_prev, first)
            return carry

        lax.fori_loop(0, SEQ // (QBLK * ATTN_GROUP), group, 0, unroll=ATTN_UNROLL)


def _attention(proj, qg2, kg2, bias_tab):
    qoff = 2 * CONV_C // LANES
    koff = qoff + ATTN_W // LANES
    voff = koff + ATTN_W // LANES
    return pl.pallas_call(
        _attn_kernel,
        grid=(N_HEADS // 2,),
        in_specs=[
            pl.BlockSpec((SEQ, LANES), lambda h: (0, qoff + h)),
            pl.BlockSpec((SEQ, LANES), lambda h: (0, koff + h)),
            pl.BlockSpec((SEQ, LANES), lambda h: (0, voff + h)),
            pl.BlockSpec((1, LANES), lambda h: (0, 0)),
            pl.BlockSpec((1, LANES), lambda h: (0, 0)),
            pl.BlockSpec((len(PATTERNS), None, 2, 8, 3 * QBLK), lambda h: (0, h, 0, 0, 0)),
        ],
        out_specs=pl.BlockSpec((SEQ, LANES), lambda h: (0, h)),
        out_shape=jax.ShapeDtypeStruct((SEQ, ATTN_W), BF16),
        scratch_shapes=[pltpu.VMEM((SEQ, LANES), F32) for _ in range(5)]
        + [pltpu.VMEM((len(PATTERNS), 2 * QBLK, 2 * QBLK), F32)],
        compiler_params=_cparams(("arbitrary",)),
        name="dilated_attn",
    )(proj, proj, proj, qg2, kg2, bias_tab)


def _outproj_kernel(x_ref, c_ref, a_ref, wc_ref, wa_ref, o_ref):
    o_ref[...] = (x_ref[...]
                  + jnp.dot(c_ref[...], wc_ref[...].astype(BF16), preferred_element_type=F32)
                  + jnp.dot(a_ref[...], wa_ref[...].astype(BF16), preferred_element_type=F32))


def _outproj(x, conv_out, attn_out, w_out):
    return pl.pallas_call(
        _outproj_kernel,
        grid=(SEQ // TM_OUT,),
        in_specs=[
            pl.BlockSpec((TM_OUT, D_MODEL), lambda i: (i, 0)),
            pl.BlockSpec((TM_OUT, CONV_C), lambda i: (i, 0)),
            pl.BlockSpec((TM_OUT, ATTN_W), lambda i: (i, 0)),
            pl.BlockSpec((CONV_C, D_MODEL), lambda i: (0, 0), pipeline_mode=pl.Buffered(1)),
            pl.BlockSpec((ATTN_W, D_MODEL), lambda i: (1, 0), pipeline_mode=pl.Buffered(1)),
        ],
        out_specs=pl.BlockSpec((TM_OUT, D_MODEL), lambda i: (i, 0)),
        out_shape=jax.ShapeDtypeStruct((SEQ, D_MODEL), F32),
        compiler_params=_cparams(("arbitrary",)),
        name="outproj",
    )(x, conv_out, attn_out, w_out, w_out)


def _split2(a):
    a1 = a.astype(BF16)
    a2 = (a - a1.astype(F32)).astype(BF16)
    return a1, a2


def _router_kernel(x_ref, g_ref, wr_ref, br_ref, eid_ref, wts_ref, rank_ref, cnt_ref):
    i = pl.program_id(0)

    @pl.when(i == 0)
    def _():
        cnt_ref[...] = jnp.zeros_like(cnt_ref)

    x = x_ref[...]
    inv = lax.rsqrt(jnp.mean(x * x, axis=-1, keepdims=True) + EPS)
    dn = (((1,), (1,)), ((), ()))
    lt = None
    for c in range(D_MODEL // KC_R):
        cols = slice(c * KC_R, (c + 1) * KC_R)
        h1, h2 = _split2(x_ref[:, cols] * inv * g_ref[:, cols])
        w1, w2 = _split2(wr_ref[:, cols])
        for wa, ha in ((w1, h1), (w1, h2), (w2, h1)):
            t = lax.dot_general(wa, ha, dn, preferred_element_type=F32)
            lt = t if lt is None else lt + t
    lt = lt + br_ref[:, 0:1]

    row8 = lax.broadcasted_iota(jnp.int32, (8, TM_R), 0)
    gl = jnp.where(row8 < N_GROUPS, lt[0:8], -jnp.inf)
    gmax = jnp.max(gl, axis=0, keepdims=True)
    gidx = jnp.min(jnp.where(gl == gmax, row8, 8), axis=0, keepdims=True)
    gw = 1.0 / jnp.sum(jnp.exp(gl - gmax), axis=0, keepdims=True)

    esel = lt[8:16]
    for g in range(1, N_GROUPS):
        esel = jnp.where(gidx == g, lt[8 + 8 * g:16 + 8 * g], esel)
    v1 = jnp.max(esel, axis=0, keepdims=True)
    i1 = jnp.min(jnp.where(esel == v1, row8, 8), axis=0, keepdims=True)
    rest = jnp.where(row8 == i1, -jnp.inf, esel)
    v2 = jnp.max(rest, axis=0, keepdims=True)
    i2 = jnp.min(jnp.where(rest == v2, row8, 8), axis=0, keepdims=True)
    e21 = jnp.exp(v2 - v1)
    den = 1.0 + e21
    e1 = gidx * E_PER_G + i1
    e2 = gidx * E_PER_G + i2
    eid_ref[0:1, :] = e1
    eid_ref[1:2, :] = e2
    wts_ref[0:1, :] = gw * (1.0 / den)
    wts_ref[1:2, :] = gw * (e21 / den)

    erow = lax.broadcasted_iota(jnp.int32, (N_EXPERTS, TM_R), 0)
    oh1 = erow == e1
    oh2 = erow == e2
    member = jnp.where(oh1 | oh2, 1.0, 0.0)
    ti = lax.broadcasted_iota(jnp.int32, (TM_R, TM_R), 0)
    tj = lax.broadcasted_iota(jnp.int32, (TM_R, TM_R), 1)
    upper = jnp.where(ti < tj, 1.0, 0.0).astype(BF16)
    before = jnp.dot(member.astype(BF16), upper, preferred_element_type=F32)
    pos = before + cnt_ref[:, 0:1]
    rank_ref[0:1, :] = jnp.sum(jnp.where(oh1, pos, 0.0), axis=0, keepdims=True).astype(jnp.int32)
    rank_ref[1:2, :] = jnp.sum(jnp.where(oh2, pos, 0.0), axis=0, keepdims=True).astype(jnp.int32)
    cnt_ref[...] = cnt_ref[...] + jnp.sum(member, axis=1, keepdims=True)


def _router(x1, g2, wr_t, br):
    return pl.pallas_call(
        _router_kernel,
        grid=(SEQ // TM_R,),
        in_specs=[
            pl.BlockSpec((TM_R, D_MODEL), lambda i: (i, 0)),
            pl.BlockSpec((1, D_MODEL), lambda i: (0, 0)),
            pl.BlockSpec((R_ROWS, D_MODEL), lambda i: (0, 0)),
            pl.BlockSpec((R_ROWS, LANES), lambda i: (0, 0)),
        ],
        out_specs=[
            pl.BlockSpec((2, TM_R), lambda i: (0, i)),
            pl.BlockSpec((2, TM_R), lambda i: (0, i)),
            pl.BlockSpec((2, TM_R), lambda i: (0, i)),
            pl.BlockSpec((N_EXPERTS, LANES), lambda i: (0, 0)),
        ],
        out_shape=[
            jax.ShapeDtypeStruct((2, SEQ), jnp.int32),
            jax.ShapeDtypeStruct((2, SEQ), F32),
            jax.ShapeDtypeStruct((2, SEQ), jnp.int32),
            jax.ShapeDtypeStruct((N_EXPERTS, LANES), F32),
        ],
        compiler_params=_cparams(("arbitrary",)),
        name="router",
    )(x1, g2, wr_t, br)


def _moe_kernel(ie_ref, row0_ref, rows_ref, nitems_ref, sorted_ref,
                x_hbm, g_ref, wg_ref, wu_ref, wd_ref, dest_hbm,
                xg_ref, xb_ref, y_ref, gsem, ssem):
    it = pl.program_id(0)
    f = pl.program_id(1)
    nrows = rows_ref[it]
    nitems = nitems_ref[0]
    slot = it % 2
    half = SUB_E // 2

    def padded(item):
        return pl.multiple_of(((rows_ref[item] + half - 1) // half) * half, half)

    nhalf = padded(it) // half
    nsub = nhalf // 2

    def start_gather(item):
        base = row0_ref[item]
        buf = item % 2

        def start(j8, c):
            for k in range(8):
                j = j8 * 8 + k
                tok = sorted_ref[base + j] & (SEQ - 1)
                pltpu.make_async_copy(x_hbm.at[pl.ds(tok, 1), :], xg_ref.at[buf, pl.ds(j, 1), :],
                                      gsem.at[buf]).start()
            return c
        lax.fori_loop(0, padded(item) // 8, start, 0)

    def wait_gather(item):
        got = pl.ds(0, padded(item))
        buf = item % 2
        pltpu.make_async_copy(x_hbm.at[got, :], xg_ref.at[buf, got, :], gsem.at[buf]).wait()

    def scatter_copy(item, j):
        dst = sorted_ref[row0_ref[item] + j]
        buf = item % 2
        return pltpu.make_async_copy(y_ref.at[buf, pl.ds(j, 1), :], dest_hbm.at[pl.ds(dst, 1), :],
                                     ssem.at[buf])

    def start_scatter(item):
        n = rows_ref[item]

        def start8(j8, c):
            for k in range(8):
                scatter_copy(item, j8 * 8 + k).start()
            return c
        lax.fori_loop(0, n // 8, start8, 0)

        def start(j, c):
            scatter_copy(item, j).start()
            return c
        lax.fori_loop((n // 8) * 8, n, start, 0)

    def wait_scatter(item):
        n = rows_ref[item]
        whole = pl.multiple_of((n // 8) * 8, 8)
        buf = item % 2

        @pl.when(whole > 0)
        def _():
            sent = pl.ds(0, whole)
            pltpu.make_async_copy(y_ref.at[buf, sent, :], dest_hbm.at[sent, :], ssem.at[buf]).wait()

        def wait(j, c):
            scatter_copy(item, j).wait()
            return c
        lax.fori_loop(whole, n, wait, 0)

    @pl.when((f == 0) & (nrows > 0))
    def _():
        @pl.when(it == 0)
        def _():
            start_gather(0)

        wait_gather(it)

        def norm(s, c):
            rows = pl.ds(pl.multiple_of(s * half, half), half)
            x = xg_ref[slot, rows, :]
            ms = jnp.mean(x * x, axis=-1, keepdims=True)
            xb_ref[rows, :] = (x * lax.rsqrt(ms + EPS) * g_ref[...]).astype(BF16)
            y_ref[slot, rows, :] = jnp.zeros((half, D_MODEL), F32)
            return c
        lax.fori_loop(0, nhalf, norm, 0)

        @pl.when(it + 1 < nitems)
        def _():
            start_gather(it + 1)

    @pl.when(nrows > 0)
    def _():
        wg = wg_ref[...].astype(BF16)
        wu = wu_ref[...].astype(BF16)
        wd = wd_ref[...].astype(BF16)

        def block(rows):
            xb = xb_ref[rows, :]
            hg = jnp.dot(xb, wg, preferred_element_type=F32)
            hu = jnp.dot(xb, wu, preferred_element_type=F32)
            h = (hg * jax.nn.sigmoid(hg) * hu).astype(BF16)
            y_ref[slot, rows, :] = y_ref[slot, rows, :] + jnp.dot(h, wd, preferred_element_type=F32)

        def sub(s, c):
            block(pl.ds(pl.multiple_of(s * SUB_E, SUB_E), SUB_E))
            return c
        lax.fori_loop(0, nsub, sub, 0)

        @pl.when(nhalf % 2 == 1)
        def _():
            block(pl.ds(pl.multiple_of(nsub * SUB_E, SUB_E), half))

    @pl.when((f == NF_E - 1) & (nrows > 0))
    def _():
        @pl.when(it > 0)
        def _():
            wait_scatter(it - 1)

        start_scatter(it)

        @pl.when(it + 1 >= nitems)
        def _():
            wait_scatter(it)


def _moe(ie, row0, rows, nitems, sorted_i, x1, g2, wg, wu, wd):
    def wmap_cols(i, f, ie_ref, row0_ref, rows_ref, n_ref, s_ref):
        return (ie_ref[i], 0, jnp.where(i < n_ref[0], f, NF_E - 1))

    def wmap_rows(i, f, ie_ref, row0_ref, rows_ref, n_ref, s_ref):
        return (ie_ref[i], jnp.where(i < n_ref[0], f, NF_E - 1), 0)

    grid_spec = pltpu.PrefetchScalarGridSpec(
        num_scalar_prefetch=5,
        grid=(nitems[0], NF_E),
        in_specs=[
            pl.BlockSpec(memory_space=pl.ANY),
            pl.BlockSpec((1, D_MODEL), lambda i, f, *_: (0, 0)),
            pl.BlockSpec((None, D_MODEL, TF_E), wmap_cols),
            pl.BlockSpec((None, D_MODEL, TF_E), wmap_cols),
            pl.BlockSpec((None, TF_E, D_MODEL), wmap_rows),
        ],
        out_specs=pl.BlockSpec(memory_space=pl.ANY),
        scratch_shapes=[
            pltpu.VMEM((2, TM_E, D_MODEL), F32),
            pltpu.VMEM((TM_E, D_MODEL), BF16),
            pltpu.VMEM((2, TM_E, D_MODEL), F32),
            pltpu.SemaphoreType.DMA((2,)),
            pltpu.SemaphoreType.DMA((2,)),
        ],
    )
    return pl.pallas_call(
        _moe_kernel,
        grid_spec=grid_spec,
        out_shape=jax.ShapeDtypeStruct((N_ASSIGN, D_MODEL), F32),
        compiler_params=_cparams(("arbitrary", "arbitrary")),
        name="moe_experts",
    )(ie, row0, rows, nitems, sorted_i, x1, g2, wg, wu, wd)


def _combine_kernel(x_ref, w_ref, d0_ref, d1_ref, o_ref):
    w = w_ref[...]
    o_ref[...] = x_ref[...] + w[:, 0:1] * d0_ref[...] + w[:, 1:2] * d1_ref[...]


def _combine(x1, wts_t, dest):
    dest3 = dest.reshape(2, SEQ, D_MODEL)
    return pl.pallas_call(
        _combine_kernel,
        grid=(SEQ // TT_COMB,),
        in_specs=[
            pl.BlockSpec((TT_COMB, D_MODEL), lambda i: (i, 0)),
            pl.BlockSpec((TT_COMB, 2), lambda i: (i, 0)),
            pl.BlockSpec((None, TT_COMB, D_MODEL), lambda i: (0, i, 0)),
            pl.BlockSpec((None, TT_COMB, D_MODEL), lambda i: (1, i, 0)),
        ],
        out_specs=pl.BlockSpec((TT_COMB, D_MODEL), lambda i: (i, 0)),
        out_shape=jax.ShapeDtypeStruct((SEQ, D_MODEL), F32),
        compiler_params=_cparams(("arbitrary",)),
        name="moe_combine",
    )(x1, wts_t, dest3, dest3)


def _dispatch_kernel(slot_ref, padrow_ref, x_ref, g_ref, xs_hbm, hbuf, zbuf, sem, zsem):
    i = pl.program_id(0)
    last = pl.num_programs(0) - 1
    buf = i % 2

    def tile_wait(b):
        for _ in range(2):
            pltpu.make_async_copy(hbuf.at[b], xs_hbm.at[pl.ds(0, TT_D), :], sem.at[b]).wait()

    @pl.when(i == 0)
    def _():
        zbuf[...] = jnp.zeros_like(zbuf)

        for c in range(XS_TAIL // ZROWS):
            pltpu.make_async_copy(zbuf, xs_hbm.at[pl.ds(XS_ROWS - XS_TAIL + c * ZROWS, ZROWS), :], zsem).start()
        for c in range(XS_TAIL // ZROWS):
            pltpu.make_async_copy(zbuf, xs_hbm.at[pl.ds(0, ZROWS), :], zsem).wait()

        def pad(e, c):
            dst = pl.multiple_of(padrow_ref[e], 8)
            pltpu.make_async_copy(zbuf.at[pl.ds(0, 8), :], xs_hbm.at[pl.ds(dst, 8), :], zsem).start()
            return c
        lax.fori_loop(0, N_EXPERTS, pad, 0)

        def padw(e, c):
            pltpu.make_async_copy(zbuf.at[pl.ds(0, 8), :], xs_hbm.at[pl.ds(0, 8), :], zsem).wait()
            return c
        lax.fori_loop(0, N_EXPERTS, padw, 0)

    @pl.when(i >= 2)
    def _():
        tile_wait(buf)

    x = x_ref[...]
    ms = jnp.mean(x * x, axis=-1, keepdims=True)
    hbuf[buf] = x * lax.rsqrt(ms + EPS) * g_ref[...]

    tok0 = i * TT_D

    def send(j8, c):
        for k8 in range(8):
            j = j8 * 8 + k8
            for k in range(2):
                dst = slot_ref[k * SEQ + tok0 + j]
                pltpu.make_async_copy(hbuf.at[buf, pl.ds(j, 1), :], xs_hbm.at[pl.ds(dst, 1), :],
                                      sem.at[buf]).start()
        return c
    lax.fori_loop(0, TT_D // 8, send, 0)

    @pl.when(i == last)
    def _():
        @pl.when(i >= 1)
        def _():
            tile_wait(1 - buf)
        tile_wait(buf)


def _dispatch(slot_flat, padrow, x1, g2):
    grid_spec = pltpu.PrefetchScalarGridSpec(
        num_scalar_prefetch=2,
        grid=(SEQ // TT_D,),
        in_specs=[
            pl.BlockSpec((TT_D, D_MODEL), lambda i, *_: (i, 0)),
            pl.BlockSpec((1, D_MODEL), lambda i, *_: (0, 0)),
        ],
        out_specs=pl.BlockSpec(memory_space=pl.ANY),
        scratch_shapes=[
            pltpu.VMEM((2, TT_D, D_MODEL), F32),
            pltpu.VMEM((ZROWS, D_MODEL), F32),
            pltpu.SemaphoreType.DMA((2,)),
            pltpu.SemaphoreType.DMA,
        ],
    )
    return pl.pallas_call(
        _dispatch_kernel,
        grid_spec=grid_spec,
        out_shape=jax.ShapeDtypeStruct((XS_ROWS, D_MODEL), F32),
        compiler_params=_cparams(("arbitrary",)),
        name="moe_dispatch",
    )(slot_flat, padrow, x1, g2)


def _experts_kernel(ie_ref, row0_ref, rows_ref, nitems_ref,
                    xs_hbm, wg_hbm, wu_hbm, wd_hbm, ys_hbm,
                    xg_ref, y_ref, zbuf, wg_ref, wu_ref, wd_ref, gsem, ssem, zsem, wsem):
    it = pl.program_id(0)
    nitems = nitems_ref[0]
    slot = it % 2
    half = SUB_E // 2
    nsteps = nitems * NF_E

    def weight_copies(s):
        e = ie_ref[s // NF_E]
        cols = pl.ds(pl.multiple_of((s % NF_E) * TF_E, TF_E), TF_E)
        b = s % W_RING
        return (pltpu.make_async_copy(wg_hbm.at[e, :, cols], wg_ref.at[b], wsem.at[b]),
                pltpu.make_async_copy(wu_hbm.at[e, :, cols], wu_ref.at[b], wsem.at[b]),
                pltpu.make_async_copy(wd_hbm.at[e, cols, :], wd_ref.at[b], wsem.at[b]))

    @pl.when(it == 0)
    def _():
        for s in range(W_RING - 1):
            for cp in weight_copies(s):
                cp.start()

    def padded(item):
        return pl.multiple_of(((rows_ref[item] + half - 1) // half) * half, half)

    nhalf = padded(it) // half
    nsub = nhalf // 2

    def chunk_copies(item, start):
        base = pl.multiple_of(row0_ref[item], 8)
        buf = item % 2

        def body(c, carry):
            off = pl.multiple_of(c * half, half)
            src = xs_hbm.at[pl.ds(pl.multiple_of(base + off, 8), half), :]
            dst = ys_hbm.at[pl.ds(pl.multiple_of(base + off, 8), half), :]
            if start == "fetch":
                pltpu.make_async_copy(src, xg_ref.at[buf, pl.ds(off, half), :], gsem.at[buf]).start()
            else:
                pltpu.make_async_copy(y_ref.at[buf, pl.ds(off, half), :], dst, ssem.at[buf]).start()
            return carry
        lax.fori_loop(0, padded(item) // half, body, 0)

    def wait_fetch(item):
        got = pl.ds(0, padded(item))
        buf = item % 2
        pltpu.make_async_copy(xs_hbm.at[got, :], xg_ref.at[buf, got, :], gsem.at[buf]).wait()

    def wait_store(item):
        put = pl.ds(0, padded(item))
        buf = item % 2
        pltpu.make_async_copy(y_ref.at[buf, put, :], ys_hbm.at[put, :], ssem.at[buf]).wait()

    @pl.when(it == 0)
    def _():
        zbuf[...] = jnp.zeros_like(zbuf)
        for c in range(XS_TAIL // ZROWS):
            pltpu.make_async_copy(zbuf, ys_hbm.at[pl.ds(XS_ROWS - XS_TAIL + c * ZROWS, ZROWS), :], zsem).start()
        for c in range(XS_TAIL // ZROWS):
            pltpu.make_async_copy(zbuf, ys_hbm.at[pl.ds(0, ZROWS), :], zsem).wait()
        chunk_copies(0, "fetch")

    wait_fetch(it)

    def clear(s, c):
        rows = pl.ds(pl.multiple_of(s * half, half), half)
        y_ref[slot, rows, :] = jnp.zeros((half, D_MODEL), F32)
        return c
    lax.fori_loop(0, nhalf, clear, 0)

    @pl.when(it + 1 < nitems)
    def _():
        chunk_copies(it + 1, "fetch")

    def chunk_step(f, carry):
        step = it * NF_E + f
        wslot = step % W_RING

        @pl.when(step + W_RING - 1 < nsteps)
        def _():
            for cp in weight_copies(step + W_RING - 1):
                cp.start()

        for cp in weight_copies(step):
            cp.wait()

        def block(rows):
            xb = xg_ref[slot, rows, :].astype(BF16)
            hg = jnp.dot(xb, wg_ref[wslot].astype(BF16), preferred_element_type=F32)
            hu = jnp.dot(xb, wu_ref[wslot].astype(BF16), preferred_element_type=F32)
            h = (hg * jax.nn.sigmoid(hg) * hu).astype(BF16)
            y_ref[slot, rows, :] = y_ref[slot, rows, :] + jnp.dot(h, wd_ref[wslot].astype(BF16),
                                                                   preferred_element_type=F32)

        def sub(s, c):
            block(pl.ds(pl.multiple_of(s * SUB_E, SUB_E), SUB_E))
            return c
        lax.fori_loop(0, nsub, sub, 0)

        @pl.when(nhalf % 2 == 1)
        def _():
            block(pl.ds(pl.multiple_of(nsub * SUB_E, SUB_E), half))
        return carry

    lax.fori_loop(0, NF_E, chunk_step, 0)

    @pl.when(it > 0)
    def _():
        wait_store(it - 1)

    chunk_copies(it, "store")

    @pl.when(it + 1 >= nitems)
    def _():
        wait_store(it)


def _experts(ie, row0, rows, nitems, xs, wg, wu, wd):
    grid_spec = pltpu.PrefetchScalarGridSpec(
        num_scalar_prefetch=4,
        grid=(nitems[0],),
        in_specs=[pl.BlockSpec(memory_space=pl.ANY) for _ in range(4)],
        out_specs=pl.BlockSpec(memory_space=pl.ANY),
        scratch_shapes=[
            pltpu.VMEM((2, TM_E, D_MODEL), F32),
            pltpu.VMEM((2, TM_E, D_MODEL), F32),
            pltpu.VMEM((ZROWS, D_MODEL), F32),
            pltpu.VMEM((W_RING, D_MODEL, TF_E), F32),
            pltpu.VMEM((W_RING, D_MODEL, TF_E), F32),
            pltpu.VMEM((W_RING, TF_E, D_MODEL), F32),
            pltpu.SemaphoreType.DMA((2,)),
            pltpu.SemaphoreType.DMA((2,)),
            pltpu.SemaphoreType.DMA,
            pltpu.SemaphoreType.DMA((W_RING,)),
        ],
    )
    return pl.pallas_call(
        _experts_kernel,
        grid_spec=grid_spec,
        out_shape=jax.ShapeDtypeStruct((XS_ROWS, D_MODEL), F32),
        compiler_params=_cparams(("arbitrary",)),
        name="moe_experts",
    )(ie, row0, rows, nitems, xs, wg, wu, wd)


def _gcombine_kernel(slot_ref, x_ref, w_ref, ys_hbm, o_ref, dbuf, sem):
    i = pl.program_id(0)
    n = pl.num_programs(0)
    buf = i % 2

    def fetch(tile, b):
        tok0 = tile * TT_G

        def body(j8, c):
            for k8 in range(8):
                j = j8 * 8 + k8
                for k in range(2):
                    src = slot_ref[k * SEQ + tok0 + j]
                    pltpu.make_async_copy(ys_hbm.at[pl.ds(src, 1), :], dbuf.at[b, k, pl.ds(j, 1), :],
                                          sem.at[b]).start()
            return c
        lax.fori_loop(0, TT_G // 8, body, 0)

    @pl.when(i == 0)
    def _():
        fetch(0, 0)

    @pl.when(i + 1 < n)
    def _():
        fetch(i + 1, 1 - buf)

    for k in range(2):
        pltpu.make_async_copy(ys_hbm.at[pl.ds(0, TT_G), :], dbuf.at[buf, k], sem.at[buf]).wait()

    w = w_ref[...]
    o_ref[...] = x_ref[...] + w[:, 0:1] * dbuf[buf, 0] + w[:, 1:2] * dbuf[buf, 1]


def _gcombine(slot_flat, x1, wts_t, ys):
    grid_spec = pltpu.PrefetchScalarGridSpec(
        num_scalar_prefetch=1,
        grid=(SEQ // TT_G,),
        in_specs=[
            pl.BlockSpec((TT_G, D_MODEL), lambda i, *_: (i, 0)),
            pl.BlockSpec((TT_G, 2), lambda i, *_: (i, 0)),
            pl.BlockSpec(memory_space=pl.ANY),
        ],
        out_specs=pl.BlockSpec((TT_G, D_MODEL), lambda i, *_: (i, 0)),
        scratch_shapes=[
            pltpu.VMEM((2, 2, TT_G, D_MODEL), F32),
            pltpu.SemaphoreType.DMA((2,)),
        ],
    )
    return pl.pallas_call(
        _gcombine_kernel,
        grid_spec=grid_spec,
        out_shape=jax.ShapeDtypeStruct((SEQ, D_MODEL), F32),
        compiler_params=_cparams(("arbitrary",)),
        name="moe_combine",
    )(slot_flat, x1, wts_t, ys)


def _routing_tables(eid, rank, counts_f):
    counts = counts_f[:, 0].astype(jnp.int32)
    aligned = ((counts + 7) // 8) * 8
    base = jnp.cumsum(aligned) - aligned
    tiles = (counts + TM_E - 1) // TM_E
    tcum = jnp.cumsum(tiles)
    tstart = tcum - tiles
    nitems = tcum[-1]
    ids = jnp.arange(MAX_ITEMS, dtype=jnp.int32)
    ie = jnp.clip(jnp.searchsorted(tcum, ids, side="right"), 0, N_EXPERTS - 1).astype(jnp.int32)
    live = ids < nitems
    ie = jnp.where(live, ie, ie[jnp.maximum(nitems - 1, 0)])
    jt = ids - tstart[ie]
    row0 = jnp.where(live, base[ie] + jt * TM_E, 0)
    rows = jnp.where(live, jnp.clip(counts[ie] - jt * TM_E, 0, TM_E), 0)
    eoh = eid[:, :, None] == jnp.arange(N_EXPERTS, dtype=jnp.int32)
    slot = jnp.sum(jnp.where(eoh, base, 0), axis=-1) + rank
    spare = XS_ROWS - 8 * (1 + jnp.arange(N_EXPERTS, dtype=jnp.int32))
    padrow = jnp.where(counts % 8 != 0, base + (counts // 8) * 8, spare)
    return (ie, row0.astype(jnp.int32), rows.astype(jnp.int32), nitems.reshape(1).astype(jnp.int32),
            slot.reshape(-1).astype(jnp.int32), padrow.astype(jnp.int32))


def _work_items(eid, rank, counts_f):
    counts = counts_f[:, 0].astype(jnp.int32)
    cum = jnp.cumsum(counts)
    base = cum - counts
    tiles = (counts + TM_E - 1) // TM_E
    tcum = jnp.cumsum(tiles)
    tstart = tcum - tiles
    nitems = tcum[-1]
    ids = jnp.arange(MAX_ITEMS, dtype=jnp.int32)
    ie = jnp.clip(jnp.searchsorted(tcum, ids, side="right"), 0, N_EXPERTS - 1).astype(jnp.int32)
    live = ids < nitems
    ie = jnp.where(live, ie, ie[jnp.maximum(nitems - 1, 0)])
    jt = ids - tstart[ie]
    row0 = jnp.where(live, base[ie] + jt * TM_E, 0)
    rows = jnp.where(live, jnp.clip(counts[ie] - jt * TM_E, 0, TM_E), 0)
    eoh = eid[:, :, None] == jnp.arange(N_EXPERTS, dtype=jnp.int32)
    slot = jnp.sum(jnp.where(eoh, base, 0), axis=-1) + rank
    sorted_i = jnp.zeros((SORTED_LEN,), jnp.int32).at[slot.reshape(-1)].set(
        jnp.arange(N_ASSIGN, dtype=jnp.int32))
    return ie, row0.astype(jnp.int32), rows.astype(jnp.int32), nitems.reshape(1).astype(jnp.int32), sorted_i


def kernel(x, norm1_g, w_in, q_norm_g, k_norm_g, conv_w, conv_b, conv_ln_g, conv_ln_b, rel_bias,
           w_out, norm2_g, w_router_group, b_router_group, w_router_expert, b_router_expert,
           w_gate, w_up, w_down):
    assert x.shape == (1, SEQ, D_MODEL) and w_in.shape[0] == 1
    xs = x[0]
    bias_tab = _attn_bias_tables(rel_bias)
    qg2 = jnp.tile(q_norm_g[0], 2)[None]
    kg2 = jnp.tile(k_norm_g[0], 2)[None]

    proj = _inproj(xs, norm1_g[0][None], w_in[0])
    conv_out = _conv_mixer(proj, conv_w[0], conv_b[0][None], conv_ln_g[0][None], conv_ln_b[0][None])
    attn_out = _attention(proj, qg2, kg2, bias_tab)
    x1 = _outproj(xs, conv_out, attn_out, w_out[0])

    wr_t = jnp.concatenate([
        w_router_group[0].T, jnp.zeros((8 - N_GROUPS, D_MODEL), F32),
        jnp.transpose(w_router_expert[0], (0, 2, 1)).reshape(N_EXPERTS, D_MODEL)], axis=0)
    br = jnp.concatenate([b_router_group[0], jnp.zeros((8 - N_GROUPS,), F32),
                          b_router_expert[0].reshape(-1)])
    br = jnp.broadcast_to(br[:, None], (R_ROWS, LANES))
    eid, wts, rank, counts_f = _router(x1, norm2_g[0][None], wr_t, br)

    ie, row0, rows, nitems, slot_flat, padrow = _routing_tables(eid, rank, counts_f)
    xs = _dispatch(slot_flat, padrow, x1, norm2_g[0][None])
    ys = _experts(ie, row0, rows, nitems, xs,
                  w_gate[0].reshape(N_EXPERTS, D_MODEL, D_FF),
                  w_up[0].reshape(N_EXPERTS, D_MODEL, D_FF),
                  w_down[0].reshape(N_EXPERTS, D_FF, D_MODEL))
    out = _gcombine(slot_flat, x1, wts.T, ys)
    return out[None]
```

```python
import functools
import math

import numpy as np
import jax
import jax.numpy as jnp
from jax import lax
from jax.experimental import pallas as pl
from jax.experimental.pallas import tpu as pltpu

F32 = jnp.float32
BF16 = jnp.bfloat16

D_MODEL = 2048
SEQ = 8192
N_HEADS = 16
HEAD_DIM = 64
ATTN_W = N_HEADS * HEAD_DIM
CONV_C = D_MODEL - ATTN_W
CONV_K = 31
IN_W = 2 * CONV_C + 3 * ATTN_W
PATTERNS = ((128, 1), (512, 4), (2048, 16))
QBLK = 128
NUM_BUCKETS = 32
MAX_DISTANCE = 2048
N_GROUPS = 4
E_PER_G = 8
N_EXPERTS = N_GROUPS * E_PER_G
D_FF = D_MODEL // 2
EPS = 1e-6
NEG_INF = -1e30
LOG2E = math.log2(math.e)

LANES = 128
VMEM_LIMIT = 56 * 1024 * 1024

TM_IN = 1024
TN_IN = 512
IN_RING = 4
TT_CONV = 512
HALO = 32
R_CONV = 64
R_LN = 16
ATTN_GROUP = 4
ATTN_UNROLL = 2
TM_OUT = 512
TM_R = 512
R_ROWS = 8 + N_EXPERTS
KC_R = 512
TM_E = 768
SUB_E = 256
TF_E = 256
NF_E = D_FF // TF_E
W_RING = 4
N_ASSIGN = 2 * SEQ
MAX_ITEMS = -(-N_ASSIGN // TM_E) + N_EXPERTS
SORTED_LEN = N_ASSIGN + TM_E
XS_TAIL = 256 + TM_E
XS_ROWS = N_ASSIGN + XS_TAIL
ZROWS = 128
TT_D = 512
TT_G = 512
TT_COMB = 512


def _cparams(sem, vmem=VMEM_LIMIT, flags=None):
    return pltpu.CompilerParams(dimension_semantics=sem, vmem_limit_bytes=vmem, flags=flags)


def _inproj_kernel(x_ref, g_ref, w_hbm, o_hbm, xn_ref, w_ref, ob_ref, wsem, osem):
    i = pl.program_id(0)
    nj = IN_W // TN_IN
    total = pl.num_programs(0) * nj

    def w_copy(s):
        cols = pl.ds(pl.multiple_of((s % nj) * TN_IN, TN_IN), TN_IN)
        b = s % IN_RING
        return pltpu.make_async_copy(w_hbm.at[:, cols], w_ref.at[b], wsem.at[b])

    def o_copy(s):
        rows = pl.ds(pl.multiple_of((s // nj) * TM_IN, TM_IN), TM_IN)
        cols = pl.ds(pl.multiple_of((s % nj) * TN_IN, TN_IN), TN_IN)
        b = s % 2
        return pltpu.make_async_copy(ob_ref.at[b], o_hbm.at[rows, cols], osem.at[b])

    @pl.when(i == 0)
    def _():
        for s in range(IN_RING - 1):
            w_copy(s).start()

    x = x_ref[...]
    ms = jnp.mean(x * x, axis=-1, keepdims=True)
    xn_ref[...] = (x * lax.rsqrt(ms + EPS) * g_ref[...]).astype(BF16)

    def col_step(j, carry):
        s = i * nj + j

        @pl.when(s + IN_RING - 1 < total)
        def _():
            w_copy(s + IN_RING - 1).start()

        @pl.when(s >= 2)
        def _():
            o_copy(s - 2).wait()

        w_copy(s).wait()
        ob_ref[s % 2] = jnp.dot(xn_ref[...], w_ref[s % IN_RING].astype(BF16), preferred_element_type=F32)
        o_copy(s).start()
        return carry

    lax.fori_loop(0, nj, col_step, 0)

    @pl.when(i == pl.num_programs(0) - 1)
    def _():
        o_copy(total - 2).wait()
        o_copy(total - 1).wait()


def _inproj(x, g, w):
    return pl.pallas_call(
        _inproj_kernel,
        grid=(SEQ // TM_IN,),
        in_specs=[
            pl.BlockSpec((TM_IN, D_MODEL), lambda i: (i, 0)),
            pl.BlockSpec((1, D_MODEL), lambda i: (0, 0)),
            pl.BlockSpec(memory_space=pl.ANY),
        ],
        out_specs=pl.BlockSpec(memory_space=pl.ANY),
        out_shape=jax.ShapeDtypeStruct((SEQ, IN_W), F32),
        scratch_shapes=[
            pltpu.VMEM((TM_IN, D_MODEL), BF16),
            pltpu.VMEM((IN_RING, D_MODEL, TN_IN), F32),
            pltpu.VMEM((2, TM_IN, TN_IN), F32),
            pltpu.SemaphoreType.DMA((IN_RING,)),
            pltpu.SemaphoreType.DMA((2,)),
        ],
        compiler_params=_cparams(("arbitrary",)),
        name="inproj",
    )(x, g, w)


def _conv_kernel(val_ref, gate_ref, hval_ref, hgate_ref, cw_ref, cb_ref, lg_ref, lb_ref,
                 o_ref, ubuf, zbuf, ybuf):
    i = pl.program_id(0)
    u = val_ref[...] * jax.nn.sigmoid(gate_ref[...])
    hu = hval_ref[...] * jax.nn.sigmoid(hgate_ref[...])
    hu = jnp.where(i > 0, hu, 0.0)
    for c in range(CONV_C // LANES):
        cols = slice(c * LANES, (c + 1) * LANES)
        ubuf[c, 0:HALO, :] = hu[:, cols]
        ubuf[c, HALO:HALO + TT_CONV, :] = u[:, cols]

    n_a = -(-CONV_K // 8)
    assert HALO == 8 * n_a
    for c in range(CONV_C // LANES):
        cols = slice(c * LANES, (c + 1) * LANES)

        def taps(r, carry, c=c, cols=cols):
            base = pl.multiple_of(r * R_CONV, R_CONV)
            win = ubuf[c, pl.ds(base, R_CONV + HALO), :]
            for b in range(8):
                z = None
                for a in range(n_a):
                    s = 8 * a + b
                    if s >= CONV_K:
                        continue
                    lo = HALO - 8 - 8 * a
                    t = cw_ref[CONV_K - 1 - s:CONV_K - s, cols] * win[lo:lo + R_CONV + 8, :]
                    z = t if z is None else z + t
                zbuf[b, pl.ds(0, R_CONV + 8, stride=2), :] = z
            acc = None
            for b in range(8):
                t = zbuf[b, pl.ds(2 * (8 - b), R_CONV, stride=2), :]
                acc = t if acc is None else acc + t
            ybuf[pl.ds(base, R_CONV), cols] = acc
            return carry

        lax.fori_loop(0, TT_CONV // R_CONV, taps, 0)

    def norm(r, carry):
        rows = pl.ds(pl.multiple_of(r * R_LN, R_LN), R_LN)
        acc = ybuf[rows, :] + cb_ref[...]
        mu = jnp.mean(acc, axis=-1, keepdims=True)
        xc = acc - mu
        var = jnp.mean(xc * xc, axis=-1, keepdims=True)
        y = xc * lax.rsqrt(var + EPS) * lg_ref[...] + lb_ref[...]
        o_ref[rows, :] = (y * jax.nn.sigmoid(y)).astype(BF16)
        return carry

    lax.fori_loop(0, TT_CONV // R_LN, norm, 0, unroll=4)


def _conv_mixer(proj, cw, cb, lg, lb):
    hb = TT_CONV // HALO
    return pl.pallas_call(
        _conv_kernel,
        grid=(SEQ // TT_CONV,),
        in_specs=[
            pl.BlockSpec((TT_CONV, CONV_C), lambda i: (i, 0)),
            pl.BlockSpec((TT_CONV, CONV_C), lambda i: (i, 1)),
            pl.BlockSpec((HALO, CONV_C), lambda i: (jnp.maximum(i * hb - 1, 0), 0)),
            pl.BlockSpec((HALO, CONV_C), lambda i: (jnp.maximum(i * hb - 1, 0), 1)),
            pl.BlockSpec((CONV_K, CONV_C), lambda i: (0, 0)),
            pl.BlockSpec((1, CONV_C), lambda i: (0, 0)),
            pl.BlockSpec((1, CONV_C), lambda i: (0, 0)),
            pl.BlockSpec((1, CONV_C), lambda i: (0, 0)),
        ],
        out_specs=pl.BlockSpec((TT_CONV, CONV_C), lambda i: (i, 0)),
        out_shape=jax.ShapeDtypeStruct((SEQ, CONV_C), BF16),
        scratch_shapes=[pltpu.VMEM((CONV_C // LANES, HALO + TT_CONV, LANES), F32),
                        pltpu.VMEM((8, 2 * (R_CONV + 8), LANES), F32),
                        pltpu.VMEM((TT_CONV, CONV_C), F32)],
        compiler_params=_cparams(("arbitrary",)),
        name="conv_mixer",
    )(proj, proj, proj, proj, cw, cb, lg, lb)


def _t5_bucket_np(dist):
    max_exact = NUM_BUCKETS // 2
    nf = np.maximum(dist, 1).astype(np.float32)
    large = max_exact + (np.log(nf / np.float32(max_exact)) / np.float32(math.log(MAX_DISTANCE / max_exact))
                         * np.float32(NUM_BUCKETS - max_exact)).astype(np.int32)
    large = np.minimum(large, NUM_BUCKETS - 1)
    return np.where(dist < max_exact, dist, large)


def _attn_bias_tables(rel_bias):
    period = 3 * QBLK
    diags = []
    for window, dil in PATTERNS:
        span = window // dil
        assert span <= QBLK
        bucket = _t5_bucket_np(np.arange(span + 1) * dil)
        onehot = np.eye(NUM_BUCKETS, dtype=np.float32)[bucket]
        vec = jnp.einsum("rb,bh->hr", onehot, rel_bias.astype(F32),
                         precision=lax.Precision.HIGHEST)
        diag = jnp.full((N_HEADS, period), NEG_INF, F32)
        diags.append(diag.at[:, 2 * QBLK - 1 - span:2 * QBLK].set(vec[:, ::-1]))
    diag = jnp.stack(diags) * LOG2E
    shifted = jnp.stack([jnp.roll(diag, b, axis=-1) for b in range(8)], axis=-2)
    return shifted.reshape(len(PATTERNS), N_HEADS // 2, 2, 8, period)


def _attn_kernel(q_ref, k_ref, v_ref, qg_ref, kg_ref, diag_ref, o_ref,
                 qn_ref, kn_ref, acc_ref, m_ref, l_ref, bias_s):
    lane = lax.broadcasted_iota(jnp.int32, (QBLK, LANES), 1)
    head_a = lane < HEAD_DIM
    hist_keys = lax.broadcasted_iota(jnp.int32, (2 * QBLK, 2 * QBLK), 1) < QBLK
    ri = lax.broadcasted_iota(jnp.int32, (LANES, LANES), 0) // HEAD_DIM
    ci = lax.broadcasted_iota(jnp.int32, (LANES, LANES), 1) // HEAD_DIM
    seg = (ri == ci).astype(BF16)

    def head_rms(x, g):
        sq = x * x
        hi = sq.astype(BF16)
        lo = (sq - hi.astype(F32)).astype(BF16)
        ss = (jnp.dot(hi, seg, preferred_element_type=F32)
              + jnp.dot(lo, seg, preferred_element_type=F32))
        return x * lax.rsqrt(ss * (1.0 / HEAD_DIM) + EPS) * g

    NCH = 512

    def norm_body(c, carry):
        rows = pl.ds(pl.multiple_of(c * NCH, NCH), NCH)
        qn_ref[rows, :] = head_rms(q_ref[rows, :], qg_ref[...]) * (LOG2E / math.sqrt(HEAD_DIM))
        kn_ref[rows, :] = head_rms(k_ref[rows, :], kg_ref[...])
        return carry

    lax.fori_loop(0, SEQ // NCH, norm_body, 0)

    for p in range(len(PATTERNS)):
        for h in range(2):
            offs = diag_ref[p, h]
            for a in range(QBLK // 8):
                lo = QBLK - 1 - 8 * a
                bias_s[p, h * QBLK + 8 * a:h * QBLK + 8 * a + 8, :] = offs[:, lo:lo + 2 * QBLK]

    order = sorted(range(len(PATTERNS)), key=lambda i: -PATTERNS[i][1])
    assert PATTERNS[order[-1]][1] == 1
    for step, p in enumerate(order):
        dil = PATTERNS[p][1]
        nb = SEQ // (dil * QBLK)
        is_first = step == 0
        is_last = step == len(order) - 1

        def rows(start, dil=dil):
            if dil == 1:
                return pl.ds(start, QBLK)
            return pl.ds(start, QBLK, stride=dil)

        def unit(cur, k_prev, v_prev, first, p=p, rows=rows, is_first=is_first, is_last=is_last):
            q = qn_ref[rows(cur), :]
            q2 = jnp.concatenate([jnp.where(head_a, q, 0.0), jnp.where(head_a, 0.0, q)],
                                 axis=0).astype(BF16)
            k_cur = kn_ref[rows(cur), :].astype(BF16)
            v_cur = v_ref[rows(cur), :].astype(BF16)
            k2 = jnp.concatenate([k_prev, k_cur], axis=0)
            v2 = jnp.concatenate([v_prev, v_cur], axis=0)
            s = lax.dot_general(q2, k2, (((1,), (1,)), ((), ())), preferred_element_type=F32)
            s = s + bias_s[p]
            if first is not None:
                s = jnp.where(hist_keys, jnp.where(first, NEG_INF * LOG2E, s), s)
            m = jnp.max(s, axis=-1, keepdims=True)
            e = jnp.exp2(s - m)
            l = jnp.sum(e, axis=-1, keepdims=True)
            pv = jnp.dot(e.astype(BF16), v2, preferred_element_type=F32)
            o_new = jnp.where(head_a, pv[:QBLK], pv[QBLK:])
            m_new = jnp.where(head_a, m[:QBLK], m[QBLK:])
            l_new = jnp.where(head_a, l[:QBLK], l[QBLK:])
            if is_first:
                acc_ref[rows(cur), :] = o_new
                m_ref[rows(cur), :] = m_new
                l_ref[rows(cur), :] = l_new
            else:
                m_old = m_ref[rows(cur), :]
                m_tot = jnp.maximum(m_old, m_new)
                a = jnp.exp2(m_old - m_tot)
                b = jnp.exp2(m_new - m_tot)
                acc = acc_ref[rows(cur), :] * a + o_new * b
                den = l_ref[rows(cur), :] * a + l_new * b
                if is_last:
                    o_ref[pl.ds(pl.multiple_of(cur, QBLK), QBLK), :] = (acc / den).astype(BF16)
                else:
                    acc_ref[rows(cur), :] = acc
                    l_ref[rows(cur), :] = den
                    m_ref[rows(cur), :] = m_tot
            return k_cur, v_cur

        per = nb // ATTN_GROUP
        assert per * ATTN_GROUP == nb

        def group(g, carry, dil=dil, per=per, rows=rows, unit=unit):
            r = g // per
            n0 = (g - r * per) * ATTN_GROUP
            hist = jnp.maximum(n0 - 1, 0) * (QBLK * dil) + r
            k_prev = kn_ref[rows(hist), :].astype(BF16)
            v_prev = v_ref[rows(hist), :].astype(BF16)
            for i in range(ATTN_GROUP):
                first = (n0 == 0) if i == 0 else None
                k_prev, v_prev = unit((n0 + i) * (QBLK * dil) + r, k_prev, v_prev, first)
            return carry

        lax.fori_loop(0, SEQ // (QBLK * ATTN_GROUP), group, 0, unroll=ATTN_UNROLL)


def _attention(proj, qg2, kg2, bias_tab):
    qoff = 2 * CONV_C // LANES
    koff = qoff + ATTN_W // LANES
    voff = koff + ATTN_W // LANES
    return pl.pallas_call(
        _attn_kernel,
        grid=(N_HEADS // 2,),
        in_specs=[
            pl.BlockSpec((SEQ, LANES), lambda h: (0, qoff + h)),
            pl.BlockSpec((SEQ, LANES), lambda h: (0, koff + h)),
            pl.BlockSpec((SEQ, LANES), lambda h: (0, voff + h)),
            pl.BlockSpec((1, LANES), lambda h: (0, 0)),
            pl.BlockSpec((1, LANES), lambda h: (0, 0)),
            pl.BlockSpec((len(PATTERNS), None, 2, 8, 3 * QBLK), lambda h: (0, h, 0, 0, 0)),
        ],
        out_specs=pl.BlockSpec((SEQ, LANES), lambda h: (0, h)),
        out_shape=jax.ShapeDtypeStruct((SEQ, ATTN_W), BF16),
        scratch_shapes=[pltpu.VMEM((SEQ, LANES), F32) for _ in range(5)]
        + [pltpu.VMEM((len(PATTERNS), 2 * QBLK, 2 * QBLK), F32)],
        compiler_params=_cparams(("arbitrary",)),
        name="dilated_attn",
    )(proj, proj, proj, qg2, kg2, bias_tab)


def _outproj_kernel(x_ref, c_ref, a_ref, wc_ref, wa_ref, o_ref):
    o_ref[...] = (x_ref[...]
                  + jnp.dot(c_ref[...], wc_ref[...].astype(BF16), preferred_element_type=F32)
                  + jnp.dot(a_ref[...], wa_ref[...].astype(BF16), preferred_element_type=F32))


def _outproj(x, conv_out, attn_out, w_out):
    return pl.pallas_call(
        _outproj_kernel,
        grid=(SEQ // TM_OUT,),
        in_specs=[
            pl.BlockSpec((TM_OUT, D_MODEL), lambda i: (i, 0)),
            pl.BlockSpec((TM_OUT, CONV_C), lambda i: (i, 0)),
            pl.BlockSpec((TM_OUT, ATTN_W), lambda i: (i, 0)),
            pl.BlockSpec((CONV_C, D_MODEL), lambda i: (0, 0), pipeline_mode=pl.Buffered(1)),
            pl.BlockSpec((ATTN_W, D_MODEL), lambda i: (1, 0), pipeline_mode=pl.Buffered(1)),
        ],
        out_specs=pl.BlockSpec((TM_OUT, D_MODEL), lambda i: (i, 0)),
        out_shape=jax.ShapeDtypeStruct((SEQ, D_MODEL), F32),
        compiler_params=_cparams(("arbitrary",)),
        name="outproj",
    )(x, conv_out, attn_out, w_out, w_out)


def _split2(a):
    a1 = a.astype(BF16)
    a2 = (a - a1.astype(F32)).astype(BF16)
    return a1, a2


def _router_kernel(x_ref, g_ref, wr_ref, br_ref, eid_ref, wts_ref, rank_ref, cnt_ref):
    i = pl.program_id(0)

    @pl.when(i == 0)
    def _():
        cnt_ref[...] = jnp.zeros_like(cnt_ref)

    x = x_ref[...]
    inv = lax.rsqrt(jnp.mean(x * x, axis=-1, keepdims=True) + EPS)
    dn = (((1,), (1,)), ((), ()))
    lt = None
    for c in range(D_MODEL // KC_R):
        cols = slice(c * KC_R, (c + 1) * KC_R)
        h1, h2 = _split2(x_ref[:, cols] * inv * g_ref[:, cols])
        w1, w2 = _split2(wr_ref[:, cols])
        for wa, ha in ((w1, h1), (w1, h2), (w2, h1)):
            t = lax.dot_general(wa, ha, dn, preferred_element_type=F32)
            lt = t if lt is None else lt + t
    lt = lt + br_ref[:, 0:1]

    row8 = lax.broadcasted_iota(jnp.int32, (8, TM_R), 0)
    gl = jnp.where(row8 < N_GROUPS, lt[0:8], -jnp.inf)
    gmax = jnp.max(gl, axis=0, keepdims=True)
    gidx = jnp.min(jnp.where(gl == gmax, row8, 8), axis=0, keepdims=True)
    gw = 1.0 / jnp.sum(jnp.exp(gl - gmax), axis=0, keepdims=True)

    esel = lt[8:16]
    for g in range(1, N_GROUPS):
        esel = jnp.where(gidx == g, lt[8 + 8 * g:16 + 8 * g], esel)
    v1 = jnp.max(esel, axis=0, keepdims=True)
    i1 = jnp.min(jnp.where(esel == v1, row8, 8), axis=0, keepdims=True)
    rest = jnp.where(row8 == i1, -jnp.inf, esel)
    v2 = jnp.max(rest, axis=0, keepdims=True)
    i2 = jnp.min(jnp.where(rest == v2, row8, 8), axis=0, keepdims=True)
    e21 = jnp.exp(v2 - v1)
    den = 1.0 + e21
    e1 = gidx * E_PER_G + i1
    e2 = gidx * E_PER_G + i2
    eid_ref[0:1, :] = e1
    eid_ref[1:2, :] = e2
    wts_ref[0:1, :] = gw * (1.0 / den)
    wts_ref[1:2, :] = gw * (e21 / den)

    erow = lax.broadcasted_iota(jnp.int32, (N_EXPERTS, TM_R), 0)
    oh1 = erow == e1
    oh2 = erow == e2
    member = jnp.where(oh1 | oh2, 1.0, 0.0)
    ti = lax.broadcasted_iota(jnp.int32, (TM_R, TM_R), 0)
    tj = lax.broadcasted_iota(jnp.int32, (TM_R, TM_R), 1)
    upper = jnp.where(ti < tj, 1.0, 0.0).astype(BF16)
    before = jnp.dot(member.astype(BF16), upper, preferred_element_type=F32)
    pos = before + cnt_ref[:, 0:1]
    rank_ref[0:1, :] = jnp.sum(jnp.where(oh1, pos, 0.0), axis=0, keepdims=True).astype(jnp.int32)
    rank_ref[1:2, :] = jnp.sum(jnp.where(oh2, pos, 0.0), axis=0, keepdims=True).astype(jnp.int32)
    cnt_ref[...] = cnt_ref[...] + jnp.sum(member, axis=1, keepdims=True)


def _router(x1, g2, wr_t, br):
    return pl.pallas_call(
        _router_kernel,
        grid=(SEQ // TM_R,),
        in_specs=[
            pl.BlockSpec((TM_R, D_MODEL), lambda i: (i, 0)),
            pl.BlockSpec((1, D_MODEL), lambda i: (0, 0)),
            pl.BlockSpec((R_ROWS, D_MODEL), lambda i: (0, 0)),
            pl.BlockSpec((R_ROWS, LANES), lambda i: (0, 0)),
        ],
        out_specs=[
            pl.BlockSpec((2, TM_R), lambda i: (0, i)),
            pl.BlockSpec((2, TM_R), lambda i: (0, i)),
            pl.BlockSpec((2, TM_R), lambda i: (0, i)),
            pl.BlockSpec((N_EXPERTS, LANES), lambda i: (0, 0)),
        ],
        out_shape=[
            jax.ShapeDtypeStruct((2, SEQ), jnp.int32),
            jax.ShapeDtypeStruct((2, SEQ), F32),
            jax.ShapeDtypeStruct((2, SEQ), jnp.int32),
            jax.ShapeDtypeStruct((N_EXPERTS, LANES), F32),
        ],
        compiler_params=_cparams(("arbitrary",)),
        name="router",
    )(x1, g2, wr_t, br)


def _moe_kernel(ie_ref, row0_ref, rows_ref, nitems_ref, sorted_ref,
                x_hbm, g_ref, wg_ref, wu_ref, wd_ref, dest_hbm,
                xg_ref, xb_ref, y_ref, gsem, ssem):
    it = pl.program_id(0)
    f = pl.program_id(1)
    nrows = rows_ref[it]
    nitems = nitems_ref[0]
    slot = it % 2
    half = SUB_E // 2

    def padded(item):
        return pl.multiple_of(((rows_ref[item] + half - 1) // half) * half, half)

    nhalf = padded(it) // half
    nsub = nhalf // 2

    def start_gather(item):
        base = row0_ref[item]
        buf = item % 2

        def start(j8, c):
            for k in range(8):
                j = j8 * 8 + k
                tok = sorted_ref[base + j] & (SEQ - 1)
                pltpu.make_async_copy(x_hbm.at[pl.ds(tok, 1), :], xg_ref.at[buf, pl.ds(j, 1), :],
                                      gsem.at[buf]).start()
            return c
        lax.fori_loop(0, padded(item) // 8, start, 0)

    def wait_gather(item):
        got = pl.ds(0, padded(item))
        buf = item % 2
        pltpu.make_async_copy(x_hbm.at[got, :], xg_ref.at[buf, got, :], gsem.at[buf]).wait()

    def scatter_copy(item, j):
        dst = sorted_ref[row0_ref[item] + j]
        buf = item % 2
        return pltpu.make_async_copy(y_ref.at[buf, pl.ds(j, 1), :], dest_hbm.at[pl.ds(dst, 1), :],
                                     ssem.at[buf])

    def start_scatter(item):
        n = rows_ref[item]

        def start8(j8, c):
            for k in range(8):
                scatter_copy(item, j8 * 8 + k).start()
            return c
        lax.fori_loop(0, n // 8, start8, 0)

        def start(j, c):
            scatter_copy(item, j).start()
            return c
        lax.fori_loop((n // 8) * 8, n, start, 0)

    def wait_scatter(item):
        n = rows_ref[item]
        whole = pl.multiple_of((n // 8) * 8, 8)
        buf = item % 2

        @pl.when(whole > 0)
        def _():
            sent = pl.ds(0, whole)
            pltpu.make_async_copy(y_ref.at[buf, sent, :], dest_hbm.at[sent, :], ssem.at[buf]).wait()

        def wait(j, c):
            scatter_copy(item, j).wait()
            return c
        lax.fori_loop(whole, n, wait, 0)

    @pl.when((f == 0) & (nrows > 0))
    def _():
        @pl.when(it == 0)
        def _():
            start_gather(0)

        wait_gather(it)

        def norm(s, c):
            rows = pl.ds(pl.multiple_of(s * half, half), half)
            x = xg_ref[slot, rows, :]
            ms = jnp.mean(x * x, axis=-1, keepdims=True)
            xb_ref[rows, :] = (x * lax.rsqrt(ms + EPS) * g_ref[...]).astype(BF16)
            y_ref[slot, rows, :] = jnp.zeros((half, D_MODEL), F32)
            return c
        lax.fori_loop(0, nhalf, norm, 0)

        @pl.when(it + 1 < nitems)
        def _():
            start_gather(it + 1)

    @pl.when(nrows > 0)
    def _():
        wg = wg_ref[...].astype(BF16)
        wu = wu_ref[...].astype(BF16)
        wd = wd_ref[...].astype(BF16)

        def block(rows):
            xb = xb_ref[rows, :]
            hg = jnp.dot(xb, wg, preferred_element_type=F32)
            hu = jnp.dot(xb, wu, preferred_element_type=F32)
            h = (hg * jax.nn.sigmoid(hg) * hu).astype(BF16)
            y_ref[slot, rows, :] = y_ref[slot, rows, :] + jnp.dot(h, wd, preferred_element_type=F32)

        def sub(s, c):
            block(pl.ds(pl.multiple_of(s * SUB_E, SUB_E), SUB_E))
            return c
        lax.fori_loop(0, nsub, sub, 0)

        @pl.when(nhalf % 2 == 1)
        def _():
            block(pl.ds(pl.multiple_of(nsub * SUB_E, SUB_E), half))

    @pl.when((f == NF_E - 1) & (nrows > 0))
    def _():
        @pl.when(it > 0)
        def _():
            wait_scatter(it - 1)

        start_scatter(it)

        @pl.when(it + 1 >= nitems)
        def _():
            wait_scatter(it)


def _moe(ie, row0, rows, nitems, sorted_i, x1, g2, wg, wu, wd):
    def wmap_cols(i, f, ie_ref, row0_ref, rows_ref, n_ref, s_ref):
        return (ie_ref[i], 0, jnp.where(i < n_ref[0], f, NF_E - 1))

    def wmap_rows(i, f, ie_ref, row0_ref, rows_ref, n_ref, s_ref):
        return (ie_ref[i], jnp.where(i < n_ref[0], f, NF_E - 1), 0)

    grid_spec = pltpu.PrefetchScalarGridSpec(
        num_scalar_prefetch=5,
        grid=(nitems[0], NF_E),
        in_specs=[
            pl.BlockSpec(memory_space=pl.ANY),
            pl.BlockSpec((1, D_MODEL), lambda i, f, *_: (0, 0)),
            pl.BlockSpec((None, D_MODEL, TF_E), wmap_cols),
            pl.BlockSpec((None, D_MODEL, TF_E), wmap_cols),
            pl.BlockSpec((None, TF_E, D_MODEL), wmap_rows),
        ],
        out_specs=pl.BlockSpec(memory_space=pl.ANY),
        scratch_shapes=[
            pltpu.VMEM((2, TM_E, D_MODEL), F32),
            pltpu.VMEM((TM_E, D_MODEL), BF16),
            pltpu.VMEM((2, TM_E, D_MODEL), F32),
            pltpu.SemaphoreType.DMA((2,)),
            pltpu.SemaphoreType.DMA((2,)),
        ],
    )
    return pl.pallas_call(
        _moe_kernel,
        grid_spec=grid_spec,
        out_shape=jax.ShapeDtypeStruct((N_ASSIGN, D_MODEL), F32),
        compiler_params=_cparams(("arbitrary", "arbitrary")),
        name="moe_experts",
    )(ie, row0, rows, nitems, sorted_i, x1, g2, wg, wu, wd)


def _combine_kernel(x_ref, w_ref, d0_ref, d1_ref, o_ref):
    w = w_ref[...]
    o_ref[...] = x_ref[...] + w[:, 0:1] * d0_ref[...] + w[:, 1:2] * d1_ref[...]


def _combine(x1, wts_t, dest):
    dest3 = dest.reshape(2, SEQ, D_MODEL)
    return pl.pallas_call(
        _combine_kernel,
        grid=(SEQ // TT_COMB,),
        in_specs=[
            pl.BlockSpec((TT_COMB, D_MODEL), lambda i: (i, 0)),
            pl.BlockSpec((TT_COMB, 2), lambda i: (i, 0)),
            pl.BlockSpec((None, TT_COMB, D_MODEL), lambda i: (0, i, 0)),
            pl.BlockSpec((None, TT_COMB, D_MODEL), lambda i: (1, i, 0)),
        ],
        out_specs=pl.BlockSpec((TT_COMB, D_MODEL), lambda i: (i, 0)),
        out_shape=jax.ShapeDtypeStruct((SEQ, D_MODEL), F32),
        compiler_params=_cparams(("arbitrary",)),
        name="moe_combine",
    )(x1, wts_t, dest3, dest3)


def _dispatch_kernel(slot_ref, padrow_ref, x_ref, g_ref, xs_hbm, hbuf, zbuf, sem, zsem):
    i = pl.program_id(0)
    last = pl.num_programs(0) - 1
    buf = i % 2

    def tile_wait(b):
        for _ in range(2):
            pltpu.make_async_copy(hbuf.at[b], xs_hbm.at[pl.ds(0, TT_D), :], sem.at[b]).wait()

    @pl.when(i == 0)
    def _():
        zbuf[...] = jnp.zeros_like(zbuf)

        for c in range(XS_TAIL // ZROWS):
            pltpu.make_async_copy(zbuf, xs_hbm.at[pl.ds(XS_ROWS - XS_TAIL + c * ZROWS, ZROWS), :], zsem).start()
        for c in range(XS_TAIL // ZROWS):
            pltpu.make_async_copy(zbuf, xs_hbm.at[pl.ds(0, ZROWS), :], zsem).wait()

        def pad(e, c):
            dst = pl.multiple_of(padrow_ref[e], 8)
            pltpu.make_async_copy(zbuf.at[pl.ds(0, 8), :], xs_hbm.at[pl.ds(dst, 8), :], zsem).start()
            return c
        lax.fori_loop(0, N_EXPERTS, pad, 0)

        def padw(e, c):
            pltpu.make_async_copy(zbuf.at[pl.ds(0, 8), :], xs_hbm.at[pl.ds(0, 8), :], zsem).wait()
            return c
        lax.fori_loop(0, N_EXPERTS, padw, 0)

    @pl.when(i >= 2)
    def _():
        tile_wait(buf)

    x = x_ref[...]
    ms = jnp.mean(x * x, axis=-1, keepdims=True)
    hbuf[buf] = x * lax.rsqrt(ms + EPS) * g_ref[...]

    tok0 = i * TT_D

    def send(j8, c):
        for k8 in range(8):
            j = j8 * 8 + k8
            for k in range(2):
                dst = slot_ref[k * SEQ + tok0 + j]
                pltpu.make_async_copy(hbuf.at[buf, pl.ds(j, 1), :], xs_hbm.at[pl.ds(dst, 1), :],
                                      sem.at[buf]).start()
        return c
    lax.fori_loop(0, TT_D // 8, send, 0)

    @pl.when(i == last)
    def _():
        @pl.when(i >= 1)
        def _():
            tile_wait(1 - buf)
        tile_wait(buf)


def _dispatch(slot_flat, padrow, x1, g2):
    grid_spec = pltpu.PrefetchScalarGridSpec(
        num_scalar_prefetch=2,
        grid=(SEQ // TT_D,),
        in_specs=[
            pl.BlockSpec((TT_D, D_MODEL), lambda i, *_: (i, 0)),
            pl.BlockSpec((1, D_MODEL), lambda i, *_: (0, 0)),
        ],
        out_specs=pl.BlockSpec(memory_space=pl.ANY),
        scratch_shapes=[
            pltpu.VMEM((2, TT_D, D_MODEL), F32),
            pltpu.VMEM((ZROWS, D_MODEL), F32),
            pltpu.SemaphoreType.DMA((2,)),
            pltpu.SemaphoreType.DMA,
        ],
    )
    return pl.pallas_call(
        _dispatch_kernel,
        grid_spec=grid_spec,
        out_shape=jax.ShapeDtypeStruct((XS_ROWS, D_MODEL), F32),
        compiler_params=_cparams(("arbitrary",)),
        name="moe_dispatch",
    )(slot_flat, padrow, x1, g2)


def _experts_kernel(ie_ref, row0_ref, rows_ref, nitems_ref,
                    xs_hbm, wg_hbm, wu_hbm, wd_hbm, ys_hbm,
                    xg_ref, y_ref, zbuf, wg_ref, wu_ref, wd_ref, gsem, ssem, zsem, wsem):
    it = pl.program_id(0)
    nitems = nitems_ref[0]
    slot = it % 2
    half = SUB_E // 2
    nsteps = nitems * NF_E

    def weight_copies(s):
        e = ie_ref[s // NF_E]
        cols = pl.ds(pl.multiple_of((s % NF_E) * TF_E, TF_E), TF_E)
        b = s % W_RING
        return (pltpu.make_async_copy(wg_hbm.at[e, :, cols], wg_ref.at[b], wsem.at[b]),
                pltpu.make_async_copy(wu_hbm.at[e, :, cols], wu_ref.at[b], wsem.at[b]),
                pltpu.make_async_copy(wd_hbm.at[e, cols, :], wd_ref.at[b], wsem.at[b]))

    @pl.when(it == 0)
    def _():
        for s in range(W_RING - 1):
            for cp in weight_copies(s):
                cp.start()

    def padded(item):
        return pl.multiple_of(((rows_ref[item] + half - 1) // half) * half, half)

    nhalf = padded(it) // half
    nsub = nhalf // 2

    def chunk_copies(item, start):
        base = pl.multiple_of(row0_ref[item], 8)
        buf = item % 2

        def body(c, carry):
            off = pl.multiple_of(c * half, half)
            src = xs_hbm.at[pl.ds(pl.multiple_of(base + off, 8), half), :]
            dst = ys_hbm.at[pl.ds(pl.multiple_of(base + off, 8), half), :]
            if start == "fetch":
                pltpu.make_async_copy(src, xg_ref.at[buf, pl.ds(off, half), :], gsem.at[buf]).start()
            else:
                pltpu.make_async_copy(y_ref.at[buf, pl.ds(off, half), :], dst, ssem.at[buf]).start()
            return carry
        lax.fori_loop(0, padded(item) // half, body, 0)

    def wait_fetch(item):
        got = pl.ds(0, padded(item))
        buf = item % 2
        pltpu.make_async_copy(xs_hbm.at[got, :], xg_ref.at[buf, got, :], gsem.at[buf]).wait()

    def wait_store(item):
        put = pl.ds(0, padded(item))
        buf = item % 2
        pltpu.make_async_copy(y_ref.at[buf, put, :], ys_hbm.at[put, :], ssem.at[buf]).wait()

    @pl.when(it == 0)
    def _():
        zbuf[...] = jnp.zeros_like(zbuf)
        for c in range(XS_TAIL // ZROWS):
            pltpu.make_async_copy(zbuf, ys_hbm.at[pl.ds(XS_ROWS - XS_TAIL + c * ZROWS, ZROWS), :], zsem).start()
        for c in range(XS_TAIL // ZROWS):
            pltpu.make_async_copy(zbuf, ys_hbm.at[pl.ds(0, ZROWS), :], zsem).wait()
        chunk_copies(0, "fetch")

    wait_fetch(it)

    def clear(s, c):
        rows = pl.ds(pl.multiple_of(s * half, half), half)
        y_ref[slot, rows, :] = jnp.zeros((half, D_MODEL), F32)
        return c
    lax.fori_loop(0, nhalf, clear, 0)

    @pl.when(it + 1 < nitems)
    def _():
        chunk_copies(it + 1, "fetch")

    def chunk_step(f, carry):
        step = it * NF_E + f
        wslot = step % W_RING

        @pl.when(step + W_RING - 1 < nsteps)
        def _():
            for cp in weight_copies(step + W_RING - 1):
                cp.start()

        for cp in weight_copies(step):
            cp.wait()

        def block(rows):
            xb = xg_ref[slot, rows, :].astype(BF16)
            hg = jnp.dot(xb, wg_ref[wslot].astype(BF16), preferred_element_type=F32)
            hu = jnp.dot(xb, wu_ref[wslot].astype(BF16), preferred_element_type=F32)
            h = (hg * jax.nn.sigmoid(hg) * hu).astype(BF16)
            y_ref[slot, rows, :] = y_ref[slot, rows, :] + jnp.dot(h, wd_ref[wslot].astype(BF16),
                                                                   preferred_element_type=F32)

        def sub(s, c):
            block(pl.ds(pl.multiple_of(s * SUB_E, SUB_E), SUB_E))
            return c
        lax.fori_loop(0, nsub, sub, 0)

        @pl.when(nhalf % 2 == 1)
        def _():
            block(pl.ds(pl.multiple_of(nsub * SUB_E, SUB_E), half))
        return carry

    lax.fori_loop(0, NF_E, chunk_step, 0)

    @pl.when(it > 0)
    def _():
        wait_store(it - 1)

    chunk_copies(it, "store")

    @pl.when(it + 1 >= nitems)
    def _():
        wait_store(it)


def _experts(ie, row0, rows, nitems, xs, wg, wu, wd):
    grid_spec = pltpu.PrefetchScalarGridSpec(
        num_scalar_prefetch=4,
        grid=(nitems[0],),
        in_specs=[pl.BlockSpec(memory_space=pl.ANY) for _ in range(4)],
        out_specs=pl.BlockSpec(memory_space=pl.ANY),
        scratch_shapes=[
            pltpu.VMEM((2, TM_E, D_MODEL), F32),
            pltpu.VMEM((2, TM_E, D_MODEL), F32),
            pltpu.VMEM((ZROWS, D_MODEL), F32),
            pltpu.VMEM((W_RING, D_MODEL, TF_E), F32),
            pltpu.VMEM((W_RING, D_MODEL, TF_E), F32),
            pltpu.VMEM((W_RING, TF_E, D_MODEL), F32),
            pltpu.SemaphoreType.DMA((2,)),
            pltpu.SemaphoreType.DMA((2,)),
            pltpu.SemaphoreType.DMA,
            pltpu.SemaphoreType.DMA((W_RING,)),
        ],
    )
    return pl.pallas_call(
        _experts_kernel,
        grid_spec=grid_spec,
        out_shape=jax.ShapeDtypeStruct((XS_ROWS, D_MODEL), F32),
        compiler_params=_cparams(("arbitrary",)),
        name="moe_experts",
    )(ie, row0, rows, nitems, xs, wg, wu, wd)


def _gcombine_kernel(slot_ref, x_ref, w_ref, ys_hbm, o_ref, dbuf, sem):
    i = pl.program_id(0)
    n = pl.num_programs(0)
    buf = i % 2

    def fetch(tile, b):
        tok0 = tile * TT_G

        def body(j8, c):
            for k8 in range(8):
                j = j8 * 8 + k8
                for k in range(2):
                    src = slot_ref[k * SEQ + tok0 + j]
                    pltpu.make_async_copy(ys_hbm.at[pl.ds(src, 1), :], dbuf.at[b, k, pl.ds(j, 1), :],
                                          sem.at[b]).start()
            return c
        lax.fori_loop(0, TT_G // 8, body, 0)

    @pl.when(i == 0)
    def _():
        fetch(0, 0)

    @pl.when(i + 1 < n)
    def _():
        fetch(i + 1, 1 - buf)

    for k in range(2):
        pltpu.make_async_copy(ys_hbm.at[pl.ds(0, TT_G), :], dbuf.at[buf, k], sem.at[buf]).wait()

    w = w_ref[...]
    o_ref[...] = x_ref[...] + w[:, 0:1] * dbuf[buf, 0] + w[:, 1:2] * dbuf[buf, 1]


def _gcombine(slot_flat, x1, wts_t, ys):
    grid_spec = pltpu.PrefetchScalarGridSpec(
        num_scalar_prefetch=1,
        grid=(SEQ // TT_G,),
        in_specs=[
            pl.BlockSpec((TT_G, D_MODEL), lambda i, *_: (i, 0)),
            pl.BlockSpec((TT_G, 2), lambda i, *_: (i, 0)),
            pl.BlockSpec(memory_space=pl.ANY),
        ],
        out_specs=pl.BlockSpec((TT_G, D_MODEL), lambda i, *_: (i, 0)),
        scratch_shapes=[
            pltpu.VMEM((2, 2, TT_G, D_MODEL), F32),
            pltpu.SemaphoreType.DMA((2,)),
        ],
    )
    return pl.pallas_call(
        _gcombine_kernel,
        grid_spec=grid_spec,
        out_shape=jax.ShapeDtypeStruct((SEQ, D_MODEL), F32),
        compiler_params=_cparams(("arbitrary",)),
        name="moe_combine",
    )(slot_flat, x1, wts_t, ys)


def _routing_tables(eid, rank, counts_f):
    counts = counts_f[:, 0].astype(jnp.int32)
    aligned = ((counts + 7) // 8) * 8
    base = jnp.cumsum(aligned) - aligned
    tiles = (counts + TM_E - 1) // TM_E
    tcum = jnp.cumsum(tiles)
    tstart = tcum - tiles
    nitems = tcum[-1]
    ids = jnp.arange(MAX_ITEMS, dtype=jnp.int32)
    ie = jnp.clip(jnp.searchsorted(tcum, ids, side="right"), 0, N_EXPERTS - 1).astype(jnp.int32)
    live = ids < nitems
    ie = jnp.where(live, ie, ie[jnp.maximum(nitems - 1, 0)])
    jt = ids - tstart[ie]
    row0 = jnp.where(live, base[ie] + jt * TM_E, 0)
    rows = jnp.where(live, jnp.clip(counts[ie] - jt * TM_E, 0, TM_E), 0)
    eoh = eid[:, :, None] == jnp.arange(N_EXPERTS, dtype=jnp.int32)
    slot = jnp.sum(jnp.where(eoh, base, 0), axis=-1) + rank
    spare = XS_ROWS - 8 * (1 + jnp.arange(N_EXPERTS, dtype=jnp.int32))
    padrow = jnp.where(counts % 8 != 0, base + (counts // 8) * 8, spare)
    return (ie, row0.astype(jnp.int32), rows.astype(jnp.int32), nitems.reshape(1).astype(jnp.int32),
            slot.reshape(-1).astype(jnp.int32), padrow.astype(jnp.int32))


def _work_items(eid, rank, counts_f):
    counts = counts_f[:, 0].astype(jnp.int32)
    cum = jnp.cumsum(counts)
    base = cum - counts
    tiles = (counts + TM_E - 1) // TM_E
    tcum = jnp.cumsum(tiles)
    tstart = tcum - tiles
    nitems = tcum[-1]
    ids = jnp.arange(MAX_ITEMS, dtype=jnp.int32)
    ie = jnp.clip(jnp.searchsorted(tcum, ids, side="right"), 0, N_EXPERTS - 1).astype(jnp.int32)
    live = ids < nitems
    ie = jnp.where(live, ie, ie[jnp.maximum(nitems - 1, 0)])
    jt = ids - tstart[ie]
    row0 = jnp.where(live, base[ie] + jt * TM_E, 0)
    rows = jnp.where(live, jnp.clip(counts[ie] - jt * TM_E, 0, TM_E), 0)
    eoh = eid[:, :, None] == jnp.arange(N_EXPERTS, dtype=jnp.int32)
    slot = jnp.sum(jnp.where(eoh, base, 0), axis=-1) + rank
    sorted_i = jnp.zeros((SORTED_LEN,), jnp.int32).at[slot.reshape(-1)].set(
        jnp.arange(N_ASSIGN, dtype=jnp.int32))
    return ie, row0.astype(jnp.int32), rows.astype(jnp.int32), nitems.reshape(1).astype(jnp.int32), sorted_i


def kernel(x, norm1_g, w_in, q_norm_g, k_norm_g, conv_w, conv_b, conv_ln_g, conv_ln_b, rel_bias,
           w_out, norm2_g, w_router_group, b_router_group, w_router_expert, b_router_expert,
           w_gate, w_up, w_down):
    assert x.shape == (1, SEQ, D_MODEL) and w_in.shape[0] == 1
    xs = x[0]
    bias_tab = _attn_bias_tables(rel_bias)
    qg2 = jnp.tile(q_norm_g[0], 2)[None]
    kg2 = jnp.tile(k_norm_g[0], 2)[None]

    proj = _inproj(xs, norm1_g[0][None], w_in[0])
    conv_out = _conv_mixer(proj, conv_w[0], conv_b[0][None], conv_ln_g[0][None], conv_ln_b[0][None])
    attn_out = _attention(proj, qg2, kg2, bias_tab)
    x1 = _outproj(xs, conv_out, attn_out, w_out[0])

    wr_t = jnp.concatenate([
        w_router_group[0].T, jnp.zeros((8 - N_GROUPS, D_MODEL), F32),
        jnp.transpose(w_router_expert[0], (0, 2, 1)).reshape(N_EXPERTS, D_MODEL)], axis=0)
    br = jnp.concatenate([b_router_group[0], jnp.zeros((8 - N_GROUPS,), F32),
                          b_router_expert[0].reshape(-1)])
    br = jnp.broadcast_to(br[:, None], (R_ROWS, LANES))
    eid, wts, rank, counts_f = _router(x1, norm2_g[0][None], wr_t, br)

    ie, row0, rows, nitems, slot_flat, padrow = _routing_tables(eid, rank, counts_f)
    xs = _dispatch(slot_flat, padrow, x1, norm2_g[0][None])
    ys = _experts(ie, row0, rows, nitems, xs,
                  w_gate[0].reshape(N_EXPERTS, D_MODEL, D_FF),
                  w_up[0].reshape(N_EXPERTS, D_MODEL, D_FF),
                  w_down[0].reshape(N_EXPERTS, D_FF, D_MODEL))
    out = _gcombine(slot_flat, x1, wts.T, ys)
    return out[None]
```

```python
import math

import numpy as np
import jax
import jax.numpy as jnp
from jax import lax
from jax.experimental import pallas as pl
from jax.experimental.pallas import tpu as pltpu

F32 = jnp.float32
BF16 = jnp.bfloat16

D_MODEL = 2048
SEQ = 8192
N_HEADS = 16
HEAD_DIM = 64
ATTN_W = N_HEADS * HEAD_DIM
CONV_C = D_MODEL - ATTN_W
CONV_K = 31
IN_W = 2 * CONV_C + 3 * ATTN_W
PATTERNS = ((128, 1), (512, 4), (2048, 16))
QBLK = 128
NUM_BUCKETS = 32
MAX_DISTANCE = 2048
N_GROUPS = 4
E_PER_G = 8
N_EXPERTS = N_GROUPS * E_PER_G
D_FF = D_MODEL // 2
EPS = 1e-6
NEG_INF = -1e30
LOG2E = math.log2(math.e)

LANES = 128
VMEM_LIMIT = 56 * 1024 * 1024

TM_IN = 1024
TN_IN = 512
IN_RING = 4
TT_CONV = 512
HALO = 32
R_CONV = 64
R_LN = 16
ATTN_GROUP = 4
ATTN_UNROLL = 2
TM_OUT = 512
TM_R = 512
R_ROWS = 8 + N_EXPERTS
KC_R = 512
TM_E = 768
SUB_E = 256
TF_E = 256
NF_E = D_FF // TF_E
W_RING = 4
N_ASSIGN = 2 * SEQ
MAX_ITEMS = -(-N_ASSIGN // TM_E) + N_EXPERTS
XS_TAIL = 256 + TM_E
XS_ROWS = N_ASSIGN + XS_TAIL
ZROWS = 128
TT_D = 512
TT_G = 512


def _cparams(sem, vmem=VMEM_LIMIT):
    return pltpu.CompilerParams(dimension_semantics=sem, vmem_limit_bytes=vmem)


def _inproj_kernel(x_ref, g_ref, w_hbm, o_hbm, xn_ref, w_ref, ob_ref, wsem, osem):
    i = pl.program_id(0)
    nj = IN_W // TN_IN
    total = pl.num_programs(0) * nj

    def w_copy(s):
        cols = pl.ds(pl.multiple_of((s % nj) * TN_IN, TN_IN), TN_IN)
        b = s % IN_RING
        return pltpu.make_async_copy(w_hbm.at[:, cols], w_ref.at[b], wsem.at[b])

    def o_copy(s):
        rows = pl.ds(pl.multiple_of((s // nj) * TM_IN, TM_IN), TM_IN)
        cols = pl.ds(pl.multiple_of((s % nj) * TN_IN, TN_IN), TN_IN)
        b = s % 2
        return pltpu.make_async_copy(ob_ref.at[b], o_hbm.at[rows, cols], osem.at[b])

    @pl.when(i == 0)
    def _():
        for s in range(IN_RING - 1):
            w_copy(s).start()

    x = x_ref[...]
    ms = jnp.mean(x * x, axis=-1, keepdims=True)
    xn_ref[...] = (x * lax.rsqrt(ms + EPS) * g_ref[...]).astype(BF16)

    def col_step(j, carry):
        s = i * nj + j

        @pl.when(s + IN_RING - 1 < total)
        def _():
            w_copy(s + IN_RING - 1).start()

        @pl.when(s >= 2)
        def _():
            o_copy(s - 2).wait()

        w_copy(s).wait()
        ob_ref[s % 2] = jnp.dot(xn_ref[...], w_ref[s % IN_RING].astype(BF16), preferred_element_type=F32)
        o_copy(s).start()
        return carry

    lax.fori_loop(0, nj, col_step, 0)

    @pl.when(i == pl.num_programs(0) - 1)
    def _():
        o_copy(total - 2).wait()
        o_copy(total - 1).wait()


def _inproj(x, g, w):
    return pl.pallas_call(
        _inproj_kernel,
        grid=(SEQ // TM_IN,),
        in_specs=[
            pl.BlockSpec((TM_IN, D_MODEL), lambda i: (i, 0)),
            pl.BlockSpec((1, D_MODEL), lambda i: (0, 0)),
            pl.BlockSpec(memory_space=pl.ANY),
        ],
        out_specs=pl.BlockSpec(memory_space=pl.ANY),
        out_shape=jax.ShapeDtypeStruct((SEQ, IN_W), F32),
        scratch_shapes=[
            pltpu.VMEM((TM_IN, D_MODEL), BF16),
            pltpu.VMEM((IN_RING, D_MODEL, TN_IN), F32),
            pltpu.VMEM((2, TM_IN, TN_IN), F32),
            pltpu.SemaphoreType.DMA((IN_RING,)),
            pltpu.SemaphoreType.DMA((2,)),
        ],
        compiler_params=_cparams(("arbitrary",)),
        name="inproj",
    )(x, g, w)


def _conv_kernel(val_ref, gate_ref, hval_ref, hgate_ref, cw_ref, cb_ref, lg_ref, lb_ref,
                 o_ref, ubuf, zbuf, ybuf):
    i = pl.program_id(0)
    u = val_ref[...] * jax.nn.sigmoid(gate_ref[...])
    hu = hval_ref[...] * jax.nn.sigmoid(hgate_ref[...])
    hu = jnp.where(i > 0, hu, 0.0)
    for c in range(CONV_C // LANES):
        cols = slice(c * LANES, (c + 1) * LANES)
        ubuf[c, 0:HALO, :] = hu[:, cols]
        ubuf[c, HALO:HALO + TT_CONV, :] = u[:, cols]

    n_a = -(-CONV_K // 8)
    assert HALO == 8 * n_a
    for c in range(CONV_C // LANES):
        cols = slice(c * LANES, (c + 1) * LANES)

        def taps(r, carry, c=c, cols=cols):
            base = pl.multiple_of(r * R_CONV, R_CONV)
            win = ubuf[c, pl.ds(base, R_CONV + HALO), :]
            for b in range(8):
                z = None
                for a in range(n_a):
                    s = 8 * a + b
                    if s >= CONV_K:
                        continue
                    lo = HALO - 8 - 8 * a
                    t = cw_ref[CONV_K - 1 - s:CONV_K - s, cols] * win[lo:lo + R_CONV + 8, :]
                    z = t if z is None else z + t
                zbuf[b, pl.ds(0, R_CONV + 8, stride=2), :] = z
            acc = None
            for b in range(8):
                t = zbuf[b, pl.ds(2 * (8 - b), R_CONV, stride=2), :]
                acc = t if acc is None else acc + t
            ybuf[pl.ds(base, R_CONV), cols] = acc
            return carry

        lax.fori_loop(0, TT_CONV // R_CONV, taps, 0)

    def norm(r, carry):
        rows = pl.ds(pl.multiple_of(r * R_LN, R_LN), R_LN)
        acc = ybuf[rows, :] + cb_ref[...]
        mu = jnp.mean(acc, axis=-1, keepdims=True)
        xc = acc - mu
        var = jnp.mean(xc * xc, axis=-1, keepdims=True)
        y = xc * lax.rsqrt(var + EPS) * lg_ref[...] + lb_ref[...]
        o_ref[rows, :] = (y * jax.nn.sigmoid(y)).astype(BF16)
        return carry

    lax.fori_loop(0, TT_CONV // R_LN, norm, 0, unroll=8)


def _conv_mixer(proj, cw, cb, lg, lb):
    hb = TT_CONV // HALO
    return pl.pallas_call(
        _conv_kernel,
        grid=(SEQ // TT_CONV,),
        in_specs=[
            pl.BlockSpec((TT_CONV, CONV_C), lambda i: (i, 0)),
            pl.BlockSpec((TT_CONV, CONV_C), lambda i: (i, 1)),
            pl.BlockSpec((HALO, CONV_C), lambda i: (jnp.maximum(i * hb - 1, 0), 0)),
            pl.BlockSpec((HALO, CONV_C), lambda i: (jnp.maximum(i * hb - 1, 0), 1)),
            pl.BlockSpec((CONV_K, CONV_C), lambda i: (0, 0)),
            pl.BlockSpec((1, CONV_C), lambda i: (0, 0)),
            pl.BlockSpec((1, CONV_C), lambda i: (0, 0)),
            pl.BlockSpec((1, CONV_C), lambda i: (0, 0)),
        ],
        out_specs=pl.BlockSpec((TT_CONV, CONV_C), lambda i: (i, 0)),
        out_shape=jax.ShapeDtypeStruct((SEQ, CONV_C), BF16),
        scratch_shapes=[pltpu.VMEM((CONV_C // LANES, HALO + TT_CONV, LANES), F32),
                        pltpu.VMEM((8, 2 * (R_CONV + 8), LANES), F32),
                        pltpu.VMEM((TT_CONV, CONV_C), F32)],
        compiler_params=_cparams(("arbitrary",)),
        name="conv_mixer",
    )(proj, proj, proj, proj, cw, cb, lg, lb)


def _t5_bucket_np(dist):
    max_exact = NUM_BUCKETS // 2
    nf = np.maximum(dist, 1).astype(np.float32)
    large = max_exact + (np.log(nf / np.float32(max_exact)) / np.float32(math.log(MAX_DISTANCE / max_exact))
                         * np.float32(NUM_BUCKETS - max_exact)).astype(np.int32)
    large = np.minimum(large, NUM_BUCKETS - 1)
    return np.where(dist < max_exact, dist, large)


def _attn_bias_tables(rel_bias):
    period = 3 * QBLK
    diags = []
    for window, dil in PATTERNS:
        span = window // dil
        assert span <= QBLK
        bucket = _t5_bucket_np(np.arange(span + 1) * dil)
        onehot = np.eye(NUM_BUCKETS, dtype=np.float32)[bucket]
        vec = jnp.einsum("rb,bh->hr", onehot, rel_bias.astype(F32),
                         precision=lax.Precision.HIGHEST)
        diag = jnp.full((N_HEADS, period), NEG_INF, F32)
        diags.append(diag.at[:, 2 * QBLK - 1 - span:2 * QBLK].set(vec[:, ::-1]))
    diag = jnp.stack(diags) * LOG2E
    shifted = jnp.stack([jnp.roll(diag, b, axis=-1) for b in range(8)], axis=-2)
    return shifted.reshape(len(PATTERNS), N_HEADS // 2, 2, 8, period)


def _attn_kernel(q_ref, k_ref, v_ref, qg_ref, kg_ref, diag_ref, o_ref,
                 qn_ref, kn_ref, acc_ref, m_ref, l_ref, bias_s):
    lane = lax.broadcasted_iota(jnp.int32, (QBLK, LANES), 1)
    head_a = lane < HEAD_DIM
    hist_keys = lax.broadcasted_iota(jnp.int32, (2 * QBLK, 2 * QBLK), 1) < QBLK
    ri = lax.broadcasted_iota(jnp.int32, (LANES, LANES), 0) // HEAD_DIM
    ci = lax.broadcasted_iota(jnp.int32, (LANES, LANES), 1) // HEAD_DIM
    seg = (ri == ci).astype(BF16)

    def head_rms(x, g):
        sq = x * x
        hi = sq.astype(BF16)
        lo = (sq - hi.astype(F32)).astype(BF16)
        ss = (jnp.dot(hi, seg, preferred_element_type=F32)
              + jnp.dot(lo, seg, preferred_element_type=F32))
        return x * lax.rsqrt(ss * (1.0 / HEAD_DIM) + EPS) * g

    NCH = 512

    def norm_body(c, carry):
        rows = pl.ds(pl.multiple_of(c * NCH, NCH), NCH)
        qn_ref[rows, :] = head_rms(q_ref[rows, :], qg_ref[...]) * (LOG2E / math.sqrt(HEAD_DIM))
        kn_ref[rows, :] = head_rms(k_ref[rows, :], kg_ref[...])
        return carry

    lax.fori_loop(0, SEQ // NCH, norm_body, 0, unroll=2)

    for p in range(len(PATTERNS)):
        for h in range(2):
            offs = diag_ref[p, h]
            for a in range(QBLK // 8):
                lo = QBLK - 1 - 8 * a
                bias_s[p, h * QBLK + 8 * a:h * QBLK + 8 * a + 8, :] = offs[:, lo:lo + 2 * QBLK]

    order = sorted(range(len(PATTERNS)), key=lambda i: -PATTERNS[i][1])
    assert PATTERNS[order[-1]][1] == 1
    for step, p in enumerate(order):
        dil = PATTERNS[p][1]
        nb = SEQ // (dil * QBLK)
        is_first = step == 0
        is_last = step == len(order) - 1

        def rows(start, dil=dil):
            if dil == 1:
                return pl.ds(start, QBLK)
            return pl.ds(start, QBLK, stride=dil)

        def unit(cur, k_prev, v_prev, first, p=p, rows=rows, is_first=is_first, is_last=is_last):
            q = qn_ref[rows(cur), :]
            q2 = jnp.concatenate([jnp.where(head_a, q, 0.0), jnp.where(head_a, 0.0, q)],
                                 axis=0).astype(BF16)
            k_cur = kn_ref[rows(cur), :].astype(BF16)
            v_cur = v_ref[rows(cur), :].astype(BF16)
            k2 = jnp.concatenate([k_prev, k_cur], axis=0)
            v2 = jnp.concatenate([v_prev, v_cur], axis=0)
            s = lax.dot_general(q2, k2, (((1,), (1,)), ((), ())), preferred_element_type=F32)
            s = s + bias_s[p]
            if first is not None:
                s = jnp.where(hist_keys, jnp.where(first, NEG_INF * LOG2E, s), s)
            m = jnp.max(s, axis=-1, keepdims=True)
            e = jnp.exp2(s - m)
            l = jnp.sum(e, axis=-1, keepdims=True)
            pv = jnp.dot(e.astype(BF16), v2, preferred_element_type=F32)
            o_new = jnp.where(head_a, pv[:QBLK], pv[QBLK:])
            m_new = jnp.where(head_a, m[:QBLK], m[QBLK:])
            l_new = jnp.where(head_a, l[:QBLK], l[QBLK:])
            if is_first:
                acc_ref[rows(cur), :] = o_new
                m_ref[rows(cur), :] = m_new
                l_ref[rows(cur), :] = l_new
            else:
                m_old = m_ref[rows(cur), :]
                m_tot = jnp.maximum(m_old, m_new)
                a = jnp.exp2(m_old - m_tot)
                b = jnp.exp2(m_new - m_tot)
                acc = acc_ref[rows(cur), :] * a + o_new * b
                den = l_ref[rows(cur), :] * a + l_new * b
                if is_last:
                    o_ref[pl.ds(pl.multiple_of(cur, QBLK), QBLK), :] = (acc / den).astype(BF16)
                else:
                    acc_ref[rows(cur), :] = acc
                    l_ref[rows(cur), :] = den
                    m_ref[rows(cur), :] = m_tot
            return k_cur, v_cur

        per = nb // ATTN_GROUP
        assert per * ATTN_GROUP == nb

        def group(g, carry, dil=dil, per=per, rows=rows, unit=unit):
            r = g // per
            n0 = (g - r * per) * ATTN_GROUP
            hist = jnp.maximum(n0 - 1, 0) * (QBLK * dil) + r
            k_prev = kn_ref[rows(hist), :].astype(BF16)
            v_prev = v_ref[rows(hist), :].astype(BF16)
            for i in range(ATTN_GROUP):
                first = (n0 == 0) if i == 0 else None
                k_prev, v_prev = unit((n0 + i) * (QBLK * dil) + r, k_prev, v_prev, first)
            return carry

        lax.fori_loop(0, SEQ // (QBLK * ATTN_GROUP), group, 0, unroll=ATTN_UNROLL)


def _attention(proj, qg2, kg2, bias_tab):
    qoff = 2 * CONV_C // LANES
    koff = qoff + ATTN_W // LANES
    voff = koff + ATTN_W // LANES
    return pl.pallas_call(
        _attn_kernel,
        grid=(N_HEADS // 2,),
        in_specs=[
            pl.BlockSpec((SEQ, LANES), lambda h: (0, qoff + h)),
            pl.BlockSpec((SEQ, LANES), lambda h: (0, koff + h)),
            pl.BlockSpec((SEQ, LANES), lambda h: (0, voff + h)),
            pl.BlockSpec((1, LANES), lambda h: (0, 0)),
            pl.BlockSpec((1, LANES), lambda h: (0, 0)),
            pl.BlockSpec((len(PATTERNS), None, 2, 8, 3 * QBLK), lambda h: (0, h, 0, 0, 0)),
        ],
        out_specs=pl.BlockSpec((SEQ, LANES), lambda h: (0, h)),
        out_shape=jax.ShapeDtypeStruct((SEQ, ATTN_W), BF16),
        scratch_shapes=[pltpu.VMEM((SEQ, LANES), F32) for _ in range(5)]
        + [pltpu.VMEM((len(PATTERNS), 2 * QBLK, 2 * QBLK), F32)],
        compiler_params=_cparams(("arbitrary",)),
        name="dilated_attn",
    )(proj, proj, proj, qg2, kg2, bias_tab)


def _outproj_kernel(x_ref, c_ref, a_ref, wc_ref, wa_ref, o_ref):
    o_ref[...] = (x_ref[...]
                  + jnp.dot(c_ref[...], wc_ref[...].astype(BF16), preferred_element_type=F32)
                  + jnp.dot(a_ref[...], wa_ref[...].astype(BF16), preferred_element_type=F32))


def _outproj(x, conv_out, attn_out, w_out):
    return pl.pallas_call(
        _outproj_kernel,
        grid=(SEQ // TM_OUT,),
        in_specs=[
            pl.BlockSpec((TM_OUT, D_MODEL), lambda i: (i, 0)),
            pl.BlockSpec((TM_OUT, CONV_C), lambda i: (i, 0)),
            pl.BlockSpec((TM_OUT, ATTN_W), lambda i: (i, 0)),
            pl.BlockSpec((CONV_C, D_MODEL), lambda i: (0, 0), pipeline_mode=pl.Buffered(1)),
            pl.BlockSpec((ATTN_W, D_MODEL), lambda i: (1, 0), pipeline_mode=pl.Buffered(1)),
        ],
        out_specs=pl.BlockSpec((TM_OUT, D_MODEL), lambda i: (i, 0)),
        out_shape=jax.ShapeDtypeStruct((SEQ, D_MODEL), F32),
        compiler_params=_cparams(("arbitrary",)),
        name="outproj",
    )(x, conv_out, attn_out, w_out, w_out)


def _split2(a):
    a1 = a.astype(BF16)
    a2 = (a - a1.astype(F32)).astype(BF16)
    return a1, a2


def _router_kernel(x_ref, g_ref, wr_ref, br_ref, eid_ref, wts_ref, rank_ref, cnt_ref):
    i = pl.program_id(0)

    @pl.when(i == 0)
    def _():
        cnt_ref[...] = jnp.zeros_like(cnt_ref)

    x = x_ref[...]
    inv = lax.rsqrt(jnp.mean(x * x, axis=-1, keepdims=True) + EPS)
    dn = (((1,), (1,)), ((), ()))
    lt = None
    for c in range(D_MODEL // KC_R):
        cols = slice(c * KC_R, (c + 1) * KC_R)
        h1, h2 = _split2(x_ref[:, cols] * inv * g_ref[:, cols])
        w1, w2 = _split2(wr_ref[:, cols])
        for wa, ha in ((w1, h1), (w1, h2), (w2, h1)):
            t = lax.dot_general(wa, ha, dn, preferred_element_type=F32)
            lt = t if lt is None else lt + t
    lt = lt + br_ref[:, 0:1]

    row8 = lax.broadcasted_iota(jnp.int32, (8, TM_R), 0)
    gl = jnp.where(row8 < N_GROUPS, lt[0:8], -jnp.inf)
    gmax = jnp.max(gl, axis=0, keepdims=True)
    gidx = jnp.min(jnp.where(gl == gmax, row8, 8), axis=0, keepdims=True)
    gw = 1.0 / jnp.sum(jnp.exp(gl - gmax), axis=0, keepdims=True)

    esel = lt[8:16]
    for g in range(1, N_GROUPS):
        esel = jnp.where(gidx == g, lt[8 + 8 * g:16 + 8 * g], esel)
    v1 = jnp.max(esel, axis=0, keepdims=True)
    i1 = jnp.min(jnp.where(esel == v1, row8, 8), axis=0, keepdims=True)
    rest = jnp.where(row8 == i1, -jnp.inf, esel)
    v2 = jnp.max(rest, axis=0, keepdims=True)
    i2 = jnp.min(jnp.where(rest == v2, row8, 8), axis=0, keepdims=True)
    e21 = jnp.exp(v2 - v1)
    den = 1.0 + e21
    e1 = gidx * E_PER_G + i1
    e2 = gidx * E_PER_G + i2
    eid_ref[0:1, :] = e1
    eid_ref[1:2, :] = e2
    wts_ref[0:1, :] = gw * (1.0 / den)
    wts_ref[1:2, :] = gw * (e21 / den)

    erow = lax.broadcasted_iota(jnp.int32, (N_EXPERTS, TM_R), 0)
    oh1 = erow == e1
    oh2 = erow == e2
    member = jnp.where(oh1 | oh2, 1.0, 0.0)
    ti = lax.broadcasted_iota(jnp.int32, (TM_R, TM_R), 0)
    tj = lax.broadcasted_iota(jnp.int32, (TM_R, TM_R), 1)
    upper = jnp.where(ti < tj, 1.0, 0.0).astype(BF16)
    before = jnp.dot(member.astype(BF16), upper, preferred_element_type=F32)
    pos = before + cnt_ref[:, 0:1]
    rank_ref[0:1, :] = jnp.sum(jnp.where(oh1, pos, 0.0), axis=0, keepdims=True).astype(jnp.int32)
    rank_ref[1:2, :] = jnp.sum(jnp.where(oh2, pos, 0.0), axis=0, keepdims=True).astype(jnp.int32)
    cnt_ref[...] = cnt_ref[...] + jnp.sum(member, axis=1, keepdims=True)


def _router(x1, g2, wr_t, br):
    return pl.pallas_call(
        _router_kernel,
        grid=(SEQ // TM_R,),
        in_specs=[
            pl.BlockSpec((TM_R, D_MODEL), lambda i: (i, 0)),
            pl.BlockSpec((1, D_MODEL), lambda i: (0, 0)),
            pl.BlockSpec((R_ROWS, D_MODEL), lambda i: (0, 0)),
            pl.BlockSpec((R_ROWS, LANES), lambda i: (0, 0)),
        ],
        out_specs=[
            pl.BlockSpec((2, TM_R), lambda i: (0, i)),
            pl.BlockSpec((2, TM_R), lambda i: (0, i)),
            pl.BlockSpec((2, TM_R), lambda i: (0, i)),
            pl.BlockSpec((N_EXPERTS, LANES), lambda i: (0, 0)),
        ],
        out_shape=[
            jax.ShapeDtypeStruct((2, SEQ), jnp.int32),
            jax.ShapeDtypeStruct((2, SEQ), F32),
            jax.ShapeDtypeStruct((2, SEQ), jnp.int32),
            jax.ShapeDtypeStruct((N_EXPERTS, LANES), F32),
        ],
        compiler_params=_cparams(("arbitrary",)),
        name="router",
    )(x1, g2, wr_t, br)


def _dispatch_kernel(slot_ref, padrow_ref, x_ref, g_ref, xs_hbm, hbuf, zbuf, sem, zsem):
    i = pl.program_id(0)
    last = pl.num_programs(0) - 1
    buf = i % 2

    def tile_wait(b):
        for _ in range(2):
            pltpu.make_async_copy(hbuf.at[b], xs_hbm.at[pl.ds(0, TT_D), :], sem.at[b]).wait()

    @pl.when(i == 0)
    def _():
        zbuf[...] = jnp.zeros_like(zbuf)

        for c in range(XS_TAIL // ZROWS):
            pltpu.make_async_copy(zbuf, xs_hbm.at[pl.ds(XS_ROWS - XS_TAIL + c * ZROWS, ZROWS), :], zsem).start()
        for c in range(XS_TAIL // ZROWS):
            pltpu.make_async_copy(zbuf, xs_hbm.at[pl.ds(0, ZROWS), :], zsem).wait()

        def pad(e, c):
            dst = pl.multiple_of(padrow_ref[e], 8)
            pltpu.make_async_copy(zbuf.at[pl.ds(0, 8), :], xs_hbm.at[pl.ds(dst, 8), :], zsem).start()
            return c
        lax.fori_loop(0, N_EXPERTS, pad, 0)

        def padw(e, c):
            pltpu.make_async_copy(zbuf.at[pl.ds(0, 8), :], xs_hbm.at[pl.ds(0, 8), :], zsem).wait()
            return c
        lax.fori_loop(0, N_EXPERTS, padw, 0)

    @pl.when(i >= 2)
    def _():
        tile_wait(buf)

    x = x_ref[...]
    ms = jnp.mean(x * x, axis=-1, keepdims=True)
    hbuf[buf] = x * lax.rsqrt(ms + EPS) * g_ref[...]

    tok0 = i * TT_D

    def send(j8, c):
        for k8 in range(8):
            j = j8 * 8 + k8
            for k in range(2):
                dst = slot_ref[k * SEQ + tok0 + j]
                pltpu.make_async_copy(hbuf.at[buf, pl.ds(j, 1), :], xs_hbm.at[pl.ds(dst, 1), :],
                                      sem.at[buf]).start()
        return c
    lax.fori_loop(0, TT_D // 8, send, 0)

    @pl.when(i == last)
    def _():
        @pl.when(i >= 1)
        def _():
            tile_wait(1 - buf)
        tile_wait(buf)


def _dispatch(slot_flat, padrow, x1, g2):
    grid_spec = pltpu.PrefetchScalarGridSpec(
        num_scalar_prefetch=2,
        grid=(SEQ // TT_D,),
        in_specs=[
            pl.BlockSpec((TT_D, D_MODEL), lambda i, *_: (i, 0)),
            pl.BlockSpec((1, D_MODEL), lambda i, *_: (0, 0)),
        ],
        out_specs=pl.BlockSpec(memory_space=pl.ANY),
        scratch_shapes=[
            pltpu.VMEM((2, TT_D, D_MODEL), F32),
            pltpu.VMEM((ZROWS, D_MODEL), F32),
            pltpu.SemaphoreType.DMA((2,)),
            pltpu.SemaphoreType.DMA,
        ],
    )
    return pl.pallas_call(
        _dispatch_kernel,
        grid_spec=grid_spec,
        out_shape=jax.ShapeDtypeStruct((XS_ROWS, D_MODEL), F32),
        compiler_params=_cparams(("arbitrary",)),
        name="moe_dispatch",
    )(slot_flat, padrow, x1, g2)


def _experts_kernel(ie_ref, row0_ref, rows_ref, nitems_ref,
                    xs_hbm, wg_hbm, wu_hbm, wd_hbm, ys_hbm,
                    xg_ref, y_ref, zbuf, wg_ref, wu_ref, wd_ref, gsem, ssem, zsem, wsem):
    it = pl.program_id(0)
    nitems = nitems_ref[0]
    slot = it % 2
    half = SUB_E // 2
    nsteps = nitems * NF_E

    def weight_copies(s):
        e = ie_ref[s // NF_E]
        cols = pl.ds(pl.multiple_of((s % NF_E) * TF_E, TF_E), TF_E)
        b = s % W_RING
        return (pltpu.make_async_copy(wg_hbm.at[e, :, cols], wg_ref.at[b], wsem.at[b]),
                pltpu.make_async_copy(wu_hbm.at[e, :, cols], wu_ref.at[b], wsem.at[b]),
                pltpu.make_async_copy(wd_hbm.at[e, cols, :], wd_ref.at[b], wsem.at[b]))

    @pl.when(it == 0)
    def _():
        for s in range(W_RING - 1):
            for cp in weight_copies(s):
                cp.start()

    def padded(item):
        return pl.multiple_of(((rows_ref[item] + half - 1) // half) * half, half)

    nhalf = padded(it) // half
    nsub = nhalf // 2

    def chunk_copies(item, start):
        base = pl.multiple_of(row0_ref[item], 8)
        buf = item % 2

        def body(c, carry):
            off = pl.multiple_of(c * half, half)
            src = xs_hbm.at[pl.ds(pl.multiple_of(base + off, 8), half), :]
            dst = ys_hbm.at[pl.ds(pl.multiple_of(base + off, 8), half), :]
            if start == "fetch":
                pltpu.make_async_copy(src, xg_ref.at[buf, pl.ds(off, half), :], gsem.at[buf]).start()
            else:
                pltpu.make_async_copy(y_ref.at[buf, pl.ds(off, half), :], dst, ssem.at[buf]).start()
            return carry
        lax.fori_loop(0, padded(item) // half, body, 0)

    def wait_fetch(item):
        got = pl.ds(0, padded(item))
        buf = item % 2
        pltpu.make_async_copy(xs_hbm.at[got, :], xg_ref.at[buf, got, :], gsem.at[buf]).wait()

    def wait_store(item):
        put = pl.ds(0, padded(item))
        buf = item % 2
        pltpu.make_async_copy(y_ref.at[buf, put, :], ys_hbm.at[put, :], ssem.at[buf]).wait()

    @pl.when(it == 0)
    def _():
        zbuf[...] = jnp.zeros_like(zbuf)
        for c in range(XS_TAIL // ZROWS):
            pltpu.make_async_copy(zbuf, ys_hbm.at[pl.ds(XS_ROWS - XS_TAIL + c * ZROWS, ZROWS), :], zsem).start()
        for c in range(XS_TAIL // ZROWS):
            pltpu.make_async_copy(zbuf, ys_hbm.at[pl.ds(0, ZROWS), :], zsem).wait()
        chunk_copies(0, "fetch")

    wait_fetch(it)

    def clear(s, c):
        rows = pl.ds(pl.multiple_of(s * half, half), half)
        y_ref[slot, rows, :] = jnp.zeros((half, D_MODEL), F32)
        return c
    lax.fori_loop(0, nhalf, clear, 0)

    @pl.when(it + 1 < nitems)
    def _():
        chunk_copies(it + 1, "fetch")

    def chunk_step(f, carry):
        step = it * NF_E + f
        wslot = step % W_RING

        @pl.when(step + W_RING - 1 < nsteps)
        def _():
            for cp in weight_copies(step + W_RING - 1):
                cp.start()

        for cp in weight_copies(step):
            cp.wait()

        def block(rows):
            xb = xg_ref[slot, rows, :].astype(BF16)
            hg = jnp.dot(xb, wg_ref[wslot].astype(BF16), preferred_element_type=F32)
            hu = jnp.dot(xb, wu_ref[wslot].astype(BF16), preferred_element_type=F32)
            h = (hg * jax.nn.sigmoid(hg) * hu).astype(BF16)
            y_ref[slot, rows, :] = y_ref[slot, rows, :] + jnp.dot(h, wd_ref[wslot].astype(BF16),
                                                                   preferred_element_type=F32)

        def sub(s, c):
            block(pl.ds(pl.multiple_of(s * SUB_E, SUB_E), SUB_E))
            return c
        lax.fori_loop(0, nsub, sub, 0)

        @pl.when(nhalf % 2 == 1)
        def _():
            block(pl.ds(pl.multiple_of(nsub * SUB_E, SUB_E), half))
        return carry

    lax.fori_loop(0, NF_E, chunk_step, 0)

    @pl.when(it > 0)
    def _():
        wait_store(it - 1)

    chunk_copies(it, "store")

    @pl.when(it + 1 >= nitems)
    def _():
        wait_store(it)


def _experts(ie, row0, rows, nitems, xs, wg, wu, wd):
    grid_spec = pltpu.PrefetchScalarGridSpec(
        num_scalar_prefetch=4,
        grid=(nitems[0],),
        in_specs=[pl.BlockSpec(memory_space=pl.ANY) for _ in range(4)],
        out_specs=pl.BlockSpec(memory_space=pl.ANY),
        scratch_shapes=[
            pltpu.VMEM((2, TM_E, D_MODEL), F32),
            pltpu.VMEM((2, TM_E, D_MODEL), F32),
            pltpu.VMEM((ZROWS, D_MODEL), F32),
            pltpu.VMEM((W_RING, D_MODEL, TF_E), F32),
            pltpu.VMEM((W_RING, D_MODEL, TF_E), F32),
            pltpu.VMEM((W_RING, TF_E, D_MODEL), F32),
            pltpu.SemaphoreType.DMA((2,)),
            pltpu.SemaphoreType.DMA((2,)),
            pltpu.SemaphoreType.DMA,
            pltpu.SemaphoreType.DMA((W_RING,)),
        ],
    )
    return pl.pallas_call(
        _experts_kernel,
        grid_spec=grid_spec,
        out_shape=jax.ShapeDtypeStruct((XS_ROWS, D_MODEL), F32),
        compiler_params=_cparams(("arbitrary",)),
        name="moe_experts",
    )(ie, row0, rows, nitems, xs, wg, wu, wd)


def _gcombine_kernel(slot_ref, x_ref, w_ref, ys_hbm, o_ref, dbuf, sem):
    i = pl.program_id(0)
    n = pl.num_programs(0)
    buf = i % 2

    def fetch(tile, b):
        tok0 = tile * TT_G

        def body(j8, c):
            for k8 in range(8):
                j = j8 * 8 + k8
                for k in range(2):
                    src = slot_ref[k * SEQ + tok0 + j]
                    pltpu.make_async_copy(ys_hbm.at[pl.ds(src, 1), :], dbuf.at[b, k, pl.ds(j, 1), :],
                                          sem.at[b]).start()
            return c
        lax.fori_loop(0, TT_G // 8, body, 0)

    @pl.when(i == 0)
    def _():
        fetch(0, 0)

    @pl.when(i + 1 < n)
    def _():
        fetch(i + 1, 1 - buf)

    for k in range(2):
        pltpu.make_async_copy(ys_hbm.at[pl.ds(0, TT_G), :], dbuf.at[buf, k], sem.at[buf]).wait()

    w = w_ref[...]
    o_ref[...] = x_ref[...] + w[:, 0:1] * dbuf[buf, 0] + w[:, 1:2] * dbuf[buf, 1]


def _gcombine(slot_flat, x1, wts_t, ys):
    grid_spec = pltpu.PrefetchScalarGridSpec(
        num_scalar_prefetch=1,
        grid=(SEQ // TT_G,),
        in_specs=[
            pl.BlockSpec((TT_G, D_MODEL), lambda i, *_: (i, 0)),
            pl.BlockSpec((TT_G, 2), lambda i, *_: (i, 0)),
            pl.BlockSpec(memory_space=pl.ANY),
        ],
        out_specs=pl.BlockSpec((TT_G, D_MODEL), lambda i, *_: (i, 0)),
        scratch_shapes=[
            pltpu.VMEM((2, 2, TT_G, D_MODEL), F32),
            pltpu.SemaphoreType.DMA((2,)),
        ],
    )
    return pl.pallas_call(
        _gcombine_kernel,
        grid_spec=grid_spec,
        out_shape=jax.ShapeDtypeStruct((SEQ, D_MODEL), F32),
        compiler_params=_cparams(("arbitrary",)),
        name="moe_combine",
    )(slot_flat, x1, wts_t, ys)


def _routing_tables(eid, rank, counts_f):
    counts = counts_f[:, 0].astype(jnp.int32)
    aligned = ((counts + 7) // 8) * 8
    base = jnp.cumsum(aligned) - aligned
    tiles = (counts + TM_E - 1) // TM_E
    tcum = jnp.cumsum(tiles)
    tstart = tcum - tiles
    nitems = tcum[-1]
    ids = jnp.arange(MAX_ITEMS, dtype=jnp.int32)
    ie = jnp.minimum(jnp.sum(ids[:, None] >= tcum[None, :], axis=1), N_EXPERTS - 1).astype(jnp.int32)
    live = ids < nitems
    sel = ie[:, None] == jnp.arange(N_EXPERTS, dtype=jnp.int32)[None, :]

    def pick(v):
        return jnp.sum(jnp.where(sel, v[None, :], 0), axis=1)

    jt = ids - pick(tstart)
    row0 = jnp.where(live, pick(base) + jt * TM_E, 0)
    rows = jnp.where(live, jnp.clip(pick(counts) - jt * TM_E, 0, TM_E), 0)
    eoh = eid[:, :, None] == jnp.arange(N_EXPERTS, dtype=jnp.int32)
    slot = jnp.sum(jnp.where(eoh, base, 0), axis=-1) + rank
    spare = XS_ROWS - 8 * (1 + jnp.arange(N_EXPERTS, dtype=jnp.int32))
    padrow = jnp.where(counts % 8 != 0, base + (counts // 8) * 8, spare)
    return (ie, row0.astype(jnp.int32), rows.astype(jnp.int32), nitems.reshape(1).astype(jnp.int32),
            slot.reshape(-1).astype(jnp.int32), padrow.astype(jnp.int32))


def kernel(x, norm1_g, w_in, q_norm_g, k_norm_g, conv_w, conv_b, conv_ln_g, conv_ln_b, rel_bias,
           w_out, norm2_g, w_router_group, b_router_group, w_router_expert, b_router_expert,
           w_gate, w_up, w_down):
    assert x.shape == (1, SEQ, D_MODEL) and w_in.shape[0] == 1
    xs = x[0]
    bias_tab = _attn_bias_tables(rel_bias)
    qg2 = jnp.tile(q_norm_g[0], 2)[None]
    kg2 = jnp.tile(k_norm_g[0], 2)[None]

    proj = _inproj(xs, norm1_g[0][None], w_in[0])
    conv_out = _conv_mixer(proj, conv_w[0], conv_b[0][None], conv_ln_g[0][None], conv_ln_b[0][None])
    attn_out = _attention(proj, qg2, kg2, bias_tab)
    x1 = _outproj(xs, conv_out, attn_out, w_out[0])

    wr_t = jnp.concatenate([
        w_router_group[0].T, jnp.zeros((8 - N_GROUPS, D_MODEL), F32),
        jnp.transpose(w_router_expert[0], (0, 2, 1)).reshape(N_EXPERTS, D_MODEL)], axis=0)
    br = jnp.concatenate([b_router_group[0], jnp.zeros((8 - N_GROUPS,), F32),
                          b_router_expert[0].reshape(-1)])
    br = jnp.broadcast_to(br[:, None], (R_ROWS, LANES))
    eid, wts, rank, counts_f = _router(x1, norm2_g[0][None], wr_t, br)

    ie, row0, rows, nitems, slot_flat, padrow = _routing_tables(eid, rank, counts_f)
    xs = _dispatch(slot_flat, padrow, x1, norm2_g[0][None])
    ys = _experts(ie, row0, rows, nitems, xs,
                  w_gate[0].reshape(N_EXPERTS, D_MODEL, D_FF),
                  w_up[0].reshape(N_EXPERTS, D_MODEL, D_FF),
                  w_down[0].reshape(N_EXPERTS, D_FF, D_MODEL))
    out = _gcombine(slot_flat, x1, wts.T, ys)
    return out[None]
```

```python
import math

import numpy as np
import jax
import jax.numpy as jnp
from jax import lax
from jax.experimental import pallas as pl
from jax.experimental.pallas import tpu as pltpu

F32 = jnp.float32
BF16 = jnp.bfloat16

D_MODEL = 2048
SEQ = 8192
N_HEADS = 16
HEAD_DIM = 64
ATTN_W = N_HEADS * HEAD_DIM
CONV_C = D_MODEL - ATTN_W
CONV_K = 31
IN_W = 2 * CONV_C + 3 * ATTN_W
PATTERNS = ((128, 1), (512, 4), (2048, 16))
QBLK = 128
NUM_BUCKETS = 32
MAX_DISTANCE = 2048
N_GROUPS = 4
E_PER_G = 8
N_EXPERTS = N_GROUPS * E_PER_G
D_FF = D_MODEL // 2
EPS = 1e-6
NEG_INF = -1e30
LOG2E = math.log2(math.e)

LANES = 128
VMEM_LIMIT = 56 * 1024 * 1024

TM_IN = 1024
TN_IN = 512
IN_RING = 4
TT_CONV = 512
HALO = 32
R_CONV = 64
R_LN = 16
ATTN_GROUP = 4
ATTN_UNROLL = 4
TM_OUT = 512
TM_R = 512
R_ROWS = 8 + N_EXPERTS
KC_R = 512
TM_E = 768
SUB_E = 256
TF_E = 256
NF_E = D_FF // TF_E
W_RING = 4
N_ASSIGN = 2 * SEQ
MAX_ITEMS = -(-N_ASSIGN // TM_E) + N_EXPERTS
XS_TAIL = 256 + TM_E
XS_ROWS = N_ASSIGN + XS_TAIL
ZROWS = 128
TT_D = 512
TT_G = 512


def _cparams(sem, vmem=VMEM_LIMIT):
    return pltpu.CompilerParams(dimension_semantics=sem, vmem_limit_bytes=vmem)


def _inproj_kernel(x_ref, g_ref, w_hbm, o_hbm, xn_ref, w_ref, ob_ref, wsem, osem):
    i = pl.program_id(0)
    nj = IN_W // TN_IN
    total = pl.num_programs(0) * nj

    def w_copy(s):
        cols = pl.ds(pl.multiple_of((s % nj) * TN_IN, TN_IN), TN_IN)
        b = s % IN_RING
        return pltpu.make_async_copy(w_hbm.at[:, cols], w_ref.at[b], wsem.at[b])

    def o_copy(s):
        rows = pl.ds(pl.multiple_of((s // nj) * TM_IN, TM_IN), TM_IN)
        cols = pl.ds(pl.multiple_of((s % nj) * TN_IN, TN_IN), TN_IN)
        b = s % 2
        return pltpu.make_async_copy(ob_ref.at[b], o_hbm.at[rows, cols], osem.at[b])

    @pl.when(i == 0)
    def _():
        for s in range(IN_RING - 1):
            w_copy(s).start()

    x = x_ref[...]
    ms = jnp.mean(x * x, axis=-1, keepdims=True)
    xn_ref[...] = (x * lax.rsqrt(ms + EPS) * g_ref[...]).astype(BF16)

    def col_step(j, carry):
        s = i * nj + j

        @pl.when(s + IN_RING - 1 < total)
        def _():
            w_copy(s + IN_RING - 1).start()

        @pl.when(s >= 2)
        def _():
            o_copy(s - 2).wait()

        w_copy(s).wait()
        ob_ref[s % 2] = jnp.dot(xn_ref[...], w_ref[s % IN_RING].astype(BF16), preferred_element_type=F32)
        o_copy(s).start()
        return carry

    lax.fori_loop(0, nj, col_step, 0)

    @pl.when(i == pl.num_programs(0) - 1)
    def _():
        o_copy(total - 2).wait()
        o_copy(total - 1).wait()


def _inproj(x, g, w):
    return pl.pallas_call(
        _inproj_kernel,
        grid=(SEQ // TM_IN,),
        in_specs=[
            pl.BlockSpec((TM_IN, D_MODEL), lambda i: (i, 0)),
            pl.BlockSpec((1, D_MODEL), lambda i: (0, 0)),
            pl.BlockSpec(memory_space=pl.ANY),
        ],
        out_specs=pl.BlockSpec(memory_space=pl.ANY),
        out_shape=jax.ShapeDtypeStruct((SEQ, IN_W), F32),
        scratch_shapes=[
            pltpu.VMEM((TM_IN, D_MODEL), BF16),
            pltpu.VMEM((IN_RING, D_MODEL, TN_IN), F32),
            pltpu.VMEM((2, TM_IN, TN_IN), F32),
            pltpu.SemaphoreType.DMA((IN_RING,)),
            pltpu.SemaphoreType.DMA((2,)),
        ],
        compiler_params=_cparams(("arbitrary",)),
        name="inproj",
    )(x, g, w)


def _conv_kernel(val_ref, gate_ref, hval_ref, hgate_ref, cw_ref, cb_ref, lg_ref, lb_ref,
                 o_ref, ubuf, zbuf, ybuf):
    i = pl.program_id(0)
    u = val_ref[...] * jax.nn.sigmoid(gate_ref[...])
    hu = hval_ref[...] * jax.nn.sigmoid(hgate_ref[...])
    hu = jnp.where(i > 0, hu, 0.0)
    for c in range(CONV_C // LANES):
        cols = slice(c * LANES, (c + 1) * LANES)
        ubuf[c, 0:HALO, :] = hu[:, cols]
        ubuf[c, HALO:HALO + TT_CONV, :] = u[:, cols]

    n_a = -(-CONV_K // 8)
    assert HALO == 8 * n_a
    for c in range(CONV_C // LANES):
        cols = slice(c * LANES, (c + 1) * LANES)

        def taps(r, carry, c=c, cols=cols):
            base = pl.multiple_of(r * R_CONV, R_CONV)
            win = ubuf[c, pl.ds(base, R_CONV + HALO), :]
            for b in range(8):
                z = None
                for a in range(n_a):
                    s = 8 * a + b
                    if s >= CONV_K:
                        continue
                    lo = HALO - 8 - 8 * a
                    t = cw_ref[CONV_K - 1 - s:CONV_K - s, cols] * win[lo:lo + R_CONV + 8, :]
                    z = t if z is None else z + t
                zbuf[b, pl.ds(0, R_CONV + 8, stride=2), :] = z
            acc = None
            for b in range(8):
                t = zbuf[b, pl.ds(2 * (8 - b), R_CONV, stride=2), :]
                acc = t if acc is None else acc + t
            ybuf[pl.ds(base, R_CONV), cols] = acc
            return carry

        lax.fori_loop(0, TT_CONV // R_CONV, taps, 0)

    def norm(r, carry):
        rows = pl.ds(pl.multiple_of(r * R_LN, R_LN), R_LN)
        acc = ybuf[rows, :] + cb_ref[...]
        mu = jnp.mean(acc, axis=-1, keepdims=True)
        xc = acc - mu
        var = jnp.mean(xc * xc, axis=-1, keepdims=True)
        y = xc * lax.rsqrt(var + EPS) * lg_ref[...] + lb_ref[...]
        o_ref[rows, :] = (y * jax.nn.sigmoid(y)).astype(BF16)
        return carry

    lax.fori_loop(0, TT_CONV // R_LN, norm, 0, unroll=8)


def _conv_mixer(proj, cw, cb, lg, lb):
    hb = TT_CONV // HALO
    return pl.pallas_call(
        _conv_kernel,
        grid=(SEQ // TT_CONV,),
        in_specs=[
            pl.BlockSpec((TT_CONV, CONV_C), lambda i: (i, 0)),
            pl.BlockSpec((TT_CONV, CONV_C), lambda i: (i, 1)),
            pl.BlockSpec((HALO, CONV_C), lambda i: (jnp.maximum(i * hb - 1, 0), 0)),
            pl.BlockSpec((HALO, CONV_C), lambda i: (jnp.maximum(i * hb - 1, 0), 1)),
            pl.BlockSpec((CONV_K, CONV_C), lambda i: (0, 0)),
            pl.BlockSpec((1, CONV_C), lambda i: (0, 0)),
            pl.BlockSpec((1, CONV_C), lambda i: (0, 0)),
            pl.BlockSpec((1, CONV_C), lambda i: (0, 0)),
        ],
        out_specs=pl.BlockSpec((TT_CONV, CONV_C), lambda i: (i, 0)),
        out_shape=jax.ShapeDtypeStruct((SEQ, CONV_C), BF16),
        scratch_shapes=[pltpu.VMEM((CONV_C // LANES, HALO + TT_CONV, LANES), F32),
                        pltpu.VMEM((8, 2 * (R_CONV + 8), LANES), F32),
                        pltpu.VMEM((TT_CONV, CONV_C), F32)],
        compiler_params=_cparams(("arbitrary",)),
        name="conv_mixer",
    )(proj, proj, proj, proj, cw, cb, lg, lb)


def _t5_bucket_np(dist):
    max_exact = NUM_BUCKETS // 2
    nf = np.maximum(dist, 1).astype(np.float32)
    large = max_exact + (np.log(nf / np.float32(max_exact)) / np.float32(math.log(MAX_DISTANCE / max_exact))
                         * np.float32(NUM_BUCKETS - max_exact)).astype(np.int32)
    large = np.minimum(large, NUM_BUCKETS - 1)
    return np.where(dist < max_exact, dist, large)


def _attn_bias_tables(rel_bias):
    period = 3 * QBLK
    diags = []
    for window, dil in PATTERNS:
        span = window // dil
        assert span <= QBLK
        bucket = _t5_bucket_np(np.arange(span + 1) * dil)
        onehot = np.eye(NUM_BUCKETS, dtype=np.float32)[bucket]
        vec = jnp.einsum("rb,bh->hr", onehot, rel_bias.astype(F32),
                         precision=lax.Precision.HIGHEST)
        diag = jnp.full((N_HEADS, period), NEG_INF, F32)
        diags.append(diag.at[:, 2 * QBLK - 1 - span:2 * QBLK].set(vec[:, ::-1]))
    diag = jnp.stack(diags) * LOG2E
    shifted = jnp.stack([jnp.roll(diag, b, axis=-1) for b in range(8)], axis=-2)
    return shifted.reshape(len(PATTERNS), N_HEADS // 2, 2, 8, period)


def _attn_kernel(q_ref, k_ref, v_ref, qg_ref, kg_ref, diag_ref, o_ref,
                 qn_ref, kn_ref, acc_ref, m_ref, l_ref, bias_s):
    lane = lax.broadcasted_iota(jnp.int32, (QBLK, LANES), 1)
    head_a = lane < HEAD_DIM
    hist_keys = lax.broadcasted_iota(jnp.int32, (2 * QBLK, 2 * QBLK), 1) < QBLK
    ri = lax.broadcasted_iota(jnp.int32, (LANES, LANES), 0) // HEAD_DIM
    ci = lax.broadcasted_iota(jnp.int32, (LANES, LANES), 1) // HEAD_DIM
    seg = (ri == ci).astype(BF16)

    def head_rms(x, g):
        sq = x * x
        hi = sq.astype(BF16)
        lo = (sq - hi.astype(F32)).astype(BF16)
        ss = (jnp.dot(hi, seg, preferred_element_type=F32)
              + jnp.dot(lo, seg, preferred_element_type=F32))
        return x * lax.rsqrt(ss * (1.0 / HEAD_DIM) + EPS) * g

    NCH = 512

    def norm_body(c, carry):
        rows = pl.ds(pl.multiple_of(c * NCH, NCH), NCH)
        qn_ref[rows, :] = head_rms(q_ref[rows, :], qg_ref[...]) * (LOG2E / math.sqrt(HEAD_DIM))
        kn_ref[rows, :] = head_rms(k_ref[rows, :], kg_ref[...])
        return carry

    lax.fori_loop(0, SEQ // NCH, norm_body, 0, unroll=2)

    for p in range(len(PATTERNS)):
        for h in range(2):
            offs = diag_ref[p, h]
            for a in range(QBLK // 8):
                lo = QBLK - 1 - 8 * a
                bias_s[p, h * QBLK + 8 * a:h * QBLK + 8 * a + 8, :] = offs[:, lo:lo + 2 * QBLK]

    order = sorted(range(len(PATTERNS)), key=lambda i: -PATTERNS[i][1])
    assert PATTERNS[order[-1]][1] == 1
    for step, p in enumerate(order):
        dil = PATTERNS[p][1]
        nb = SEQ // (dil * QBLK)
        is_first = step == 0
        is_last = step == len(order) - 1

        def rows(start, dil=dil):
            if dil == 1:
                return pl.ds(start, QBLK)
            return pl.ds(start, QBLK, stride=dil)

        def unit(cur, k_prev, v_prev, first, p=p, rows=rows, is_first=is_first, is_last=is_last):
            q = qn_ref[rows(cur), :]
            q2 = jnp.concatenate([jnp.where(head_a, q, 0.0), jnp.where(head_a, 0.0, q)],
                                 axis=0).astype(BF16)
            k_cur = kn_ref[rows(cur), :].astype(BF16)
            v_cur = v_ref[rows(cur), :].astype(BF16)
            k2 = jnp.concatenate([k_prev, k_cur], axis=0)
            v2 = jnp.concatenate([v_prev, v_cur], axis=0)
            s = lax.dot_general(q2, k2, (((1,), (1,)), ((), ())), preferred_element_type=F32)
            s = s + bias_s[p]
            if first is not None:
                s = jnp.where(hist_keys, jnp.where(first, NEG_INF * LOG2E, s), s)
            m = jnp.max(s, axis=-1, keepdims=True)
            e = jnp.exp2(s - m)
            l = jnp.sum(e, axis=-1, keepdims=True)
            pv = jnp.dot(e.astype(BF16), v2, preferred_element_type=F32)
            o_new = jnp.where(head_a, pv[:QBLK], pv[QBLK:])
            m_new = jnp.where(head_a, m[:QBLK], m[QBLK:])
            l_new = jnp.where(head_a, l[:QBLK], l[QBLK:])
            if is_first:
                acc_ref[rows(cur), :] = o_new
                m_ref[rows(cur), :] = m_new
                l_ref[rows(cur), :] = l_new
            else:
                m_old = m_ref[rows(cur), :]
                m_tot = jnp.maximum(m_old, m_new)
                a = jnp.exp2(m_old - m_tot)
                b = jnp.exp2(m_new - m_tot)
                acc = acc_ref[rows(cur), :] * a + o_new * b
                den = l_ref[rows(cur), :] * a + l_new * b
                if is_last:
                    o_ref[pl.ds(pl.multiple_of(cur, QBLK), QBLK), :] = (acc / den).astype(BF16)
                else:
                    acc_ref[rows(cur), :] = acc
                    l_ref[rows(cur), :] = den
                    m_ref[rows(cur), :] = m_tot
            return k_cur, v_cur

        per = nb // ATTN_GROUP
        assert per * ATTN_GROUP == nb

        def group(g, carry, dil=dil, per=per, rows=rows, unit=unit):
            r = g // per
            n0 = (g - r * per) * ATTN_GROUP
            hist = jnp.maximum(n0 - 1, 0) * (QBLK * dil) + r
            k_prev = kn_ref[rows(hist), :].astype(BF16)
            v_prev = v_ref[rows(hist), :].astype(BF16)
            for i in range(ATTN_GROUP):
                first = (n0 == 0) if i == 0 else None
                k_prev, v_prev = unit((n0 + i) * (QBLK * dil) + r, k_prev, v_prev, first)
            return carry

        lax.fori_loop(0, SEQ // (QBLK * ATTN_GROUP), group, 0, unroll=ATTN_UNROLL)


def _attention(proj, qg2, kg2, bias_tab):
    qoff = 2 * CONV_C // LANES
    koff = qoff + ATTN_W // LANES
    voff = koff + ATTN_W // LANES
    return pl.pallas_call(
        _attn_kernel,
        grid=(N_HEADS // 2,),
        in_specs=[
            pl.BlockSpec((SEQ, LANES), lambda h: (0, qoff + h)),
            pl.BlockSpec((SEQ, LANES), lambda h: (0, koff + h)),
            pl.BlockSpec((SEQ, LANES), lambda h: (0, voff + h)),
            pl.BlockSpec((1, LANES), lambda h: (0, 0)),
            pl.BlockSpec((1, LANES), lambda h: (0, 0)),
            pl.BlockSpec((len(PATTERNS), None, 2, 8, 3 * QBLK), lambda h: (0, h, 0, 0, 0)),
        ],
        out_specs=pl.BlockSpec((SEQ, LANES), lambda h: (0, h)),
        out_shape=jax.ShapeDtypeStruct((SEQ, ATTN_W), BF16),
        scratch_shapes=[pltpu.VMEM((SEQ, LANES), F32) for _ in range(5)]
        + [pltpu.VMEM((len(PATTERNS), 2 * QBLK, 2 * QBLK), F32)],
        compiler_params=_cparams(("arbitrary",)),
        name="dilated_attn",
    )(proj, proj, proj, qg2, kg2, bias_tab)


def _outproj_kernel(x_ref, c_ref, a_ref, wc_ref, wa_ref, o_ref):
    o_ref[...] = (x_ref[...]
                  + jnp.dot(c_ref[...], wc_ref[...].astype(BF16), preferred_element_type=F32)
                  + jnp.dot(a_ref[...], wa_ref[...].astype(BF16), preferred_element_type=F32))


def _outproj(x, conv_out, attn_out, w_out):
    return pl.pallas_call(
        _outproj_kernel,
        grid=(SEQ // TM_OUT,),
        in_specs=[
            pl.BlockSpec((TM_OUT, D_MODEL), lambda i: (i, 0)),
            pl.BlockSpec((TM_OUT, CONV_C), lambda i: (i, 0)),
            pl.BlockSpec((TM_OUT, ATTN_W), lambda i: (i, 0)),
            pl.BlockSpec((CONV_C, D_MODEL), lambda i: (0, 0), pipeline_mode=pl.Buffered(1)),
            pl.BlockSpec((ATTN_W, D_MODEL), lambda i: (1, 0), pipeline_mode=pl.Buffered(1)),
        ],
        out_specs=pl.BlockSpec((TM_OUT, D_MODEL), lambda i: (i, 0)),
        out_shape=jax.ShapeDtypeStruct((SEQ, D_MODEL), F32),
        compiler_params=_cparams(("arbitrary",)),
        name="outproj",
    )(x, conv_out, attn_out, w_out, w_out)


def _split2(a):
    a1 = a.astype(BF16)
    a2 = (a - a1.astype(F32)).astype(BF16)
    return a1, a2


def _router_kernel(x_ref, g_ref, wr_ref, br_ref, eid_ref, wts_ref, rank_ref, cnt_ref):
    i = pl.program_id(0)

    @pl.when(i == 0)
    def _():
        cnt_ref[...] = jnp.zeros_like(cnt_ref)

    x = x_ref[...]
    inv = lax.rsqrt(jnp.mean(x * x, axis=-1, keepdims=True) + EPS)
    dn = (((1,), (1,)), ((), ()))
    lt = None
    for c in range(D_MODEL // KC_R):
        cols = slice(c * KC_R, (c + 1) * KC_R)
        h1, h2 = _split2(x_ref[:, cols] * inv * g_ref[:, cols])
        w1, w2 = _split2(wr_ref[:, cols])
        for wa, ha in ((w1, h1), (w1, h2), (w2, h1)):
            t = lax.dot_general(wa, ha, dn, preferred_element_type=F32)
            lt = t if lt is None else lt + t
    lt = lt + br_ref[:, 0:1]

    row8 = lax.broadcasted_iota(jnp.int32, (8, TM_R), 0)
    gl = jnp.where(row8 < N_GROUPS, lt[0:8], -jnp.inf)
    gmax = jnp.max(gl, axis=0, keepdims=True)
    gidx = jnp.min(jnp.where(gl == gmax, row8, 8), axis=0, keepdims=True)
    gw = 1.0 / jnp.sum(jnp.exp(gl - gmax), axis=0, keepdims=True)

    esel = lt[8:16]
    for g in range(1, N_GROUPS):
        esel = jnp.where(gidx == g, lt[8 + 8 * g:16 + 8 * g], esel)
    v1 = jnp.max(esel, axis=0, keepdims=True)
    i1 = jnp.min(jnp.where(esel == v1, row8, 8), axis=0, keepdims=True)
    rest = jnp.where(row8 == i1, -jnp.inf, esel)
    v2 = jnp.max(rest, axis=0, keepdims=True)
    i2 = jnp.min(jnp.where(rest == v2, row8, 8), axis=0, keepdims=True)
    e21 = jnp.exp(v2 - v1)
    den = 1.0 + e21
    e1 = gidx * E_PER_G + i1
    e2 = gidx * E_PER_G + i2
    eid_ref[0:1, :] = e1
    eid_ref[1:2, :] = e2
    wts_ref[0:1, :] = gw * (1.0 / den)
    wts_ref[1:2, :] = gw * (e21 / den)

    erow = lax.broadcasted_iota(jnp.int32, (N_EXPERTS, TM_R), 0)
    oh1 = erow == e1
    oh2 = erow == e2
    member = jnp.where(oh1 | oh2, 1.0, 0.0)
    ti = lax.broadcasted_iota(jnp.int32, (TM_R, TM_R), 0)
    tj = lax.broadcasted_iota(jnp.int32, (TM_R, TM_R), 1)
    upper = jnp.where(ti < tj, 1.0, 0.0).astype(BF16)
    before = jnp.dot(member.astype(BF16), upper, preferred_element_type=F32)
    pos = before + cnt_ref[:, 0:1]
    rank_ref[0:1, :] = jnp.sum(jnp.where(oh1, pos, 0.0), axis=0, keepdims=True).astype(jnp.int32)
    rank_ref[1:2, :] = jnp.sum(jnp.where(oh2, pos, 0.0), axis=0, keepdims=True).astype(jnp.int32)
    cnt_ref[...] = cnt_ref[...] + jnp.sum(member, axis=1, keepdims=True)


def _router(x1, g2, wr_t, br):
    return pl.pallas_call(
        _router_kernel,
        grid=(SEQ // TM_R,),
        in_specs=[
            pl.BlockSpec((TM_R, D_MODEL), lambda i: (i, 0)),
            pl.BlockSpec((1, D_MODEL), lambda i: (0, 0)),
            pl.BlockSpec((R_ROWS, D_MODEL), lambda i: (0, 0)),
            pl.BlockSpec((R_ROWS, LANES), lambda i: (0, 0)),
        ],
        out_specs=[
            pl.BlockSpec((2, TM_R), lambda i: (0, i)),
            pl.BlockSpec((2, TM_R), lambda i: (0, i)),
            pl.BlockSpec((2, TM_R), lambda i: (0, i)),
            pl.BlockSpec((N_EXPERTS, LANES), lambda i: (0, 0)),
        ],
        out_shape=[
            jax.ShapeDtypeStruct((2, SEQ), jnp.int32),
            jax.ShapeDtypeStruct((2, SEQ), F32),
            jax.ShapeDtypeStruct((2, SEQ), jnp.int32),
            jax.ShapeDtypeStruct((N_EXPERTS, LANES), F32),
        ],
        compiler_params=_cparams(("arbitrary",)),
        name="router",
    )(x1, g2, wr_t, br)


def _pack_bf16_pairs(lo, hi):
    lo_bits = pltpu.bitcast(lo.astype(BF16).astype(F32), jnp.uint32) >> 16
    hi_bits = pltpu.bitcast(hi.astype(BF16).astype(F32), jnp.uint32) & jnp.uint32(0xFFFF0000)
    return hi_bits | lo_bits


def _unpack_bf16_pairs(w):
    lo = pltpu.bitcast(w << 16, F32).astype(BF16)
    hi = pltpu.bitcast(w & jnp.uint32(0xFFFF0000), F32).astype(BF16)
    return lo, hi


def _dispatch_kernel(slot_ref, padrow_ref, x_ref, g_ref, xs_hbm, hbuf, zbuf, sem, zsem):
    i = pl.program_id(0)
    last = pl.num_programs(0) - 1
    buf = i % 2

    def tile_wait(b):
        for _ in range(2):
            pltpu.make_async_copy(hbuf.at[b], xs_hbm.at[pl.ds(0, TT_D), :], sem.at[b]).wait()

    @pl.when(i == 0)
    def _():
        zbuf[...] = jnp.zeros_like(zbuf)

        for c in range(XS_TAIL // ZROWS):
            pltpu.make_async_copy(zbuf, xs_hbm.at[pl.ds(XS_ROWS - XS_TAIL + c * ZROWS, ZROWS), :], zsem).start()
        for c in range(XS_TAIL // ZROWS):
            pltpu.make_async_copy(zbuf, xs_hbm.at[pl.ds(0, ZROWS), :], zsem).wait()

        def pad(e, c):
            dst = pl.multiple_of(padrow_ref[e], 8)
            pltpu.make_async_copy(zbuf.at[pl.ds(0, 8), :], xs_hbm.at[pl.ds(dst, 8), :], zsem).start()
            return c
        lax.fori_loop(0, N_EXPERTS, pad, 0)

        def padw(e, c):
            pltpu.make_async_copy(zbuf.at[pl.ds(0, 8), :], xs_hbm.at[pl.ds(0, 8), :], zsem).wait()
            return c
        lax.fori_loop(0, N_EXPERTS, padw, 0)

    @pl.when(i >= 2)
    def _():
        tile_wait(buf)

    x = x_ref[...]
    ms = jnp.mean(x * x, axis=-1, keepdims=True)
    hn = x * lax.rsqrt(ms + EPS) * g_ref[...]
    hbuf[buf] = _pack_bf16_pairs(hn[:, :D_MODEL // 2], hn[:, D_MODEL // 2:])

    tok0 = i * TT_D

    def send(j8, c):
        for k8 in range(8):
            j = j8 * 8 + k8
            for k in range(2):
                dst = slot_ref[k * SEQ + tok0 + j]
                pltpu.make_async_copy(hbuf.at[buf, pl.ds(j, 1), :], xs_hbm.at[pl.ds(dst, 1), :],
                                      sem.at[buf]).start()
        return c
    lax.fori_loop(0, TT_D // 8, send, 0)

    @pl.when(i == last)
    def _():
        @pl.when(i >= 1)
        def _():
            tile_wait(1 - buf)
        tile_wait(buf)


def _dispatch(slot_flat, padrow, x1, g2):
    grid_spec = pltpu.PrefetchScalarGridSpec(
        num_scalar_prefetch=2,
        grid=(SEQ // TT_D,),
        in_specs=[
            pl.BlockSpec((TT_D, D_MODEL), lambda i, *_: (i, 0)),
            pl.BlockSpec((1, D_MODEL), lambda i, *_: (0, 0)),
        ],
        out_specs=pl.BlockSpec(memory_space=pl.ANY),
        scratch_shapes=[
            pltpu.VMEM((2, TT_D, D_MODEL // 2), jnp.uint32),
            pltpu.VMEM((ZROWS, D_MODEL // 2), jnp.uint32),
            pltpu.SemaphoreType.DMA((2,)),
            pltpu.SemaphoreType.DMA,
        ],
    )
    return pl.pallas_call(
        _dispatch_kernel,
        grid_spec=grid_spec,
        out_shape=jax.ShapeDtypeStruct((XS_ROWS, D_MODEL // 2), jnp.uint32),
        compiler_params=_cparams(("arbitrary",)),
        name="moe_dispatch",
    )(slot_flat, padrow, x1, g2)


def _experts_kernel(ie_ref, row0_ref, rows_ref, nitems_ref,
                    xs_hbm, wg_hbm, wu_hbm, wd_hbm, ys_hbm,
                    xg_ref, y_ref, zbuf, wg_ref, wu_ref, wd_ref, gsem, ssem, zsem, wsem):
    it = pl.program_id(0)
    nitems = nitems_ref[0]
    slot = it % 2
    half = SUB_E // 2
    nsteps = nitems * NF_E

    def weight_copies(s):
        e = ie_ref[s // NF_E]
        cols = pl.ds(pl.multiple_of((s % NF_E) * TF_E, TF_E), TF_E)
        b = s % W_RING
        return (pltpu.make_async_copy(wg_hbm.at[e, :, cols], wg_ref.at[b], wsem.at[b]),
                pltpu.make_async_copy(wu_hbm.at[e, :, cols], wu_ref.at[b], wsem.at[b]),
                pltpu.make_async_copy(wd_hbm.at[e, cols, :], wd_ref.at[b], wsem.at[b]))

    @pl.when(it == 0)
    def _():
        for s in range(W_RING - 1):
            for cp in weight_copies(s):
                cp.start()

    def padded(item):
        return pl.multiple_of(((rows_ref[item] + half - 1) // half) * half, half)

    nhalf = padded(it) // half
    nsub = nhalf // 2

    def chunk_copies(item, start):
        base = pl.multiple_of(row0_ref[item], 8)
        buf = item % 2

        def body(c, carry):
            off = pl.multiple_of(c * half, half)
            src = xs_hbm.at[pl.ds(pl.multiple_of(base + off, 8), half), :]
            dst = ys_hbm.at[pl.ds(pl.multiple_of(base + off, 8), half), :]
            if start == "fetch":
                pltpu.make_async_copy(src, xg_ref.at[buf, pl.ds(off, half), :], gsem.at[buf]).start()
            else:
                pltpu.make_async_copy(y_ref.at[buf, pl.ds(off, half), :], dst, ssem.at[buf]).start()
            return carry
        lax.fori_loop(0, padded(item) // half, body, 0)

    def wait_fetch(item):
        got = pl.ds(0, padded(item))
        buf = item % 2
        pltpu.make_async_copy(xs_hbm.at[got, :], xg_ref.at[buf, got, :], gsem.at[buf]).wait()

    def wait_store(item):
        put = pl.ds(0, padded(item))
        buf = item % 2
        pltpu.make_async_copy(y_ref.at[buf, put, :], ys_hbm.at[put, :], ssem.at[buf]).wait()

    @pl.when(it == 0)
    def _():
        zbuf[...] = jnp.zeros_like(zbuf)
        for c in range(XS_TAIL // ZROWS):
            pltpu.make_async_copy(zbuf, ys_hbm.at[pl.ds(XS_ROWS - XS_TAIL + c * ZROWS, ZROWS), :], zsem).start()
        for c in range(XS_TAIL // ZROWS):
            pltpu.make_async_copy(zbuf, ys_hbm.at[pl.ds(0, ZROWS), :], zsem).wait()
        chunk_copies(0, "fetch")

    wait_fetch(it)

    def clear(s, c):
        rows = pl.ds(pl.multiple_of(s * half, half), half)
        y_ref[slot, rows, :] = jnp.zeros((half, D_MODEL), F32)
        return c
    lax.fori_loop(0, nhalf, clear, 0)

    @pl.when(it + 1 < nitems)
    def _():
        chunk_copies(it + 1, "fetch")

    def chunk_step(f, carry):
        step = it * NF_E + f
        wslot = step % W_RING

        @pl.when(step + W_RING - 1 < nsteps)
        def _():
            for cp in weight_copies(step + W_RING - 1):
                cp.start()

        for cp in weight_copies(step):
            cp.wait()

        def block(rows):
            x_lo, x_hi = _unpack_bf16_pairs(xg_ref[slot, rows, :])
            kh = D_MODEL // 2

            def up(w_ref):
                return (jnp.dot(x_lo, w_ref[wslot, :kh, :].astype(BF16), preferred_element_type=F32)
                        + jnp.dot(x_hi, w_ref[wslot, kh:, :].astype(BF16), preferred_element_type=F32))

            hg = up(wg_ref)
            hu = up(wu_ref)
            h = (hg * jax.nn.sigmoid(hg) * hu).astype(BF16)
            y_ref[slot, rows, :] = y_ref[slot, rows, :] + jnp.dot(h, wd_ref[wslot].astype(BF16),
                                                                   preferred_element_type=F32)

        def sub(s, c):
            block(pl.ds(pl.multiple_of(s * SUB_E, SUB_E), SUB_E))
            return c
        lax.fori_loop(0, nsub, sub, 0)

        @pl.when(nhalf % 2 == 1)
        def _():
            block(pl.ds(pl.multiple_of(nsub * SUB_E, SUB_E), half))
        return carry

    lax.fori_loop(0, NF_E, chunk_step, 0)

    @pl.when(it > 0)
    def _():
        wait_store(it - 1)

    chunk_copies(it, "store")

    @pl.when(it + 1 >= nitems)
    def _():
        wait_store(it)


def _experts(ie, row0, rows, nitems, xs, wg, wu, wd):
    grid_spec = pltpu.PrefetchScalarGridSpec(
        num_scalar_prefetch=4,
        grid=(nitems[0],),
        in_specs=[pl.BlockSpec(memory_space=pl.ANY) for _ in range(4)],
        out_specs=pl.BlockSpec(memory_space=pl.ANY),
        scratch_shapes=[
            pltpu.VMEM((2, TM_E, D_MODEL // 2), jnp.uint32),
            pltpu.VMEM((2, TM_E, D_MODEL), F32),
            pltpu.VMEM((ZROWS, D_MODEL), F32),
            pltpu.VMEM((W_RING, D_MODEL, TF_E), F32),
            pltpu.VMEM((W_RING, D_MODEL, TF_E), F32),
            pltpu.VMEM((W_RING, TF_E, D_MODEL), F32),
            pltpu.SemaphoreType.DMA((2,)),
            pltpu.SemaphoreType.DMA((2,)),
            pltpu.SemaphoreType.DMA,
            pltpu.SemaphoreType.DMA((W_RING,)),
        ],
    )
    return pl.pallas_call(
        _experts_kernel,
        grid_spec=grid_spec,
        out_shape=jax.ShapeDtypeStruct((XS_ROWS, D_MODEL), F32),
        compiler_params=_cparams(("arbitrary",)),
        name="moe_experts",
    )(ie, row0, rows, nitems, xs, wg, wu, wd)


def _gcombine_kernel(slot_ref, x_ref, w_ref, ys_hbm, o_ref, dbuf, sem):
    i = pl.program_id(0)
    n = pl.num_programs(0)
    buf = i % 2

    def fetch(tile, b):
        tok0 = tile * TT_G

        def body(j8, c):
            for k8 in range(8):
                j = j8 * 8 + k8
                for k in range(2):
                    src = slot_ref[k * SEQ + tok0 + j]
                    pltpu.make_async_copy(ys_hbm.at[pl.ds(src, 1), :], dbuf.at[b, k, pl.ds(j, 1), :],
                                          sem.at[b]).start()
            return c
        lax.fori_loop(0, TT_G // 8, body, 0)

    @pl.when(i == 0)
    def _():
        fetch(0, 0)

    @pl.when(i + 1 < n)
    def _():
        fetch(i + 1, 1 - buf)

    for k in range(2):
        pltpu.make_async_copy(ys_hbm.at[pl.ds(0, TT_G), :], dbuf.at[buf, k], sem.at[buf]).wait()

    w = w_ref[...]
    o_ref[...] = x_ref[...] + w[:, 0:1] * dbuf[buf, 0] + w[:, 1:2] * dbuf[buf, 1]


def _gcombine(slot_flat, x1, wts_t, ys):
    grid_spec = pltpu.PrefetchScalarGridSpec(
        num_scalar_prefetch=1,
        grid=(SEQ // TT_G,),
        in_specs=[
            pl.BlockSpec((TT_G, D_MODEL), lambda i, *_: (i, 0)),
            pl.BlockSpec((TT_G, 2), lambda i, *_: (i, 0)),
            pl.BlockSpec(memory_space=pl.ANY),
        ],
        out_specs=pl.BlockSpec((TT_G, D_MODEL), lambda i, *_: (i, 0)),
        scratch_shapes=[
            pltpu.VMEM((2, 2, TT_G, D_MODEL), F32),
            pltpu.SemaphoreType.DMA((2,)),
        ],
    )
    return pl.pallas_call(
        _gcombine_kernel,
        grid_spec=grid_spec,
        out_shape=jax.ShapeDtypeStruct((SEQ, D_MODEL), F32),
        compiler_params=_cparams(("arbitrary",)),
        name="moe_combine",
    )(slot_flat, x1, wts_t, ys)


def _routing_tables(eid, rank, counts_f):
    counts = counts_f[:, 0].astype(jnp.int32)
    aligned = ((counts + 7) // 8) * 8
    base = jnp.cumsum(aligned) - aligned
    tiles = (counts + TM_E - 1) // TM_E
    tcum = jnp.cumsum(tiles)
    tstart = tcum - tiles
    nitems = tcum[-1]
    ids = jnp.arange(MAX_ITEMS, dtype=jnp.int32)
    ie = jnp.minimum(jnp.sum(ids[:, None] >= tcum[None, :], axis=1), N_EXPERTS - 1).astype(jnp.int32)
    live = ids < nitems
    sel = ie[:, None] == jnp.arange(N_EXPERTS, dtype=jnp.int32)[None, :]

    def pick(v):
        return jnp.sum(jnp.where(sel, v[None, :], 0), axis=1)

    jt = ids - pick(tstart)
    row0 = jnp.where(live, pick(base) + jt * TM_E, 0)
    rows = jnp.where(live, jnp.clip(pick(counts) - jt * TM_E, 0, TM_E), 0)
    eoh = eid[:, :, None] == jnp.arange(N_EXPERTS, dtype=jnp.int32)
    slot = jnp.sum(jnp.where(eoh, base, 0), axis=-1) + rank
    spare = XS_ROWS - 8 * (1 + jnp.arange(N_EXPERTS, dtype=jnp.int32))
    padrow = jnp.where(counts % 8 != 0, base + (counts // 8) * 8, spare)
    return (ie, row0.astype(jnp.int32), rows.astype(jnp.int32), nitems.reshape(1).astype(jnp.int32),
            slot.reshape(-1).astype(jnp.int32), padrow.astype(jnp.int32))


def kernel(x, norm1_g, w_in, q_norm_g, k_norm_g, conv_w, conv_b, conv_ln_g, conv_ln_b, rel_bias,
           w_out, norm2_g, w_router_group, b_router_group, w_router_expert, b_router_expert,
           w_gate, w_up, w_down):
    assert x.shape == (1, SEQ, D_MODEL) and w_in.shape[0] == 1
    xs = x[0]
    bias_tab = _attn_bias_tables(rel_bias)
    qg2 = jnp.tile(q_norm_g[0], 2)[None]
    kg2 = jnp.tile(k_norm_g[0], 2)[None]

    proj = _inproj(xs, norm1_g[0][None], w_in[0])
    conv_out = _conv_mixer(proj, conv_w[0], conv_b[0][None], conv_ln_g[0][None], conv_ln_b[0][None])
    attn_out = _attention(proj, qg2, kg2, bias_tab)
    x1 = _outproj(xs, conv_out, attn_out, w_out[0])

    wr_t = jnp.concatenate([
        w_router_group[0].T, jnp.zeros((8 - N_GROUPS, D_MODEL), F32),
        jnp.transpose(w_router_expert[0], (0, 2, 1)).reshape(N_EXPERTS, D_MODEL)], axis=0)
    br = jnp.concatenate([b_router_group[0], jnp.zeros((8 - N_GROUPS,), F32),
                          b_router_expert[0].reshape(-1)])
    br = jnp.broadcast_to(br[:, None], (R_ROWS, LANES))
    eid, wts, rank, counts_f = _router(x1, norm2_g[0][None], wr_t, br)

    ie, row0, rows, nitems, slot_flat, padrow = _routing_tables(eid, rank, counts_f)
    xs = _dispatch(slot_flat, padrow, x1, norm2_g[0][None])
    ys = _experts(ie, row0, rows, nitems, xs,
                  w_gate[0].reshape(N_EXPERTS, D_MODEL, D_FF),
                  w_up[0].reshape(N_EXPERTS, D_MODEL, D_FF),
                  w_down[0].reshape(N_EXPERTS, D_FF, D_MODEL))
    out = _gcombine(slot_flat, x1, wts.T, ys)
    return out[None]
```

```python
import math

import numpy as np
import jax
import jax.numpy as jnp
from jax import lax
from jax.experimental import pallas as pl
from jax.experimental.pallas import tpu as pltpu

F32 = jnp.float32
BF16 = jnp.bfloat16

D_MODEL = 2048
SEQ = 8192
N_HEADS = 16
HEAD_DIM = 64
ATTN_W = N_HEADS * HEAD_DIM
CONV_C = D_MODEL - ATTN_W
CONV_K = 31
IN_W = 2 * CONV_C + 3 * ATTN_W
PATTERNS = ((128, 1), (512, 4), (2048, 16))
QBLK = 128
NUM_BUCKETS = 32
MAX_DISTANCE = 2048
N_GROUPS = 4
E_PER_G = 8
N_EXPERTS = N_GROUPS * E_PER_G
D_FF = D_MODEL // 2
EPS = 1e-6
NEG_INF = -1e30
LOG2E = math.log2(math.e)

LANES = 128
VMEM_LIMIT = 56 * 1024 * 1024

TM_IN = 1024
TN_IN = 512
IN_RING = 4
TT_CONV = 512
HALO = 32
R_CONV = 64
R_LN = 16
ATTN_GROUP = 4
ATTN_UNROLL = 4
TM_OUT = 512
TM_R = 512
R_ROWS = 8 + N_EXPERTS
KC_R = 512
TM_E = 768
SUB_E = 256
TF_E = 256
NF_E = D_FF // TF_E
W_RING = 4
N_ASSIGN = 2 * SEQ
MAX_ITEMS = -(-N_ASSIGN // TM_E) + N_EXPERTS
XS_TAIL = 256 + TM_E
XS_ROWS = N_ASSIGN + XS_TAIL
ZROWS = 128
TT_D = 512
TT_G = 512


def _cparams(sem, vmem=VMEM_LIMIT):
    return pltpu.CompilerParams(dimension_semantics=sem, vmem_limit_bytes=vmem)


def _inproj_kernel(x_ref, g_ref, w_hbm, o_hbm, xn_ref, w_ref, ob_ref, wsem, osem):
    i = pl.program_id(0)
    nj = IN_W // TN_IN
    total = pl.num_programs(0) * nj

    def w_copy(s):
        cols = pl.ds(pl.multiple_of((s % nj) * TN_IN, TN_IN), TN_IN)
        b = s % IN_RING
        return pltpu.make_async_copy(w_hbm.at[:, cols], w_ref.at[b], wsem.at[b])

    def o_copy(s):
        rows = pl.ds(pl.multiple_of((s // nj) * TM_IN, TM_IN), TM_IN)
        cols = pl.ds(pl.multiple_of((s % nj) * TN_IN, TN_IN), TN_IN)
        b = s % 2
        return pltpu.make_async_copy(ob_ref.at[b], o_hbm.at[rows, cols], osem.at[b])

    @pl.when(i == 0)
    def _():
        for s in range(IN_RING - 1):
            w_copy(s).start()

    x = x_ref[...]
    ms = jnp.mean(x * x, axis=-1, keepdims=True)
    xn_ref[...] = (x * lax.rsqrt(ms + EPS) * g_ref[...]).astype(BF16)

    def col_step(j, carry):
        s = i * nj + j

        @pl.when(s + IN_RING - 1 < total)
        def _():
            w_copy(s + IN_RING - 1).start()

        @pl.when(s >= 2)
        def _():
            o_copy(s - 2).wait()

        w_copy(s).wait()
        ob_ref[s % 2] = jnp.dot(xn_ref[...], w_ref[s % IN_RING].astype(BF16), preferred_element_type=F32)
        o_copy(s).start()
        return carry

    lax.fori_loop(0, nj, col_step, 0)

    @pl.when(i == pl.num_programs(0) - 1)
    def _():
        o_copy(total - 2).wait()
        o_copy(total - 1).wait()


def _inproj(x, g, w):
    return pl.pallas_call(
        _inproj_kernel,
        grid=(SEQ // TM_IN,),
        in_specs=[
            pl.BlockSpec((TM_IN, D_MODEL), lambda i: (i, 0)),
            pl.BlockSpec((1, D_MODEL), lambda i: (0, 0)),
            pl.BlockSpec(memory_space=pl.ANY),
        ],
        out_specs=pl.BlockSpec(memory_space=pl.ANY),
        out_shape=jax.ShapeDtypeStruct((SEQ, IN_W), F32),
        scratch_shapes=[
            pltpu.VMEM((TM_IN, D_MODEL), BF16),
            pltpu.VMEM((IN_RING, D_MODEL, TN_IN), F32),
            pltpu.VMEM((2, TM_IN, TN_IN), F32),
            pltpu.SemaphoreType.DMA((IN_RING,)),
            pltpu.SemaphoreType.DMA((2,)),
        ],
        compiler_params=_cparams(("arbitrary",)),
        name="inproj",
    )(x, g, w)


def _conv_kernel(val_ref, gate_ref, hval_ref, hgate_ref, cw_ref, cb_ref, lg_ref, lb_ref,
                 o_ref, ubuf, zbuf, ybuf):
    i = pl.program_id(0)
    u = val_ref[...] * jax.nn.sigmoid(gate_ref[...])
    hu = hval_ref[...] * jax.nn.sigmoid(hgate_ref[...])
    hu = jnp.where(i > 0, hu, 0.0)
    for c in range(CONV_C // LANES):
        cols = slice(c * LANES, (c + 1) * LANES)
        ubuf[c, 0:HALO, :] = hu[:, cols]
        ubuf[c, HALO:HALO + TT_CONV, :] = u[:, cols]

    n_a = -(-CONV_K // 8)
    assert HALO == 8 * n_a
    for c in range(CONV_C // LANES):
        cols = slice(c * LANES, (c + 1) * LANES)

        def taps(r, carry, c=c, cols=cols):
            base = pl.multiple_of(r * R_CONV, R_CONV)
            win = ubuf[c, pl.ds(base, R_CONV + HALO), :]
            for b in range(8):
                z = None
                for a in range(n_a):
                    s = 8 * a + b
                    if s >= CONV_K:
                        continue
                    lo = HALO - 8 - 8 * a
                    t = cw_ref[CONV_K - 1 - s:CONV_K - s, cols] * win[lo:lo + R_CONV + 8, :]
                    z = t if z is None else z + t
                zbuf[b, pl.ds(0, R_CONV + 8, stride=2), :] = z
            acc = None
            for b in range(8):
                t = zbuf[b, pl.ds(2 * (8 - b), R_CONV, stride=2), :]
                acc = t if acc is None else acc + t
            ybuf[pl.ds(base, R_CONV), cols] = acc
            return carry

        lax.fori_loop(0, TT_CONV // R_CONV, taps, 0)

    def norm(r, carry):
        rows = pl.ds(pl.multiple_of(r * R_LN, R_LN), R_LN)
        acc = ybuf[rows, :] + cb_ref[...]
        mu = jnp.mean(acc, axis=-1, keepdims=True)
        xc = acc - mu
        var = jnp.mean(xc * xc, axis=-1, keepdims=True)
        y = xc * lax.rsqrt(var + EPS) * lg_ref[...] + lb_ref[...]
        o_ref[rows, :] = (y * jax.nn.sigmoid(y)).astype(BF16)
        return carry

    lax.fori_loop(0, TT_CONV // R_LN, norm, 0, unroll=8)


def _conv_mixer(proj, cw, cb, lg, lb):
    hb = TT_CONV // HALO
    return pl.pallas_call(
        _conv_kernel,
        grid=(SEQ // TT_CONV,),
        in_specs=[
            pl.BlockSpec((TT_CONV, CONV_C), lambda i: (i, 0)),
            pl.BlockSpec((TT_CONV, CONV_C), lambda i: (i, 1)),
            pl.BlockSpec((HALO, CONV_C), lambda i: (jnp.maximum(i * hb - 1, 0), 0)),
            pl.BlockSpec((HALO, CONV_C), lambda i: (jnp.maximum(i * hb - 1, 0), 1)),
            pl.BlockSpec((CONV_K, CONV_C), lambda i: (0, 0)),
            pl.BlockSpec((1, CONV_C), lambda i: (0, 0)),
            pl.BlockSpec((1, CONV_C), lambda i: (0, 0)),
            pl.BlockSpec((1, CONV_C), lambda i: (0, 0)),
        ],
        out_specs=pl.BlockSpec((TT_CONV, CONV_C), lambda i: (i, 0)),
        out_shape=jax.ShapeDtypeStruct((SEQ, CONV_C), BF16),
        scratch_shapes=[pltpu.VMEM((CONV_C // LANES, HALO + TT_CONV, LANES), F32),
                        pltpu.VMEM((8, 2 * (R_CONV + 8), LANES), F32),
                        pltpu.VMEM((TT_CONV, CONV_C), F32)],
        compiler_params=_cparams(("arbitrary",)),
        name="conv_mixer",
    )(proj, proj, proj, proj, cw, cb, lg, lb)


def _t5_bucket_np(dist):
    max_exact = NUM_BUCKETS // 2
    nf = np.maximum(dist, 1).astype(np.float32)
    large = max_exact + (np.log(nf / np.float32(max_exact)) / np.float32(math.log(MAX_DISTANCE / max_exact))
                         * np.float32(NUM_BUCKETS - max_exact)).astype(np.int32)
    large = np.minimum(large, NUM_BUCKETS - 1)
    return np.where(dist < max_exact, dist, large)


def _attn_bias_tables(rel_bias):
    period = 3 * QBLK
    diags = []
    for window, dil in PATTERNS:
        span = window // dil
        assert span <= QBLK
        bucket = _t5_bucket_np(np.arange(span + 1) * dil)
        onehot = np.eye(NUM_BUCKETS, dtype=np.float32)[bucket]
        vec = jnp.einsum("rb,bh->hr", onehot, rel_bias.astype(F32),
                         precision=lax.Precision.HIGHEST)
        diag = jnp.full((N_HEADS, period), NEG_INF, F32)
        diags.append(diag.at[:, 2 * QBLK - 1 - span:2 * QBLK].set(vec[:, ::-1]))
    diag = jnp.stack(diags) * LOG2E
    shifted = jnp.stack([jnp.roll(diag, b, axis=-1) for b in range(8)], axis=-2)
    return shifted.reshape(len(PATTERNS), N_HEADS // 2, 2, 8, period)


def _attn_kernel(q_ref, k_ref, v_ref, qg_ref, kg_ref, diag_ref, o_ref,
                 qn_ref, kn_ref, acc_ref, m_ref, l_ref, bias_s):
    lane = lax.broadcasted_iota(jnp.int32, (QBLK, LANES), 1)
    head_a = lane < HEAD_DIM
    hist_keys = lax.broadcasted_iota(jnp.int32, (2 * QBLK, 2 * QBLK), 1) < QBLK
    ri = lax.broadcasted_iota(jnp.int32, (LANES, LANES), 0) // HEAD_DIM
    ci = lax.broadcasted_iota(jnp.int32, (LANES, LANES), 1) // HEAD_DIM
    seg = (ri == ci).astype(BF16)

    def head_rms(x, g):
        sq = x * x
        hi = sq.astype(BF16)
        lo = (sq - hi.astype(F32)).astype(BF16)
        ss = (jnp.dot(hi, seg, preferred_element_type=F32)
              + jnp.dot(lo, seg, preferred_element_type=F32))
        return x * lax.rsqrt(ss * (1.0 / HEAD_DIM) + EPS) * g

    NCH = 512

    def norm_body(c, carry):
        rows = pl.ds(pl.multiple_of(c * NCH, NCH), NCH)
        qn_ref[rows, :] = head_rms(q_ref[rows, :], qg_ref[...]) * (LOG2E / math.sqrt(HEAD_DIM))
        kn_ref[rows, :] = head_rms(k_ref[rows, :], kg_ref[...])
        return carry

    lax.fori_loop(0, SEQ // NCH, norm_body, 0, unroll=2)

    for p in range(len(PATTERNS)):
        for h in range(2):
            offs = diag_ref[p, h]
            for a in range(QBLK // 8):
                lo = QBLK - 1 - 8 * a
                bias_s[p, h * QBLK + 8 * a:h * QBLK + 8 * a + 8, :] = offs[:, lo:lo + 2 * QBLK]

    order = sorted(range(len(PATTERNS)), key=lambda i: -PATTERNS[i][1])
    assert PATTERNS[order[-1]][1] == 1
    for step, p in enumerate(order):
        dil = PATTERNS[p][1]
        nb = SEQ // (dil * QBLK)
        is_first = step == 0
        is_last = step == len(order) - 1

        def rows(start, dil=dil):
            if dil == 1:
                return pl.ds(start, QBLK)
            return pl.ds(start, QBLK, stride=dil)

        def unit(cur, k_prev, v_prev, first, p=p, rows=rows, is_first=is_first, is_last=is_last):
            q = qn_ref[rows(cur), :]
            q2 = jnp.concatenate([jnp.where(head_a, q, 0.0), jnp.where(head_a, 0.0, q)],
                                 axis=0).astype(BF16)
            k_cur = kn_ref[rows(cur), :].astype(BF16)
            v_cur = v_ref[rows(cur), :].astype(BF16)
            k2 = jnp.concatenate([k_prev, k_cur], axis=0)
            v2 = jnp.concatenate([v_prev, v_cur], axis=0)
            s = lax.dot_general(q2, k2, (((1,), (1,)), ((), ())), preferred_element_type=F32)
            s = s + bias_s[p]
            if first is not None:
                s = jnp.where(hist_keys, jnp.where(first, NEG_INF * LOG2E, s), s)
            m = jnp.max(s, axis=-1, keepdims=True)
            e = jnp.exp2(s - m)
            l = jnp.sum(e, axis=-1, keepdims=True)
            pv = jnp.dot(e.astype(BF16), v2, preferred_element_type=F32)
            o_new = jnp.where(head_a, pv[:QBLK], pv[QBLK:])
            m_new = jnp.where(head_a, m[:QBLK], m[QBLK:])
            l_new = jnp.where(head_a, l[:QBLK], l[QBLK:])
            if is_first:
                acc_ref[rows(cur), :] = o_new
                m_ref[rows(cur), :] = m_new
                l_ref[rows(cur), :] = l_new
            else:
                m_old = m_ref[rows(cur), :]
                m_tot = jnp.maximum(m_old, m_new)
                a = jnp.exp2(m_old - m_tot)
                b = jnp.exp2(m_new - m_tot)
                acc = acc_ref[rows(cur), :] * a + o_new * b
                den = l_ref[rows(cur), :] * a + l_new * b
                if is_last:
                    o_ref[pl.ds(pl.multiple_of(cur, QBLK), QBLK), :] = (acc / den).astype(BF16)
                else:
                    acc_ref[rows(cur), :] = acc
                    l_ref[rows(cur), :] = den
                    m_ref[rows(cur), :] = m_tot
            return k_cur, v_cur

        per = nb // ATTN_GROUP
        assert per * ATTN_GROUP == nb

        def group(g, carry, dil=dil, per=per, rows=rows, unit=unit):
            r = g // per
            n0 = (g - r * per) * ATTN_GROUP
            hist = jnp.maximum(n0 - 1, 0) * (QBLK * dil) + r
            k_prev = kn_ref[rows(hist), :].astype(BF16)
            v_prev = v_ref[rows(hist), :].astype(BF16)
            for i in range(ATTN_GROUP):
                first = (n0 == 0) if i == 0 else None
                k_prev, v_prev = unit((n0 + i) * (QBLK * dil) + r, k_prev, v_prev, first)
            return carry

        lax.fori_loop(0, SEQ // (QBLK * ATTN_GROUP), group, 0, unroll=ATTN_UNROLL)


def _attention(proj, qg2, kg2, bias_tab):
    qoff = 2 * CONV_C // LANES
    koff = qoff + ATTN_W // LANES
    voff = koff + ATTN_W // LANES
    return pl.pallas_call(
        _attn_kernel,
        grid=(N_HEADS // 2,),
        in_specs=[
            pl.BlockSpec((SEQ, LANES), lambda h: (0, qoff + h)),
            pl.BlockSpec((SEQ, LANES), lambda h: (0, koff + h)),
            pl.BlockSpec((SEQ, LANES), lambda h: (0, voff + h)),
            pl.BlockSpec((1, LANES), lambda h: (0, 0)),
            pl.BlockSpec((1, LANES), lambda h: (0, 0)),
            pl.BlockSpec((len(PATTERNS), None, 2, 8, 3 * QBLK), lambda h: (0, h, 0, 0, 0)),
        ],
        out_specs=pl.BlockSpec((SEQ, LANES), lambda h: (0, h)),
        out_shape=jax.ShapeDtypeStruct((SEQ, ATTN_W), BF16),
        scratch_shapes=[pltpu.VMEM((SEQ, LANES), F32) for _ in range(5)]
        + [pltpu.VMEM((len(PATTERNS), 2 * QBLK, 2 * QBLK), F32)],
        compiler_params=_cparams(("arbitrary",)),
        name="dilated_attn",
    )(proj, proj, proj, qg2, kg2, bias_tab)


def _outproj_kernel(x_ref, c_ref, a_ref, wc_ref, wa_ref, o_ref):
    o_ref[...] = (x_ref[...]
                  + jnp.dot(c_ref[...], wc_ref[...].astype(BF16), preferred_element_type=F32)
                  + jnp.dot(a_ref[...], wa_ref[...].astype(BF16), preferred_element_type=F32))


def _outproj(x, conv_out, attn_out, w_out):
    return pl.pallas_call(
        _outproj_kernel,
        grid=(SEQ // TM_OUT,),
        in_specs=[
            pl.BlockSpec((TM_OUT, D_MODEL), lambda i: (i, 0)),
            pl.BlockSpec((TM_OUT, CONV_C), lambda i: (i, 0)),
            pl.BlockSpec((TM_OUT, ATTN_W), lambda i: (i, 0)),
            pl.BlockSpec((CONV_C, D_MODEL), lambda i: (0, 0), pipeline_mode=pl.Buffered(1)),
            pl.BlockSpec((ATTN_W, D_MODEL), lambda i: (1, 0), pipeline_mode=pl.Buffered(1)),
        ],
        out_specs=pl.BlockSpec((TM_OUT, D_MODEL), lambda i: (i, 0)),
        out_shape=jax.ShapeDtypeStruct((SEQ, D_MODEL), F32),
        compiler_params=_cparams(("arbitrary",)),
        name="outproj",
    )(x, conv_out, attn_out, w_out, w_out)


def _split2(a):
    a1 = a.astype(BF16)
    a2 = (a - a1.astype(F32)).astype(BF16)
    return a1, a2


def _router_kernel(x_ref, g_ref, wr_ref, br_ref, eid_ref, wts_ref, rank_ref, cnt_ref):
    i = pl.program_id(0)

    @pl.when(i == 0)
    def _():
        cnt_ref[...] = jnp.zeros_like(cnt_ref)

    x = x_ref[...]
    inv = lax.rsqrt(jnp.mean(x * x, axis=-1, keepdims=True) + EPS)
    dn = (((1,), (1,)), ((), ()))
    lt = None
    for c in range(D_MODEL // KC_R):
        cols = slice(c * KC_R, (c + 1) * KC_R)
        h1, h2 = _split2(x_ref[:, cols] * inv * g_ref[:, cols])
        w1, w2 = _split2(wr_ref[:, cols])
        for wa, ha in ((w1, h1), (w1, h2), (w2, h1)):
            t = lax.dot_general(wa, ha, dn, preferred_element_type=F32)
            lt = t if lt is None else lt + t
    lt = lt + br_ref[:, 0:1]

    row8 = lax.broadcasted_iota(jnp.int32, (8, TM_R), 0)
    gl = jnp.where(row8 < N_GROUPS, lt[0:8], -jnp.inf)
    gmax = jnp.max(gl, axis=0, keepdims=True)
    gidx = jnp.min(jnp.where(gl == gmax, row8, 8), axis=0, keepdims=True)
    gw = 1.0 / jnp.sum(jnp.exp(gl - gmax), axis=0, keepdims=True)

    esel = lt[8:16]
    for g in range(1, N_GROUPS):
        esel = jnp.where(gidx == g, lt[8 + 8 * g:16 + 8 * g], esel)
    v1 = jnp.max(esel, axis=0, keepdims=True)
    i1 = jnp.min(jnp.where(esel == v1, row8, 8), axis=0, keepdims=True)
    rest = jnp.where(row8 == i1, -jnp.inf, esel)
    v2 = jnp.max(rest, axis=0, keepdims=True)
    i2 = jnp.min(jnp.where(rest == v2, row8, 8), axis=0, keepdims=True)
    e21 = jnp.exp(v2 - v1)
    den = 1.0 + e21
    e1 = gidx * E_PER_G + i1
    e2 = gidx * E_PER_G + i2
    eid_ref[0:1, :] = e1
    eid_ref[1:2, :] = e2
    wts_ref[0:1, :] = gw * (1.0 / den)
    wts_ref[1:2, :] = gw * (e21 / den)

    erow = lax.broadcasted_iota(jnp.int32, (N_EXPERTS, TM_R), 0)
    oh1 = erow == e1
    oh2 = erow == e2
    member = jnp.where(oh1 | oh2, 1.0, 0.0)
    ti = lax.broadcasted_iota(jnp.int32, (TM_R, TM_R), 0)
    tj = lax.broadcasted_iota(jnp.int32, (TM_R, TM_R), 1)
    upper = jnp.where(ti < tj, 1.0, 0.0).astype(BF16)
    before = jnp.dot(member.astype(BF16), upper, preferred_element_type=F32)
    pos = before + cnt_ref[:, 0:1]
    rank_ref[0:1, :] = jnp.sum(jnp.where(oh1, pos, 0.0), axis=0, keepdims=True).astype(jnp.int32)
    rank_ref[1:2, :] = jnp.sum(jnp.where(oh2, pos, 0.0), axis=0, keepdims=True).astype(jnp.int32)
    cnt_ref[...] = cnt_ref[...] + jnp.sum(member, axis=1, keepdims=True)


def _router(x1, g2, wr_t, br):
    return pl.pallas_call(
        _router_kernel,
        grid=(SEQ // TM_R,),
        in_specs=[
            pl.BlockSpec((TM_R, D_MODEL), lambda i: (i, 0)),
            pl.BlockSpec((1, D_MODEL), lambda i: (0, 0)),
            pl.BlockSpec((R_ROWS, D_MODEL), lambda i: (0, 0)),
            pl.BlockSpec((R_ROWS, LANES), lambda i: (0, 0)),
        ],
        out_specs=[
            pl.BlockSpec((2, TM_R), lambda i: (0, i)),
            pl.BlockSpec((2, TM_R), lambda i: (0, i)),
            pl.BlockSpec((2, TM_R), lambda i: (0, i)),
            pl.BlockSpec((N_EXPERTS, LANES), lambda i: (0, 0)),
        ],
        out_shape=[
            jax.ShapeDtypeStruct((2, SEQ), jnp.int32),
            jax.ShapeDtypeStruct((2, SEQ), F32),
            jax.ShapeDtypeStruct((2, SEQ), jnp.int32),
            jax.ShapeDtypeStruct((N_EXPERTS, LANES), F32),
        ],
        compiler_params=_cparams(("arbitrary",)),
        name="router",
    )(x1, g2, wr_t, br)


def _dispatch_kernel(slot_ref, padrow_ref, x_ref, g_ref, xs_hbm, hbuf, zbuf, sem, zsem):
    i = pl.program_id(0)
    last = pl.num_programs(0) - 1
    buf = i % 2

    def tile_wait(b):
        for _ in range(2):
            pltpu.make_async_copy(hbuf.at[b], xs_hbm.at[pl.ds(0, TT_D), :], sem.at[b]).wait()

    @pl.when(i == 0)
    def _():
        zbuf[...] = jnp.zeros_like(zbuf)

        for c in range(XS_TAIL // ZROWS):
            pltpu.make_async_copy(zbuf, xs_hbm.at[pl.ds(XS_ROWS - XS_TAIL + c * ZROWS, ZROWS), :], zsem).start()
        for c in range(XS_TAIL // ZROWS):
            pltpu.make_async_copy(zbuf, xs_hbm.at[pl.ds(0, ZROWS), :], zsem).wait()

        def pad(e, c):
            dst = pl.multiple_of(padrow_ref[e], 8)
            pltpu.make_async_copy(zbuf.at[pl.ds(0, 8), :], xs_hbm.at[pl.ds(dst, 8), :], zsem).start()
            return c
        lax.fori_loop(0, N_EXPERTS, pad, 0)

        def padw(e, c):
            pltpu.make_async_copy(zbuf.at[pl.ds(0, 8), :], xs_hbm.at[pl.ds(0, 8), :], zsem).wait()
            return c
        lax.fori_loop(0, N_EXPERTS, padw, 0)

    @pl.when(i >= 2)
    def _():
        tile_wait(buf)

    x = x_ref[...]
    ms = jnp.mean(x * x, axis=-1, keepdims=True)
    hbuf[buf] = x * lax.rsqrt(ms + EPS) * g_ref[...]

    tok0 = i * TT_D

    def send(j8, c):
        for k8 in range(8):
            j = j8 * 8 + k8
            for k in range(2):
                dst = slot_ref[k * SEQ + tok0 + j]
                pltpu.make_async_copy(hbuf.at[buf, pl.ds(j, 1), :], xs_hbm.at[pl.ds(dst, 1), :],
                                      sem.at[buf]).start()
        return c
    lax.fori_loop(0, TT_D // 8, send, 0)

    @pl.when(i == last)
    def _():
        @pl.when(i >= 1)
        def _():
            tile_wait(1 - buf)
        tile_wait(buf)


def _dispatch(slot_flat, padrow, x1, g2):
    grid_spec = pltpu.PrefetchScalarGridSpec(
        num_scalar_prefetch=2,
        grid=(SEQ // TT_D,),
        in_specs=[
            pl.BlockSpec((TT_D, D_MODEL), lambda i, *_: (i, 0)),
            pl.BlockSpec((1, D_MODEL), lambda i, *_: (0, 0)),
        ],
        out_specs=pl.BlockSpec(memory_space=pl.ANY),
        scratch_shapes=[
            pltpu.VMEM((2, TT_D, D_MODEL), F32),
            pltpu.VMEM((ZROWS, D_MODEL), F32),
            pltpu.SemaphoreType.DMA((2,)),
            pltpu.SemaphoreType.DMA,
        ],
    )
    return pl.pallas_call(
        _dispatch_kernel,
        grid_spec=grid_spec,
        out_shape=jax.ShapeDtypeStruct((XS_ROWS, D_MODEL), F32),
        compiler_params=_cparams(("arbitrary",)),
        name="moe_dispatch",
    )(slot_flat, padrow, x1, g2)


def _experts_kernel(ie_ref, row0_ref, rows_ref, nitems_ref,
                    xs_hbm, wg_hbm, wu_hbm, wd_hbm, ys_hbm,
                    xg_ref, y_ref, zbuf, wg_ref, wu_ref, wd_ref, gsem, ssem, zsem, wsem):
    it = pl.program_id(0)
    nitems = nitems_ref[0]
    slot = it % 2
    half = SUB_E // 2
    nsteps = nitems * NF_E

    def weight_copies(s):
        e = ie_ref[s // NF_E]
        cols = pl.ds(pl.multiple_of((s % NF_E) * TF_E, TF_E), TF_E)
        b = s % W_RING
        return (pltpu.make_async_copy(wg_hbm.at[e, :, cols], wg_ref.at[b], wsem.at[b]),
                pltpu.make_async_copy(wu_hbm.at[e, :, cols], wu_ref.at[b], wsem.at[b]),
                pltpu.make_async_copy(wd_hbm.at[e, cols, :], wd_ref.at[b], wsem.at[b]))

    @pl.when(it == 0)
    def _():
        for s in range(W_RING - 1):
            for cp in weight_copies(s):
                cp.start()

    def padded(item):
        return pl.multiple_of(((rows_ref[item] + half - 1) // half) * half, half)

    nhalf = padded(it) // half
    nsub = nhalf // 2

    def chunk_copies(item, start):
        base = pl.multiple_of(row0_ref[item], 8)
        buf = item % 2

        def body(c, carry):
            off = pl.multiple_of(c * half, half)
            src = xs_hbm.at[pl.ds(pl.multiple_of(base + off, 8), half), :]
            dst = ys_hbm.at[pl.ds(pl.multiple_of(base + off, 8), half), :]
            if start == "fetch":
                pltpu.make_async_copy(src, xg_ref.at[buf, pl.ds(off, half), :], gsem.at[buf]).start()
            else:
                pltpu.make_async_copy(y_ref.at[buf, pl.ds(off, half), :], dst, ssem.at[buf]).start()
            return carry
        lax.fori_loop(0, padded(item) // half, body, 0)

    def wait_fetch(item):
        got = pl.ds(0, padded(item))
        buf = item % 2
        pltpu.make_async_copy(xs_hbm.at[got, :], xg_ref.at[buf, got, :], gsem.at[buf]).wait()

    def wait_store(item):
        put = pl.ds(0, padded(item))
        buf = item % 2
        pltpu.make_async_copy(y_ref.at[buf, put, :], ys_hbm.at[put, :], ssem.at[buf]).wait()

    @pl.when(it == 0)
    def _():
        zbuf[...] = jnp.zeros_like(zbuf)
        for c in range(XS_TAIL // ZROWS):
            pltpu.make_async_copy(zbuf, ys_hbm.at[pl.ds(XS_ROWS - XS_TAIL + c * ZROWS, ZROWS), :], zsem).start()
        for c in range(XS_TAIL // ZROWS):
            pltpu.make_async_copy(zbuf, ys_hbm.at[pl.ds(0, ZROWS), :], zsem).wait()
        chunk_copies(0, "fetch")

    wait_fetch(it)

    def clear(s, c):
        rows = pl.ds(pl.multiple_of(s * half, half), half)
        y_ref[slot, rows, :] = jnp.zeros((half, D_MODEL), F32)
        return c
    lax.fori_loop(0, nhalf, clear, 0)

    @pl.when(it + 1 < nitems)
    def _():
        chunk_copies(it + 1, "fetch")

    def chunk_step(f, carry):
        step = it * NF_E + f
        wslot = step % W_RING

        @pl.when(step + W_RING - 1 < nsteps)
        def _():
            for cp in weight_copies(step + W_RING - 1):
                cp.start()

        for cp in weight_copies(step):
            cp.wait()

        def block(rows):
            xb = xg_ref[slot, rows, :].astype(BF16)
            hg = jnp.dot(xb, wg_ref[wslot].astype(BF16), preferred_element_type=F32)
            hu = jnp.dot(xb, wu_ref[wslot].astype(BF16), preferred_element_type=F32)
            h = (hg * jax.nn.sigmoid(hg) * hu).astype(BF16)
            y_ref[slot, rows, :] = y_ref[slot, rows, :] + jnp.dot(h, wd_ref[wslot].astype(BF16),
                                                                   preferred_element_type=F32)

        def sub(s, c):
            block(pl.ds(pl.multiple_of(s * SUB_E, SUB_E), SUB_E))
            return c
        lax.fori_loop(0, nsub, sub, 0)

        @pl.when(nhalf % 2 == 1)
        def _():
            block(pl.ds(pl.multiple_of(nsub * SUB_E, SUB_E), half))
        return carry

    lax.fori_loop(0, NF_E, chunk_step, 0)

    @pl.when(it > 0)
    def _():
        wait_store(it - 1)

    chunk_copies(it, "store")

    @pl.when(it + 1 >= nitems)
    def _():
        wait_store(it)


def _experts(ie, row0, rows, nitems, xs, wg, wu, wd):
    grid_spec = pltpu.PrefetchScalarGridSpec(
        num_scalar_prefetch=4,
        grid=(nitems[0],),
        in_specs=[pl.BlockSpec(memory_space=pl.ANY) for _ in range(4)],
        out_specs=pl.BlockSpec(memory_space=pl.ANY),
        scratch_shapes=[
            pltpu.VMEM((2, TM_E, D_MODEL), F32),
            pltpu.VMEM((2, TM_E, D_MODEL), F32),
            pltpu.VMEM((ZROWS, D_MODEL), F32),
            pltpu.VMEM((W_RING, D_MODEL, TF_E), F32),
            pltpu.VMEM((W_RING, D_MODEL, TF_E), F32),
            pltpu.VMEM((W_RING, TF_E, D_MODEL), F32),
            pltpu.SemaphoreType.DMA((2,)),
            pltpu.SemaphoreType.DMA((2,)),
            pltpu.SemaphoreType.DMA,
            pltpu.SemaphoreType.DMA((W_RING,)),
        ],
    )
    return pl.pallas_call(
        _experts_kernel,
        grid_spec=grid_spec,
        out_shape=jax.ShapeDtypeStruct((XS_ROWS, D_MODEL), F32),
        compiler_params=_cparams(("arbitrary",)),
        name="moe_experts",
    )(ie, row0, rows, nitems, xs, wg, wu, wd)


def _gcombine_kernel(slot_ref, x_ref, w_ref, ys_hbm, o_ref, dbuf, sem):
    i = pl.program_id(0)
    n = pl.num_programs(0)
    buf = i % 2

    def fetch(tile, b):
        tok0 = tile * TT_G

        def body(j8, c):
            for k8 in range(8):
                j = j8 * 8 + k8
                for k in range(2):
                    src = slot_ref[k * SEQ + tok0 + j]
                    pltpu.make_async_copy(ys_hbm.at[pl.ds(src, 1), :], dbuf.at[b, k, pl.ds(j, 1), :],
                                          sem.at[b]).start()
            return c
        lax.fori_loop(0, TT_G // 8, body, 0)

    @pl.when(i == 0)
    def _():
        fetch(0, 0)

    @pl.when(i + 1 < n)
    def _():
        fetch(i + 1, 1 - buf)

    for k in range(2):
        pltpu.make_async_copy(ys_hbm.at[pl.ds(0, TT_G), :], dbuf.at[buf, k], sem.at[buf]).wait()

    w = w_ref[...]
    o_ref[...] = x_ref[...] + w[:, 0:1] * dbuf[buf, 0] + w[:, 1:2] * dbuf[buf, 1]


def _gcombine(slot_flat, x1, wts_t, ys):
    grid_spec = pltpu.PrefetchScalarGridSpec(
        num_scalar_prefetch=1,
        grid=(SEQ // TT_G,),
        in_specs=[
            pl.BlockSpec((TT_G, D_MODEL), lambda i, *_: (i, 0)),
            pl.BlockSpec((TT_G, 2), lambda i, *_: (i, 0)),
            pl.BlockSpec(memory_space=pl.ANY),
        ],
        out_specs=pl.BlockSpec((TT_G, D_MODEL), lambda i, *_: (i, 0)),
        scratch_shapes=[
            pltpu.VMEM((2, 2, TT_G, D_MODEL), F32),
            pltpu.SemaphoreType.DMA((2,)),
        ],
    )
    return pl.pallas_call(
        _gcombine_kernel,
        grid_spec=grid_spec,
        out_shape=jax.ShapeDtypeStruct((SEQ, D_MODEL), F32),
        compiler_params=_cparams(("arbitrary",)),
        name="moe_combine",
    )(slot_flat, x1, wts_t, ys)


def _routing_tables(eid, rank, counts_f):
    counts = counts_f[:, 0].astype(jnp.int32)
    aligned = ((counts + 7) // 8) * 8
    base = jnp.cumsum(aligned) - aligned
    tiles = (counts + TM_E - 1) // TM_E
    tcum = jnp.cumsum(tiles)
    tstart = tcum - tiles
    nitems = tcum[-1]
    ids = jnp.arange(MAX_ITEMS, dtype=jnp.int32)
    ie = jnp.minimum(jnp.sum(ids[:, None] >= tcum[None, :], axis=1), N_EXPERTS - 1).astype(jnp.int32)
    live = ids < nitems
    sel = ie[:, None] == jnp.arange(N_EXPERTS, dtype=jnp.int32)[None, :]

    def pick(v):
        return jnp.sum(jnp.where(sel, v[None, :], 0), axis=1)

    jt = ids - pick(tstart)
    row0 = jnp.where(live, pick(base) + jt * TM_E, 0)
    rows = jnp.where(live, jnp.clip(pick(counts) - jt * TM_E, 0, TM_E), 0)
    eoh = eid[:, :, None] == jnp.arange(N_EXPERTS, dtype=jnp.int32)
    slot = jnp.sum(jnp.where(eoh, base, 0), axis=-1) + rank
    spare = XS_ROWS - 8 * (1 + jnp.arange(N_EXPERTS, dtype=jnp.int32))
    padrow = jnp.where(counts % 8 != 0, base + (counts // 8) * 8, spare)
    return (ie, row0.astype(jnp.int32), rows.astype(jnp.int32), nitems.reshape(1).astype(jnp.int32),
            slot.reshape(-1).astype(jnp.int32), padrow.astype(jnp.int32))


def kernel(x, norm1_g, w_in, q_norm_g, k_norm_g, conv_w, conv_b, conv_ln_g, conv_ln_b, rel_bias,
           w_out, norm2_g, w_router_group, b_router_group, w_router_expert, b_router_expert,
           w_gate, w_up, w_down):
    assert x.shape == (1, SEQ, D_MODEL) and w_in.shape[0] == 1
    xs = x[0]
    bias_tab = _attn_bias_tables(rel_bias)
    qg2 = jnp.tile(q_norm_g[0], 2)[None]
    kg2 = jnp.tile(k_norm_g[0], 2)[None]

    proj = _inproj(xs, norm1_g[0][None], w_in[0])
    conv_out = _conv_mixer(proj, conv_w[0], conv_b[0][None], conv_ln_g[0][None], conv_ln_b[0][None])
    attn_out = _attention(proj, qg2, kg2, bias_tab)
    x1 = _outproj(xs, conv_out, attn_out, w_out[0])

    wr_t = jnp.concatenate([
        w_router_group[0].T, jnp.zeros((8 - N_GROUPS, D_MODEL), F32),
        jnp.transpose(w_router_expert[0], (0, 2, 1)).reshape(N_EXPERTS, D_MODEL)], axis=0)
    br = jnp.concatenate([b_router_group[0], jnp.zeros((8 - N_GROUPS,), F32),
                          b_router_expert[0].reshape(-1)])
    br = jnp.broadcast_to(br[:, None], (R_ROWS, LANES))
    eid, wts, rank, counts_f = _router(x1, norm2_g[0][None], wr_t, br)

    ie, row0, rows, nitems, slot_flat, padrow = _routing_tables(eid, rank, counts_f)
    xs = _dispatch(slot_flat, padrow, x1, norm2_g[0][None])
    ys = _experts(ie, row0, rows, nitems, xs,
                  w_gate[0].reshape(N_EXPERTS, D_MODEL, D_FF),
                  w_up[0].reshape(N_EXPERTS, D_MODEL, D_FF),
                  w_down[0].reshape(N_EXPERTS, D_FF, D_MODEL))
    out = _gcombine(slot_flat, x1, wts.T, ys)
    return out[None]
```

```python
import math

import numpy as np
import jax
import jax.numpy as jnp
from jax import lax
from jax.experimental import pallas as pl
from jax.experimental.pallas import tpu as pltpu

F32 = jnp.float32
BF16 = jnp.bfloat16

D_MODEL = 2048
SEQ = 8192
N_HEADS = 16
HEAD_DIM = 64
ATTN_W = N_HEADS * HEAD_DIM
CONV_C = D_MODEL - ATTN_W
CONV_K = 31
IN_W = 2 * CONV_C + 3 * ATTN_W
PATTERNS = ((128, 1), (512, 4), (2048, 16))
QBLK = 128
NUM_BUCKETS = 32
MAX_DISTANCE = 2048
N_GROUPS = 4
E_PER_G = 8
N_EXPERTS = N_GROUPS * E_PER_G
D_FF = D_MODEL // 2
EPS = 1e-6
NEG_INF = -1e30
LOG2E = math.log2(math.e)

LANES = 128
VMEM_LIMIT = 56 * 1024 * 1024

TM_IN = 1024
TN_IN = 512
IN_RING = 4
TT_CONV = 512
HALO = 32
R_CONV = 64
R_LN = 16
ATTN_GROUP = 4
ATTN_UNROLL = 4
TM_OUT = 512
TM_R = 512
R_ROWS = 8 + N_EXPERTS
KC_R = 512
TM_E = 768
SUB_E = 256
TF_E = 256
NF_E = D_FF // TF_E
W_RING = 4
N_ASSIGN = 2 * SEQ
MAX_ITEMS = -(-N_ASSIGN // TM_E) + N_EXPERTS
XS_TAIL = 256 + TM_E
XS_ROWS = N_ASSIGN + XS_TAIL
ZROWS = 128
TT_D = 512
TT_G = 512


def _cparams(sem, vmem=VMEM_LIMIT):
    return pltpu.CompilerParams(dimension_semantics=sem, vmem_limit_bytes=vmem)


def _inproj_kernel(x_ref, g_ref, w_hbm, o_hbm, xn_ref, w_ref, ob_ref, wsem, osem):
    i = pl.program_id(0)
    nj = IN_W // TN_IN
    total = pl.num_programs(0) * nj

    def w_copy(s):
        cols = pl.ds(pl.multiple_of((s % nj) * TN_IN, TN_IN), TN_IN)
        b = s % IN_RING
        return pltpu.make_async_copy(w_hbm.at[:, cols], w_ref.at[b], wsem.at[b])

    def o_copy(s):
        rows = pl.ds(pl.multiple_of((s // nj) * TM_IN, TM_IN), TM_IN)
        cols = pl.ds(pl.multiple_of((s % nj) * TN_IN, TN_IN), TN_IN)
        b = s % 2
        return pltpu.make_async_copy(ob_ref.at[b], o_hbm.at[rows, cols], osem.at[b])

    @pl.when(i == 0)
    def _():
        for s in range(IN_RING - 1):
            w_copy(s).start()

    x = x_ref[...]
    ms = jnp.mean(x * x, axis=-1, keepdims=True)
    xn_ref[...] = (x * lax.rsqrt(ms + EPS) * g_ref[...]).astype(BF16)

    def col_step(j, carry):
        s = i * nj + j

        @pl.when(s + IN_RING - 1 < total)
        def _():
            w_copy(s + IN_RING - 1).start()

        @pl.when(s >= 2)
        def _():
            o_copy(s - 2).wait()

        w_copy(s).wait()
        ob_ref[s % 2] = jnp.dot(xn_ref[...], w_ref[s % IN_RING].astype(BF16), preferred_element_type=F32)
        o_copy(s).start()
        return carry

    lax.fori_loop(0, nj, col_step, 0)

    @pl.when(i == pl.num_programs(0) - 1)
    def _():
        o_copy(total - 2).wait()
        o_copy(total - 1).wait()


def _inproj(x, g, w):
    return pl.pallas_call(
        _inproj_kernel,
        grid=(SEQ // TM_IN,),
        in_specs=[
            pl.BlockSpec((TM_IN, D_MODEL), lambda i: (i, 0)),
            pl.BlockSpec((1, D_MODEL), lambda i: (0, 0)),
            pl.BlockSpec(memory_space=pl.ANY),
        ],
        out_specs=pl.BlockSpec(memory_space=pl.ANY),
        out_shape=jax.ShapeDtypeStruct((SEQ, IN_W), F32),
        scratch_shapes=[
            pltpu.VMEM((TM_IN, D_MODEL), BF16),
            pltpu.VMEM((IN_RING, D_MODEL, TN_IN), F32),
            pltpu.VMEM((2, TM_IN, TN_IN), F32),
            pltpu.SemaphoreType.DMA((IN_RING,)),
            pltpu.SemaphoreType.DMA((2,)),
        ],
        compiler_params=_cparams(("arbitrary",)),
        name="inproj",
    )(x, g, w)


def _conv_kernel(val_ref, gate_ref, hval_ref, hgate_ref, cw_ref, cb_ref, lg_ref, lb_ref,
                 o_ref, ubuf, zbuf, ybuf):
    i = pl.program_id(0)
    u = val_ref[...] * jax.nn.sigmoid(gate_ref[...])
    hu = hval_ref[...] * jax.nn.sigmoid(hgate_ref[...])
    hu = jnp.where(i > 0, hu, 0.0)
    for c in range(CONV_C // LANES):
        cols = slice(c * LANES, (c + 1) * LANES)
        ubuf[c, 0:HALO, :] = hu[:, cols]
        ubuf[c, HALO:HALO + TT_CONV, :] = u[:, cols]

    n_a = -(-CONV_K // 8)
    assert HALO == 8 * n_a
    for c in range(CONV_C // LANES):
        cols = slice(c * LANES, (c + 1) * LANES)

        def taps(r, carry, c=c, cols=cols):
            base = pl.multiple_of(r * R_CONV, R_CONV)
            win = ubuf[c, pl.ds(base, R_CONV + HALO), :]
            for b in range(8):
                z = None
                for a in range(n_a):
                    s = 8 * a + b
                    if s >= CONV_K:
                        continue
                    lo = HALO - 8 - 8 * a
                    t = cw_ref[CONV_K - 1 - s:CONV_K - s, cols] * win[lo:lo + R_CONV + 8, :]
                    z = t if z is None else z + t
                zbuf[b, pl.ds(0, R_CONV + 8, stride=2), :] = z
            acc = None
            for b in range(8):
                t = zbuf[b, pl.ds(2 * (8 - b), R_CONV, stride=2), :]
                acc = t if acc is None else acc + t
            ybuf[pl.ds(base, R_CONV), cols] = acc
            return carry

        lax.fori_loop(0, TT_CONV // R_CONV, taps, 0)

    def norm(r, carry):
        rows = pl.ds(pl.multiple_of(r * R_LN, R_LN), R_LN)
        acc = ybuf[rows, :] + cb_ref[...]
        mu = jnp.mean(acc, axis=-1, keepdims=True)
        xc = acc - mu
        var = jnp.mean(xc * xc, axis=-1, keepdims=True)
        y = xc * lax.rsqrt(var + EPS) * lg_ref[...] + lb_ref[...]
        o_ref[rows, :] = (y * jax.nn.sigmoid(y)).astype(BF16)
        return carry

    lax.fori_loop(0, TT_CONV // R_LN, norm, 0, unroll=8)


def _conv_mixer(proj, cw, cb, lg, lb):
    hb = TT_CONV // HALO
    return pl.pallas_call(
        _conv_kernel,
        grid=(SEQ // TT_CONV,),
        in_specs=[
            pl.BlockSpec((TT_CONV, CONV_C), lambda i: (i, 0)),
            pl.BlockSpec((TT_CONV, CONV_C), lambda i: (i, 1)),
            pl.BlockSpec((HALO, CONV_C), lambda i: (jnp.maximum(i * hb - 1, 0), 0)),
            pl.BlockSpec((HALO, CONV_C), lambda i: (jnp.maximum(i * hb - 1, 0), 1)),
            pl.BlockSpec((CONV_K, CONV_C), lambda i: (0, 0)),
            pl.BlockSpec((1, CONV_C), lambda i: (0, 0)),
            pl.BlockSpec((1, CONV_C), lambda i: (0, 0)),
            pl.BlockSpec((1, CONV_C), lambda i: (0, 0)),
        ],
        out_specs=pl.BlockSpec((TT_CONV, CONV_C), lambda i: (i, 0)),
        out_shape=jax.ShapeDtypeStruct((SEQ, CONV_C), BF16),
        scratch_shapes=[pltpu.VMEM((CONV_C // LANES, HALO + TT_CONV, LANES), F32),
                        pltpu.VMEM((8, 2 * (R_CONV + 8), LANES), F32),
                        pltpu.VMEM((TT_CONV, CONV_C), F32)],
        compiler_params=_cparams(("arbitrary",)),
        name="conv_mixer",
    )(proj, proj, proj, proj, cw, cb, lg, lb)


def _t5_bucket_np(dist):
    max_exact = NUM_BUCKETS // 2
    nf = np.maximum(dist, 1).astype(np.float32)
    large = max_exact + (np.log(nf / np.float32(max_exact)) / np.float32(math.log(MAX_DISTANCE / max_exact))
                         * np.float32(NUM_BUCKETS - max_exact)).astype(np.int32)
    large = np.minimum(large, NUM_BUCKETS - 1)
    return np.where(dist < max_exact, dist, large)


def _attn_bias_tables(rel_bias):
    period = 3 * QBLK
    diags = []
    for window, dil in PATTERNS:
        span = window // dil
        assert span <= QBLK
        bucket = _t5_bucket_np(np.arange(span + 1) * dil)
        onehot = np.eye(NUM_BUCKETS, dtype=np.float32)[bucket]
        vec = jnp.einsum("rb,bh->hr", onehot, rel_bias.astype(F32),
                         precision=lax.Precision.HIGHEST)
        diag = jnp.full((N_HEADS, period), NEG_INF, F32)
        diags.append(diag.at[:, 2 * QBLK - 1 - span:2 * QBLK].set(vec[:, ::-1]))
    diag = jnp.stack(diags) * LOG2E
    shifted = jnp.stack([jnp.roll(diag, b, axis=-1) for b in range(8)], axis=-2)
    return shifted.reshape(len(PATTERNS), N_HEADS // 2, 2, 8, period)


def _attn_kernel(q_ref, k_ref, v_ref, qg_ref, kg_ref, diag_ref, o_ref,
                 qn_ref, kn_ref, acc_ref, m_ref, l_ref, bias_s):
    lane = lax.broadcasted_iota(jnp.int32, (QBLK, LANES), 1)
    head_a = lane < HEAD_DIM
    hist_keys = lax.broadcasted_iota(jnp.int32, (2 * QBLK, 2 * QBLK), 1) < QBLK
    ri = lax.broadcasted_iota(jnp.int32, (LANES, LANES), 0) // HEAD_DIM
    ci = lax.broadcasted_iota(jnp.int32, (LANES, LANES), 1) // HEAD_DIM
    seg = (ri == ci).astype(BF16)

    def head_rms(x, g):
        sq = x * x
        hi = sq.astype(BF16)
        lo = (sq - hi.astype(F32)).astype(BF16)
        ss = (jnp.dot(hi, seg, preferred_element_type=F32)
              + jnp.dot(lo, seg, preferred_element_type=F32))
        return x * lax.rsqrt(ss * (1.0 / HEAD_DIM) + EPS) * g

    NCH = 512

    def norm_body(c, carry):
        rows = pl.ds(pl.multiple_of(c * NCH, NCH), NCH)
        qn_ref[rows, :] = head_rms(q_ref[rows, :], qg_ref[...]) * (LOG2E / math.sqrt(HEAD_DIM))
        kn_ref[rows, :] = head_rms(k_ref[rows, :], kg_ref[...])
        return carry

    lax.fori_loop(0, SEQ // NCH, norm_body, 0, unroll=2)

    for p in range(len(PATTERNS)):
        for h in range(2):
            offs = diag_ref[p, h]
            for a in range(QBLK // 8):
                lo = QBLK - 1 - 8 * a
                bias_s[p, h * QBLK + 8 * a:h * QBLK + 8 * a + 8, :] = offs[:, lo:lo + 2 * QBLK]

    order = sorted(range(len(PATTERNS)), key=lambda i: -PATTERNS[i][1])
    assert PATTERNS[order[-1]][1] == 1
    for step, p in enumerate(order):
        dil = PATTERNS[p][1]
        nb = SEQ // (dil * QBLK)
        is_first = step == 0
        is_last = step == len(order) - 1

        def rows(start, dil=dil):
            if dil == 1:
                return pl.ds(start, QBLK)
            return pl.ds(start, QBLK, stride=dil)

        def unit(cur, k_prev, v_prev, first, p=p, rows=rows, is_first=is_first, is_last=is_last):
            q = qn_ref[rows(cur), :]
            q2 = jnp.concatenate([jnp.where(head_a, q, 0.0), jnp.where(head_a, 0.0, q)],
                                 axis=0).astype(BF16)
            k_cur = kn_ref[rows(cur), :].astype(BF16)
            v_cur = v_ref[rows(cur), :].astype(BF16)
            k2 = jnp.concatenate([k_prev, k_cur], axis=0)
            v2 = jnp.concatenate([v_prev, v_cur], axis=0)
            s = lax.dot_general(q2, k2, (((1,), (1,)), ((), ())), preferred_element_type=F32)
            s = s + bias_s[p]
            if first is not None:
                s = jnp.where(hist_keys, jnp.where(first, NEG_INF * LOG2E, s), s)
            m = jnp.max(s, axis=-1, keepdims=True)
            e = jnp.exp2(s - m)
            l = jnp.sum(e, axis=-1, keepdims=True)
            pv = jnp.dot(e.astype(BF16), v2, preferred_element_type=F32)
            o_new = jnp.where(head_a, pv[:QBLK], pv[QBLK:])
            m_new = jnp.where(head_a, m[:QBLK], m[QBLK:])
            l_new = jnp.where(head_a, l[:QBLK], l[QBLK:])
            if is_first:
                acc_ref[rows(cur), :] = o_new
                m_ref[rows(cur), :] = m_new
                l_ref[rows(cur), :] = l_new
            else:
                m_old = m_ref[rows(cur), :]
                m_tot = jnp.maximum(m_old, m_new)
                a = jnp.exp2(m_old - m_tot)
                b = jnp.exp2(m_new - m_tot)
                acc = acc_ref[rows(cur), :] * a + o_new * b
                den = l_ref[rows(cur), :] * a + l_new * b
                if is_last:
                    o_ref[pl.ds(pl.multiple_of(cur, QBLK), QBLK), :] = (acc / den).astype(BF16)
                else:
                    acc_ref[rows(cur), :] = acc
                    l_ref[rows(cur), :] = den
                    m_ref[rows(cur), :] = m_tot
            return k_cur, v_cur

        per = nb // ATTN_GROUP
        assert per * ATTN_GROUP == nb

        def group(g, carry, dil=dil, per=per, rows=rows, unit=unit):
            r = g // per
            n0 = (g - r * per) * ATTN_GROUP
            hist = jnp.maximum(n0 - 1, 0) * (QBLK * dil) + r
            k_prev = kn_ref[rows(hist), :].astype(BF16)
            v_prev = v_ref[rows(hist), :].astype(BF16)
            for i in range(ATTN_GROUP):
                first = (n0 == 0) if i == 0 else None
                k_prev, v_prev = unit((n0 + i) * (QBLK * dil) + r, k_prev, v_prev, first)
            return carry

        lax.fori_loop(0, SEQ // (QBLK * ATTN_GROUP), group, 0, unroll=ATTN_UNROLL)


def _attention(proj, qg2, kg2, bias_tab):
    qoff = 2 * CONV_C // LANES
    koff = qoff + ATTN_W // LANES
    voff = koff + ATTN_W // LANES
    return pl.pallas_call(
        _attn_kernel,
        grid=(N_HEADS // 2,),
        in_specs=[
            pl.BlockSpec((SEQ, LANES), lambda h: (0, qoff + h)),
            pl.BlockSpec((SEQ, LANES), lambda h: (0, koff + h)),
            pl.BlockSpec((SEQ, LANES), lambda h: (0, voff + h)),
            pl.BlockSpec((1, LANES), lambda h: (0, 0)),
            pl.BlockSpec((1, LANES), lambda h: (0, 0)),
            pl.BlockSpec((len(PATTERNS), None, 2, 8, 3 * QBLK), lambda h: (0, h, 0, 0, 0)),
        ],
        out_specs=pl.BlockSpec((SEQ, LANES), lambda h: (0, h)),
        out_shape=jax.ShapeDtypeStruct((SEQ, ATTN_W), BF16),
        scratch_shapes=[pltpu.VMEM((SEQ, LANES), F32) for _ in range(5)]
        + [pltpu.VMEM((len(PATTERNS), 2 * QBLK, 2 * QBLK), F32)],
        compiler_params=_cparams(("arbitrary",)),
        name="dilated_attn",
    )(proj, proj, proj, qg2, kg2, bias_tab)


def _outproj_kernel(x_ref, c_ref, a_ref, wc_ref, wa_ref, o_ref):
    o_ref[...] = (x_ref[...]
                  + jnp.dot(c_ref[...], wc_ref[...].astype(BF16), preferred_element_type=F32)
                  + jnp.dot(a_ref[...], wa_ref[...].astype(BF16), preferred_element_type=F32))


def _outproj(x, conv_out, attn_out, w_out):
    return pl.pallas_call(
        _outproj_kernel,
        grid=(SEQ // TM_OUT,),
        in_specs=[
            pl.BlockSpec((TM_OUT, D_MODEL), lambda i: (i, 0)),
            pl.BlockSpec((TM_OUT, CONV_C), lambda i: (i, 0)),
            pl.BlockSpec((TM_OUT, ATTN_W), lambda i: (i, 0)),
            pl.BlockSpec((CONV_C, D_MODEL), lambda i: (0, 0), pipeline_mode=pl.Buffered(1)),
            pl.BlockSpec((ATTN_W, D_MODEL), lambda i: (1, 0), pipeline_mode=pl.Buffered(1)),
        ],
        out_specs=pl.BlockSpec((TM_OUT, D_MODEL), lambda i: (i, 0)),
        out_shape=jax.ShapeDtypeStruct((SEQ, D_MODEL), F32),
        compiler_params=_cparams(("arbitrary",)),
        name="outproj",
    )(x, conv_out, attn_out, w_out, w_out)


def _split2(a):
    a1 = a.astype(BF16)
    a2 = (a - a1.astype(F32)).astype(BF16)
    return a1, a2


def _router_kernel(x_ref, g_ref, wr_ref, br_ref, eid_ref, wts_ref, rank_ref, cnt_ref):
    i = pl.program_id(0)

    @pl.when(i == 0)
    def _():
        cnt_ref[...] = jnp.zeros_like(cnt_ref)

    x = x_ref[...]
    inv = lax.rsqrt(jnp.mean(x * x, axis=-1, keepdims=True) + EPS)
    dn = (((1,), (1,)), ((), ()))
    lt = None
    for c in range(D_MODEL // KC_R):
        cols = slice(c * KC_R, (c + 1) * KC_R)
        h1, h2 = _split2(x_ref[:, cols] * inv * g_ref[:, cols])
        w1, w2 = _split2(wr_ref[:, cols])
        for wa, ha in ((w1, h1), (w1, h2), (w2, h1)):
            t = lax.dot_general(wa, ha, dn, preferred_element_type=F32)
            lt = t if lt is None else lt + t
    lt = lt + br_ref[:, 0:1]

    row8 = lax.broadcasted_iota(jnp.int32, (8, TM_R), 0)
    gl = jnp.where(row8 < N_GROUPS, lt[0:8], -jnp.inf)
    gmax = jnp.max(gl, axis=0, keepdims=True)
    gidx = jnp.min(jnp.where(gl == gmax, row8, 8), axis=0, keepdims=True)
    gw = 1.0 / jnp.sum(jnp.exp(gl - gmax), axis=0, keepdims=True)

    esel = lt[8:16]
    for g in range(1, N_GROUPS):
        esel = jnp.where(gidx == g, lt[8 + 8 * g:16 + 8 * g], esel)
    v1 = jnp.max(esel, axis=0, keepdims=True)
    i1 = jnp.min(jnp.where(esel == v1, row8, 8), axis=0, keepdims=True)
    rest = jnp.where(row8 == i1, -jnp.inf, esel)
    v2 = jnp.max(rest, axis=0, keepdims=True)
    i2 = jnp.min(jnp.where(rest == v2, row8, 8), axis=0, keepdims=True)
    e21 = jnp.exp(v2 - v1)
    den = 1.0 + e21
    e1 = gidx * E_PER_G + i1
    e2 = gidx * E_PER_G + i2
    eid_ref[0:1, :] = e1
    eid_ref[1:2, :] = e2
    wts_ref[0:1, :] = gw * (1.0 / den)
    wts_ref[1:2, :] = gw * (e21 / den)

    erow = lax.broadcasted_iota(jnp.int32, (N_EXPERTS, TM_R), 0)
    oh1 = erow == e1
    oh2 = erow == e2
    member = jnp.where(oh1 | oh2, 1.0, 0.0)
    ti = lax.broadcasted_iota(jnp.int32, (TM_R, TM_R), 0)
    tj = lax.broadcasted_iota(jnp.int32, (TM_R, TM_R), 1)
    upper = jnp.where(ti < tj, 1.0, 0.0).astype(BF16)
    before = jnp.dot(member.astype(BF16), upper, preferred_element_type=F32)
    pos = before + cnt_ref[:, 0:1]
    rank_ref[0:1, :] = jnp.sum(jnp.where(oh1, pos, 0.0), axis=0, keepdims=True).astype(jnp.int32)
    rank_ref[1:2, :] = jnp.sum(jnp.where(oh2, pos, 0.0), axis=0, keepdims=True).astype(jnp.int32)
    cnt_ref[...] = cnt_ref[...] + jnp.sum(member, axis=1, keepdims=True)


def _router(x1, g2, wr_t, br):
    return pl.pallas_call(
        _router_kernel,
        grid=(SEQ // TM_R,),
        in_specs=[
            pl.BlockSpec((TM_R, D_MODEL), lambda i: (i, 0)),
            pl.BlockSpec((1, D_MODEL), lambda i: (0, 0)),
            pl.BlockSpec((R_ROWS, D_MODEL), lambda i: (0, 0)),
            pl.BlockSpec((R_ROWS, LANES), lambda i: (0, 0)),
        ],
        out_specs=[
            pl.BlockSpec((2, TM_R), lambda i: (0, i)),
            pl.BlockSpec((2, TM_R), lambda i: (0, i)),
            pl.BlockSpec((2, TM_R), lambda i: (0, i)),
            pl.BlockSpec((N_EXPERTS, LANES), lambda i: (0, 0)),
        ],
        out_shape=[
            jax.ShapeDtypeStruct((2, SEQ), jnp.int32),
            jax.ShapeDtypeStruct((2, SEQ), F32),
            jax.ShapeDtypeStruct((2, SEQ), jnp.int32),
            jax.ShapeDtypeStruct((N_EXPERTS, LANES), F32),
        ],
        compiler_params=_cparams(("arbitrary",)),
        name="router",
    )(x1, g2, wr_t, br)


def _dispatch_kernel(slot_ref, padrow_ref, x_ref, g_ref, xs_hbm, hbuf, zbuf, sem, zsem):
    i = pl.program_id(0)
    last = pl.num_programs(0) - 1
    buf = i % 2

    def tile_wait(b):
        for _ in range(2):
            pltpu.make_async_copy(hbuf.at[b], xs_hbm.at[pl.ds(0, TT_D), :], sem.at[b]).wait()

    @pl.when(i == 0)
    def _():
        zbuf[...] = jnp.zeros_like(zbuf)

        for c in range(XS_TAIL // ZROWS):
            pltpu.make_async_copy(zbuf, xs_hbm.at[pl.ds(XS_ROWS - XS_TAIL + c * ZROWS, ZROWS), :], zsem).start()
        for c in range(XS_TAIL // ZROWS):
            pltpu.make_async_copy(zbuf, xs_hbm.at[pl.ds(0, ZROWS), :], zsem).wait()

        def pad(e, c):
            dst = pl.multiple_of(padrow_ref[e], 8)
            pltpu.make_async_copy(zbuf.at[pl.ds(0, 8), :], xs_hbm.at[pl.ds(dst, 8), :], zsem).start()
            return c
        lax.fori_loop(0, N_EXPERTS, pad, 0)

        def padw(e, c):
            pltpu.make_async_copy(zbuf.at[pl.ds(0, 8), :], xs_hbm.at[pl.ds(0, 8), :], zsem).wait()
            return c
        lax.fori_loop(0, N_EXPERTS, padw, 0)

    @pl.when(i >= 2)
    def _():
        tile_wait(buf)

    x = x_ref[...]
    ms = jnp.mean(x * x, axis=-1, keepdims=True)
    hbuf[buf] = x * lax.rsqrt(ms + EPS) * g_ref[...]

    tok0 = i * TT_D

    def send(j8, c):
        for k8 in range(8):
            j = j8 * 8 + k8
            for k in range(2):
                dst = slot_ref[k * SEQ + tok0 + j]
                pltpu.make_async_copy(hbuf.at[buf, pl.ds(j, 1), :], xs_hbm.at[pl.ds(dst, 1), :],
                                      sem.at[buf]).start()
        return c
    lax.fori_loop(0, TT_D // 8, send, 0)

    @pl.when(i == last)
    def _():
        @pl.when(i >= 1)
        def _():
            tile_wait(1 - buf)
        tile_wait(buf)


def _dispatch(slot_flat, padrow, x1, g2):
    grid_spec = pltpu.PrefetchScalarGridSpec(
        num_scalar_prefetch=2,
        grid=(SEQ // TT_D,),
        in_specs=[
            pl.BlockSpec((TT_D, D_MODEL), lambda i, *_: (i, 0)),
            pl.BlockSpec((1, D_MODEL), lambda i, *_: (0, 0)),
        ],
        out_specs=pl.BlockSpec(memory_space=pl.ANY),
        scratch_shapes=[
            pltpu.VMEM((2, TT_D, D_MODEL), F32),
            pltpu.VMEM((ZROWS, D_MODEL), F32),
            pltpu.SemaphoreType.DMA((2,)),
            pltpu.SemaphoreType.DMA,
        ],
    )
    return pl.pallas_call(
        _dispatch_kernel,
        grid_spec=grid_spec,
        out_shape=jax.ShapeDtypeStruct((XS_ROWS, D_MODEL), F32),
        compiler_params=_cparams(("arbitrary",)),
        name="moe_dispatch",
    )(slot_flat, padrow, x1, g2)


def _experts_kernel(ie_ref, row0_ref, rows_ref, nitems_ref,
                    xs_hbm, wg_hbm, wu_hbm, wd_hbm, ys_hbm,
                    xg_ref, y_ref, zbuf, wg_ref, wu_ref, wd_ref, gsem, ssem, zsem, wsem):
    it = pl.program_id(0)
    nitems = nitems_ref[0]
    slot = it % 2
    half = SUB_E // 2
    nsteps = nitems * NF_E

    def weight_copies(s):
        e = ie_ref[s // NF_E]
        cols = pl.ds(pl.multiple_of((s % NF_E) * TF_E, TF_E), TF_E)
        b = s % W_RING
        return (pltpu.make_async_copy(wg_hbm.at[e, :, cols], wg_ref.at[b], wsem.at[b]),
                pltpu.make_async_copy(wu_hbm.at[e, :, cols], wu_ref.at[b], wsem.at[b]),
                pltpu.make_async_copy(wd_hbm.at[e, cols, :], wd_ref.at[b], wsem.at[b]))

    @pl.when(it == 0)
    def _():
        for s in range(W_RING - 1):
            for cp in weight_copies(s):
                cp.start()

    def padded(item):
        return pl.multiple_of(((rows_ref[item] + half - 1) // half) * half, half)

    nhalf = padded(it) // half
    nsub = nhalf // 2

    def chunk_copies(item, start):
        base = pl.multiple_of(row0_ref[item], 8)
        buf = item % 2

        def body(c, carry):
            off = pl.multiple_of(c * half, half)
            src = xs_hbm.at[pl.ds(pl.multiple_of(base + off, 8), half), :]
            dst = ys_hbm.at[pl.ds(pl.multiple_of(base + off, 8), half), :]
            if start == "fetch":
                pltpu.make_async_copy(src, xg_ref.at[buf, pl.ds(off, half), :], gsem.at[buf]).start()
            else:
                pltpu.make_async_copy(y_ref.at[buf, pl.ds(off, half), :], dst, ssem.at[buf]).start()
            return carry
        lax.fori_loop(0, padded(item) // half, body, 0)

    def wait_fetch(item):
        got = pl.ds(0, padded(item))
        buf = item % 2
        pltpu.make_async_copy(xs_hbm.at[got, :], xg_ref.at[buf, got, :], gsem.at[buf]).wait()

    def wait_store(item):
        put = pl.ds(0, padded(item))
        buf = item % 2
        pltpu.make_async_copy(y_ref.at[buf, put, :], ys_hbm.at[put, :], ssem.at[buf]).wait()

    @pl.when(it == 0)
    def _():
        zbuf[...] = jnp.zeros_like(zbuf)
        for c in range(XS_TAIL // ZROWS):
            pltpu.make_async_copy(zbuf, ys_hbm.at[pl.ds(XS_ROWS - XS_TAIL + c * ZROWS, ZROWS), :], zsem).start()
        for c in range(XS_TAIL // ZROWS):
            pltpu.make_async_copy(zbuf, ys_hbm.at[pl.ds(0, ZROWS), :], zsem).wait()
        chunk_copies(0, "fetch")

    wait_fetch(it)

    def clear(s, c):
        rows = pl.ds(pl.multiple_of(s * half, half), half)
        y_ref[slot, rows, :] = jnp.zeros((half, D_MODEL), F32)
        return c
    lax.fori_loop(0, nhalf, clear, 0)

    @pl.when(it + 1 < nitems)
    def _():
        chunk_copies(it + 1, "fetch")

    def chunk_step(f, carry):
        step = it * NF_E + f
        wslot = step % W_RING

        @pl.when(step + W_RING - 1 < nsteps)
        def _():
            for cp in weight_copies(step + W_RING - 1):
                cp.start()

        for cp in weight_copies(step):
            cp.wait()

        def block(rows):
            xb = xg_ref[slot, rows, :].astype(BF16)
            hg = jnp.dot(xb, wg_ref[wslot].astype(BF16), preferred_element_type=F32)
            hu = jnp.dot(xb, wu_ref[wslot].astype(BF16), preferred_element_type=F32)
            h = (hg * jax.nn.sigmoid(hg) * hu).astype(BF16)
            y_ref[slot, rows, :] = y_ref[slot, rows, :] + jnp.dot(h, wd_ref[wslot].astype(BF16),
                                                                   preferred_element_type=F32)

        def sub_pair(s, c):
            block(pl.ds(pl.multiple_of(2 * s * SUB_E, SUB_E), SUB_E))
            block(pl.ds(pl.multiple_of((2 * s + 1) * SUB_E, SUB_E), SUB_E))
            return c
        lax.fori_loop(0, nsub // 2, sub_pair, 0)

        @pl.when(nsub % 2 == 1)
        def _():
            block(pl.ds(pl.multiple_of((nsub - 1) * SUB_E, SUB_E), SUB_E))

        @pl.when(nhalf % 2 == 1)
        def _():
            block(pl.ds(pl.multiple_of(nsub * SUB_E, SUB_E), half))
        return carry

    lax.fori_loop(0, NF_E, chunk_step, 0)

    @pl.when(it > 0)
    def _():
        wait_store(it - 1)

    chunk_copies(it, "store")

    @pl.when(it + 1 >= nitems)
    def _():
        wait_store(it)


def _experts(ie, row0, rows, nitems, xs, wg, wu, wd):
    grid_spec = pltpu.PrefetchScalarGridSpec(
        num_scalar_prefetch=4,
        grid=(nitems[0],),
        in_specs=[pl.BlockSpec(memory_space=pl.ANY) for _ in range(4)],
        out_specs=pl.BlockSpec(memory_space=pl.ANY),
        scratch_shapes=[
            pltpu.VMEM((2, TM_E, D_MODEL), F32),
            pltpu.VMEM((2, TM_E, D_MODEL), F32),
            pltpu.VMEM((ZROWS, D_MODEL), F32),
            pltpu.VMEM((W_RING, D_MODEL, TF_E), F32),
            pltpu.VMEM((W_RING, D_MODEL, TF_E), F32),
            pltpu.VMEM((W_RING, TF_E, D_MODEL), F32),
            pltpu.SemaphoreType.DMA((2,)),
            pltpu.SemaphoreType.DMA((2,)),
            pltpu.SemaphoreType.DMA,
            pltpu.SemaphoreType.DMA((W_RING,)),
        ],
    )
    return pl.pallas_call(
        _experts_kernel,
        grid_spec=grid_spec,
        out_shape=jax.ShapeDtypeStruct((XS_ROWS, D_MODEL), F32),
        compiler_params=_cparams(("arbitrary",)),
        name="moe_experts",
    )(ie, row0, rows, nitems, xs, wg, wu, wd)


def _gcombine_kernel(slot_ref, x_ref, w_ref, ys_hbm, o_ref, dbuf, sem):
    i = pl.program_id(0)
    n = pl.num_programs(0)
    buf = i % 2

    def fetch(tile, b):
        tok0 = tile * TT_G

        def body(j8, c):
            for k8 in range(8):
                j = j8 * 8 + k8
                for k in range(2):
                    src = slot_ref[k * SEQ + tok0 + j]
                    pltpu.make_async_copy(ys_hbm.at[pl.ds(src, 1), :], dbuf.at[b, k, pl.ds(j, 1), :],
                                          sem.at[b]).start()
            return c
        lax.fori_loop(0, TT_G // 8, body, 0)

    @pl.when(i == 0)
    def _():
        fetch(0, 0)

    @pl.when(i + 1 < n)
    def _():
        fetch(i + 1, 1 - buf)

    for k in range(2):
        pltpu.make_async_copy(ys_hbm.at[pl.ds(0, TT_G), :], dbuf.at[buf, k], sem.at[buf]).wait()

    w = w_ref[...]
    o_ref[...] = x_ref[...] + w[:, 0:1] * dbuf[buf, 0] + w[:, 1:2] * dbuf[buf, 1]


def _gcombine(slot_flat, x1, wts_t, ys):
    grid_spec = pltpu.PrefetchScalarGridSpec(
        num_scalar_prefetch=1,
        grid=(SEQ // TT_G,),
        in_specs=[
            pl.BlockSpec((TT_G, D_MODEL), lambda i, *_: (i, 0)),
            pl.BlockSpec((TT_G, 2), lambda i, *_: (i, 0)),
            pl.BlockSpec(memory_space=pl.ANY),
        ],
        out_specs=pl.BlockSpec((TT_G, D_MODEL), lambda i, *_: (i, 0)),
        scratch_shapes=[
            pltpu.VMEM((2, 2, TT_G, D_MODEL), F32),
            pltpu.SemaphoreType.DMA((2,)),
        ],
    )
    return pl.pallas_call(
        _gcombine_kernel,
        grid_spec=grid_spec,
        out_shape=jax.ShapeDtypeStruct((SEQ, D_MODEL), F32),
        compiler_params=_cparams(("arbitrary",)),
        name="moe_combine",
    )(slot_flat, x1, wts_t, ys)


def _routing_tables(eid, rank, counts_f):
    counts = counts_f[:, 0].astype(jnp.int32)
    aligned = ((counts + 7) // 8) * 8
    base = jnp.cumsum(aligned) - aligned
    tiles = (counts + TM_E - 1) // TM_E
    tcum = jnp.cumsum(tiles)
    tstart = tcum - tiles
    nitems = tcum[-1]
    ids = jnp.arange(MAX_ITEMS, dtype=jnp.int32)
    ie = jnp.minimum(jnp.sum(ids[:, None] >= tcum[None, :], axis=1), N_EXPERTS - 1).astype(jnp.int32)
    live = ids < nitems
    sel = ie[:, None] == jnp.arange(N_EXPERTS, dtype=jnp.int32)[None, :]

    def pick(v):
        return jnp.sum(jnp.where(sel, v[None, :], 0), axis=1)

    jt = ids - pick(tstart)
    row0 = jnp.where(live, pick(base) + jt * TM_E, 0)
    rows = jnp.where(live, jnp.clip(pick(counts) - jt * TM_E, 0, TM_E), 0)
    eoh = eid[:, :, None] == jnp.arange(N_EXPERTS, dtype=jnp.int32)
    slot = jnp.sum(jnp.where(eoh, base, 0), axis=-1) + rank
    spare = XS_ROWS - 8 * (1 + jnp.arange(N_EXPERTS, dtype=jnp.int32))
    padrow = jnp.where(counts % 8 != 0, base + (counts // 8) * 8, spare)
    return (ie, row0.astype(jnp.int32), rows.astype(jnp.int32), nitems.reshape(1).astype(jnp.int32),
            slot.reshape(-1).astype(jnp.int32), padrow.astype(jnp.int32))


def kernel(x, norm1_g, w_in, q_norm_g, k_norm_g, conv_w, conv_b, conv_ln_g, conv_ln_b, rel_bias,
           w_out, norm2_g, w_router_group, b_router_group, w_router_expert, b_router_expert,
           w_gate, w_up, w_down):
    assert x.shape == (1, SEQ, D_MODEL) and w_in.shape[0] == 1
    xs = x[0]
    bias_tab = _attn_bias_tables(rel_bias)
    qg2 = jnp.tile(q_norm_g[0], 2)[None]
    kg2 = jnp.tile(k_norm_g[0], 2)[None]

    proj = _inproj(xs, norm1_g[0][None], w_in[0])
    conv_out = _conv_mixer(proj, conv_w[0], conv_b[0][None], conv_ln_g[0][None], conv_ln_b[0][None])
    attn_out = _attention(proj, qg2, kg2, bias_tab)
    x1 = _outproj(xs, conv_out, attn_out, w_out[0])

    wr_t = jnp.concatenate([
        w_router_group[0].T, jnp.zeros((8 - N_GROUPS, D_MODEL), F32),
        jnp.transpose(w_router_expert[0], (0, 2, 1)).reshape(N_EXPERTS, D_MODEL)], axis=0)
    br = jnp.concatenate([b_router_group[0], jnp.zeros((8 - N_GROUPS,), F32),
                          b_router_expert[0].reshape(-1)])
    br = jnp.broadcast_to(br[:, None], (R_ROWS, LANES))
    eid, wts, rank, counts_f = _router(x1, norm2_g[0][None], wr_t, br)

    ie, row0, rows, nitems, slot_flat, padrow = _routing_tables(eid, rank, counts_f)
    xs = _dispatch(slot_flat, padrow, x1, norm2_g[0][None])
    ys = _experts(ie, row0, rows, nitems, xs,
                  w_gate[0].reshape(N_EXPERTS, D_MODEL, D_FF),
                  w_up[0].reshape(N_EXPERTS, D_MODEL, D_FF),
                  w_down[0].reshape(N_EXPERTS, D_FF, D_MODEL))
    out = _gcombine(slot_flat, x1, wts.T, ys)
    return out[None]
```

```python
import math

import numpy as np
import jax
import jax.numpy as jnp
from jax import lax
from jax.experimental import pallas as pl
from jax.experimental.pallas import tpu as pltpu

F32 = jnp.float32
BF16 = jnp.bfloat16

D_MODEL = 2048
SEQ = 8192
N_HEADS = 16
HEAD_DIM = 64
ATTN_W = N_HEADS * HEAD_DIM
CONV_C = D_MODEL - ATTN_W
CONV_K = 31
IN_W = 2 * CONV_C + 3 * ATTN_W
PATTERNS = ((128, 1), (512, 4), (2048, 16))
QBLK = 128
NUM_BUCKETS = 32
MAX_DISTANCE = 2048
N_GROUPS = 4
E_PER_G = 8
N_EXPERTS = N_GROUPS * E_PER_G
D_FF = D_MODEL // 2
EPS = 1e-6
NEG_INF = -1e30
LOG2E = math.log2(math.e)

LANES = 128
VMEM_LIMIT = 56 * 1024 * 1024

TM_IN = 1024
TN_IN = 512
IN_RING = 4
TT_CONV = 512
HALO = 32
R_CONV = 64
R_LN = 16
ATTN_GROUP = 4
ATTN_UNROLL = 4
TM_OUT = 512
TM_R = 512
R_ROWS = 8 + N_EXPERTS
KC_R = 512
TM_E = 768
SUB_E = 256
TF_E = 256
NF_E = D_FF // TF_E
W_RING = 4
N_ASSIGN = 2 * SEQ
MAX_ITEMS = -(-N_ASSIGN // TM_E) + N_EXPERTS
XS_TAIL = 256 + TM_E
XS_ROWS = N_ASSIGN + XS_TAIL
ZROWS = 128
TT_D = 512
TT_G = 512


def _cparams(sem, vmem=VMEM_LIMIT):
    return pltpu.CompilerParams(dimension_semantics=sem, vmem_limit_bytes=vmem)


def _inproj_kernel(x_ref, g_ref, w_hbm, o_hbm, xn_ref, w_ref, ob_ref, wsem, osem):
    i = pl.program_id(0)
    nj = IN_W // TN_IN
    total = pl.num_programs(0) * nj

    def w_copy(s):
        cols = pl.ds(pl.multiple_of((s % nj) * TN_IN, TN_IN), TN_IN)
        b = s % IN_RING
        return pltpu.make_async_copy(w_hbm.at[:, cols], w_ref.at[b], wsem.at[b])

    def o_copy(s):
        rows = pl.ds(pl.multiple_of((s // nj) * TM_IN, TM_IN), TM_IN)
        cols = pl.ds(pl.multiple_of((s % nj) * TN_IN, TN_IN), TN_IN)
        b = s % 2
        return pltpu.make_async_copy(ob_ref.at[b], o_hbm.at[rows, cols], osem.at[b])

    @pl.when(i == 0)
    def _():
        for s in range(IN_RING - 1):
            w_copy(s).start()

    x = x_ref[...]
    ms = jnp.mean(x * x, axis=-1, keepdims=True)
    xn_ref[...] = (x * lax.rsqrt(ms + EPS) * g_ref[...]).astype(BF16)

    def col_step(j, carry):
        s = i * nj + j

        @pl.when(s + IN_RING - 1 < total)
        def _():
            w_copy(s + IN_RING - 1).start()

        @pl.when(s >= 2)
        def _():
            o_copy(s - 2).wait()

        w_copy(s).wait()
        ob_ref[s % 2] = jnp.dot(xn_ref[...], w_ref[s % IN_RING].astype(BF16), preferred_element_type=F32)
        o_copy(s).start()
        return carry

    lax.fori_loop(0, nj, col_step, 0)

    @pl.when(i == pl.num_programs(0) - 1)
    def _():
        o_copy(total - 2).wait()
        o_copy(total - 1).wait()


def _inproj(x, g, w):
    return pl.pallas_call(
        _inproj_kernel,
        grid=(SEQ // TM_IN,),
        in_specs=[
            pl.BlockSpec((TM_IN, D_MODEL), lambda i: (i, 0)),
            pl.BlockSpec((1, D_MODEL), lambda i: (0, 0)),
            pl.BlockSpec(memory_space=pl.ANY),
        ],
        out_specs=pl.BlockSpec(memory_space=pl.ANY),
        out_shape=jax.ShapeDtypeStruct((SEQ, IN_W), F32),
        scratch_shapes=[
            pltpu.VMEM((TM_IN, D_MODEL), BF16),
            pltpu.VMEM((IN_RING, D_MODEL, TN_IN), F32),
            pltpu.VMEM((2, TM_IN, TN_IN), F32),
            pltpu.SemaphoreType.DMA((IN_RING,)),
            pltpu.SemaphoreType.DMA((2,)),
        ],
        compiler_params=_cparams(("arbitrary",)),
        name="inproj",
    )(x, g, w)


def _conv_kernel(val_ref, gate_ref, hval_ref, hgate_ref, cw_ref, cb_ref, lg_ref, lb_ref,
                 o_ref, ubuf, zbuf, ybuf):
    i = pl.program_id(0)
    u = val_ref[...] * jax.nn.sigmoid(gate_ref[...])
    hu = hval_ref[...] * jax.nn.sigmoid(hgate_ref[...])
    hu = jnp.where(i > 0, hu, 0.0)
    for c in range(CONV_C // LANES):
        cols = slice(c * LANES, (c + 1) * LANES)
        ubuf[c, 0:HALO, :] = hu[:, cols]
        ubuf[c, HALO:HALO + TT_CONV, :] = u[:, cols]

    n_a = -(-CONV_K // 8)
    assert HALO == 8 * n_a
    for c in range(CONV_C // LANES):
        cols = slice(c * LANES, (c + 1) * LANES)

        def taps(r, carry, c=c, cols=cols):
            base = pl.multiple_of(r * R_CONV, R_CONV)
            win = ubuf[c, pl.ds(base, R_CONV + HALO), :]
            for b in range(8):
                z = None
                for a in range(n_a):
                    s = 8 * a + b
                    if s >= CONV_K:
                        continue
                    lo = HALO - 8 - 8 * a
                    t = cw_ref[CONV_K - 1 - s:CONV_K - s, cols] * win[lo:lo + R_CONV + 8, :]
                    z = t if z is None else z + t
                zbuf[b, pl.ds(0, R_CONV + 8, stride=2), :] = z
            acc = None
            for b in range(8):
                t = zbuf[b, pl.ds(2 * (8 - b), R_CONV, stride=2), :]
                acc = t if acc is None else acc + t
            ybuf[pl.ds(base, R_CONV), cols] = acc
            return carry

        lax.fori_loop(0, TT_CONV // R_CONV, taps, 0)

    def norm(r, carry):
        rows = pl.ds(pl.multiple_of(r * R_LN, R_LN), R_LN)
        acc = ybuf[rows, :] + cb_ref[...]
        mu = jnp.mean(acc, axis=-1, keepdims=True)
        xc = acc - mu
        var = jnp.mean(xc * xc, axis=-1, keepdims=True)
        y = xc * lax.rsqrt(var + EPS) * lg_ref[...] + lb_ref[...]
        o_ref[rows, :] = (y * jax.nn.sigmoid(y)).astype(BF16)
        return carry

    lax.fori_loop(0, TT_CONV // R_LN, norm, 0, unroll=8)


def _conv_mixer(proj, cw, cb, lg, lb):
    hb = TT_CONV // HALO
    return pl.pallas_call(
        _conv_kernel,
        grid=(SEQ // TT_CONV,),
        in_specs=[
            pl.BlockSpec((TT_CONV, CONV_C), lambda i: (i, 0)),
            pl.BlockSpec((TT_CONV, CONV_C), lambda i: (i, 1)),
            pl.BlockSpec((HALO, CONV_C), lambda i: (jnp.maximum(i * hb - 1, 0), 0)),
            pl.BlockSpec((HALO, CONV_C), lambda i: (jnp.maximum(i * hb - 1, 0), 1)),
            pl.BlockSpec((CONV_K, CONV_C), lambda i: (0, 0)),
            pl.BlockSpec((1, CONV_C), lambda i: (0, 0)),
            pl.BlockSpec((1, CONV_C), lambda i: (0, 0)),
            pl.BlockSpec((1, CONV_C), lambda i: (0, 0)),
        ],
        out_specs=pl.BlockSpec((TT_CONV, CONV_C), lambda i: (i, 0)),
        out_shape=jax.ShapeDtypeStruct((SEQ, CONV_C), BF16),
        scratch_shapes=[pltpu.VMEM((CONV_C // LANES, HALO + TT_CONV, LANES), F32),
                        pltpu.VMEM((8, 2 * (R_CONV + 8), LANES), F32),
                        pltpu.VMEM((TT_CONV, CONV_C), F32)],
        compiler_params=_cparams(("arbitrary",)),
        name="conv_mixer",
    )(proj, proj, proj, proj, cw, cb, lg, lb)


def _t5_bucket_np(dist):
    max_exact = NUM_BUCKETS // 2
    nf = np.maximum(dist, 1).astype(np.float32)
    large = max_exact + (np.log(nf / np.float32(max_exact)) / np.float32(math.log(MAX_DISTANCE / max_exact))
                         * np.float32(NUM_BUCKETS - max_exact)).astype(np.int32)
    large = np.minimum(large, NUM_BUCKETS - 1)
    return np.where(dist < max_exact, dist, large)


def _attn_bias_tables(rel_bias):
    period = 3 * QBLK
    diags = []
    for window, dil in PATTERNS:
        span = window // dil
        assert span <= QBLK
        bucket = _t5_bucket_np(np.arange(span + 1) * dil)
        onehot = np.eye(NUM_BUCKETS, dtype=np.float32)[bucket]
        vec = jnp.einsum("rb,bh->hr", onehot, rel_bias.astype(F32),
                         precision=lax.Precision.HIGHEST)
        diag = jnp.full((N_HEADS, period), NEG_INF, F32)
        diags.append(diag.at[:, 2 * QBLK - 1 - span:2 * QBLK].set(vec[:, ::-1]))
    diag = jnp.stack(diags) * LOG2E
    shifted = jnp.stack([jnp.roll(diag, b, axis=-1) for b in range(8)], axis=-2)
    return shifted.reshape(len(PATTERNS), N_HEADS // 2, 2, 8, period)


def _attn_kernel(q_ref, k_ref, v_ref, qg_ref, kg_ref, diag_ref, o_ref,
                 qn_ref, kn_ref, acc_ref, m_ref, l_ref, bias_s):
    lane = lax.broadcasted_iota(jnp.int32, (QBLK, LANES), 1)
    head_a = lane < HEAD_DIM
    hist_keys = lax.broadcasted_iota(jnp.int32, (2 * QBLK, 2 * QBLK), 1) < QBLK
    ri = lax.broadcasted_iota(jnp.int32, (LANES, LANES), 0) // HEAD_DIM
    ci = lax.broadcasted_iota(jnp.int32, (LANES, LANES), 1) // HEAD_DIM
    seg = (ri == ci).astype(BF16)

    def head_rms(x, g):
        sq = x * x
        hi = sq.astype(BF16)
        lo = (sq - hi.astype(F32)).astype(BF16)
        ss = (jnp.dot(hi, seg, preferred_element_type=F32)
              + jnp.dot(lo, seg, preferred_element_type=F32))
        return x * lax.rsqrt(ss * (1.0 / HEAD_DIM) + EPS) * g

    NCH = 512

    def norm_body(c, carry):
        rows = pl.ds(pl.multiple_of(c * NCH, NCH), NCH)
        qn_ref[rows, :] = head_rms(q_ref[rows, :], qg_ref[...]) * (LOG2E / math.sqrt(HEAD_DIM))
        kn_ref[rows, :] = head_rms(k_ref[rows, :], kg_ref[...])
        return carry

    lax.fori_loop(0, SEQ // NCH, norm_body, 0, unroll=2)

    for p in range(len(PATTERNS)):
        for h in range(2):
            offs = diag_ref[p, h]
            for a in range(QBLK // 8):
                lo = QBLK - 1 - 8 * a
                bias_s[p, h * QBLK + 8 * a:h * QBLK + 8 * a + 8, :] = offs[:, lo:lo + 2 * QBLK]

    order = sorted(range(len(PATTERNS)), key=lambda i: -PATTERNS[i][1])
    assert PATTERNS[order[-1]][1] == 1
    for step, p in enumerate(order):
        dil = PATTERNS[p][1]
        nb = SEQ // (dil * QBLK)
        is_first = step == 0
        is_last = step == len(order) - 1

        def rows(start, dil=dil):
            if dil == 1:
                return pl.ds(start, QBLK)
            return pl.ds(start, QBLK, stride=dil)

        def unit(cur, k_prev, v_prev, first, p=p, rows=rows, is_first=is_first, is_last=is_last):
            q = qn_ref[rows(cur), :]
            q2 = jnp.concatenate([jnp.where(head_a, q, 0.0), jnp.where(head_a, 0.0, q)],
                                 axis=0).astype(BF16)
            k_cur = kn_ref[rows(cur), :].astype(BF16)
            v_cur = v_ref[rows(cur), :].astype(BF16)
            k2 = jnp.concatenate([k_prev, k_cur], axis=0)
            v2 = jnp.concatenate([v_prev, v_cur], axis=0)
            s = lax.dot_general(q2, k2, (((1,), (1,)), ((), ())), preferred_element_type=F32)
            s = s + bias_s[p]
            if first is not None:
                s = jnp.where(hist_keys, jnp.where(first, NEG_INF * LOG2E, s), s)
            m = jnp.max(s, axis=-1, keepdims=True)
            e = jnp.exp2(s - m)
            l = jnp.sum(e, axis=-1, keepdims=True)
            pv = jnp.dot(e.astype(BF16), v2, preferred_element_type=F32)
            o_new = jnp.where(head_a, pv[:QBLK], pv[QBLK:])
            m_new = jnp.where(head_a, m[:QBLK], m[QBLK:])
            l_new = jnp.where(head_a, l[:QBLK], l[QBLK:])
            if is_first:
                acc_ref[rows(cur), :] = o_new
                m_ref[rows(cur), :] = m_new
                l_ref[rows(cur), :] = l_new
            else:
                m_old = m_ref[rows(cur), :]
                m_tot = jnp.maximum(m_old, m_new)
                a = jnp.exp2(m_old - m_tot)
                b = jnp.exp2(m_new - m_tot)
                acc = acc_ref[rows(cur), :] * a + o_new * b
                den = l_ref[rows(cur), :] * a + l_new * b
                if is_last:
                    o_ref[pl.ds(pl.multiple_of(cur, QBLK), QBLK), :] = (acc / den).astype(BF16)
                else:
                    acc_ref[rows(cur), :] = acc
                    l_ref[rows(cur), :] = den
                    m_ref[rows(cur), :] = m_tot
            return k_cur, v_cur

        per = nb // ATTN_GROUP
        assert per * ATTN_GROUP == nb

        def group(g, carry, dil=dil, per=per, rows=rows, unit=unit):
            r = g // per
            n0 = (g - r * per) * ATTN_GROUP
            hist = jnp.maximum(n0 - 1, 0) * (QBLK * dil) + r
            k_prev = kn_ref[rows(hist), :].astype(BF16)
            v_prev = v_ref[rows(hist), :].astype(BF16)
            for i in range(ATTN_GROUP):
                first = (n0 == 0) if i == 0 else None
                k_prev, v_prev = unit((n0 + i) * (QBLK * dil) + r, k_prev, v_prev, first)
            return carry

        lax.fori_loop(0, SEQ // (QBLK * ATTN_GROUP), group, 0, unroll=ATTN_UNROLL)


def _attention(proj, qg2, kg2, bias_tab):
    qoff = 2 * CONV_C // LANES
    koff = qoff + ATTN_W // LANES
    voff = koff + ATTN_W // LANES
    return pl.pallas_call(
        _attn_kernel,
        grid=(N_HEADS // 2,),
        in_specs=[
            pl.BlockSpec((SEQ, LANES), lambda h: (0, qoff + h)),
            pl.BlockSpec((SEQ, LANES), lambda h: (0, koff + h)),
            pl.BlockSpec((SEQ, LANES), lambda h: (0, voff + h)),
            pl.BlockSpec((1, LANES), lambda h: (0, 0)),
            pl.BlockSpec((1, LANES), lambda h: (0, 0)),
            pl.BlockSpec((len(PATTERNS), None, 2, 8, 3 * QBLK), lambda h: (0, h, 0, 0, 0)),
        ],
        out_specs=pl.BlockSpec((SEQ, LANES), lambda h: (0, h)),
        out_shape=jax.ShapeDtypeStruct((SEQ, ATTN_W), BF16),
        scratch_shapes=[pltpu.VMEM((SEQ, LANES), F32) for _ in range(5)]
        + [pltpu.VMEM((len(PATTERNS), 2 * QBLK, 2 * QBLK), F32)],
        compiler_params=_cparams(("arbitrary",)),
        name="dilated_attn",
    )(proj, proj, proj, qg2, kg2, bias_tab)


def _outproj_kernel(x_ref, c_ref, a_ref, wc_ref, wa_ref, o_ref):
    o_ref[...] = (x_ref[...]
                  + jnp.dot(c_ref[...], wc_ref[...].astype(BF16), preferred_element_type=F32)
                  + jnp.dot(a_ref[...], wa_ref[...].astype(BF16), preferred_element_type=F32))


def _outproj(x, conv_out, attn_out, w_out):
    return pl.pallas_call(
        _outproj_kernel,
        grid=(SEQ // TM_OUT,),
        in_specs=[
            pl.BlockSpec((TM_OUT, D_MODEL), lambda i: (i, 0)),
            pl.BlockSpec((TM_OUT, CONV_C), lambda i: (i, 0)),
            pl.BlockSpec((TM_OUT, ATTN_W), lambda i: (i, 0)),
            pl.BlockSpec((CONV_C, D_MODEL), lambda i: (0, 0), pipeline_mode=pl.Buffered(1)),
            pl.BlockSpec((ATTN_W, D_MODEL), lambda i: (1, 0), pipeline_mode=pl.Buffered(1)),
        ],
        out_specs=pl.BlockSpec((TM_OUT, D_MODEL), lambda i: (i, 0)),
        out_shape=jax.ShapeDtypeStruct((SEQ, D_MODEL), F32),
        compiler_params=_cparams(("arbitrary",)),
        name="outproj",
    )(x, conv_out, attn_out, w_out, w_out)


def _split2(a):
    a1 = a.astype(BF16)
    a2 = (a - a1.astype(F32)).astype(BF16)
    return a1, a2


def _router_kernel(x_ref, g_ref, wr_ref, br_ref, eid_ref, wts_ref, rank_ref, cnt_ref):
    i = pl.program_id(0)

    @pl.when(i == 0)
    def _():
        cnt_ref[...] = jnp.zeros_like(cnt_ref)

    x = x_ref[...]
    inv = lax.rsqrt(jnp.mean(x * x, axis=-1, keepdims=True) + EPS)
    dn = (((1,), (1,)), ((), ()))
    lt = None
    for c in range(D_MODEL // KC_R):
        cols = slice(c * KC_R, (c + 1) * KC_R)
        h1, h2 = _split2(x_ref[:, cols] * inv * g_ref[:, cols])
        w1, w2 = _split2(wr_ref[:, cols])
        for wa, ha in ((w1, h1), (w1, h2), (w2, h1)):
            t = lax.dot_general(wa, ha, dn, preferred_element_type=F32)
            lt = t if lt is None else lt + t
    lt = lt + br_ref[:, 0:1]

    row8 = lax.broadcasted_iota(jnp.int32, (8, TM_R), 0)
    gl = jnp.where(row8 < N_GROUPS, lt[0:8], -jnp.inf)
    gmax = jnp.max(gl, axis=0, keepdims=True)
    gidx = jnp.min(jnp.where(gl == gmax, row8, 8), axis=0, keepdims=True)
    gw = 1.0 / jnp.sum(jnp.exp(gl - gmax), axis=0, keepdims=True)

    esel = lt[8:16]
    for g in range(1, N_GROUPS):
        esel = jnp.where(gidx == g, lt[8 + 8 * g:16 + 8 * g], esel)
    v1 = jnp.max(esel, axis=0, keepdims=True)
    i1 = jnp.min(jnp.where(esel == v1, row8, 8), axis=0, keepdims=True)
    rest = jnp.where(row8 == i1, -jnp.inf, esel)
    v2 = jnp.max(rest, axis=0, keepdims=True)
    i2 = jnp.min(jnp.where(rest == v2, row8, 8), axis=0, keepdims=True)
    e21 = jnp.exp(v2 - v1)
    den = 1.0 + e21
    e1 = gidx * E_PER_G + i1
    e2 = gidx * E_PER_G + i2
    eid_ref[0:1, :] = e1
    eid_ref[1:2, :] = e2
    wts_ref[0:1, :] = gw * (1.0 / den)
    wts_ref[1:2, :] = gw * (e21 / den)

    erow = lax.broadcasted_iota(jnp.int32, (N_EXPERTS, TM_R), 0)
    oh1 = erow == e1
    oh2 = erow == e2
    member = jnp.where(oh1 | oh2, 1.0, 0.0)
    ti = lax.broadcasted_iota(jnp.int32, (TM_R, TM_R), 0)
    tj = lax.broadcasted_iota(jnp.int32, (TM_R, TM_R), 1)
    upper = jnp.where(ti < tj, 1.0, 0.0).astype(BF16)
    before = jnp.dot(member.astype(BF16), upper, preferred_element_type=F32)
    pos = before + cnt_ref[:, 0:1]
    rank_ref[0:1, :] = jnp.sum(jnp.where(oh1, pos, 0.0), axis=0, keepdims=True).astype(jnp.int32)
    rank_ref[1:2, :] = jnp.sum(jnp.where(oh2, pos, 0.0), axis=0, keepdims=True).astype(jnp.int32)
    cnt_ref[...] = cnt_ref[...] + jnp.sum(member, axis=1, keepdims=True)


def _router(x1, g2, wr_t, br):
    return pl.pallas_call(
        _router_kernel,
        grid=(SEQ // TM_R,),
        in_specs=[
            pl.BlockSpec((TM_R, D_MODEL), lambda i: (i, 0)),
            pl.BlockSpec((1, D_MODEL), lambda i: (0, 0)),
            pl.BlockSpec((R_ROWS, D_MODEL), lambda i: (0, 0)),
            pl.BlockSpec((R_ROWS, LANES), lambda i: (0, 0)),
        ],
        out_specs=[
            pl.BlockSpec((2, TM_R), lambda i: (0, i)),
            pl.BlockSpec((2, TM_R), lambda i: (0, i)),
            pl.BlockSpec((2, TM_R), lambda i: (0, i)),
            pl.BlockSpec((N_EXPERTS, LANES), lambda i: (0, 0)),
        ],
        out_shape=[
            jax.ShapeDtypeStruct((2, SEQ), jnp.int32),
            jax.ShapeDtypeStruct((2, SEQ), F32),
            jax.ShapeDtypeStruct((2, SEQ), jnp.int32),
            jax.ShapeDtypeStruct((N_EXPERTS, LANES), F32),
        ],
        compiler_params=_cparams(("arbitrary",)),
        name="router",
    )(x1, g2, wr_t, br)


def _dispatch_kernel(slot_ref, padrow_ref, x_ref, g_ref, xs_hbm, hbuf, zbuf, sem, zsem):
    i = pl.program_id(0)
    last = pl.num_programs(0) - 1
    buf = i % 2

    def tile_wait(b):
        for _ in range(2):
            pltpu.make_async_copy(hbuf.at[b], xs_hbm.at[pl.ds(0, TT_D), :], sem.at[b]).wait()

    @pl.when(i == 0)
    def _():
        zbuf[...] = jnp.zeros_like(zbuf)

        for c in range(XS_TAIL // ZROWS):
            pltpu.make_async_copy(zbuf, xs_hbm.at[pl.ds(XS_ROWS - XS_TAIL + c * ZROWS, ZROWS), :], zsem).start()
        for c in range(XS_TAIL // ZROWS):
            pltpu.make_async_copy(zbuf, xs_hbm.at[pl.ds(0, ZROWS), :], zsem).wait()

        def pad(e, c):
            dst = pl.multiple_of(padrow_ref[e], 8)
            pltpu.make_async_copy(zbuf.at[pl.ds(0, 8), :], xs_hbm.at[pl.ds(dst, 8), :], zsem).start()
            return c
        lax.fori_loop(0, N_EXPERTS, pad, 0)

        def padw(e, c):
            pltpu.make_async_copy(zbuf.at[pl.ds(0, 8), :], xs_hbm.at[pl.ds(0, 8), :], zsem).wait()
            return c
        lax.fori_loop(0, N_EXPERTS, padw, 0)

    @pl.when(i >= 2)
    def _():
        tile_wait(buf)

    x = x_ref[...]
    ms = jnp.mean(x * x, axis=-1, keepdims=True)
    hbuf[buf] = x * lax.rsqrt(ms + EPS) * g_ref[...]

    tok0 = i * TT_D

    def send(j8, c):
        for k8 in range(8):
            j = j8 * 8 + k8
            for k in range(2):
                dst = slot_ref[k * SEQ + tok0 + j]
                pltpu.make_async_copy(hbuf.at[buf, pl.ds(j, 1), :], xs_hbm.at[pl.ds(dst, 1), :],
                                      sem.at[buf]).start()
        return c
    lax.fori_loop(0, TT_D // 8, send, 0)

    @pl.when(i == last)
    def _():
        @pl.when(i >= 1)
        def _():
            tile_wait(1 - buf)
        tile_wait(buf)


def _dispatch(slot_flat, padrow, x1, g2):
    grid_spec = pltpu.PrefetchScalarGridSpec(
        num_scalar_prefetch=2,
        grid=(SEQ // TT_D,),
        in_specs=[
            pl.BlockSpec((TT_D, D_MODEL), lambda i, *_: (i, 0)),
            pl.BlockSpec((1, D_MODEL), lambda i, *_: (0, 0)),
        ],
        out_specs=pl.BlockSpec(memory_space=pl.ANY),
        scratch_shapes=[
            pltpu.VMEM((2, TT_D, D_MODEL), F32),
            pltpu.VMEM((ZROWS, D_MODEL), F32),
            pltpu.SemaphoreType.DMA((2,)),
            pltpu.SemaphoreType.DMA,
        ],
    )
    return pl.pallas_call(
        _dispatch_kernel,
        grid_spec=grid_spec,
        out_shape=jax.ShapeDtypeStruct((XS_ROWS, D_MODEL), F32),
        compiler_params=_cparams(("arbitrary",)),
        name="moe_dispatch",
    )(slot_flat, padrow, x1, g2)


def _experts_kernel(ie_ref, row0_ref, rows_ref, nitems_ref,
                    xs_hbm, wg_hbm, wu_hbm, wd_hbm, ys_hbm,
                    xg_ref, y_ref, zbuf, wg_ref, wu_ref, wd_ref, gsem, ssem, zsem, wsem):
    it = pl.program_id(0)
    nitems = nitems_ref[0]
    slot = it % 2
    half = SUB_E // 2
    nsteps = nitems * NF_E

    def weight_copies(s):
        e = ie_ref[s // NF_E]
        cols = pl.ds(pl.multiple_of((s % NF_E) * TF_E, TF_E), TF_E)
        b = s % W_RING
        return (pltpu.make_async_copy(wg_hbm.at[e, :, cols], wg_ref.at[b], wsem.at[b]),
                pltpu.make_async_copy(wu_hbm.at[e, :, cols], wu_ref.at[b], wsem.at[b]),
                pltpu.make_async_copy(wd_hbm.at[e, cols, :], wd_ref.at[b], wsem.at[b]))

    @pl.when(it == 0)
    def _():
        for s in range(W_RING - 1):
            for cp in weight_copies(s):
                cp.start()

    def padded(item):
        return pl.multiple_of(((rows_ref[item] + half - 1) // half) * half, half)

    nhalf = padded(it) // half
    nsub = nhalf // 2

    def chunk_copies(item, start):
        base = pl.multiple_of(row0_ref[item], 8)
        buf = item % 2

        def body(c, carry):
            off = pl.multiple_of(c * half, half)
            src = xs_hbm.at[pl.ds(pl.multiple_of(base + off, 8), half), :]
            dst = ys_hbm.at[pl.ds(pl.multiple_of(base + off, 8), half), :]
            if start == "fetch":
                pltpu.make_async_copy(src, xg_ref.at[buf, pl.ds(off, half), :], gsem.at[buf]).start()
            else:
                pltpu.make_async_copy(y_ref.at[buf, pl.ds(off, half), :], dst, ssem.at[buf]).start()
            return carry
        lax.fori_loop(0, padded(item) // half, body, 0)

    def wait_fetch(item):
        got = pl.ds(0, padded(item))
        buf = item % 2
        pltpu.make_async_copy(xs_hbm.at[got, :], xg_ref.at[buf, got, :], gsem.at[buf]).wait()

    def wait_store(item):
        put = pl.ds(0, padded(item))
        buf = item % 2
        pltpu.make_async_copy(y_ref.at[buf, put, :], ys_hbm.at[put, :], ssem.at[buf]).wait()

    @pl.when(it == 0)
    def _():
        zbuf[...] = jnp.zeros_like(zbuf)
        for c in range(XS_TAIL // ZROWS):
            pltpu.make_async_copy(zbuf, ys_hbm.at[pl.ds(XS_ROWS - XS_TAIL + c * ZROWS, ZROWS), :], zsem).start()
        for c in range(XS_TAIL // ZROWS):
            pltpu.make_async_copy(zbuf, ys_hbm.at[pl.ds(0, ZROWS), :], zsem).wait()
        chunk_copies(0, "fetch")

    wait_fetch(it)

    def clear(s, c):
        rows = pl.ds(pl.multiple_of(s * half, half), half)
        y_ref[slot, rows, :] = jnp.zeros((half, D_MODEL), F32)
        return c
    lax.fori_loop(0, nhalf, clear, 0)

    @pl.when(it + 1 < nitems)
    def _():
        chunk_copies(it + 1, "fetch")

    def chunk_step(f, carry):
        step = it * NF_E + f
        wslot = step % W_RING

        @pl.when(step + W_RING - 1 < nsteps)
        def _():
            for cp in weight_copies(step + W_RING - 1):
                cp.start()

        for cp in weight_copies(step):
            cp.wait()

        def block(rows):
            xb = xg_ref[slot, rows, :].astype(BF16)
            hg = jnp.dot(xb, wg_ref[wslot].astype(BF16), preferred_element_type=F32)
            hu = jnp.dot(xb, wu_ref[wslot].astype(BF16), preferred_element_type=F32)
            h = (hg * jax.nn.sigmoid(hg) * hu).astype(BF16)
            y_ref[slot, rows, :] = y_ref[slot, rows, :] + jnp.dot(h, wd_ref[wslot].astype(BF16),
                                                                   preferred_element_type=F32)

        def full(s):
            return pl.ds(pl.multiple_of(s * SUB_E, SUB_E), SUB_E)

        has_tail = nhalf % 2 == 1
        odd = nsub % 2 == 1
        tail_rows = pl.ds(pl.multiple_of(nsub * SUB_E, SUB_E), half)
        npairs = jnp.where(has_tail & jnp.logical_not(odd) & (nsub >= 2), nsub // 2 - 1, nsub // 2)

        def sub_pair(s, c):
            block(full(2 * s))
            block(full(2 * s + 1))
            return c
        lax.fori_loop(0, npairs, sub_pair, 0)

        @pl.when(odd & jnp.logical_not(has_tail))
        def _():
            block(full(nsub - 1))

        @pl.when(odd & has_tail)
        def _():
            block(full(nsub - 1))
            block(tail_rows)

        @pl.when(jnp.logical_not(odd) & has_tail & (nsub >= 2))
        def _():
            block(full(nsub - 2))
            block(full(nsub - 1))
            block(tail_rows)

        @pl.when(has_tail & (nsub == 0))
        def _():
            block(tail_rows)
        return carry

    lax.fori_loop(0, NF_E, chunk_step, 0)

    @pl.when(it > 0)
    def _():
        wait_store(it - 1)

    chunk_copies(it, "store")

    @pl.when(it + 1 >= nitems)
    def _():
        wait_store(it)


def _experts(ie, row0, rows, nitems, xs, wg, wu, wd):
    grid_spec = pltpu.PrefetchScalarGridSpec(
        num_scalar_prefetch=4,
        grid=(nitems[0],),
        in_specs=[pl.BlockSpec(memory_space=pl.ANY) for _ in range(4)],
        out_specs=pl.BlockSpec(memory_space=pl.ANY),
        scratch_shapes=[
            pltpu.VMEM((2, TM_E, D_MODEL), F32),
            pltpu.VMEM((2, TM_E, D_MODEL), F32),
            pltpu.VMEM((ZROWS, D_MODEL), F32),
            pltpu.VMEM((W_RING, D_MODEL, TF_E), F32),
            pltpu.VMEM((W_RING, D_MODEL, TF_E), F32),
            pltpu.VMEM((W_RING, TF_E, D_MODEL), F32),
            pltpu.SemaphoreType.DMA((2,)),
            pltpu.SemaphoreType.DMA((2,)),
            pltpu.SemaphoreType.DMA,
            pltpu.SemaphoreType.DMA((W_RING,)),
        ],
    )
    return pl.pallas_call(
        _experts_kernel,
        grid_spec=grid_spec,
        out_shape=jax.ShapeDtypeStruct((XS_ROWS, D_MODEL), F32),
        compiler_params=_cparams(("arbitrary",)),
        name="moe_experts",
    )(ie, row0, rows, nitems, xs, wg, wu, wd)


def _gcombine_kernel(slot_ref, x_ref, w_ref, ys_hbm, o_ref, dbuf, sem):
    i = pl.program_id(0)
    n = pl.num_programs(0)
    buf = i % 2

    def fetch(tile, b):
        tok0 = tile * TT_G

        def body(j8, c):
            for k8 in range(8):
                j = j8 * 8 + k8
                for k in range(2):
                    src = slot_ref[k * SEQ + tok0 + j]
                    pltpu.make_async_copy(ys_hbm.at[pl.ds(src, 1), :], dbuf.at[b, k, pl.ds(j, 1), :],
                                          sem.at[b]).start()
            return c
        lax.fori_loop(0, TT_G // 8, body, 0)

    @pl.when(i == 0)
    def _():
        fetch(0, 0)

    @pl.when(i + 1 < n)
    def _():
        fetch(i + 1, 1 - buf)

    for k in range(2):
        pltpu.make_async_copy(ys_hbm.at[pl.ds(0, TT_G), :], dbuf.at[buf, k], sem.at[buf]).wait()

    w = w_ref[...]
    o_ref[...] = x_ref[...] + w[:, 0:1] * dbuf[buf, 0] + w[:, 1:2] * dbuf[buf, 1]


def _gcombine(slot_flat, x1, wts_t, ys):
    grid_spec = pltpu.PrefetchScalarGridSpec(
        num_scalar_prefetch=1,
        grid=(SEQ // TT_G,),
        in_specs=[
            pl.BlockSpec((TT_G, D_MODEL), lambda i, *_: (i, 0)),
            pl.BlockSpec((TT_G, 2), lambda i, *_: (i, 0)),
            pl.BlockSpec(memory_space=pl.ANY),
        ],
        out_specs=pl.BlockSpec((TT_G, D_MODEL), lambda i, *_: (i, 0)),
        scratch_shapes=[
            pltpu.VMEM((2, 2, TT_G, D_MODEL), F32),
            pltpu.SemaphoreType.DMA((2,)),
        ],
    )
    return pl.pallas_call(
        _gcombine_kernel,
        grid_spec=grid_spec,
        out_shape=jax.ShapeDtypeStruct((SEQ, D_MODEL), F32),
        compiler_params=_cparams(("arbitrary",)),
        name="moe_combine",
    )(slot_flat, x1, wts_t, ys)


def _routing_tables(eid, rank, counts_f):
    counts = counts_f[:, 0].astype(jnp.int32)
    aligned = ((counts + 7) // 8) * 8
    base = jnp.cumsum(aligned) - aligned
    tiles = (counts + TM_E - 1) // TM_E
    tcum = jnp.cumsum(tiles)
    tstart = tcum - tiles
    nitems = tcum[-1]
    ids = jnp.arange(MAX_ITEMS, dtype=jnp.int32)
    ie = jnp.minimum(jnp.sum(ids[:, None] >= tcum[None, :], axis=1), N_EXPERTS - 1).astype(jnp.int32)
    live = ids < nitems
    sel = ie[:, None] == jnp.arange(N_EXPERTS, dtype=jnp.int32)[None, :]

    def pick(v):
        return jnp.sum(jnp.where(sel, v[None, :], 0), axis=1)

    jt = ids - pick(tstart)
    row0 = jnp.where(live, pick(base) + jt * TM_E, 0)
    rows = jnp.where(live, jnp.clip(pick(counts) - jt * TM_E, 0, TM_E), 0)
    eoh = eid[:, :, None] == jnp.arange(N_EXPERTS, dtype=jnp.int32)
    slot = jnp.sum(jnp.where(eoh, base, 0), axis=-1) + rank
    spare = XS_ROWS - 8 * (1 + jnp.arange(N_EXPERTS, dtype=jnp.int32))
    padrow = jnp.where(counts % 8 != 0, base + (counts // 8) * 8, spare)
    return (ie, row0.astype(jnp.int32), rows.astype(jnp.int32), nitems.reshape(1).astype(jnp.int32),
            slot.reshape(-1).astype(jnp.int32), padrow.astype(jnp.int32))


def kernel(x, norm1_g, w_in, q_norm_g, k_norm_g, conv_w, conv_b, conv_ln_g, conv_ln_b, rel_bias,
           w_out, norm2_g, w_router_group, b_router_group, w_router_expert, b_router_expert,
           w_gate, w_up, w_down):
    assert x.shape == (1, SEQ, D_MODEL) and w_in.shape[0] == 1
    xs = x[0]
    bias_tab = _attn_bias_tables(rel_bias)
    qg2 = jnp.tile(q_norm_g[0], 2)[None]
    kg2 = jnp.tile(k_norm_g[0], 2)[None]

    proj = _inproj(xs, norm1_g[0][None], w_in[0])
    conv_out = _conv_mixer(proj, conv_w[0], conv_b[0][None], conv_ln_g[0][None], conv_ln_b[0][None])
    attn_out = _attention(proj, qg2, kg2, bias_tab)
    x1 = _outproj(xs, conv_out, attn_out, w_out[0])

    wr_t = jnp.concatenate([
        w_router_group[0].T, jnp.zeros((8 - N_GROUPS, D_MODEL), F32),
        jnp.transpose(w_router_expert[0], (0, 2, 1)).reshape(N_EXPERTS, D_MODEL)], axis=0)
    br = jnp.concatenate([b_router_group[0], jnp.zeros((8 - N_GROUPS,), F32),
                          b_router_expert[0].reshape(-1)])
    br = jnp.broadcast_to(br[:, None], (R_ROWS, LANES))
    eid, wts, rank, counts_f = _router(x1, norm2_g[0][None], wr_t, br)

    ie, row0, rows, nitems, slot_flat, padrow = _routing_tables(eid, rank, counts_f)
    xs = _dispatch(slot_flat, padrow, x1, norm2_g[0][None])
    ys = _experts(ie, row0, rows, nitems, xs,
                  w_gate[0].reshape(N_EXPERTS, D_MODEL, D_FF),
                  w_up[0].reshape(N_EXPERTS, D_MODEL, D_FF),
                  w_down[0].reshape(N_EXPERTS, D_FF, D_MODEL))
    out = _gcombine(slot_flat, x1, wts.T, ys)
    return out[None]
```

```python
import math

import numpy as np
import jax
import jax.numpy as jnp
from jax import lax
from jax.experimental import pallas as pl
from jax.experimental.pallas import tpu as pltpu

F32 = jnp.float32
BF16 = jnp.bfloat16

D_MODEL = 2048
SEQ = 8192
N_HEADS = 16
HEAD_DIM = 64
ATTN_W = N_HEADS * HEAD_DIM
CONV_C = D_MODEL - ATTN_W
CONV_K = 31
IN_W = 2 * CONV_C + 3 * ATTN_W
PATTERNS = ((128, 1), (512, 4), (2048, 16))
QBLK = 128
NUM_BUCKETS = 32
MAX_DISTANCE = 2048
N_GROUPS = 4
E_PER_G = 8
N_EXPERTS = N_GROUPS * E_PER_G
D_FF = D_MODEL // 2
EPS = 1e-6
NEG_INF = -1e30
LOG2E = math.log2(math.e)

LANES = 128
VMEM_LIMIT = 56 * 1024 * 1024

TM_IN = 1024
TN_IN = 512
IN_RING = 4
TT_CONV = 512
HALO = 32
R_CONV = 64
R_LN = 16
ATTN_GROUP = 4
ATTN_UNROLL = 4
TM_OUT = 512
TM_R = 512
R_ROWS = 8 + N_EXPERTS
KC_R = 512
TM_E = 768
SUB_E = 256
TF_E = 256
NF_E = D_FF // TF_E
W_RING = 4
N_ASSIGN = 2 * SEQ
MAX_ITEMS = -(-N_ASSIGN // TM_E) + N_EXPERTS
XS_TAIL = 256 + TM_E
XS_ROWS = N_ASSIGN + XS_TAIL
ZROWS = 128
TT_D = 512
TT_G = 512


def _cparams(sem, vmem=VMEM_LIMIT):
    return pltpu.CompilerParams(dimension_semantics=sem, vmem_limit_bytes=vmem)


def _inproj_kernel(x_ref, g_ref, w_hbm, o_hbm, xn_ref, w_ref, ob_ref, wsem, osem):
    i = pl.program_id(0)
    nj = IN_W // TN_IN
    total = pl.num_programs(0) * nj

    def w_copy(s):
        cols = pl.ds(pl.multiple_of((s % nj) * TN_IN, TN_IN), TN_IN)
        b = s % IN_RING
        return pltpu.make_async_copy(w_hbm.at[:, cols], w_ref.at[b], wsem.at[b])

    def o_copy(s):
        rows = pl.ds(pl.multiple_of((s // nj) * TM_IN, TM_IN), TM_IN)
        cols = pl.ds(pl.multiple_of((s % nj) * TN_IN, TN_IN), TN_IN)
        b = s % 2
        return pltpu.make_async_copy(ob_ref.at[b], o_hbm.at[rows, cols], osem.at[b])

    @pl.when(i == 0)
    def _():
        for s in range(IN_RING - 1):
            w_copy(s).start()

    x = x_ref[...]
    ms = jnp.mean(x * x, axis=-1, keepdims=True)
    xn_ref[...] = (x * lax.rsqrt(ms + EPS) * g_ref[...]).astype(BF16)

    def col_step(j, carry):
        s = i * nj + j

        @pl.when(s + IN_RING - 1 < total)
        def _():
            w_copy(s + IN_RING - 1).start()

        @pl.when(s >= 2)
        def _():
            o_copy(s - 2).wait()

        w_copy(s).wait()
        ob_ref[s % 2] = jnp.dot(xn_ref[...], w_ref[s % IN_RING].astype(BF16), preferred_element_type=F32)
        o_copy(s).start()
        return carry

    lax.fori_loop(0, nj, col_step, 0)

    @pl.when(i == pl.num_programs(0) - 1)
    def _():
        o_copy(total - 2).wait()
        o_copy(total - 1).wait()


def _inproj(x, g, w):
    return pl.pallas_call(
        _inproj_kernel,
        grid=(SEQ // TM_IN,),
        in_specs=[
            pl.BlockSpec((TM_IN, D_MODEL), lambda i: (i, 0)),
            pl.BlockSpec((1, D_MODEL), lambda i: (0, 0)),
            pl.BlockSpec(memory_space=pl.ANY),
        ],
        out_specs=pl.BlockSpec(memory_space=pl.ANY),
        out_shape=jax.ShapeDtypeStruct((SEQ, IN_W), F32),
        scratch_shapes=[
            pltpu.VMEM((TM_IN, D_MODEL), BF16),
            pltpu.VMEM((IN_RING, D_MODEL, TN_IN), F32),
            pltpu.VMEM((2, TM_IN, TN_IN), F32),
            pltpu.SemaphoreType.DMA((IN_RING,)),
            pltpu.SemaphoreType.DMA((2,)),
        ],
        compiler_params=_cparams(("arbitrary",)),
        name="inproj",
    )(x, g, w)


def _conv_kernel(val_ref, gate_ref, hval_ref, hgate_ref, cw_ref, cb_ref, lg_ref, lb_ref,
                 o_ref, ubuf, zbuf, ybuf):
    i = pl.program_id(0)
    u = val_ref[...] * jax.nn.sigmoid(gate_ref[...])
    hu = hval_ref[...] * jax.nn.sigmoid(hgate_ref[...])
    hu = jnp.where(i > 0, hu, 0.0)
    for c in range(CONV_C // LANES):
        cols = slice(c * LANES, (c + 1) * LANES)
        ubuf[c, 0:HALO, :] = hu[:, cols]
        ubuf[c, HALO:HALO + TT_CONV, :] = u[:, cols]

    n_a = -(-CONV_K // 8)
    assert HALO == 8 * n_a
    for c in range(CONV_C // LANES):
        cols = slice(c * LANES, (c + 1) * LANES)

        def taps(r, carry, c=c, cols=cols):
            base = pl.multiple_of(r * R_CONV, R_CONV)
            win = ubuf[c, pl.ds(base, R_CONV + HALO), :]
            for b in range(8):
                z = None
                for a in range(n_a):
                    s = 8 * a + b
                    if s >= CONV_K:
                        continue
                    lo = HALO - 8 - 8 * a
                    t = cw_ref[CONV_K - 1 - s:CONV_K - s, cols] * win[lo:lo + R_CONV + 8, :]
                    z = t if z is None else z + t
                zbuf[b, pl.ds(0, R_CONV + 8, stride=2), :] = z
            acc = None
            for b in range(8):
                t = zbuf[b, pl.ds(2 * (8 - b), R_CONV, stride=2), :]
                acc = t if acc is None else acc + t
            ybuf[pl.ds(base, R_CONV), cols] = acc
            return carry

        lax.fori_loop(0, TT_CONV // R_CONV, taps, 0)

    def norm(r, carry):
        rows = pl.ds(pl.multiple_of(r * R_LN, R_LN), R_LN)
        acc = ybuf[rows, :] + cb_ref[...]
        mu = jnp.mean(acc, axis=-1, keepdims=True)
        xc = acc - mu
        var = jnp.mean(xc * xc, axis=-1, keepdims=True)
        y = xc * lax.rsqrt(var + EPS) * lg_ref[...] + lb_ref[...]
        o_ref[rows, :] = (y * jax.nn.sigmoid(y)).astype(BF16)
        return carry

    lax.fori_loop(0, TT_CONV // R_LN, norm, 0, unroll=8)


def _conv_mixer(proj, cw, cb, lg, lb):
    hb = TT_CONV // HALO
    return pl.pallas_call(
        _conv_kernel,
        grid=(SEQ // TT_CONV,),
        in_specs=[
            pl.BlockSpec((TT_CONV, CONV_C), lambda i: (i, 0)),
            pl.BlockSpec((TT_CONV, CONV_C), lambda i: (i, 1)),
            pl.BlockSpec((HALO, CONV_C), lambda i: (jnp.maximum(i * hb - 1, 0), 0)),
            pl.BlockSpec((HALO, CONV_C), lambda i: (jnp.maximum(i * hb - 1, 0), 1)),
            pl.BlockSpec((CONV_K, CONV_C), lambda i: (0, 0)),
            pl.BlockSpec((1, CONV_C), lambda i: (0, 0)),
            pl.BlockSpec((1, CONV_C), lambda i: (0, 0)),
            pl.BlockSpec((1, CONV_C), lambda i: (0, 0)),
        ],
        out_specs=pl.BlockSpec((TT_CONV, CONV_C), lambda i: (i, 0)),
        out_shape=jax.ShapeDtypeStruct((SEQ, CONV_C), BF16),
        scratch_shapes=[pltpu.VMEM((CONV_C // LANES, HALO + TT_CONV, LANES), F32),
                        pltpu.VMEM((8, 2 * (R_CONV + 8), LANES), F32),
                        pltpu.VMEM((TT_CONV, CONV_C), F32)],
        compiler_params=_cparams(("arbitrary",)),
        name="conv_mixer",
    )(proj, proj, proj, proj, cw, cb, lg, lb)


def _t5_bucket_np(dist):
    max_exact = NUM_BUCKETS // 2
    nf = np.maximum(dist, 1).astype(np.float32)
    large = max_exact + (np.log(nf / np.float32(max_exact)) / np.float32(math.log(MAX_DISTANCE / max_exact))
                         * np.float32(NUM_BUCKETS - max_exact)).astype(np.int32)
    large = np.minimum(large, NUM_BUCKETS - 1)
    return np.where(dist < max_exact, dist, large)


def _attn_bias_tables(rel_bias):
    period = 3 * QBLK
    diags = []
    for window, dil in PATTERNS:
        span = window // dil
        assert span <= QBLK
        bucket = _t5_bucket_np(np.arange(span + 1) * dil)
        onehot = np.eye(NUM_BUCKETS, dtype=np.float32)[bucket]
        vec = jnp.einsum("rb,bh->hr", onehot, rel_bias.astype(F32),
                         precision=lax.Precision.HIGHEST)
        diag = jnp.full((N_HEADS, period), NEG_INF, F32)
        diags.append(diag.at[:, 2 * QBLK - 1 - span:2 * QBLK].set(vec[:, ::-1]))
    diag = jnp.stack(diags) * LOG2E
    shifted = jnp.stack([jnp.roll(diag, b, axis=-1) for b in range(8)], axis=-2)
    return shifted.reshape(len(PATTERNS), N_HEADS // 2, 2, 8, period)


def _attn_kernel(q_ref, k_ref, v_ref, qg_ref, kg_ref, diag_ref, o_ref,
                 qn_ref, kn_ref, acc_ref, m_ref, l_ref, bias_s):
    lane = lax.broadcasted_iota(jnp.int32, (QBLK, LANES), 1)
    head_a = lane < HEAD_DIM
    hist_keys = lax.broadcasted_iota(jnp.int32, (2 * QBLK, 2 * QBLK), 1) < QBLK
    ri = lax.broadcasted_iota(jnp.int32, (LANES, LANES), 0) // HEAD_DIM
    ci = lax.broadcasted_iota(jnp.int32, (LANES, LANES), 1) // HEAD_DIM
    seg = (ri == ci).astype(BF16)

    def head_rms(x, g):
        sq = x * x
        hi = sq.astype(BF16)
        lo = (sq - hi.astype(F32)).astype(BF16)
        ss = (jnp.dot(hi, seg, preferred_element_type=F32)
              + jnp.dot(lo, seg, preferred_element_type=F32))
        return x * lax.rsqrt(ss * (1.0 / HEAD_DIM) + EPS) * g

    NCH = 512

    def norm_body(c, carry):
        rows = pl.ds(pl.multiple_of(c * NCH, NCH), NCH)
        qn_ref[rows, :] = head_rms(q_ref[rows, :], qg_ref[...]) * (LOG2E / math.sqrt(HEAD_DIM))
        kn_ref[rows, :] = head_rms(k_ref[rows, :], kg_ref[...])
        return carry

    lax.fori_loop(0, SEQ // NCH, norm_body, 0, unroll=2)

    for p in range(len(PATTERNS)):
        for h in range(2):
            offs = diag_ref[p, h]
            for a in range(QBLK // 8):
                lo = QBLK - 1 - 8 * a
                bias_s[p, h * QBLK + 8 * a:h * QBLK + 8 * a + 8, :] = offs[:, lo:lo + 2 * QBLK]

    order = sorted(range(len(PATTERNS)), key=lambda i: -PATTERNS[i][1])
    assert PATTERNS[order[-1]][1] == 1
    for step, p in enumerate(order):
        dil = PATTERNS[p][1]
        nb = SEQ // (dil * QBLK)
        is_first = step == 0
        is_last = step == len(order) - 1

        def rows(start, dil=dil):
            if dil == 1:
                return pl.ds(start, QBLK)
            return pl.ds(start, QBLK, stride=dil)

        def unit(cur, k_prev, v_prev, first, p=p, rows=rows, is_first=is_first, is_last=is_last):
            q = qn_ref[rows(cur), :]
            q2 = jnp.concatenate([jnp.where(head_a, q, 0.0), jnp.where(head_a, 0.0, q)],
                                 axis=0).astype(BF16)
            k_cur = kn_ref[rows(cur), :].astype(BF16)
            v_cur = v_ref[rows(cur), :].astype(BF16)
            k2 = jnp.concatenate([k_prev, k_cur], axis=0)
            v2 = jnp.concatenate([v_prev, v_cur], axis=0)
            s = lax.dot_general(q2, k2, (((1,), (1,)), ((), ())), preferred_element_type=F32)
            s = s + bias_s[p]
            if first is not None:
                s = jnp.where(hist_keys, jnp.where(first, NEG_INF * LOG2E, s), s)
            m = jnp.max(s, axis=-1, keepdims=True)
            e = jnp.exp2(s - m)
            l = jnp.sum(e, axis=-1, keepdims=True)
            pv = jnp.dot(e.astype(BF16), v2, preferred_element_type=F32)
            o_new = jnp.where(head_a, pv[:QBLK], pv[QBLK:])
            m_new = jnp.where(head_a, m[:QBLK], m[QBLK:])
            l_new = jnp.where(head_a, l[:QBLK], l[QBLK:])
            if is_first:
                acc_ref[rows(cur), :] = o_new
                m_ref[rows(cur), :] = m_new
                l_ref[rows(cur), :] = l_new
            else:
                m_old = m_ref[rows(cur), :]
                m_tot = jnp.maximum(m_old, m_new)
                a = jnp.exp2(m_old - m_tot)
                b = jnp.exp2(m_new - m_tot)
                acc = acc_ref[rows(cur), :] * a + o_new * b
                den = l_ref[rows(cur), :] * a + l_new * b
                if is_last:
                    o_ref[pl.ds(pl.multiple_of(cur, QBLK), QBLK), :] = (acc / den).astype(BF16)
                else:
                    acc_ref[rows(cur), :] = acc
                    l_ref[rows(cur), :] = den
                    m_ref[rows(cur), :] = m_tot
            return k_cur, v_cur

        per = nb // ATTN_GROUP
        assert per * ATTN_GROUP == nb

        def group(g, carry, dil=dil, per=per, rows=rows, unit=unit):
            r = g // per
            n0 = (g - r * per) * ATTN_GROUP
            hist = jnp.maximum(n0 - 1, 0) * (QBLK * dil) + r
            k_prev = kn_ref[rows(hist), :].astype(BF16)
            v_prev = v_ref[rows(hist), :].astype(BF16)
            for i in range(ATTN_GROUP):
                first = (n0 == 0) if i == 0 else None
                k_prev, v_prev = unit((n0 + i) * (QBLK * dil) + r, k_prev, v_prev, first)
            return carry

        lax.fori_loop(0, SEQ // (QBLK * ATTN_GROUP), group, 0, unroll=ATTN_UNROLL)


def _attention(proj, qg2, kg2, bias_tab):
    qoff = 2 * CONV_C // LANES
    koff = qoff + ATTN_W // LANES
    voff = koff + ATTN_W // LANES
    return pl.pallas_call(
        _attn_kernel,
        grid=(N_HEADS // 2,),
        in_specs=[
            pl.BlockSpec((SEQ, LANES), lambda h: (0, qoff + h)),
            pl.BlockSpec((SEQ, LANES), lambda h: (0, koff + h)),
            pl.BlockSpec((SEQ, LANES), lambda h: (0, voff + h)),
            pl.BlockSpec((1, LANES), lambda h: (0, 0)),
            pl.BlockSpec((1, LANES), lambda h: (0, 0)),
            pl.BlockSpec((len(PATTERNS), None, 2, 8, 3 * QBLK), lambda h: (0, h, 0, 0, 0)),
        ],
        out_specs=pl.BlockSpec((SEQ, LANES), lambda h: (0, h)),
        out_shape=jax.ShapeDtypeStruct((SEQ, ATTN_W), BF16),
        scratch_shapes=[pltpu.VMEM((SEQ, LANES), F32) for _ in range(5)]
        + [pltpu.VMEM((len(PATTERNS), 2 * QBLK, 2 * QBLK), F32)],
        compiler_params=_cparams(("arbitrary",)),
        name="dilated_attn",
    )(proj, proj, proj, qg2, kg2, bias_tab)


def _outproj_kernel(x_ref, c_ref, a_ref, wc_ref, wa_ref, o_ref):
    o_ref[...] = (x_ref[...]
                  + jnp.dot(c_ref[...], wc_ref[...].astype(BF16), preferred_element_type=F32)
                  + jnp.dot(a_ref[...], wa_ref[...].astype(BF16), preferred_element_type=F32))


def _outproj(x, conv_out, attn_out, w_out):
    return pl.pallas_call(
        _outproj_kernel,
        grid=(SEQ // TM_OUT,),
        in_specs=[
            pl.BlockSpec((TM_OUT, D_MODEL), lambda i: (i, 0)),
            pl.BlockSpec((TM_OUT, CONV_C), lambda i: (i, 0)),
            pl.BlockSpec((TM_OUT, ATTN_W), lambda i: (i, 0)),
            pl.BlockSpec((CONV_C, D_MODEL), lambda i: (0, 0), pipeline_mode=pl.Buffered(1)),
            pl.BlockSpec((ATTN_W, D_MODEL), lambda i: (1, 0), pipeline_mode=pl.Buffered(1)),
        ],
        out_specs=pl.BlockSpec((TM_OUT, D_MODEL), lambda i: (i, 0)),
        out_shape=jax.ShapeDtypeStruct((SEQ, D_MODEL), F32),
        compiler_params=_cparams(("arbitrary",)),
        name="outproj",
    )(x, conv_out, attn_out, w_out, w_out)


def _split2(a):
    a1 = a.astype(BF16)
    a2 = (a - a1.astype(F32)).astype(BF16)
    return a1, a2


def _router_kernel(x_ref, g_ref, wr_ref, br_ref, eid_ref, wts_ref, rank_ref, cnt_ref):
    i = pl.program_id(0)

    @pl.when(i == 0)
    def _():
        cnt_ref[...] = jnp.zeros_like(cnt_ref)

    x = x_ref[...]
    inv = lax.rsqrt(jnp.mean(x * x, axis=-1, keepdims=True) + EPS)
    dn = (((1,), (1,)), ((), ()))
    lt = None
    for c in range(D_MODEL // KC_R):
        cols = slice(c * KC_R, (c + 1) * KC_R)
        h1, h2 = _split2(x_ref[:, cols] * inv * g_ref[:, cols])
        w1, w2 = _split2(wr_ref[:, cols])
        for wa, ha in ((w1, h1), (w1, h2), (w2, h1)):
            t = lax.dot_general(wa, ha, dn, preferred_element_type=F32)
            lt = t if lt is None else lt + t
    lt = lt + br_ref[:, 0:1]

    row8 = lax.broadcasted_iota(jnp.int32, (8, TM_R), 0)
    gl = jnp.where(row8 < N_GROUPS, lt[0:8], -jnp.inf)
    gmax = jnp.max(gl, axis=0, keepdims=True)
    gidx = jnp.min(jnp.where(gl == gmax, row8, 8), axis=0, keepdims=True)
    gw = 1.0 / jnp.sum(jnp.exp(gl - gmax), axis=0, keepdims=True)

    esel = lt[8:16]
    for g in range(1, N_GROUPS):
        esel = jnp.where(gidx == g, lt[8 + 8 * g:16 + 8 * g], esel)
    v1 = jnp.max(esel, axis=0, keepdims=True)
    i1 = jnp.min(jnp.where(esel == v1, row8, 8), axis=0, keepdims=True)
    rest = jnp.where(row8 == i1, -jnp.inf, esel)
    v2 = jnp.max(rest, axis=0, keepdims=True)
    i2 = jnp.min(jnp.where(rest == v2, row8, 8), axis=0, keepdims=True)
    e21 = jnp.exp(v2 - v1)
    den = 1.0 + e21
    e1 = gidx * E_PER_G + i1
    e2 = gidx * E_PER_G + i2
    eid_ref[0:1, :] = e1
    eid_ref[1:2, :] = e2
    wts_ref[0:1, :] = gw * (1.0 / den)
    wts_ref[1:2, :] = gw * (e21 / den)

    erow = lax.broadcasted_iota(jnp.int32, (N_EXPERTS, TM_R), 0)
    oh1 = erow == e1
    oh2 = erow == e2
    member = jnp.where(oh1 | oh2, 1.0, 0.0)
    ti = lax.broadcasted_iota(jnp.int32, (TM_R, TM_R), 0)
    tj = lax.broadcasted_iota(jnp.int32, (TM_R, TM_R), 1)
    upper = jnp.where(ti < tj, 1.0, 0.0).astype(BF16)
    before = jnp.dot(member.astype(BF16), upper, preferred_element_type=F32)
    pos = before + cnt_ref[:, 0:1]
    rank_ref[0:1, :] = jnp.sum(jnp.where(oh1, pos, 0.0), axis=0, keepdims=True).astype(jnp.int32)
    rank_ref[1:2, :] = jnp.sum(jnp.where(oh2, pos, 0.0), axis=0, keepdims=True).astype(jnp.int32)
    cnt_ref[...] = cnt_ref[...] + jnp.sum(member, axis=1, keepdims=True)


def _router(x1, g2, wr_t, br):
    return pl.pallas_call(
        _router_kernel,
        grid=(SEQ // TM_R,),
        in_specs=[
            pl.BlockSpec((TM_R, D_MODEL), lambda i: (i, 0)),
            pl.BlockSpec((1, D_MODEL), lambda i: (0, 0)),
            pl.BlockSpec((R_ROWS, D_MODEL), lambda i: (0, 0)),
            pl.BlockSpec((R_ROWS, LANES), lambda i: (0, 0)),
        ],
        out_specs=[
            pl.BlockSpec((2, TM_R), lambda i: (0, i)),
            pl.BlockSpec((2, TM_R), lambda i: (0, i)),
            pl.BlockSpec((2, TM_R), lambda i: (0, i)),
            pl.BlockSpec((N_EXPERTS, LANES), lambda i: (0, 0)),
        ],
        out_shape=[
            jax.ShapeDtypeStruct((2, SEQ), jnp.int32),
            jax.ShapeDtypeStruct((2, SEQ), F32),
            jax.ShapeDtypeStruct((2, SEQ), jnp.int32),
            jax.ShapeDtypeStruct((N_EXPERTS, LANES), F32),
        ],
        compiler_params=_cparams(("arbitrary",)),
        name="router",
    )(x1, g2, wr_t, br)


def _dispatch_kernel(slot_ref, padrow_ref, x_ref, g_ref, xs_hbm, hbuf, zbuf, sem, zsem):
    i = pl.program_id(0)
    last = pl.num_programs(0) - 1
    buf = i % 2

    def tile_wait(b):
        for _ in range(2):
            pltpu.make_async_copy(hbuf.at[b], xs_hbm.at[pl.ds(0, TT_D), :], sem.at[b]).wait()

    @pl.when(i == 0)
    def _():
        zbuf[...] = jnp.zeros_like(zbuf)

        for c in range(XS_TAIL // ZROWS):
            pltpu.make_async_copy(zbuf, xs_hbm.at[pl.ds(XS_ROWS - XS_TAIL + c * ZROWS, ZROWS), :], zsem).start()
        for c in range(XS_TAIL // ZROWS):
            pltpu.make_async_copy(zbuf, xs_hbm.at[pl.ds(0, ZROWS), :], zsem).wait()

        def pad(e, c):
            dst = pl.multiple_of(padrow_ref[e], 8)
            pltpu.make_async_copy(zbuf.at[pl.ds(0, 8), :], xs_hbm.at[pl.ds(dst, 8), :], zsem).start()
            return c
        lax.fori_loop(0, N_EXPERTS, pad, 0)

        def padw(e, c):
            pltpu.make_async_copy(zbuf.at[pl.ds(0, 8), :], xs_hbm.at[pl.ds(0, 8), :], zsem).wait()
            return c
        lax.fori_loop(0, N_EXPERTS, padw, 0)

    @pl.when(i >= 2)
    def _():
        tile_wait(buf)

    x = x_ref[...]
    ms = jnp.mean(x * x, axis=-1, keepdims=True)
    hbuf[buf] = x * lax.rsqrt(ms + EPS) * g_ref[...]

    tok0 = i * TT_D

    def send(j8, c):
        for k8 in range(8):
            j = j8 * 8 + k8
            for k in range(2):
                dst = slot_ref[k * SEQ + tok0 + j]
                pltpu.make_async_copy(hbuf.at[buf, pl.ds(j, 1), :], xs_hbm.at[pl.ds(dst, 1), :],
                                      sem.at[buf]).start()
        return c
    lax.fori_loop(0, TT_D // 8, send, 0)

    @pl.when(i == last)
    def _():
        @pl.when(i >= 1)
        def _():
            tile_wait(1 - buf)
        tile_wait(buf)


def _dispatch(slot_flat, padrow, x1, g2):
    grid_spec = pltpu.PrefetchScalarGridSpec(
        num_scalar_prefetch=2,
        grid=(SEQ // TT_D,),
        in_specs=[
            pl.BlockSpec((TT_D, D_MODEL), lambda i, *_: (i, 0)),
            pl.BlockSpec((1, D_MODEL), lambda i, *_: (0, 0)),
        ],
        out_specs=pl.BlockSpec(memory_space=pl.ANY),
        scratch_shapes=[
            pltpu.VMEM((2, TT_D, D_MODEL), F32),
            pltpu.VMEM((ZROWS, D_MODEL), F32),
            pltpu.SemaphoreType.DMA((2,)),
            pltpu.SemaphoreType.DMA,
        ],
    )
    return pl.pallas_call(
        _dispatch_kernel,
        grid_spec=grid_spec,
        out_shape=jax.ShapeDtypeStruct((XS_ROWS, D_MODEL), F32),
        compiler_params=_cparams(("arbitrary",)),
        name="moe_dispatch",
    )(slot_flat, padrow, x1, g2)


def _experts_kernel(ie_ref, row0_ref, rows_ref, nitems_ref,
                    xs_hbm, wg_hbm, wu_hbm, wd_hbm, ys_hbm,
                    xg_ref, y_ref, zbuf, wg_ref, wu_ref, wd_ref, gsem, ssem, zsem, wsem):
    it = pl.program_id(0)
    nitems = nitems_ref[0]
    slot = it % 2
    half = SUB_E // 2
    nsteps = nitems * NF_E

    def weight_copies(s):
        e = ie_ref[s // NF_E]
        cols = pl.ds(pl.multiple_of((s % NF_E) * TF_E, TF_E), TF_E)
        b = s % W_RING
        return (pltpu.make_async_copy(wg_hbm.at[e, :, cols], wg_ref.at[b], wsem.at[b]),
                pltpu.make_async_copy(wu_hbm.at[e, :, cols], wu_ref.at[b], wsem.at[b]),
                pltpu.make_async_copy(wd_hbm.at[e, cols, :], wd_ref.at[b], wsem.at[b]))

    @pl.when(it == 0)
    def _():
        for s in range(W_RING - 1):
            for cp in weight_copies(s):
                cp.start()

    def padded(item):
        return pl.multiple_of(((rows_ref[item] + half - 1) // half) * half, half)

    nhalf = padded(it) // half
    nsub = nhalf // 2

    def chunk_copies(item, start):
        base = pl.multiple_of(row0_ref[item], 8)
        buf = item % 2

        def body(c, carry):
            off = pl.multiple_of(c * half, half)
            src = xs_hbm.at[pl.ds(pl.multiple_of(base + off, 8), half), :]
            dst = ys_hbm.at[pl.ds(pl.multiple_of(base + off, 8), half), :]
            if start == "fetch":
                pltpu.make_async_copy(src, xg_ref.at[buf, pl.ds(off, half), :], gsem.at[buf]).start()
            else:
                pltpu.make_async_copy(y_ref.at[buf, pl.ds(off, half), :], dst, ssem.at[buf]).start()
            return carry
        lax.fori_loop(0, padded(item) // half, body, 0)

    def wait_fetch(item):
        got = pl.ds(0, padded(item))
        buf = item % 2
        pltpu.make_async_copy(xs_hbm.at[got, :], xg_ref.at[buf, got, :], gsem.at[buf]).wait()

    def wait_store(item):
        put = pl.ds(0, padded(item))
        buf = item % 2
        pltpu.make_async_copy(y_ref.at[buf, put, :], ys_hbm.at[put, :], ssem.at[buf]).wait()

    @pl.when(it == 0)
    def _():
        zbuf[...] = jnp.zeros_like(zbuf)
        for c in range(XS_TAIL // ZROWS):
            pltpu.make_async_copy(zbuf, ys_hbm.at[pl.ds(XS_ROWS - XS_TAIL + c * ZROWS, ZROWS), :], zsem).start()
        for c in range(XS_TAIL // ZROWS):
            pltpu.make_async_copy(zbuf, ys_hbm.at[pl.ds(0, ZROWS), :], zsem).wait()
        chunk_copies(0, "fetch")

    wait_fetch(it)

    @pl.when(it + 1 < nitems)
    def _():
        chunk_copies(it + 1, "fetch")

    def chunk_step(f, carry, first=False):
        step = it * NF_E + f
        wslot = step % W_RING

        @pl.when(step + W_RING - 1 < nsteps)
        def _():
            for cp in weight_copies(step + W_RING - 1):
                cp.start()

        for cp in weight_copies(step):
            cp.wait()

        def block(rows):
            xb = xg_ref[slot, rows, :].astype(BF16)
            hg = jnp.dot(xb, wg_ref[wslot].astype(BF16), preferred_element_type=F32)
            hu = jnp.dot(xb, wu_ref[wslot].astype(BF16), preferred_element_type=F32)
            h = (hg * jax.nn.sigmoid(hg) * hu).astype(BF16)
            y = jnp.dot(h, wd_ref[wslot].astype(BF16), preferred_element_type=F32)
            y_ref[slot, rows, :] = y if first else y_ref[slot, rows, :] + y

        def full(s):
            return pl.ds(pl.multiple_of(s * SUB_E, SUB_E), SUB_E)

        has_tail = nhalf % 2 == 1
        odd = nsub % 2 == 1
        tail_rows = pl.ds(pl.multiple_of(nsub * SUB_E, SUB_E), half)
        npairs = jnp.where(has_tail & jnp.logical_not(odd) & (nsub >= 2), nsub // 2 - 1, nsub // 2)

        def sub_pair(s, c):
            block(full(2 * s))
            block(full(2 * s + 1))
            return c
        lax.fori_loop(0, npairs, sub_pair, 0)

        @pl.when(odd & jnp.logical_not(has_tail))
        def _():
            block(full(nsub - 1))

        @pl.when(odd & has_tail)
        def _():
            block(full(nsub - 1))
            block(tail_rows)

        @pl.when(jnp.logical_not(odd) & has_tail & (nsub >= 2))
        def _():
            block(full(nsub - 2))
            block(full(nsub - 1))
            block(tail_rows)

        @pl.when(has_tail & (nsub == 0))
        def _():
            block(tail_rows)
        return carry

    chunk_step(0, 0, first=True)
    lax.fori_loop(1, NF_E, chunk_step, 0)

    @pl.when(it > 0)
    def _():
        wait_store(it - 1)

    chunk_copies(it, "store")

    @pl.when(it + 1 >= nitems)
    def _():
        wait_store(it)


def _experts(ie, row0, rows, nitems, xs, wg, wu, wd):
    grid_spec = pltpu.PrefetchScalarGridSpec(
        num_scalar_prefetch=4,
        grid=(nitems[0],),
        in_specs=[pl.BlockSpec(memory_space=pl.ANY) for _ in range(4)],
        out_specs=pl.BlockSpec(memory_space=pl.ANY),
        scratch_shapes=[
            pltpu.VMEM((2, TM_E, D_MODEL), F32),
            pltpu.VMEM((2, TM_E, D_MODEL), F32),
            pltpu.VMEM((ZROWS, D_MODEL), F32),
            pltpu.VMEM((W_RING, D_MODEL, TF_E), F32),
            pltpu.VMEM((W_RING, D_MODEL, TF_E), F32),
            pltpu.VMEM((W_RING, TF_E, D_MODEL), F32),
            pltpu.SemaphoreType.DMA((2,)),
            pltpu.SemaphoreType.DMA((2,)),
            pltpu.SemaphoreType.DMA,
            pltpu.SemaphoreType.DMA((W_RING,)),
        ],
    )
    return pl.pallas_call(
        _experts_kernel,
        grid_spec=grid_spec,
        out_shape=jax.ShapeDtypeStruct((XS_ROWS, D_MODEL), F32),
        compiler_params=_cparams(("arbitrary",)),
        name="moe_experts",
    )(ie, row0, rows, nitems, xs, wg, wu, wd)


def _gcombine_kernel(slot_ref, x_ref, w_ref, ys_hbm, o_ref, dbuf, sem):
    i = pl.program_id(0)
    n = pl.num_programs(0)
    buf = i % 2

    def fetch(tile, b):
        tok0 = tile * TT_G

        def body(j8, c):
            for k8 in range(8):
                j = j8 * 8 + k8
                for k in range(2):
                    src = slot_ref[k * SEQ + tok0 + j]
                    pltpu.make_async_copy(ys_hbm.at[pl.ds(src, 1), :], dbuf.at[b, k, pl.ds(j, 1), :],
                                          sem.at[b]).start()
            return c
        lax.fori_loop(0, TT_G // 8, body, 0)

    @pl.when(i == 0)
    def _():
        fetch(0, 0)

    @pl.when(i + 1 < n)
    def _():
        fetch(i + 1, 1 - buf)

    for k in range(2):
        pltpu.make_async_copy(ys_hbm.at[pl.ds(0, TT_G), :], dbuf.at[buf, k], sem.at[buf]).wait()

    w = w_ref[...]
    o_ref[...] = x_ref[...] + w[:, 0:1] * dbuf[buf, 0] + w[:, 1:2] * dbuf[buf, 1]


def _gcombine(slot_flat, x1, wts_t, ys):
    grid_spec = pltpu.PrefetchScalarGridSpec(
        num_scalar_prefetch=1,
        grid=(SEQ // TT_G,),
        in_specs=[
            pl.BlockSpec((TT_G, D_MODEL), lambda i, *_: (i, 0)),
            pl.BlockSpec((TT_G, 2), lambda i, *_: (i, 0)),
            pl.BlockSpec(memory_space=pl.ANY),
        ],
        out_specs=pl.BlockSpec((TT_G, D_MODEL), lambda i, *_: (i, 0)),
        scratch_shapes=[
            pltpu.VMEM((2, 2, TT_G, D_MODEL), F32),
            pltpu.SemaphoreType.DMA((2,)),
        ],
    )
    return pl.pallas_call(
        _gcombine_kernel,
        grid_spec=grid_spec,
        out_shape=jax.ShapeDtypeStruct((SEQ, D_MODEL), F32),
        compiler_params=_cparams(("arbitrary",)),
        name="moe_combine",
    )(slot_flat, x1, wts_t, ys)


def _routing_tables(eid, rank, counts_f):
    counts = counts_f[:, 0].astype(jnp.int32)
    aligned = ((counts + 7) // 8) * 8
    base = jnp.cumsum(aligned) - aligned
    tiles = (counts + TM_E - 1) // TM_E
    tcum = jnp.cumsum(tiles)
    tstart = tcum - tiles
    nitems = tcum[-1]
    ids = jnp.arange(MAX_ITEMS, dtype=jnp.int32)
    ie = jnp.minimum(jnp.sum(ids[:, None] >= tcum[None, :], axis=1), N_EXPERTS - 1).astype(jnp.int32)
    live = ids < nitems
    sel = ie[:, None] == jnp.arange(N_EXPERTS, dtype=jnp.int32)[None, :]

    def pick(v):
        return jnp.sum(jnp.where(sel, v[None, :], 0), axis=1)

    jt = ids - pick(tstart)
    row0 = jnp.where(live, pick(base) + jt * TM_E, 0)
    rows = jnp.where(live, jnp.clip(pick(counts) - jt * TM_E, 0, TM_E), 0)
    eoh = eid[:, :, None] == jnp.arange(N_EXPERTS, dtype=jnp.int32)
    slot = jnp.sum(jnp.where(eoh, base, 0), axis=-1) + rank
    spare = XS_ROWS - 8 * (1 + jnp.arange(N_EXPERTS, dtype=jnp.int32))
    padrow = jnp.where(counts % 8 != 0, base + (counts // 8) * 8, spare)
    return (ie, row0.astype(jnp.int32), rows.astype(jnp.int32), nitems.reshape(1).astype(jnp.int32),
            slot.reshape(-1).astype(jnp.int32), padrow.astype(jnp.int32))


def kernel(x, norm1_g, w_in, q_norm_g, k_norm_g, conv_w, conv_b, conv_ln_g, conv_ln_b, rel_bias,
           w_out, norm2_g, w_router_group, b_router_group, w_router_expert, b_router_expert,
           w_gate, w_up, w_down):
    assert x.shape == (1, SEQ, D_MODEL) and w_in.shape[0] == 1
    xs = x[0]
    bias_tab = _attn_bias_tables(rel_bias)
    qg2 = jnp.tile(q_norm_g[0], 2)[None]
    kg2 = jnp.tile(k_norm_g[0], 2)[None]

    proj = _inproj(xs, norm1_g[0][None], w_in[0])
    conv_out = _conv_mixer(proj, conv_w[0], conv_b[0][None], conv_ln_g[0][None], conv_ln_b[0][None])
    attn_out = _attention(proj, qg2, kg2, bias_tab)
    x1 = _outproj(xs, conv_out, attn_out, w_out[0])

    wr_t = jnp.concatenate([
        w_router_group[0].T, jnp.zeros((8 - N_GROUPS, D_MODEL), F32),
        jnp.transpose(w_router_expert[0], (0, 2, 1)).reshape(N_EXPERTS, D_MODEL)], axis=0)
    br = jnp.concatenate([b_router_group[0], jnp.zeros((8 - N_GROUPS,), F32),
                          b_router_expert[0].reshape(-1)])
    br = jnp.broadcast_to(br[:, None], (R_ROWS, LANES))
    eid, wts, rank, counts_f = _router(x1, norm2_g[0][None], wr_t, br)

    ie, row0, rows, nitems, slot_flat, padrow = _routing_tables(eid, rank, counts_f)
    xs = _dispatch(slot_flat, padrow, x1, norm2_g[0][None])
    ys = _experts(ie, row0, rows, nitems, xs,
                  w_gate[0].reshape(N_EXPERTS, D_MODEL, D_FF),
                  w_up[0].reshape(N_EXPERTS, D_MODEL, D_FF),
                  w_down[0].reshape(N_EXPERTS, D_FF, D_MODEL))
    out = _gcombine(slot_flat, x1, wts.T, ys)
    return out[None]
```
